```python
import math
import jax
import jax.numpy as jnp
from jax import lax
import numpy as np

D_MODEL = 1024
BATCH = 4
SEQ = 4096
DEPTH = 1
DEC_BATCH = 16
DEC_SEQ = 32
PAST_LEN = 2048

CHUNK = 64
N_META = 16
N_HEADS = 8
QK_NOPE = 128
QK_ROPE = 64
V_DIM = 128
QK_DIM = QK_NOPE + QK_ROPE
Q_LORA = 384
KV_LORA = 256
D_SSM = D_MODEL
SSM_GROUP = 16
N_GROUPS = D_SSM // SSM_GROUP
SSM_STATE = 64
D_FF = 4 * D_MODEL
Q_BLOCK = 128
ROPE_THETA = 10000.0
EPS = 1e-6
ATTN_SCALE = QK_DIM ** -0.5
NEG_INF = -1e30

O_KV = Q_LORA
O_KR = O_KV + KV_LORA
O_SSM = O_KR + QK_ROPE
O_GA = O_SSM + D_SSM
O_GB = O_GA + D_MODEL
IN_COLS = O_GB + D_MODEL

kernel_name = "mla_s5_gated_streaming_encoder_step"


def rmsnorm(x, g):
    xf = x.astype(jnp.float32)
    y = xf * lax.rsqrt(jnp.mean(xf * xf, axis=-1, keepdims=True) + EPS)
    return (y * g.astype(jnp.float32)).astype(x.dtype)


def rope(x, pos):
    half = QK_ROPE // 2
    inv = ROPE_THETA ** (-jnp.arange(half, dtype=jnp.float32) / half)
    ang = pos.astype(jnp.float32)[:, None] * inv[None, :]
    ang = ang.reshape((ang.shape[0],) + (1,) * (x.ndim - 3) + (half,))
    cos, sin = jnp.cos(ang), jnp.sin(ang)
    xf = x.astype(jnp.float32)
    x1, x2 = xf[..., :half], xf[..., half:]
    return jnp.concatenate([x1 * cos - x2 * sin, x2 * cos + x1 * sin], axis=-1).astype(x.dtype)


def attend(q, k, v, mask):
    s = jnp.einsum("bqhd,bkhd->bhqk", q.astype(jnp.float32), k.astype(jnp.float32)) * ATTN_SCALE
    if mask is not None:
        s = jnp.where(mask[None, None], s, NEG_INF)
    p = jax.nn.softmax(s, axis=-1)
    return jnp.einsum("bhqk,bkhv->bqhv", p, v.astype(jnp.float32)).astype(v.dtype)


def mla_expand(c_kv, k_rope, p):
    kv = jnp.einsum("bkc,chd->bkhd", c_kv, p["w_ukv"])
    k_nope = rmsnorm(kv[..., :QK_NOPE], p["k_nope_norm"])
    v = kv[..., QK_NOPE:]
    kr = jnp.broadcast_to(k_rope[:, :, None, :], k_nope.shape[:3] + (QK_ROPE,)).astype(k_nope.dtype)
    return jnp.concatenate([k_nope, kr], axis=-1), v


def block_causal_attention(q, k, v):
    bsz, L = q.shape[0], q.shape[1]
    n_real = L - N_META
    o_meta = attend(q[:, :N_META], k[:, :N_META], v[:, :N_META], None)
    key_chunk = jnp.concatenate([jnp.full((N_META,), -1, jnp.int32),
                                 jnp.arange(n_real, dtype=jnp.int32) // CHUNK])
    n_blk = n_real // Q_BLOCK
    q_blocks = q[:, N_META:].reshape(bsz, n_blk, Q_BLOCK, N_HEADS, QK_DIM).transpose(1, 0, 2, 3, 4)

    def one_block(args):
        qb, i = args
        q_chunk = (i * Q_BLOCK + jnp.arange(Q_BLOCK, dtype=jnp.int32)) // CHUNK
        mask = key_chunk[None, :] <= q_chunk[:, None]
        return attend(qb, k, v, mask)

    o = lax.map(one_block, (q_blocks, jnp.arange(n_blk, dtype=jnp.int32)))
    o = o.transpose(1, 0, 2, 3, 4).reshape(bsz, n_real, N_HEADS, V_DIM)
    return jnp.concatenate([o_meta, o], axis=1)


def complex_affine_combine(e1, e2):
    a1r, a1i, b1r, b1i = e1
    a2r, a2i, b2r, b2i = e2
    return (a2r * a1r - a2i * a1i,
            a2r * a1i + a2i * a1r,
            a2r * b1r - a2i * b1i + b2r,
            a2r * b1i + a2i * b1r + b2i)


def s5_scan(u, h0_re, h0_im, p):
    f32 = jnp.float32
    a_re = p["ssm_a_re"].astype(f32)
    a_im = p["ssm_a_im"].astype(f32)
    dt = jnp.exp(p["ssm_log_dt"].astype(f32))[:, None]
    mag = jnp.exp(a_re * dt)
    lam_re, lam_im = mag * jnp.cos(a_im * dt), mag * jnp.sin(a_im * dt)
    den = a_re * a_re + a_im * a_im
    f_re = ((lam_re - 1.0) * a_re + lam_im * a_im) / den
    f_im = (lam_im * a_re - (lam_re - 1.0) * a_im) / den
    bsz, L = u.shape[0], u.shape[1]
    ug = u.astype(f32).reshape(bsz, L, N_GROUPS, SSM_GROUP)
    bu_re = jnp.einsum("blgc,gnc->blgn", ug, p["ssm_b_re"].astype(f32))
    bu_im = jnp.einsum("blgc,gnc->blgn", ug, p["ssm_b_im"].astype(f32))
    x_re = f_re * bu_re - f_im * bu_im
    x_im = f_re * bu_im + f_im * bu_re
    ar = jnp.broadcast_to(lam_re, x_re.shape)
    ai = jnp.broadcast_to(lam_im, x_re.shape)
    ar, ai, sr, si = lax.associative_scan(complex_affine_combine, (ar, ai, x_re, x_im), axis=1)
    h0r = h0_re.astype(f32)[:, None]
    h0i = h0_im.astype(f32)[:, None]
    hr = sr + ar * h0r - ai * h0i
    hi = si + ar * h0i + ai * h0r
    y = (jnp.einsum("blgn,gcn->blgc", hr, p["ssm_c_re"].astype(f32))
         - jnp.einsum("blgn,gcn->blgc", hi, p["ssm_c_im"].astype(f32)))
    y = y.reshape(bsz, L, D_SSM) + p["ssm_d"].astype(f32) * u.astype(f32)
    return y, hr[:, -1], hi[:, -1]


def hybrid_layer(h, pos, prefix, h0_re, h0_im, p):
    xn = rmsnorm(h, p["norm_mix"])
    z = jnp.einsum("bld,dc->blc", xn, p["w_in"])
    q_lat, kv_lat, kr_raw = z[..., :O_KV], z[..., O_KV:O_KR], z[..., O_KR:O_SSM]
    u, g_a, g_b = z[..., O_SSM:O_GA], z[..., O_GA:O_GB], z[..., O_GB:]
    c_q = rmsnorm(q_lat, p["q_lora_norm"])
    q = jnp.einsum("blr,rhd->blhd", c_q, p["w_uq"])
    q = jnp.concatenate([rmsnorm(q[..., :QK_NOPE], p["q_nope_norm"]),
                         rope(rmsnorm(q[..., QK_NOPE:], p["q_rope_norm"]), pos)], axis=-1)
    c_kv = rmsnorm(kv_lat, p["kv_lora_norm"])
    k_rope = rope(rmsnorm(kr_raw, p["k_rope_norm"]), pos)
    if prefix is None:
        k, v = mla_expand(c_kv, k_rope, p)
        o_attn = block_causal_attention(q, k, v)
    else:
        pre_c, pre_kr = prefix
        k, v = mla_expand(jnp.concatenate([pre_c, c_kv], axis=1),
                          jnp.concatenate([pre_kr, k_rope], axis=1), p)
        o_attn = attend(q, k, v, None)
    o_a = jnp.einsum("blhv,hvd->bld", o_attn, p["w_o_attn"])
    y_ssm, hr, hi = s5_scan(u, h0_re, h0_im, p)
    s = jax.nn.gelu(y_ssm)
    o_b = (s @ p["w_glu_v"]) * jax.nn.sigmoid(s @ p["w_glu_g"])
    merged = jax.nn.sigmoid(g_a) * o_a + jax.nn.sigmoid(g_b) * o_b
    h = h + merged @ p["w_out"]
    a = jax.nn.relu(rmsnorm(h, p["norm_mlp"]) @ p["w_mlp_up"])
    h = h + (a * a) @ p["w_mlp_down"]
    return h, c_kv, k_rope, hr, hi


def setup_inputs(seed: int = 0) -> dict:
    key = jax.random.key(seed)
    ks = jax.random.split(key, 40)
    nrm = jax.random.normal
    f32 = jnp.float32

    def gain(k, shape):
        return 1.0 + 0.02 * nrm(k, shape, f32)

    n_idx = jnp.arange(SSM_STATE, dtype=f32)
    return {
        "x_prompt": nrm(ks[0], (BATCH, SEQ, D_MODEL), f32),
        "x_sample": nrm(ks[1], (DEC_BATCH, DEC_SEQ, D_MODEL), f32),
        "cache_latent": nrm(ks[2], (DEPTH, DEC_BATCH, PAST_LEN, KV_LORA), f32),
        "cache_krope": nrm(ks[3], (DEPTH, DEC_BATCH, PAST_LEN, QK_ROPE), f32),
        "cache_meta_latent": nrm(ks[4], (DEPTH, DEC_BATCH, N_META, KV_LORA), f32),
        "cache_meta_krope": nrm(ks[5], (DEPTH, DEC_BATCH, N_META, QK_ROPE), f32),
        "state_ssm_re": 0.5 * nrm(ks[6], (DEPTH, DEC_BATCH, N_GROUPS, SSM_STATE), f32),
        "state_ssm_im": 0.5 * nrm(ks[7], (DEPTH, DEC_BATCH, N_GROUPS, SSM_STATE), f32),
        "meta_tokens": nrm(ks[8], (N_META, D_MODEL), f32),
        "norm_mix": gain(ks[9], (DEPTH, D_MODEL)),
        "w_in": nrm(ks[10], (DEPTH, D_MODEL, IN_COLS), f32) * D_MODEL ** -0.5,
        "q_lora_norm": gain(ks[11], (DEPTH, Q_LORA)),
        "w_uq": nrm(ks[12], (DEPTH, Q_LORA, N_HEADS, QK_DIM), f32) * Q_LORA ** -0.5,
        "q_nope_norm": gain(ks[13], (DEPTH, QK_NOPE)),
        "q_rope_norm": gain(ks[14], (DEPTH, QK_ROPE)),
        "kv_lora_norm": gain(ks[15], (DEPTH, KV_LORA)),
        "k_rope_norm": gain(ks[16], (DEPTH, QK_ROPE)),
        "w_ukv": nrm(ks[17], (DEPTH, KV_LORA, N_HEADS, QK_NOPE + V_DIM), f32) * KV_LORA ** -0.5,
        "k_nope_norm": gain(ks[18], (DEPTH, QK_NOPE)),
        "w_o_attn": nrm(ks[19], (DEPTH, N_HEADS, V_DIM, D_MODEL), f32) * (N_HEADS * V_DIM) ** -0.5,
        "ssm_a_re": -0.5 + 0.01 * nrm(ks[20], (DEPTH, N_GROUPS, SSM_STATE), f32),
        "ssm_a_im": math.pi * n_idx + 0.01 * nrm(ks[21], (DEPTH, N_GROUPS, SSM_STATE), f32),
        "ssm_log_dt": jax.random.uniform(ks[22], (DEPTH, N_GROUPS), f32, math.log(1e-3), math.log(1e-1)),
        "ssm_b_re": nrm(ks[23], (DEPTH, N_GROUPS, SSM_STATE, SSM_GROUP), f32) * (2 * SSM_GROUP) ** -0.5,
        "ssm_b_im": nrm(ks[24], (DEPTH, N_GROUPS, SSM_STATE, SSM_GROUP), f32) * (2 * SSM_GROUP) ** -0.5,
        "ssm_c_re": nrm(ks[25], (DEPTH, N_GROUPS, SSM_GROUP, SSM_STATE), f32) * SSM_STATE ** -0.5,
        "ssm_c_im": nrm(ks[26], (DEPTH, N_GROUPS, SSM_GROUP, SSM_STATE), f32) * SSM_STATE ** -0.5,
        "ssm_d": nrm(ks[27], (DEPTH, D_SSM), f32),
        "w_glu_v": nrm(ks[28], (DEPTH, D_SSM, D_MODEL), f32) * D_SSM ** -0.5,
        "w_glu_g": nrm(ks[29], (DEPTH, D_SSM, D_MODEL), f32) * D_SSM ** -0.5,
        "w_out": nrm(ks[30], (DEPTH, D_MODEL, D_MODEL), f32) * D_MODEL ** -0.5,
        "norm_mlp": gain(ks[31], (DEPTH, D_MODEL)),
        "w_mlp_up": nrm(ks[32], (DEPTH, D_MODEL, D_FF), f32) * D_MODEL ** -0.5,
        "w_mlp_down": nrm(ks[33], (DEPTH, D_FF, D_MODEL), f32) * D_FF ** -0.5,
    }


def reference(x_prompt, x_sample, cache_latent, cache_krope, cache_meta_latent, cache_meta_krope,
              state_ssm_re, state_ssm_im, meta_tokens, norm_mix, w_in, q_lora_norm, w_uq,
              q_nope_norm, q_rope_norm, kv_lora_norm, k_rope_norm, w_ukv, k_nope_norm, w_o_attn,
              ssm_a_re, ssm_a_im, ssm_log_dt, ssm_b_re, ssm_b_im, ssm_c_re, ssm_c_im, ssm_d,
              w_glu_v, w_glu_g, w_out, norm_mlp, w_mlp_up, w_mlp_down):
    bsz_p, seq_p = x_prompt.shape[0], x_prompt.shape[1]
    seq_s = x_sample.shape[1]
    meta = jnp.broadcast_to(meta_tokens[None].astype(x_prompt.dtype), (bsz_p, N_META, D_MODEL))
    hp = jnp.concatenate([meta, x_prompt], axis=1)
    pos_p = jnp.arange(N_META + seq_p, dtype=jnp.int32) - N_META
    hs = x_sample
    pos_s = PAST_LEN + jnp.arange(seq_s, dtype=jnp.int32)
    h0_zero = jnp.zeros((bsz_p, N_GROUPS, SSM_STATE), jnp.float32)

    lat_p, kr_p, mlat_p, mkr_p, sre_p, sim_p = [], [], [], [], [], []
    lat_s, kr_s, sre_s, sim_s = [], [], [], []
    for l in range(DEPTH):
        p = dict(norm_mix=norm_mix[l], w_in=w_in[l], q_lora_norm=q_lora_norm[l], w_uq=w_uq[l],
                 q_nope_norm=q_nope_norm[l], q_rope_norm=q_rope_norm[l], kv_lora_norm=kv_lora_norm[l],
                 k_rope_norm=k_rope_norm[l], w_ukv=w_ukv[l], k_nope_norm=k_nope_norm[l],
                 w_o_attn=w_o_attn[l], ssm_a_re=ssm_a_re[l], ssm_a_im=ssm_a_im[l],
                 ssm_log_dt=ssm_log_dt[l], ssm_b_re=ssm_b_re[l], ssm_b_im=ssm_b_im[l],
                 ssm_c_re=ssm_c_re[l], ssm_c_im=ssm_c_im[l], ssm_d=ssm_d[l], w_glu_v=w_glu_v[l],
                 w_glu_g=w_glu_g[l], w_out=w_out[l], norm_mlp=norm_mlp[l], w_mlp_up=w_mlp_up[l],
                 w_mlp_down=w_mlp_down[l])
        hp, c_kv, k_rope, hr, hi = hybrid_layer(hp, pos_p, None, h0_zero, h0_zero, p)
        mlat_p.append(c_kv[:, :N_META])
        mkr_p.append(k_rope[:, :N_META])
        lat_p.append(c_kv[:, N_META:])
        kr_p.append(k_rope[:, N_META:])
        sre_p.append(hr)
        sim_p.append(hi)
        prefix = (jnp.concatenate([cache_meta_latent[l], cache_latent[l]], axis=1),
                  jnp.concatenate([cache_meta_krope[l], cache_krope[l]], axis=1))
        hs, c_kv, k_rope, hr, hi = hybrid_layer(hs, pos_s, prefix, state_ssm_re[l], state_ssm_im[l], p)
        lat_s.append(c_kv)
        kr_s.append(k_rope)
        sre_s.append(hr)
        sim_s.append(hi)

    y_prompt = hp[:, N_META:]
    y_sample = hs
    return (y_prompt, y_sample,
            jnp.stack(lat_p), jnp.stack(kr_p), jnp.stack(mlat_p), jnp.stack(mkr_p),
            jnp.stack(sre_p), jnp.stack(sim_p),
            jnp.stack(lat_s), jnp.stack(kr_s), jnp.stack(sre_s), jnp.stack(sim_s))
```

```python
import functools
import math

import jax
import jax.numpy as jnp
from jax import lax
from jax.experimental import pallas as pl
from jax.experimental.pallas import tpu as pltpu

F32 = jnp.float32
BF16 = jnp.bfloat16

D_MODEL = 1024
CHUNK = 64
N_META = 16
N_HEADS = 8
QK_NOPE = 128
QK_ROPE = 64
V_DIM = 128
QK_DIM = QK_NOPE + QK_ROPE
Q_LORA = 384
KV_LORA = 256
SSM_GROUP = 16
N_GROUPS = D_MODEL // SSM_GROUP
SSM_STATE = 64
D_FF = 4 * D_MODEL
ROPE_THETA = 10000.0
EPS = 1e-6
ATTN_SCALE = QK_DIM ** -0.5
NEG_INF = -1e30

LANES = 128
HEAD_PAD = 2 * LANES
SSM_T = 8
SSM_NJ = D_MODEL // LANES
SSM_GPB = LANES // SSM_GROUP
SSM_SW = SSM_GPB * SSM_STATE
VMEM_LIMIT = 56 * 1024 * 1024


def _cp(sem):
    return pltpu.CompilerParams(dimension_semantics=sem, vmem_limit_bytes=VMEM_LIMIT)


def _dot(a, b):
    return jnp.dot(a, b, preferred_element_type=F32)


def _dot_nt(a, b):
    return lax.dot_general(a, b, (((1,), (1,)), ((), ())), preferred_element_type=F32)


def _rms(x, g, n=None):
    n = x.shape[-1] if n is None else n
    ms = jnp.sum(x * x, axis=-1, keepdims=True) * (1.0 / n)
    return x * lax.rsqrt(ms + EPS) * g


def _rope128(b, ct, s1, s2):
    return b * ct + pltpu.roll(b, LANES - QK_ROPE // 2, 1) * s1 + pltpu.roll(b, QK_ROPE // 2, 1) * s2


def _front_kernel(x_ref, ct_ref, s1_ref, s2_ref, nmix_ref, wq_ref, wkv_ref, wkr_ref, wu_ref, wga_ref,
                  wgb_ref, gq_ref, wuqn_ref, wuqr_ref, gqn_ref, gqr_ref, gkv_ref, gkr_ref, wuk_ref,
                  wuv_ref, gkn_ref,
                  qn_ref, qr_ref, kn_ref, krb_ref, v_ref, ckv_ref, kr_ref, u_ref, ga_ref, gb_ref):
    x = x_ref[...]
    xn = _rms(x, nmix_ref[...]).astype(BF16)
    ct, s1, s2 = ct_ref[...], s1_ref[...], s2_ref[...]

    u = _dot(xn, wu_ref[...])
    for j in range(SSM_NJ):
        u_ref[j] = u[:, j * LANES:(j + 1) * LANES]
    ga_ref[...] = jax.nn.sigmoid(_dot(xn, wga_ref[...]))
    gb_ref[...] = jax.nn.sigmoid(_dot(xn, wgb_ref[...]))

    c_kv = _rms(_dot(xn, wkv_ref[...]), gkv_ref[...])
    ckv_ref[...] = c_kv
    kr = _rms(_dot(xn, wkr_ref[...]), gkr_ref[...], QK_ROPE)
    kr = _rope128(kr, ct, s1, s2)
    kr_ref[...] = kr[:, :QK_ROPE]
    krb_ref[...] = kr.astype(BF16)
    c16 = c_kv.astype(BF16)
    kn = _dot(c16, wuk_ref[...])
    for h in range(N_HEADS):
        sl = slice(h * LANES, (h + 1) * LANES)
        kn_ref[:, sl] = _rms(kn[:, sl], gkn_ref[...]).astype(BF16)
    v_ref[...] = _dot(c16, wuv_ref[...]).astype(BF16)

    c_q = _rms(_dot(xn, wq_ref[...]), gq_ref[...]).astype(BF16)
    qn = _dot(c_q, wuqn_ref[...])
    qr = _dot(c_q, wuqr_ref[...])
    for h in range(N_HEADS):
        sl = slice(h * LANES, (h + 1) * LANES)
        qn_ref[:, sl] = (_rms(qn[:, sl], gqn_ref[...]) * ATTN_SCALE).astype(BF16)
        r = _rope128(_rms(qr[:, sl], gqr_ref[...], QK_ROPE), ct, s1, s2)
        qr_ref[:, sl] = (r * ATTN_SCALE).astype(BF16)


def _front(x, tabs, tab_blocks, w, tm):
    n = x.shape[0]
    nt = n // tm
    ct, s1, s2 = tabs

    def row(i):
        return (i, 0)

    def tab(i):
        return (i % tab_blocks, 0)

    def const(i):
        return (0, 0)

    def full(a):
        return pl.BlockSpec(a.shape, const)

    weights = [w["nmix"], w["wq"], w["wkv"], w["wkr"], w["wu"], w["wga"], w["wgb"], w["gq"], w["wuqn"],
               w["wuqr"], w["gqn"], w["gqr"], w["gkv"], w["gkr"], w["wuk"], w["wuv"], w["gkn"]]
    wide = N_HEADS * LANES
    out_shape = [
        jax.ShapeDtypeStruct((n, wide), BF16),
        jax.ShapeDtypeStruct((n, wide), BF16),
        jax.ShapeDtypeStruct((n, wide), BF16),
        jax.ShapeDtypeStruct((n, LANES), BF16),
        jax.ShapeDtypeStruct((n, wide), BF16),
        jax.ShapeDtypeStruct((n, KV_LORA), F32),
        jax.ShapeDtypeStruct((n, QK_ROPE), F32),
        jax.ShapeDtypeStruct((SSM_NJ, n, LANES), F32),
        jax.ShapeDtypeStruct((n, D_MODEL), F32),
        jax.ShapeDtypeStruct((n, D_MODEL), F32),
    ]
    out_specs = [
        pl.BlockSpec((tm, wide), row), pl.BlockSpec((tm, wide), row), pl.BlockSpec((tm, wide), row),
        pl.BlockSpec((tm, LANES), row), pl.BlockSpec((tm, wide), row), pl.BlockSpec((tm, KV_LORA), row),
        pl.BlockSpec((tm, QK_ROPE), row), pl.BlockSpec((SSM_NJ, tm, LANES), lambda i: (0, i, 0)),
        pl.BlockSpec((tm, D_MODEL), row), pl.BlockSpec((tm, D_MODEL), row),
    ]
    in_specs = ([pl.BlockSpec((tm, D_MODEL), row)] + [pl.BlockSpec((tm, LANES), tab)] * 3
                + [full(a) for a in weights])
    return pl.pallas_call(
        _front_kernel, grid=(nt,), in_specs=in_specs, out_specs=out_specs, out_shape=out_shape,
        compiler_params=_cp(("arbitrary",)), name="front",
    )(x, ct, s1, s2, *weights)


def _expand_kernel(c_ref, wuk_ref, wuv_ref, gkn_ref, kn_ref, v_ref):
    c16 = c_ref[...].astype(BF16)
    kn = _dot(c16, wuk_ref[...])
    for h in range(N_HEADS):
        sl = slice(h * LANES, (h + 1) * LANES)
        kn_ref[:, sl] = _rms(kn[:, sl], gkn_ref[...]).astype(BF16)
    v_ref[...] = _dot(c16, wuv_ref[...]).astype(BF16)


def _expand(c, w, tm):
    n = c.shape[0]
    wide = N_HEADS * LANES
    row = lambda i: (i, 0)
    const = lambda i: (0, 0)
    return pl.pallas_call(
        _expand_kernel, grid=(n // tm,),
        in_specs=[pl.BlockSpec((tm, KV_LORA), row), pl.BlockSpec(w["wuk"].shape, const),
                  pl.BlockSpec(w["wuv"].shape, const), pl.BlockSpec(w["gkn"].shape, const)],
        out_specs=[pl.BlockSpec((tm, wide), row), pl.BlockSpec((tm, wide), row)],
        out_shape=[jax.ShapeDtypeStruct((n, wide), BF16), jax.ShapeDtypeStruct((n, wide), BF16)],
        compiler_params=_cp(("arbitrary",)), name="expand",
    )(c, w["wuk"], w["wuv"], w["gkn"])


ATT_T = 256


def _softmax_step(s, v, m_ref, l_ref, acc_ref):
    m_prev = m_ref[...]
    m_new = jnp.maximum(m_prev, jnp.max(s, axis=-1, keepdims=True))
    alpha = jnp.exp(m_prev - m_new)
    p = jnp.exp(s - m_new)
    l_ref[...] = alpha * l_ref[...] + jnp.sum(p, axis=-1, keepdims=True)
    acc_ref[...] = alpha * acc_ref[...] + _dot(p.astype(BF16), v)
    m_ref[...] = m_new


def _attn_prompt_kernel(qn_ref, qr_ref, kn_ref, kr_ref, v_ref, mkn_ref, mkr_ref, mv_ref, o_ref,
                        m_ref, l_ref, acc_ref):
    i = pl.program_id(2)
    q = jnp.concatenate([qn_ref[...], qr_ref[...]], axis=1)

    km = jnp.concatenate([mkn_ref[...], mkr_ref[...]], axis=1)
    s = _dot_nt(q, km)
    m0 = jnp.max(s, axis=-1, keepdims=True)
    p = jnp.exp(s - m0)
    m_ref[...] = m0
    l_ref[...] = jnp.sum(p, axis=-1, keepdims=True)
    acc_ref[...] = _dot(p.astype(BF16), mv_ref[...])

    def tile_scores(kt):
        rows = pl.ds(pl.multiple_of(kt * ATT_T, ATT_T), ATT_T)
        k = jnp.concatenate([kn_ref[rows, :], kr_ref[rows, :]], axis=1)
        return _dot_nt(q, k), v_ref[rows, :]

    def body(kt, carry):
        s, v = tile_scores(kt)
        _softmax_step(s, v, m_ref, l_ref, acc_ref)
        return carry

    lax.fori_loop(0, i, body, 0)

    s, v = tile_scores(i)
    qc = lax.broadcasted_iota(jnp.int32, (ATT_T, ATT_T), 0) // CHUNK
    kc = lax.broadcasted_iota(jnp.int32, (ATT_T, ATT_T), 1) // CHUNK
    s = jnp.where(kc <= qc, s, NEG_INF)
    _softmax_step(s, v, m_ref, l_ref, acc_ref)
    o_ref[...] = (acc_ref[...] / l_ref[...]).astype(BF16)


def _attn_prompt(qn, qr, kn, krb, v, mkn, mkrb, mv, bsz, seq):
    nq = seq // ATT_T
    n = bsz * seq
    qmap = lambda b, h, i: (b * nq + i, h)
    return pl.pallas_call(
        _attn_prompt_kernel, grid=(bsz, N_HEADS, nq),
        in_specs=[pl.BlockSpec((ATT_T, LANES), qmap), pl.BlockSpec((ATT_T, LANES), qmap),
                  pl.BlockSpec((seq, LANES), lambda b, h, i: (b, h)),
                  pl.BlockSpec((seq, LANES), lambda b, h, i: (b, 0)),
                  pl.BlockSpec((seq, LANES), lambda b, h, i: (b, h)),
                  pl.BlockSpec((N_META, LANES), lambda b, h, i: (0, h)),
                  pl.BlockSpec((N_META, LANES), lambda b, h, i: (0, 0)),
                  pl.BlockSpec((N_META, LANES), lambda b, h, i: (0, h))],
        out_specs=pl.BlockSpec((ATT_T, LANES), qmap),
        out_shape=jax.ShapeDtypeStruct((n, N_HEADS * LANES), BF16),
        scratch_shapes=[pltpu.VMEM((ATT_T, 1), F32), pltpu.VMEM((ATT_T, 1), F32),
                        pltpu.VMEM((ATT_T, V_DIM), F32)],
        compiler_params=_cp(("arbitrary", "arbitrary", "arbitrary")), name="attn_prompt",
    )(qn, qr, kn, krb, v, mkn, mkrb, mv)


def _attn_sample_kernel(qn_ref, qr_ref, ckn_ref, ckr_ref, cv_ref, mkn_ref, mkr_ref, mv_ref,
                        nkn_ref, nkr_ref, nv_ref, o_ref):
    q = jnp.concatenate([qn_ref[...], qr_ref[...]], axis=1)
    s_c = _dot_nt(q, jnp.concatenate([ckn_ref[...], ckr_ref[...]], axis=1))
    s_m = _dot_nt(q, jnp.concatenate([mkn_ref[...], mkr_ref[...]], axis=1))
    s_n = _dot_nt(q, jnp.concatenate([nkn_ref[...], nkr_ref[...]], axis=1))
    m = jnp.maximum(jnp.max(s_c, axis=-1, keepdims=True),
                    jnp.maximum(jnp.max(s_m, axis=-1, keepdims=True), jnp.max(s_n, axis=-1, keepdims=True)))
    p_c, p_m, p_n = jnp.exp(s_c - m), jnp.exp(s_m - m), jnp.exp(s_n - m)
    l = (jnp.sum(p_c, axis=-1, keepdims=True) + jnp.sum(p_m, axis=-1, keepdims=True)
         + jnp.sum(p_n, axis=-1, keepdims=True))
    acc = (_dot(p_c.astype(BF16), cv_ref[...]) + _dot(p_m.astype(BF16), mv_ref[...])
           + _dot(p_n.astype(BF16), nv_ref[...]))
    o_ref[...] = (acc / l).astype(BF16)


def _attn_sample(qn, qr, ckn, ckrb, cv, mkn, mkrb, mv, nkn, nkrb, nv, bsz, seq, past):
    bh = lambda b, h: (b, h)
    b0 = lambda b, h: (b, 0)
    return pl.pallas_call(
        _attn_sample_kernel, grid=(bsz, N_HEADS),
        in_specs=[pl.BlockSpec((seq, LANES), bh), pl.BlockSpec((seq, LANES), bh),
                  pl.BlockSpec((past, LANES), bh), pl.BlockSpec((past, LANES), b0),
                  pl.BlockSpec((past, LANES), bh),
                  pl.BlockSpec((N_META, LANES), bh), pl.BlockSpec((N_META, LANES), b0),
                  pl.BlockSpec((N_META, LANES), bh),
                  pl.BlockSpec((seq, LANES), bh), pl.BlockSpec((seq, LANES), b0),
                  pl.BlockSpec((seq, LANES), bh)],
        out_specs=pl.BlockSpec((seq, LANES), bh),
        out_shape=jax.ShapeDtypeStruct((bsz * seq, N_HEADS * LANES), BF16),
        compiler_params=_cp(("arbitrary", "arbitrary")), name="attn_sample",
    )(qn, qr, ckn, ckrb, cv, mkn, mkrb, mv, nkn, nkrb, nv)


def _ssm_scan_rows(s_sc, hin, lam, nb, steps):
    lr = lam[:, :SSM_SW]
    li = lam[:, SSM_SW:]

    def step(r, h):
        s = jnp.concatenate([s_sc[b, pl.ds(r, 1), :] for b in range(nb)], axis=0)
        for b in range(nb):
            s_sc[b, pl.ds(r, 1), :] = h[b:b + 1, :]
        hre, him = h[:, :SSM_SW], h[:, SSM_SW:]
        nre = lr * hre - li * him + s[:, :SSM_SW]
        nim = lr * him + li * hre + s[:, SSM_SW:]
        return jnp.concatenate([nre, nim], axis=1)

    return lax.fori_loop(0, steps, step, hin)


def _ssm_prompt_kernel(u_ref, m_ref, p_ref, q_ref, lam_ref, d_ref, h0_ref, s_ref, hfin_ref, s_sc, h_sc,
                       *, nb, rt):
    t = pl.program_id(1)

    @pl.when(t == 0)
    def _():
        h_sc[...] = h0_ref[...]

    for b in range(nb):
        s_sc[b] = _dot(u_ref[b].astype(BF16), p_ref[...])
    h_sc[...] = _ssm_scan_rows(s_sc, h_sc[...], lam_ref[...], nb, rt)
    for b in range(nb):
        ub = u_ref[b]
        y = _dot(ub.astype(BF16), m_ref[...]) + _dot(s_sc[b].astype(BF16), q_ref[...]) + ub * d_ref[...]
        s_ref[b] = jax.nn.gelu(y).astype(BF16)

    @pl.when(t == pl.num_programs(1) - 1)
    def _():
        hfin_ref[...] = h_sc[...]


def _ssm_prompt(u, sw, h0, nb, lc, rt):
    tw = SSM_T * LANES
    u4 = u.reshape(SSM_NJ, nb, lc, tw)
    sq = 2 * SSM_SW
    kern = functools.partial(_ssm_prompt_kernel, nb=nb, rt=rt)
    s, hfin = pl.pallas_call(
        kern, grid=(SSM_NJ, lc // rt),
        in_specs=[pl.BlockSpec((None, nb, rt, tw), lambda j, t: (j, 0, t, 0)),
                  pl.BlockSpec((None, tw, tw), lambda j, t: (j, 0, 0)),
                  pl.BlockSpec((None, tw, sq), lambda j, t: (j, 0, 0)),
                  pl.BlockSpec((None, sq, tw), lambda j, t: (j, 0, 0)),
                  pl.BlockSpec((None, 1, sq), lambda j, t: (j, 0, 0)),
                  pl.BlockSpec((None, 1, tw), lambda j, t: (j, 0, 0)),
                  pl.BlockSpec((None, nb, sq), lambda j, t: (j, 0, 0))],
        out_specs=[pl.BlockSpec((None, nb, rt, tw), lambda j, t: (j, 0, t, 0)),
                   pl.BlockSpec((None, nb, sq), lambda j, t: (j, 0, 0))],
        out_shape=[jax.ShapeDtypeStruct((SSM_NJ, nb, lc, tw), BF16),
                   jax.ShapeDtypeStruct((SSM_NJ, nb, sq), F32)],
        scratch_shapes=[pltpu.VMEM((nb, rt, sq), F32), pltpu.VMEM((nb, sq), F32)],
        compiler_params=_cp(("arbitrary", "arbitrary")), name="ssm_prompt",
    )(u4, sw["M"], sw["P"], sw["Q"], sw["lam"], sw["D"], h0)
    return s.reshape(SSM_NJ, nb * lc * SSM_T, LANES), hfin


def _ssm_sample_kernel(u_ref, m_ref, p_ref, q_ref, lam_ref, d_ref, h0_ref, s_ref, hfin_ref, s_sc,
                       *, nb, lc):
    u = u_ref[...]
    u16 = u.astype(BF16)
    s_sc[...] = _dot(u16, p_ref[...])
    lam = lam_ref[...]
    lr, li = lam[:, :SSM_SW], lam[:, SSM_SW:]
    h = h0_ref[...]
    for r in range(lc):
        rows = slice(r * nb, (r + 1) * nb)
        s = s_sc[rows, :]
        s_sc[rows, :] = h
        hre, him = h[:, :SSM_SW], h[:, SSM_SW:]
        h = jnp.concatenate([lr * hre - li * him + s[:, :SSM_SW], lr * him + li * hre + s[:, SSM_SW:]], axis=1)
    hfin_ref[...] = h
    y = _dot(u16, m_ref[...]) + _dot(s_sc[...].astype(BF16), q_ref[...]) + u * d_ref[...]
    s_ref[...] = jax.nn.gelu(y).astype(BF16)


def _ssm_sample(u, sw, h0, nb, lc):
    tw = SSM_T * LANES
    sq = 2 * SSM_SW
    u2 = u.reshape(SSM_NJ, nb, lc, tw).transpose(0, 2, 1, 3).reshape(SSM_NJ, lc * nb, tw)
    kern = functools.partial(_ssm_sample_kernel, nb=nb, lc=lc)
    blk = lambda j: (j, 0, 0)
    s, hfin = pl.pallas_call(
        kern, grid=(SSM_NJ,),
        in_specs=[pl.BlockSpec((None, lc * nb, tw), blk), pl.BlockSpec((None, tw, tw), blk),
                  pl.BlockSpec((None, tw, sq), blk), pl.BlockSpec((None, sq, tw), blk),
                  pl.BlockSpec((None, 1, sq), blk), pl.BlockSpec((None, 1, tw), blk),
                  pl.BlockSpec((None, nb, sq), blk)],
        out_specs=[pl.BlockSpec((None, lc * nb, tw), blk), pl.BlockSpec((None, nb, sq), blk)],
        out_shape=[jax.ShapeDtypeStruct((SSM_NJ, lc * nb, tw), BF16),
                   jax.ShapeDtypeStruct((SSM_NJ, nb, sq), F32)],
        scratch_shapes=[pltpu.VMEM((lc * nb, sq), F32)],
        compiler_params=_cp(("arbitrary",)), name="ssm_sample",
    )(u2, sw["M"], sw["P"], sw["Q"], sw["lam"], sw["D"], h0)
    s = s.reshape(SSM_NJ, lc, nb, tw).transpose(0, 2, 1, 3).reshape(SSM_NJ, nb * lc * SSM_T, LANES)
    return s, hfin


def _ssm_weights(a_re, a_im, log_dt, b_re, b_im, c_re, c_im, d):
    hi = lax.Precision.HIGHEST
    t_ = SSM_T
    dt = jnp.exp(log_dt)[:, None]
    mag = jnp.exp(a_re * dt)
    lam_re, lam_im = mag * jnp.cos(a_im * dt), mag * jnp.sin(a_im * dt)
    den = a_re * a_re + a_im * a_im
    f_re = ((lam_re - 1.0) * a_re + lam_im * a_im) / den
    f_im = (lam_im * a_re - (lam_re - 1.0) * a_im) / den
    w_re = f_re[:, :, None] * b_re - f_im[:, :, None] * b_im
    w_im = f_re[:, :, None] * b_im + f_im[:, :, None] * b_re
    k = jnp.arange(t_ + 1, dtype=F32)[None, :, None]
    pmag = jnp.exp(a_re[:, None, :] * dt[:, :, None] * k)
    pang = a_im[:, None, :] * dt[:, :, None] * k
    pw_re, pw_im = pmag * jnp.cos(pang), pmag * jnp.sin(pang)
    v_re = pw_re[:, :t_, :, None] * w_re[:, None] - pw_im[:, :t_, :, None] * w_im[:, None]
    v_im = pw_re[:, :t_, :, None] * w_im[:, None] + pw_im[:, :t_, :, None] * w_re[:, None]
    kmat = (jnp.einsum("gcn,gjnd->gjcd", c_re, v_re, precision=hi)
            - jnp.einsum("gcn,gjnd->gjcd", c_im, v_im, precision=hi))
    eye = jnp.eye(SSM_GPB, dtype=F32)
    tw = t_ * LANES
    si = jnp.arange(t_)[:, None]
    ti = jnp.arange(t_)[None, :]
    lag = jnp.maximum(ti - si, 0)
    kl = kmat[:, lag] * (ti >= si).astype(F32)[None, :, :, None, None]
    kl = kl.reshape(SSM_NJ, SSM_GPB, t_, t_, SSM_GROUP, SSM_GROUP)
    m = jnp.einsum("jgstcd,gh->jsgdthc", kl, eye).reshape(SSM_NJ, tw, tw)
    p_c = jnp.stack([v_re[:, ::-1], v_im[:, ::-1]])
    p_c = p_c.reshape(2, SSM_NJ, SSM_GPB, t_, SSM_STATE, SSM_GROUP)
    p = jnp.einsum("rjgsnd,gh->jsgdrhn", p_c, eye).reshape(SSM_NJ, tw, 2 * SSM_SW)
    l_re, l_im = pw_re[:, 1:, None, :], pw_im[:, 1:, None, :]
    q_re = c_re[:, None] * l_re - c_im[:, None] * l_im
    q_im = -(c_re[:, None] * l_im + c_im[:, None] * l_re)
    q_c = jnp.stack([q_re, q_im]).reshape(2, SSM_NJ, SSM_GPB, t_, SSM_GROUP, SSM_STATE)
    q = jnp.einsum("rjgtcn,gh->jrgnthc", q_c, eye).reshape(SSM_NJ, 2 * SSM_SW, tw)
    lam_t = jnp.concatenate([pw_re[:, t_].reshape(SSM_NJ, 1, SSM_SW), pw_im[:, t_].reshape(SSM_NJ, 1, SSM_SW)],
                            axis=2)
    dvec = jnp.tile(d.reshape(SSM_NJ, 1, LANES), (1, 1, t_))
    return {"M": m.astype(BF16), "P": p.astype(BF16), "Q": q.astype(BF16), "lam": lam_t, "D": dvec}


def _mixer_kernel(o_ref, s_ref, ga_ref, gb_ref, x_ref, wo_ref, wv_ref, wg_ref, wout_ref, h_ref):
    o_a = _dot(o_ref[...], wo_ref[...])
    s = jnp.concatenate([s_ref[j] for j in range(SSM_NJ)], axis=1)
    o_b = _dot(s, wv_ref[...]) * jax.nn.sigmoid(_dot(s, wg_ref[...]))
    merged = ga_ref[...] * o_a + gb_ref[...] * o_b
    h_ref[...] = x_ref[...] + _dot(merged.astype(BF16), wout_ref[...])


def _mixer(o, s, ga, gb, x, w, tm):
    n = x.shape[0]
    row = lambda i: (i, 0)
    const = lambda i: (0, 0)
    wspec = pl.BlockSpec((D_MODEL, D_MODEL), const)
    act = pl.BlockSpec((tm, D_MODEL), row)
    return pl.pallas_call(
        _mixer_kernel, grid=(n // tm,),
        in_specs=[act, pl.BlockSpec((SSM_NJ, tm, LANES), lambda i: (0, i, 0)), act, act, act,
                  wspec, wspec, wspec, wspec],
        out_specs=act, out_shape=jax.ShapeDtypeStruct((n, D_MODEL), F32),
        compiler_params=_cp(("arbitrary",)), name="mixer_out",
    )(o, s, ga, gb, x, w["wo"], w["wv"], w["wg"], w["wout"])


def _mlp_kernel(h_ref, g_ref, wup_ref, wdn_ref, y_ref):
    h = h_ref[...]
    hn = _rms(h, g_ref[...]).astype(BF16)
    a = jnp.maximum(_dot(hn, wup_ref[...]), 0.0)
    y_ref[...] = h + _dot((a * a).astype(BF16), wdn_ref[...])


def _mlp(h, w, tm):
    n = h.shape[0]
    row = lambda i: (i, 0)
    const = lambda i: (0, 0)
    act = pl.BlockSpec((tm, D_MODEL), row)
    return pl.pallas_call(
        _mlp_kernel, grid=(n // tm,),
        in_specs=[act, pl.BlockSpec((1, D_MODEL), const), pl.BlockSpec((D_MODEL, D_FF), const),
                  pl.BlockSpec((D_FF, D_MODEL), const)],
        out_specs=act, out_shape=jax.ShapeDtypeStruct((n, D_MODEL), F32),
        compiler_params=_cp(("arbitrary",)), name="mlp",
    )(h, w["gmlp"], w["wup"], w["wdn"])


def _rope_tables(pos):
    half = QK_ROPE // 2
    inv = ROPE_THETA ** (-jnp.arange(half, dtype=F32) / half)
    ang = pos.astype(F32)[:, None] * inv[None, :]
    cos, sin = jnp.cos(ang), jnp.sin(ang)
    z = jnp.zeros_like(cos)
    pad = jnp.zeros((pos.shape[0], LANES - QK_ROPE), F32)
    ct = jnp.concatenate([cos, cos, pad], axis=1)
    s1 = jnp.concatenate([-sin, z, pad], axis=1)
    s2 = jnp.concatenate([z, sin, pad], axis=1)
    return ct, s1, s2


def _pad_lanes(a, width):
    return jnp.pad(a, [(0, 0)] * (a.ndim - 1) + [(0, width - a.shape[-1])])


def _layer_weights(norm_mix, w_in, q_lora_norm, w_uq, q_nope_norm, q_rope_norm, kv_lora_norm, k_rope_norm,
                   w_ukv, k_nope_norm, w_o_attn, w_glu_v, w_glu_g, w_out, norm_mlp, w_mlp_up, w_mlp_down):
    o_kv = Q_LORA
    o_kr = o_kv + KV_LORA
    o_ssm = o_kr + QK_ROPE
    o_ga = o_ssm + D_MODEL
    o_gb = o_ga + D_MODEL
    bf = lambda a: a.astype(BF16)
    r2 = lambda a: a.reshape(1, -1).astype(F32)
    return {
        "nmix": r2(norm_mix),
        "wq": bf(w_in[:, :o_kv]), "wkv": bf(w_in[:, o_kv:o_kr]),
        "wkr": bf(_pad_lanes(w_in[:, o_kr:o_ssm], LANES)),
        "wu": bf(w_in[:, o_ssm:o_ga]), "wga": bf(w_in[:, o_ga:o_gb]), "wgb": bf(w_in[:, o_gb:]),
        "gq": r2(q_lora_norm),
        "wuqn": bf(w_uq[:, :, :QK_NOPE].reshape(Q_LORA, N_HEADS * QK_NOPE)),
        "wuqr": bf(_pad_lanes(w_uq[:, :, QK_NOPE:], LANES).reshape(Q_LORA, N_HEADS * LANES)),
        "gqn": r2(q_nope_norm), "gqr": _pad_lanes(r2(q_rope_norm), LANES),
        "gkv": r2(kv_lora_norm), "gkr": _pad_lanes(r2(k_rope_norm), LANES),
        "wuk": bf(w_ukv[:, :, :QK_NOPE].reshape(KV_LORA, N_HEADS * QK_NOPE)),
        "wuv": bf(w_ukv[:, :, QK_NOPE:].reshape(KV_LORA, N_HEADS * V_DIM)),
        "gkn": r2(k_nope_norm),
        "wo": bf(w_o_attn.reshape(N_HEADS * V_DIM, D_MODEL)),
        "wv": bf(w_glu_v), "wg": bf(w_glu_g), "wout": bf(w_out),
        "gmlp": r2(norm_mlp), "wup": bf(w_mlp_up), "wdn": bf(w_mlp_down),
    }


def kernel(x_prompt, x_sample, cache_latent, cache_krope, cache_meta_latent, cache_meta_krope, state_ssm_re, state_ssm_im, meta_tokens, norm_mix, w_in, q_lora_norm, w_uq, q_nope_norm, q_rope_norm, kv_lora_norm, k_rope_norm, w_ukv, k_nope_norm, w_o_attn, ssm_a_re, ssm_a_im, ssm_log_dt, ssm_b_re, ssm_b_im, ssm_c_re, ssm_c_im, ssm_d, w_glu_v, w_glu_g, w_out, norm_mlp, w_mlp_up, w_mlp_down):
    bsz_p, seq_p = x_prompt.shape[0], x_prompt.shape[1]
    bsz_s, seq_s = x_sample.shape[0], x_sample.shape[1]
    past = cache_latent.shape[2]
    depth = w_in.shape[0]
    assert depth == 1, "single-layer step"
    l = 0
    w = _layer_weights(norm_mix[l], w_in[l], q_lora_norm[l], w_uq[l], q_nope_norm[l], q_rope_norm[l],
                       kv_lora_norm[l], k_rope_norm[l], w_ukv[l], k_nope_norm[l], w_o_attn[l], w_glu_v[l],
                       w_glu_g[l], w_out[l], norm_mlp[l], w_mlp_up[l], w_mlp_down[l])
    sw = _ssm_weights(ssm_a_re[l].astype(F32), ssm_a_im[l].astype(F32), ssm_log_dt[l].astype(F32),
                      ssm_b_re[l].astype(F32), ssm_b_im[l].astype(F32), ssm_c_re[l].astype(F32),
                      ssm_c_im[l].astype(F32), ssm_d[l].astype(F32))
    sq = 2 * SSM_SW

    tabs_m = _rope_tables(jnp.arange(N_META, dtype=jnp.int32) - N_META)
    (_, _, mkn, mkrb, mv, mckv, mkr, mu, _, _) = _front(meta_tokens.astype(F32), tabs_m, 1, w, N_META)
    h0_zero = jnp.zeros((SSM_NJ, 1, sq), F32)
    _, h_meta = _ssm_prompt(mu, sw, h0_zero, 1, N_META // SSM_T, N_META // SSM_T)

    n_p = bsz_p * seq_p
    tm = 256
    xp = x_prompt.reshape(n_p, D_MODEL)
    tabs_p = _rope_tables(jnp.arange(seq_p, dtype=jnp.int32))
    qn, qr, kn, krb, v, ckv_p, kr_p, u_p, ga, gb = _front(xp, tabs_p, seq_p // tm, w, tm)
    o_p = _attn_prompt(qn, qr, kn, krb, v, mkn, mkrb, mv, bsz_p, seq_p)
    h0_p = jnp.broadcast_to(h_meta, (SSM_NJ, bsz_p, sq))
    s_p, hfin_p = _ssm_prompt(u_p, sw, h0_p, bsz_p, seq_p // SSM_T, 128)
    h1_p = _mixer(o_p, s_p, ga, gb, xp, w, tm)
    y_p = _mlp(h1_p, w, tm)

    n_s = bsz_s * seq_s
    xs = x_sample.reshape(n_s, D_MODEL)
    pos_s = past + jnp.arange(seq_s, dtype=jnp.int32)
    tabs_s = tuple(jnp.tile(t, (bsz_s, 1)) for t in _rope_tables(pos_s))
    tm_s = 256
    qn_s, qr_s, kn_s, krb_s, v_s, ckv_s, kr_s, u_s, ga_s, gb_s = _front(xs, tabs_s, n_s // tm_s, w, tm_s)
    ckn, cv = _expand(cache_latent[l].reshape(bsz_s * past, KV_LORA).astype(F32), w, 512)
    cmkn, cmv = _expand(cache_meta_latent[l].reshape(bsz_s * N_META, KV_LORA).astype(F32), w, bsz_s * N_META)
    ckrb = _pad_lanes(cache_krope[l].reshape(bsz_s * past, QK_ROPE), LANES).astype(BF16)
    cmkrb = _pad_lanes(cache_meta_krope[l].reshape(bsz_s * N_META, QK_ROPE), LANES).astype(BF16)
    o_s = _attn_sample(qn_s, qr_s, ckn, ckrb, cv, cmkn, cmkrb, cmv, kn_s, krb_s, v_s, bsz_s, seq_s, past)

    def to_blocks(st):
        return st.astype(F32).reshape(bsz_s, SSM_NJ, SSM_SW).transpose(1, 0, 2)

    h0_s = jnp.concatenate([to_blocks(state_ssm_re[l]), to_blocks(state_ssm_im[l])], axis=2)
    s_s, hfin_s = _ssm_sample(u_s, sw, h0_s, bsz_s, seq_s // SSM_T)
    h1_s = _mixer(o_s, s_s, ga_s, gb_s, xs, w, tm_s)
    y_s = _mlp(h1_s, w, tm_s)

    def from_blocks(hf, nb):
        re = hf[:, :, :SSM_SW].transpose(1, 0, 2).reshape(1, nb, N_GROUPS, SSM_STATE)
        im = hf[:, :, SSM_SW:].transpose(1, 0, 2).reshape(1, nb, N_GROUPS, SSM_STATE)
        return re, im

    sre_p, sim_p = from_blocks(hfin_p, bsz_p)
    sre_s, sim_s = from_blocks(hfin_s, bsz_s)
    return (y_p.reshape(bsz_p, seq_p, D_MODEL), y_s.reshape(bsz_s, seq_s, D_MODEL),
            ckv_p.reshape(1, bsz_p, seq_p, KV_LORA), kr_p.reshape(1, bsz_p, seq_p, QK_ROPE),
            jnp.broadcast_to(mckv[None, None], (1, bsz_p, N_META, KV_LORA)),
            jnp.broadcast_to(mkr[None, None], (1, bsz_p, N_META, QK_ROPE)),
            sre_p, sim_p,
            ckv_s.reshape(1, bsz_s, seq_s, KV_LORA), kr_s.reshape(1, bsz_s, seq_s, QK_ROPE),
            sre_s, sim_s)
```

```python
import functools
import math

import jax
import jax.numpy as jnp
from jax import lax
from jax.experimental import pallas as pl
from jax.experimental.pallas import tpu as pltpu

F32 = jnp.float32
BF16 = jnp.bfloat16

D_MODEL = 1024
CHUNK = 64
N_META = 16
N_HEADS = 8
QK_NOPE = 128
QK_ROPE = 64
V_DIM = 128
QK_DIM = QK_NOPE + QK_ROPE
Q_LORA = 384
KV_LORA = 256
SSM_GROUP = 16
N_GROUPS = D_MODEL // SSM_GROUP
SSM_STATE = 64
D_FF = 4 * D_MODEL
ROPE_THETA = 10000.0
EPS = 1e-6
ATTN_SCALE = QK_DIM ** -0.5
NEG_INF = -1e30

LANES = 128
HEAD_PAD = 2 * LANES
SSM_T = 8
SSM_NJ = D_MODEL // LANES
SSM_GPB = LANES // SSM_GROUP
SSM_SW = SSM_GPB * SSM_STATE
VMEM_LIMIT = 56 * 1024 * 1024


def _cp(sem):
    return pltpu.CompilerParams(dimension_semantics=sem, vmem_limit_bytes=VMEM_LIMIT)


def _dot(a, b):
    return jnp.dot(a, b, preferred_element_type=F32)


def _dot_nt(a, b):
    return lax.dot_general(a, b, (((1,), (1,)), ((), ())), preferred_element_type=F32)


def _rms(x, g, n=None):
    n = x.shape[-1] if n is None else n
    ms = jnp.sum(x * x, axis=-1, keepdims=True) * (1.0 / n)
    return x * lax.rsqrt(ms + EPS) * g


def _rope128(b, ct, s1, s2):
    return b * ct + pltpu.roll(b, LANES - QK_ROPE // 2, 1) * s1 + pltpu.roll(b, QK_ROPE // 2, 1) * s2


def _front_kernel(x_ref, ct_ref, s1_ref, s2_ref, nmix_ref, wq_ref, wkv_ref, wkr_ref, wu_ref, wga_ref,
                  wgb_ref, gq_ref, wuqn_ref, wuqr_ref, gqn_ref, gqr_ref, gkv_ref, gkr_ref, wuk_ref,
                  wuv_ref, gkn_ref,
                  q_ref, k_ref, v_ref, ckv_ref, kr_ref, u_ref, ga_ref, gb_ref, u_sc):
    x = x_ref[...]
    xn = _rms(x, nmix_ref[...]).astype(BF16)
    ct, s1, s2 = ct_ref[...], s1_ref[...], s2_ref[...]

    u = _dot(xn, wu_ref[...])
    nchunk = u_sc.shape[1] // SSM_T
    for j in range(SSM_NJ):
        u_sc[j] = u[:, j * LANES:(j + 1) * LANES]
        for t in range(SSM_T):
            u_ref[j, :, t * LANES:(t + 1) * LANES] = u_sc[j, pl.ds(t, nchunk, stride=SSM_T), :]
    ga_ref[...] = jax.nn.sigmoid(_dot(xn, wga_ref[...]))
    gb_ref[...] = jax.nn.sigmoid(_dot(xn, wgb_ref[...]))

    c_kv = _rms(_dot(xn, wkv_ref[...]), gkv_ref[...])
    ckv_ref[...] = c_kv
    kr = _rms(_dot(xn, wkr_ref[...]), gkr_ref[...], QK_ROPE)
    kr = _rope128(kr, ct, s1, s2)
    kr_ref[...] = kr[:, :QK_ROPE]
    kr16 = kr.astype(BF16)
    c16 = c_kv.astype(BF16)
    kn = _dot(c16, wuk_ref[...])
    for h in range(N_HEADS):
        sl = slice(h * LANES, (h + 1) * LANES)
        k_ref[:, h * HEAD_PAD:h * HEAD_PAD + LANES] = _rms(kn[:, sl], gkn_ref[...]).astype(BF16)
        k_ref[:, h * HEAD_PAD + LANES:(h + 1) * HEAD_PAD] = kr16
    v_ref[...] = _dot(c16, wuv_ref[...]).astype(BF16)

    c_q = _rms(_dot(xn, wq_ref[...]), gq_ref[...]).astype(BF16)
    qn = _dot(c_q, wuqn_ref[...])
    qr = _dot(c_q, wuqr_ref[...])
    for h in range(N_HEADS):
        sl = slice(h * LANES, (h + 1) * LANES)
        q_ref[:, h * HEAD_PAD:h * HEAD_PAD + LANES] = (_rms(qn[:, sl], gqn_ref[...]) * ATTN_SCALE).astype(BF16)
        r = _rope128(_rms(qr[:, sl], gqr_ref[...], QK_ROPE), ct, s1, s2)
        q_ref[:, h * HEAD_PAD + LANES:(h + 1) * HEAD_PAD] = (r * ATTN_SCALE).astype(BF16)


def _front(x, tabs, tab_blocks, w, tm):
    n = x.shape[0]
    nt = n // tm
    ct, s1, s2 = tabs

    def row(i):
        return (i, 0)

    def tab(i):
        return (i % tab_blocks, 0)

    def const(i):
        return (0, 0)

    def full(a):
        return pl.BlockSpec(a.shape, const)

    weights = [w["nmix"], w["wq"], w["wkv"], w["wkr"], w["wu"], w["wga"], w["wgb"], w["gq"], w["wuqn"],
               w["wuqr"], w["gqn"], w["gqr"], w["gkv"], w["gkr"], w["wuk"], w["wuv"], w["gkn"]]
    wide = N_HEADS * LANES
    qk_wide = N_HEADS * HEAD_PAD
    out_shape = [
        jax.ShapeDtypeStruct((n, qk_wide), BF16),
        jax.ShapeDtypeStruct((n, qk_wide), BF16),
        jax.ShapeDtypeStruct((n, wide), BF16),
        jax.ShapeDtypeStruct((n, KV_LORA), F32),
        jax.ShapeDtypeStruct((n, QK_ROPE), F32),
        jax.ShapeDtypeStruct((SSM_NJ, n // SSM_T, SSM_T * LANES), F32),
        jax.ShapeDtypeStruct((n, D_MODEL), F32),
        jax.ShapeDtypeStruct((n, D_MODEL), F32),
    ]
    out_specs = [
        pl.BlockSpec((tm, qk_wide), row), pl.BlockSpec((tm, qk_wide), row),
        pl.BlockSpec((tm, wide), row), pl.BlockSpec((tm, KV_LORA), row),
        pl.BlockSpec((tm, QK_ROPE), row),
        pl.BlockSpec((SSM_NJ, tm // SSM_T, SSM_T * LANES), lambda i: (0, i, 0)),
        pl.BlockSpec((tm, D_MODEL), row), pl.BlockSpec((tm, D_MODEL), row),
    ]
    in_specs = ([pl.BlockSpec((tm, D_MODEL), row)] + [pl.BlockSpec((tm, LANES), tab)] * 3
                + [full(a) for a in weights])
    return pl.pallas_call(
        _front_kernel, grid=(nt,), in_specs=in_specs, out_specs=out_specs, out_shape=out_shape,
        scratch_shapes=[pltpu.VMEM((SSM_NJ, tm, LANES), F32)],
        compiler_params=_cp(("arbitrary",)), name="front",
    )(x, ct, s1, s2, *weights)


def _expand_kernel(c_ref, kr_ref, wuk_ref, wuv_ref, gkn_ref, k_ref, v_ref):
    c16 = c_ref[...].astype(BF16)
    kr16 = kr_ref[...]
    kn = _dot(c16, wuk_ref[...])
    for h in range(N_HEADS):
        sl = slice(h * LANES, (h + 1) * LANES)
        k_ref[:, h * HEAD_PAD:h * HEAD_PAD + LANES] = _rms(kn[:, sl], gkn_ref[...]).astype(BF16)
        k_ref[:, h * HEAD_PAD + LANES:(h + 1) * HEAD_PAD] = kr16
    v_ref[...] = _dot(c16, wuv_ref[...]).astype(BF16)


def _expand(c, krb, w, tm):
    n = c.shape[0]
    wide = N_HEADS * LANES
    qk_wide = N_HEADS * HEAD_PAD
    row = lambda i: (i, 0)
    const = lambda i: (0, 0)
    return pl.pallas_call(
        _expand_kernel, grid=(n // tm,),
        in_specs=[pl.BlockSpec((tm, KV_LORA), row), pl.BlockSpec((tm, LANES), row),
                  pl.BlockSpec(w["wuk"].shape, const),
                  pl.BlockSpec(w["wuv"].shape, const), pl.BlockSpec(w["gkn"].shape, const)],
        out_specs=[pl.BlockSpec((tm, qk_wide), row), pl.BlockSpec((tm, wide), row)],
        out_shape=[jax.ShapeDtypeStruct((n, qk_wide), BF16), jax.ShapeDtypeStruct((n, wide), BF16)],
        compiler_params=_cp(("arbitrary",)), name="expand",
    )(c, krb, w["wuk"], w["wuv"], w["gkn"])


ATT_T = 512


def _softmax_step(s, v, m_ref, l_ref, acc_ref):
    m_prev = m_ref[...]
    m_new = jnp.maximum(m_prev, jnp.max(s, axis=-1, keepdims=True))
    alpha = jnp.exp(m_prev - m_new)
    p = jnp.exp(s - m_new)
    l_ref[...] = alpha * l_ref[...] + jnp.sum(p, axis=-1, keepdims=True)
    acc_ref[...] = alpha * acc_ref[...] + _dot(p.astype(BF16), v)
    m_ref[...] = m_new


def _attn_prompt_kernel(q_ref, k_ref, v_ref, mk_ref, mv_ref, o_ref, a_sc, b_sc, m_ref, l_ref, acc_ref):
    i = pl.program_id(2)

    def key_rows(kt):
        return pl.ds(pl.multiple_of(kt * ATT_T, ATT_T), ATT_T)

    def scores(kt):
        return _dot_nt(q_ref[...], k_ref[key_rows(kt), :])

    def consume(buf, kt, diagonal):
        s = buf[...]
        if diagonal:
            qc = lax.broadcasted_iota(jnp.int32, (ATT_T, ATT_T), 0) // CHUNK
            kc = lax.broadcasted_iota(jnp.int32, (ATT_T, ATT_T), 1) // CHUNK
            s = jnp.where(kc <= qc, s, NEG_INF)
        _softmax_step(s, v_ref[key_rows(kt), :], m_ref, l_ref, acc_ref)

    a_sc[...] = scores(0)

    s = _dot_nt(q_ref[...], mk_ref[...])
    m0 = jnp.max(s, axis=-1, keepdims=True)
    p = jnp.exp(s - m0)
    m_ref[...] = m0
    l_ref[...] = jnp.sum(p, axis=-1, keepdims=True)
    acc_ref[...] = _dot(p.astype(BF16), mv_ref[...])

    def pair(pi, carry):
        kt = 2 * pi
        b_sc[...] = scores(kt + 1)
        consume(a_sc, kt, False)
        a_sc[...] = scores(kt + 2)
        consume(b_sc, kt + 1, False)
        return carry

    lax.fori_loop(0, lax.shift_right_logical(i, 1), pair, 0)

    @pl.when((i & 1) == 1)
    def _():
        b_sc[...] = scores(i)
        consume(a_sc, i - 1, False)
        consume(b_sc, i, True)

    @pl.when((i & 1) == 0)
    def _():
        consume(a_sc, i, True)

    o_ref[...] = (acc_ref[...] / l_ref[...]).astype(BF16)


def _attn_prompt(q, k, v, mk, mv, bsz, seq):
    nq = seq // ATT_T
    n = bsz * seq
    qmap = lambda b, h, i: (b * nq + i, h)
    return pl.pallas_call(
        _attn_prompt_kernel, grid=(bsz, N_HEADS, nq),
        in_specs=[pl.BlockSpec((ATT_T, HEAD_PAD), qmap),
                  pl.BlockSpec((seq, HEAD_PAD), lambda b, h, i: (b, h)),
                  pl.BlockSpec((seq, V_DIM), lambda b, h, i: (b, h)),
                  pl.BlockSpec((N_META, HEAD_PAD), lambda b, h, i: (0, h)),
                  pl.BlockSpec((N_META, V_DIM), lambda b, h, i: (0, h))],
        out_specs=pl.BlockSpec((ATT_T, V_DIM), qmap),
        out_shape=jax.ShapeDtypeStruct((n, N_HEADS * V_DIM), BF16),
        scratch_shapes=[pltpu.VMEM((ATT_T, ATT_T), F32), pltpu.VMEM((ATT_T, ATT_T), F32),
                        pltpu.VMEM((ATT_T, 1), F32),
                        pltpu.VMEM((ATT_T, 1), F32), pltpu.VMEM((ATT_T, V_DIM), F32)],
        compiler_params=_cp(("arbitrary", "arbitrary", "arbitrary")), name="attn_prompt",
    )(q, k, v, mk, mv)


def _attn_sample_kernel(q_ref, ck_ref, cv_ref, mk_ref, mv_ref, nk_ref, nv_ref, o_ref):
    q = q_ref[...]
    s_c = _dot_nt(q, ck_ref[...])
    s_m = _dot_nt(q, mk_ref[...])
    s_n = _dot_nt(q, nk_ref[...])
    m = jnp.maximum(jnp.max(s_c, axis=-1, keepdims=True),
                    jnp.maximum(jnp.max(s_m, axis=-1, keepdims=True), jnp.max(s_n, axis=-1, keepdims=True)))
    p_c, p_m, p_n = jnp.exp(s_c - m), jnp.exp(s_m - m), jnp.exp(s_n - m)
    l = (jnp.sum(p_c, axis=-1, keepdims=True) + jnp.sum(p_m, axis=-1, keepdims=True)
         + jnp.sum(p_n, axis=-1, keepdims=True))
    acc = (_dot(p_c.astype(BF16), cv_ref[...]) + _dot(p_m.astype(BF16), mv_ref[...])
           + _dot(p_n.astype(BF16), nv_ref[...]))
    o_ref[...] = (acc / l).astype(BF16)


def _attn_sample(q, ck, cv, mk, mv, nk, nv, bsz, seq, past):
    bh = lambda b, h: (b, h)
    return pl.pallas_call(
        _attn_sample_kernel, grid=(bsz, N_HEADS),
        in_specs=[pl.BlockSpec((seq, HEAD_PAD), bh),
                  pl.BlockSpec((past, HEAD_PAD), bh), pl.BlockSpec((past, V_DIM), bh),
                  pl.BlockSpec((N_META, HEAD_PAD), bh), pl.BlockSpec((N_META, V_DIM), bh),
                  pl.BlockSpec((seq, HEAD_PAD), bh), pl.BlockSpec((seq, V_DIM), bh)],
        out_specs=pl.BlockSpec((seq, V_DIM), bh),
        out_shape=jax.ShapeDtypeStruct((bsz * seq, N_HEADS * V_DIM), BF16),
        compiler_params=_cp(("arbitrary", "arbitrary")), name="attn_sample",
    )(q, ck, cv, mk, mv, nk, nv)


def _ssm_scan_rows(s_sc, hin, lam, nb, steps):
    lr = lam[:, :SSM_SW]
    li = lam[:, SSM_SW:]

    def step(r, h):
        s = jnp.concatenate([s_sc[b, pl.ds(r, 1), :] for b in range(nb)], axis=0)
        for b in range(nb):
            s_sc[b, pl.ds(r, 1), :] = h[b:b + 1, :]
        hre, him = h[:, :SSM_SW], h[:, SSM_SW:]
        nre = lr * hre - li * him + s[:, :SSM_SW]
        nim = lr * him + li * hre + s[:, SSM_SW:]
        return jnp.concatenate([nre, nim], axis=1)

    return lax.fori_loop(0, steps, step, hin)


def _ssm_prompt_kernel(u_ref, m_ref, p_ref, q_ref, lam_ref, d_ref, h0_ref, s_ref, hfin_ref, s_sc, h_sc,
                       *, nb, rt):
    t = pl.program_id(1)

    @pl.when(t == 0)
    def _():
        h_sc[...] = h0_ref[...]

    for b in range(nb):
        s_sc[b] = _dot(u_ref[b].astype(BF16), p_ref[...])
    h_sc[...] = _ssm_scan_rows(s_sc, h_sc[...], lam_ref[...], nb, rt)
    for b in range(nb):
        ub = u_ref[b]
        y = _dot(ub.astype(BF16), m_ref[...]) + _dot(s_sc[b].astype(BF16), q_ref[...]) + ub * d_ref[...]
        s_ref[b] = jax.nn.gelu(y).astype(BF16)

    @pl.when(t == pl.num_programs(1) - 1)
    def _():
        hfin_ref[...] = h_sc[...]


def _ssm_prompt(u, sw, h0, nb, lc, rt):
    tw = SSM_T * LANES
    u4 = u.reshape(SSM_NJ, nb, lc, tw)
    sq = 2 * SSM_SW
    kern = functools.partial(_ssm_prompt_kernel, nb=nb, rt=rt)
    s, hfin = pl.pallas_call(
        kern, grid=(SSM_NJ, lc // rt),
        in_specs=[pl.BlockSpec((None, nb, rt, tw), lambda j, t: (j, 0, t, 0)),
                  pl.BlockSpec((None, tw, tw), lambda j, t: (j, 0, 0)),
                  pl.BlockSpec((None, tw, sq), lambda j, t: (j, 0, 0)),
                  pl.BlockSpec((None, sq, tw), lambda j, t: (j, 0, 0)),
                  pl.BlockSpec((None, 1, sq), lambda j, t: (j, 0, 0)),
                  pl.BlockSpec((None, 1, tw), lambda j, t: (j, 0, 0)),
                  pl.BlockSpec((None, nb, sq), lambda j, t: (j, 0, 0))],
        out_specs=[pl.BlockSpec((None, nb, rt, tw), lambda j, t: (j, 0, t, 0)),
                   pl.BlockSpec((None, nb, sq), lambda j, t: (j, 0, 0))],
        out_shape=[jax.ShapeDtypeStruct((SSM_NJ, nb, lc, tw), BF16),
                   jax.ShapeDtypeStruct((SSM_NJ, nb, sq), F32)],
        scratch_shapes=[pltpu.VMEM((nb, rt, sq), F32), pltpu.VMEM((nb, sq), F32)],
        compiler_params=_cp(("arbitrary", "arbitrary")), name="ssm_prompt",
    )(u4, sw["M"], sw["P"], sw["Q"], sw["lam"], sw["D"], h0)
    return s.reshape(SSM_NJ, nb * lc, tw), hfin


def _ssm_sample_kernel(u_ref, m_ref, p_ref, q_ref, lam_ref, d_ref, h0_ref, s_ref, hfin_ref, s_sc,
                       *, nb, lc):
    u = u_ref[...]
    u16 = u.astype(BF16)
    s_sc[...] = _dot(u16, p_ref[...])
    lam = lam_ref[...]
    lr, li = lam[:, :SSM_SW], lam[:, SSM_SW:]
    h = h0_ref[...]
    for r in range(lc):
        rows = slice(r * nb, (r + 1) * nb)
        s = s_sc[rows, :]
        s_sc[rows, :] = h
        hre, him = h[:, :SSM_SW], h[:, SSM_SW:]
        h = jnp.concatenate([lr * hre - li * him + s[:, :SSM_SW], lr * him + li * hre + s[:, SSM_SW:]], axis=1)
    hfin_ref[...] = h
    y = _dot(u16, m_ref[...]) + _dot(s_sc[...].astype(BF16), q_ref[...]) + u * d_ref[...]
    s_ref[...] = jax.nn.gelu(y).astype(BF16)


def _ssm_sample(u, sw, h0, nb, lc):
    tw = SSM_T * LANES
    sq = 2 * SSM_SW
    u2 = u.reshape(SSM_NJ, nb, lc, tw).transpose(0, 2, 1, 3).reshape(SSM_NJ, lc * nb, tw)
    kern = functools.partial(_ssm_sample_kernel, nb=nb, lc=lc)
    blk = lambda j: (j, 0, 0)
    s, hfin = pl.pallas_call(
        kern, grid=(SSM_NJ,),
        in_specs=[pl.BlockSpec((None, lc * nb, tw), blk), pl.BlockSpec((None, tw, tw), blk),
                  pl.BlockSpec((None, tw, sq), blk), pl.BlockSpec((None, sq, tw), blk),
                  pl.BlockSpec((None, 1, sq), blk), pl.BlockSpec((None, 1, tw), blk),
                  pl.BlockSpec((None, nb, sq), blk)],
        out_specs=[pl.BlockSpec((None, lc * nb, tw), blk), pl.BlockSpec((None, nb, sq), blk)],
        out_shape=[jax.ShapeDtypeStruct((SSM_NJ, lc * nb, tw), BF16),
                   jax.ShapeDtypeStruct((SSM_NJ, nb, sq), F32)],
        scratch_shapes=[pltpu.VMEM((lc * nb, sq), F32)],
        compiler_params=_cp(("arbitrary",)), name="ssm_sample",
    )(u2, sw["M"], sw["P"], sw["Q"], sw["lam"], sw["D"], h0)
    s = s.reshape(SSM_NJ, lc, nb, tw).transpose(0, 2, 1, 3).reshape(SSM_NJ, nb * lc, tw)
    return s, hfin


def _ssm_weights(a_re, a_im, log_dt, b_re, b_im, c_re, c_im, d):
    hi = lax.Precision.HIGHEST
    t_ = SSM_T
    dt = jnp.exp(log_dt)[:, None]
    mag = jnp.exp(a_re * dt)
    lam_re, lam_im = mag * jnp.cos(a_im * dt), mag * jnp.sin(a_im * dt)
    den = a_re * a_re + a_im * a_im
    f_re = ((lam_re - 1.0) * a_re + lam_im * a_im) / den
    f_im = (lam_im * a_re - (lam_re - 1.0) * a_im) / den
    w_re = f_re[:, :, None] * b_re - f_im[:, :, None] * b_im
    w_im = f_re[:, :, None] * b_im + f_im[:, :, None] * b_re
    k = jnp.arange(t_ + 1, dtype=F32)[None, :, None]
    pmag = jnp.exp(a_re[:, None, :] * dt[:, :, None] * k)
    pang = a_im[:, None, :] * dt[:, :, None] * k
    pw_re, pw_im = pmag * jnp.cos(pang), pmag * jnp.sin(pang)
    v_re = pw_re[:, :t_, :, None] * w_re[:, None] - pw_im[:, :t_, :, None] * w_im[:, None]
    v_im = pw_re[:, :t_, :, None] * w_im[:, None] + pw_im[:, :t_, :, None] * w_re[:, None]
    kmat = (jnp.einsum("gcn,gjnd->gjcd", c_re, v_re, precision=hi)
            - jnp.einsum("gcn,gjnd->gjcd", c_im, v_im, precision=hi))
    eye = jnp.eye(SSM_GPB, dtype=F32)
    tw = t_ * LANES
    si = jnp.arange(t_)[:, None]
    ti = jnp.arange(t_)[None, :]
    lag = jnp.maximum(ti - si, 0)
    kl = kmat[:, lag] * (ti >= si).astype(F32)[None, :, :, None, None]
    kl = kl.reshape(SSM_NJ, SSM_GPB, t_, t_, SSM_GROUP, SSM_GROUP)
    m = jnp.einsum("jgstcd,gh->jsgdthc", kl, eye).reshape(SSM_NJ, tw, tw)
    p_c = jnp.stack([v_re[:, ::-1], v_im[:, ::-1]])
    p_c = p_c.reshape(2, SSM_NJ, SSM_GPB, t_, SSM_STATE, SSM_GROUP)
    p = jnp.einsum("rjgsnd,gh->jsgdrhn", p_c, eye).reshape(SSM_NJ, tw, 2 * SSM_SW)
    l_re, l_im = pw_re[:, 1:, None, :], pw_im[:, 1:, None, :]
    q_re = c_re[:, None] * l_re - c_im[:, None] * l_im
    q_im = -(c_re[:, None] * l_im + c_im[:, None] * l_re)
    q_c = jnp.stack([q_re, q_im]).reshape(2, SSM_NJ, SSM_GPB, t_, SSM_GROUP, SSM_STATE)
    q = jnp.einsum("rjgtcn,gh->jrgnthc", q_c, eye).reshape(SSM_NJ, 2 * SSM_SW, tw)
    lam_t = jnp.concatenate([pw_re[:, t_].reshape(SSM_NJ, 1, SSM_SW), pw_im[:, t_].reshape(SSM_NJ, 1, SSM_SW)],
                            axis=2)
    dvec = jnp.tile(d.reshape(SSM_NJ, 1, LANES), (1, 1, t_))
    return {"M": m.astype(BF16), "P": p.astype(BF16), "Q": q.astype(BF16), "lam": lam_t, "D": dvec}


def _mixer_kernel(o_ref, s_ref, ga_ref, gb_ref, x_ref, wo_ref, wv_ref, wg_ref, wout_ref, h_ref, s_sc):
    o_a = _dot(o_ref[...], wo_ref[...])
    nchunk = s_sc.shape[1] // SSM_T
    for j in range(SSM_NJ):
        sj = s_ref[j].astype(F32)
        for t in range(SSM_T):
            s_sc[j, pl.ds(t, nchunk, stride=SSM_T), :] = sj[:, t * LANES:(t + 1) * LANES]
    s = jnp.concatenate([s_sc[j] for j in range(SSM_NJ)], axis=1).astype(BF16)
    o_b = _dot(s, wv_ref[...]) * jax.nn.sigmoid(_dot(s, wg_ref[...]))
    merged = ga_ref[...] * o_a + gb_ref[...] * o_b
    h_ref[...] = x_ref[...] + _dot(merged.astype(BF16), wout_ref[...])


def _mixer(o, s, ga, gb, x, w, tm):
    n = x.shape[0]
    row = lambda i: (i, 0)
    const = lambda i: (0, 0)
    wspec = pl.BlockSpec((D_MODEL, D_MODEL), const)
    act = pl.BlockSpec((tm, D_MODEL), row)
    return pl.pallas_call(
        _mixer_kernel, grid=(n // tm,),
        in_specs=[act, pl.BlockSpec((SSM_NJ, tm // SSM_T, SSM_T * LANES), lambda i: (0, i, 0)), act, act, act,
                  wspec, wspec, wspec, wspec],
        out_specs=act, out_shape=jax.ShapeDtypeStruct((n, D_MODEL), F32),
        scratch_shapes=[pltpu.VMEM((SSM_NJ, tm, LANES), F32)],
        compiler_params=_cp(("arbitrary",)), name="mixer_out",
    )(o, s, ga, gb, x, w["wo"], w["wv"], w["wg"], w["wout"])


def _mlp_kernel(h_ref, g_ref, wup_ref, wdn_ref, y_ref):
    h = h_ref[...]
    hn = _rms(h, g_ref[...]).astype(BF16)
    a = jnp.maximum(_dot(hn, wup_ref[...]), 0.0)
    y_ref[...] = h + _dot((a * a).astype(BF16), wdn_ref[...])


def _mlp(h, w, tm):
    n = h.shape[0]
    row = lambda i: (i, 0)
    const = lambda i: (0, 0)
    act = pl.BlockSpec((tm, D_MODEL), row)
    return pl.pallas_call(
        _mlp_kernel, grid=(n // tm,),
        in_specs=[act, pl.BlockSpec((1, D_MODEL), const), pl.BlockSpec((D_MODEL, D_FF), const),
                  pl.BlockSpec((D_FF, D_MODEL), const)],
        out_specs=act, out_shape=jax.ShapeDtypeStruct((n, D_MODEL), F32),
        compiler_params=_cp(("arbitrary",)), name="mlp",
    )(h, w["gmlp"], w["wup"], w["wdn"])


def _rope_tables(pos):
    half = QK_ROPE // 2
    inv = ROPE_THETA ** (-jnp.arange(half, dtype=F32) / half)
    ang = pos.astype(F32)[:, None] * inv[None, :]
    cos, sin = jnp.cos(ang), jnp.sin(ang)
    z = jnp.zeros_like(cos)
    pad = jnp.zeros((pos.shape[0], LANES - QK_ROPE), F32)
    ct = jnp.concatenate([cos, cos, pad], axis=1)
    s1 = jnp.concatenate([-sin, z, pad], axis=1)
    s2 = jnp.concatenate([z, sin, pad], axis=1)
    return ct, s1, s2


def _pad_lanes(a, width):
    return jnp.pad(a, [(0, 0)] * (a.ndim - 1) + [(0, width - a.shape[-1])])


def _layer_weights(norm_mix, w_in, q_lora_norm, w_uq, q_nope_norm, q_rope_norm, kv_lora_norm, k_rope_norm,
                   w_ukv, k_nope_norm, w_o_attn, w_glu_v, w_glu_g, w_out, norm_mlp, w_mlp_up, w_mlp_down):
    o_kv = Q_LORA
    o_kr = o_kv + KV_LORA
    o_ssm = o_kr + QK_ROPE
    o_ga = o_ssm + D_MODEL
    o_gb = o_ga + D_MODEL
    bf = lambda a: a.astype(BF16)
    r2 = lambda a: a.reshape(1, -1).astype(F32)
    return {
        "nmix": r2(norm_mix),
        "wq": bf(w_in[:, :o_kv]), "wkv": bf(w_in[:, o_kv:o_kr]),
        "wkr": bf(_pad_lanes(w_in[:, o_kr:o_ssm], LANES)),
        "wu": bf(w_in[:, o_ssm:o_ga]), "wga": bf(w_in[:, o_ga:o_gb]), "wgb": bf(w_in[:, o_gb:]),
        "gq": r2(q_lora_norm),
        "wuqn": bf(w_uq[:, :, :QK_NOPE].reshape(Q_LORA, N_HEADS * QK_NOPE)),
        "wuqr": bf(_pad_lanes(w_uq[:, :, QK_NOPE:], LANES).reshape(Q_LORA, N_HEADS * LANES)),
        "gqn": r2(q_nope_norm), "gqr": _pad_lanes(r2(q_rope_norm), LANES),
        "gkv": r2(kv_lora_norm), "gkr": _pad_lanes(r2(k_rope_norm), LANES),
        "wuk": bf(w_ukv[:, :, :QK_NOPE].reshape(KV_LORA, N_HEADS * QK_NOPE)),
        "wuv": bf(w_ukv[:, :, QK_NOPE:].reshape(KV_LORA, N_HEADS * V_DIM)),
        "gkn": r2(k_nope_norm),
        "wo": bf(w_o_attn.reshape(N_HEADS * V_DIM, D_MODEL)),
        "wv": bf(w_glu_v), "wg": bf(w_glu_g), "wout": bf(w_out),
        "gmlp": r2(norm_mlp), "wup": bf(w_mlp_up), "wdn": bf(w_mlp_down),
    }


def kernel(x_prompt, x_sample, cache_latent, cache_krope, cache_meta_latent, cache_meta_krope, state_ssm_re, state_ssm_im, meta_tokens, norm_mix, w_in, q_lora_norm, w_uq, q_nope_norm, q_rope_norm, kv_lora_norm, k_rope_norm, w_ukv, k_nope_norm, w_o_attn, ssm_a_re, ssm_a_im, ssm_log_dt, ssm_b_re, ssm_b_im, ssm_c_re, ssm_c_im, ssm_d, w_glu_v, w_glu_g, w_out, norm_mlp, w_mlp_up, w_mlp_down):
    bsz_p, seq_p = x_prompt.shape[0], x_prompt.shape[1]
    bsz_s, seq_s = x_sample.shape[0], x_sample.shape[1]
    past = cache_latent.shape[2]
    depth = w_in.shape[0]
    assert depth == 1, "single-layer step"
    l = 0
    w = _layer_weights(norm_mix[l], w_in[l], q_lora_norm[l], w_uq[l], q_nope_norm[l], q_rope_norm[l],
                       kv_lora_norm[l], k_rope_norm[l], w_ukv[l], k_nope_norm[l], w_o_attn[l], w_glu_v[l],
                       w_glu_g[l], w_out[l], norm_mlp[l], w_mlp_up[l], w_mlp_down[l])
    sw = _ssm_weights(ssm_a_re[l].astype(F32), ssm_a_im[l].astype(F32), ssm_log_dt[l].astype(F32),
                      ssm_b_re[l].astype(F32), ssm_b_im[l].astype(F32), ssm_c_re[l].astype(F32),
                      ssm_c_im[l].astype(F32), ssm_d[l].astype(F32))
    sq = 2 * SSM_SW

    tabs_m = _rope_tables(jnp.arange(N_META, dtype=jnp.int32) - N_META)
    (_, mk, mv, mckv, mkr, mu, _, _) = _front(meta_tokens.astype(F32), tabs_m, 1, w, N_META)
    h0_zero = jnp.zeros((SSM_NJ, 1, sq), F32)
    _, h_meta = _ssm_prompt(mu, sw, h0_zero, 1, N_META // SSM_T, N_META // SSM_T)

    n_p = bsz_p * seq_p
    tm = 256
    xp = x_prompt.reshape(n_p, D_MODEL)
    tabs_p = _rope_tables(jnp.arange(seq_p, dtype=jnp.int32))
    q, k, v, ckv_p, kr_p, u_p, ga, gb = _front(xp, tabs_p, seq_p // tm, w, tm)
    o_p = _attn_prompt(q, k, v, mk, mv, bsz_p, seq_p)
    h0_p = jnp.broadcast_to(h_meta, (SSM_NJ, bsz_p, sq))
    s_p, hfin_p = _ssm_prompt(u_p, sw, h0_p, bsz_p, seq_p // SSM_T, 128)
    h1_p = _mixer(o_p, s_p, ga, gb, xp, w, tm)
    y_p = _mlp(h1_p, w, tm)

    n_s = bsz_s * seq_s
    xs = x_sample.reshape(n_s, D_MODEL)
    pos_s = past + jnp.arange(seq_s, dtype=jnp.int32)
    tabs_s = tuple(jnp.tile(t, (bsz_s, 1)) for t in _rope_tables(pos_s))
    tm_s = 256
    q_s, k_s, v_s, ckv_s, kr_s, u_s, ga_s, gb_s = _front(xs, tabs_s, n_s // tm_s, w, tm_s)
    ckrb = _pad_lanes(cache_krope[l].reshape(bsz_s * past, QK_ROPE), LANES).astype(BF16)
    cmkrb = _pad_lanes(cache_meta_krope[l].reshape(bsz_s * N_META, QK_ROPE), LANES).astype(BF16)
    ck, cv = _expand(cache_latent[l].reshape(bsz_s * past, KV_LORA).astype(F32), ckrb, w, 512)
    cmk, cmv = _expand(cache_meta_latent[l].reshape(bsz_s * N_META, KV_LORA).astype(F32), cmkrb, w,
                       bsz_s * N_META)
    o_s = _attn_sample(q_s, ck, cv, cmk, cmv, k_s, v_s, bsz_s, seq_s, past)

    def to_blocks(st):
        return st.astype(F32).reshape(bsz_s, SSM_NJ, SSM_SW).transpose(1, 0, 2)

    h0_s = jnp.concatenate([to_blocks(state_ssm_re[l]), to_blocks(state_ssm_im[l])], axis=2)
    s_s, hfin_s = _ssm_sample(u_s, sw, h0_s, bsz_s, seq_s // SSM_T)
    h1_s = _mixer(o_s, s_s, ga_s, gb_s, xs, w, tm_s)
    y_s = _mlp(h1_s, w, tm_s)

    def from_blocks(hf, nb):
        re = hf[:, :, :SSM_SW].transpose(1, 0, 2).reshape(1, nb, N_GROUPS, SSM_STATE)
        im = hf[:, :, SSM_SW:].transpose(1, 0, 2).reshape(1, nb, N_GROUPS, SSM_STATE)
        return re, im

    sre_p, sim_p = from_blocks(hfin_p, bsz_p)
    sre_s, sim_s = from_blocks(hfin_s, bsz_s)
    return (y_p.reshape(bsz_p, seq_p, D_MODEL), y_s.reshape(bsz_s, seq_s, D_MODEL),
            ckv_p.reshape(1, bsz_p, seq_p, KV_LORA), kr_p.reshape(1, bsz_p, seq_p, QK_ROPE),
            jnp.broadcast_to(mckv[None, None], (1, bsz_p, N_META, KV_LORA)),
            jnp.broadcast_to(mkr[None, None], (1, bsz_p, N_META, QK_ROPE)),
            sre_p, sim_p,
            ckv_s.reshape(1, bsz_s, seq_s, KV_LORA), kr_s.reshape(1, bsz_s, seq_s, QK_ROPE),
            sre_s, sim_s)
```

```python
import functools
import math

import jax
import jax.numpy as jnp
from jax import lax
from jax.experimental import pallas as pl
from jax.experimental.pallas import tpu as pltpu

F32 = jnp.float32
BF16 = jnp.bfloat16

D_MODEL = 1024
CHUNK = 64
N_META = 16
N_HEADS = 8
QK_NOPE = 128
QK_ROPE = 64
V_DIM = 128
QK_DIM = QK_NOPE + QK_ROPE
Q_LORA = 384
KV_LORA = 256
SSM_GROUP = 16
N_GROUPS = D_MODEL // SSM_GROUP
SSM_STATE = 64
D_FF = 4 * D_MODEL
ROPE_THETA = 10000.0
EPS = 1e-6
ATTN_SCALE = QK_DIM ** -0.5
NEG_INF = -1e30

LANES = 128
HEAD_PAD = 2 * LANES
SSM_T = 8
SSM_NJ = D_MODEL // LANES
SSM_GPB = LANES // SSM_GROUP
SSM_SW = SSM_GPB * SSM_STATE
VMEM_LIMIT = 56 * 1024 * 1024


def _cp(sem):
    return pltpu.CompilerParams(dimension_semantics=sem, vmem_limit_bytes=VMEM_LIMIT)


def _dot(a, b):
    return jnp.dot(a, b, preferred_element_type=F32)


def _dot_nt(a, b):
    return lax.dot_general(a, b, (((1,), (1,)), ((), ())), preferred_element_type=F32)


def _rms(x, g, n=None):
    n = x.shape[-1] if n is None else n
    ms = jnp.sum(x * x, axis=-1, keepdims=True) * (1.0 / n)
    return x * lax.rsqrt(ms + EPS) * g


def _rope128(b, ct, s1, s2):
    return b * ct + pltpu.roll(b, LANES - QK_ROPE // 2, 1) * s1 + pltpu.roll(b, QK_ROPE // 2, 1) * s2


def _front_kernel(x_ref, ct_ref, s1_ref, s2_ref, nmix_ref, wq_ref, wkv_ref, wkr_ref, wu_ref, wga_ref,
                  wgb_ref, gq_ref, wuqn_ref, wuqr_ref, gqn_ref, gqr_ref, gkv_ref, gkr_ref, wuk_ref,
                  wuv_ref, gkn_ref,
                  q_ref, k_ref, v_ref, ckv_ref, kr_ref, u_ref, ga_ref, gb_ref, u_sc):
    x = x_ref[...]
    xn = _rms(x, nmix_ref[...]).astype(BF16)
    ct, s1, s2 = ct_ref[...], s1_ref[...], s2_ref[...]

    u = _dot(xn, wu_ref[...])
    nchunk = u_sc.shape[1] // SSM_T
    for j in range(SSM_NJ):
        u_sc[j] = u[:, j * LANES:(j + 1) * LANES]
        for t in range(SSM_T):
            u_ref[j, :, t * LANES:(t + 1) * LANES] = u_sc[j, pl.ds(t, nchunk, stride=SSM_T), :]
    ga_ref[...] = jax.nn.sigmoid(_dot(xn, wga_ref[...]))
    gb_ref[...] = jax.nn.sigmoid(_dot(xn, wgb_ref[...]))

    c_kv = _rms(_dot(xn, wkv_ref[...]), gkv_ref[...])
    ckv_ref[...] = c_kv
    kr = _rms(_dot(xn, wkr_ref[...]), gkr_ref[...], QK_ROPE)
    kr = _rope128(kr, ct, s1, s2)
    kr_ref[...] = kr[:, :QK_ROPE]
    kr16 = kr.astype(BF16)
    c16 = c_kv.astype(BF16)
    kn = _dot(c16, wuk_ref[...])
    for h in range(N_HEADS):
        sl = slice(h * LANES, (h + 1) * LANES)
        k_ref[:, h * HEAD_PAD:h * HEAD_PAD + LANES] = _rms(kn[:, sl], gkn_ref[...]).astype(BF16)
        k_ref[:, h * HEAD_PAD + LANES:(h + 1) * HEAD_PAD] = kr16
    v_ref[...] = _dot(c16, wuv_ref[...]).astype(BF16)

    c_q = _rms(_dot(xn, wq_ref[...]), gq_ref[...]).astype(BF16)
    qn = _dot(c_q, wuqn_ref[...])
    qr = _dot(c_q, wuqr_ref[...])
    for h in range(N_HEADS):
        sl = slice(h * LANES, (h + 1) * LANES)
        q_ref[:, h * HEAD_PAD:h * HEAD_PAD + LANES] = (_rms(qn[:, sl], gqn_ref[...]) * ATTN_SCALE).astype(BF16)
        r = _rope128(_rms(qr[:, sl], gqr_ref[...], QK_ROPE), ct, s1, s2)
        q_ref[:, h * HEAD_PAD + LANES:(h + 1) * HEAD_PAD] = (r * ATTN_SCALE).astype(BF16)


def _front(x, tabs, tab_blocks, w, tm):
    n = x.shape[0]
    nt = n // tm
    ct, s1, s2 = tabs

    def row(i):
        return (i, 0)

    def tab(i):
        return (i % tab_blocks, 0)

    def const(i):
        return (0, 0)

    def full(a):
        return pl.BlockSpec(a.shape, const)

    weights = [w["nmix"], w["wq"], w["wkv"], w["wkr"], w["wu"], w["wga"], w["wgb"], w["gq"], w["wuqn"],
               w["wuqr"], w["gqn"], w["gqr"], w["gkv"], w["gkr"], w["wuk"], w["wuv"], w["gkn"]]
    wide = N_HEADS * LANES
    qk_wide = N_HEADS * HEAD_PAD
    out_shape = [
        jax.ShapeDtypeStruct((n, qk_wide), BF16),
        jax.ShapeDtypeStruct((n, qk_wide), BF16),
        jax.ShapeDtypeStruct((n, wide), BF16),
        jax.ShapeDtypeStruct((n, KV_LORA), F32),
        jax.ShapeDtypeStruct((n, QK_ROPE), F32),
        jax.ShapeDtypeStruct((SSM_NJ, n // SSM_T, SSM_T * LANES), F32),
        jax.ShapeDtypeStruct((n, D_MODEL), F32),
        jax.ShapeDtypeStruct((n, D_MODEL), F32),
    ]
    out_specs = [
        pl.BlockSpec((tm, qk_wide), row), pl.BlockSpec((tm, qk_wide), row),
        pl.BlockSpec((tm, wide), row), pl.BlockSpec((tm, KV_LORA), row),
        pl.BlockSpec((tm, QK_ROPE), row),
        pl.BlockSpec((SSM_NJ, tm // SSM_T, SSM_T * LANES), lambda i: (0, i, 0)),
        pl.BlockSpec((tm, D_MODEL), row), pl.BlockSpec((tm, D_MODEL), row),
    ]
    in_specs = ([pl.BlockSpec((tm, D_MODEL), row)] + [pl.BlockSpec((tm, LANES), tab)] * 3
                + [full(a) for a in weights])
    return pl.pallas_call(
        _front_kernel, grid=(nt,), in_specs=in_specs, out_specs=out_specs, out_shape=out_shape,
        scratch_shapes=[pltpu.VMEM((SSM_NJ, tm, LANES), F32)],
        compiler_params=_cp(("arbitrary",)), name="front",
    )(x, ct, s1, s2, *weights)


def _expand_kernel(c_ref, kr_ref, wuk_ref, wuv_ref, gkn_ref, k_ref, v_ref):
    c16 = c_ref[...].astype(BF16)
    kr16 = kr_ref[...]
    kn = _dot(c16, wuk_ref[...])
    for h in range(N_HEADS):
        sl = slice(h * LANES, (h + 1) * LANES)
        k_ref[:, h * HEAD_PAD:h * HEAD_PAD + LANES] = _rms(kn[:, sl], gkn_ref[...]).astype(BF16)
        k_ref[:, h * HEAD_PAD + LANES:(h + 1) * HEAD_PAD] = kr16
    v_ref[...] = _dot(c16, wuv_ref[...]).astype(BF16)


def _expand(c, krb, w, tm):
    n = c.shape[0]
    wide = N_HEADS * LANES
    qk_wide = N_HEADS * HEAD_PAD
    row = lambda i: (i, 0)
    const = lambda i: (0, 0)
    return pl.pallas_call(
        _expand_kernel, grid=(n // tm,),
        in_specs=[pl.BlockSpec((tm, KV_LORA), row), pl.BlockSpec((tm, LANES), row),
                  pl.BlockSpec(w["wuk"].shape, const),
                  pl.BlockSpec(w["wuv"].shape, const), pl.BlockSpec(w["gkn"].shape, const)],
        out_specs=[pl.BlockSpec((tm, qk_wide), row), pl.BlockSpec((tm, wide), row)],
        out_shape=[jax.ShapeDtypeStruct((n, qk_wide), BF16), jax.ShapeDtypeStruct((n, wide), BF16)],
        compiler_params=_cp(("arbitrary",)), name="expand",
    )(c, krb, w["wuk"], w["wuv"], w["gkn"])


ATT_T = 512


def _with_ones(v):
    return jnp.concatenate([v, jnp.ones_like(v)], axis=1)


def _softmax_step(s, v, m_ref, acc_ref):
    m_prev = m_ref[...]
    m_new = jnp.maximum(m_prev, jnp.max(s, axis=-1, keepdims=True))
    alpha = jnp.exp(m_prev - m_new)
    p = jnp.exp(s - jnp.tile(m_new, (1, s.shape[1] // LANES)))
    acc_ref[...] = jnp.tile(alpha, (1, 2)) * acc_ref[...] + _dot(p.astype(BF16), _with_ones(v))
    m_ref[...] = m_new


def _attn_prompt_kernel(q_ref, k_ref, v_ref, mk_ref, mv_ref, o_ref, a_sc, b_sc, m_ref, acc_ref):
    i = pl.program_id(2)

    def key_rows(kt):
        return pl.ds(pl.multiple_of(kt * ATT_T, ATT_T), ATT_T)

    def scores(kt):
        return _dot_nt(q_ref[...], k_ref[key_rows(kt), :])

    def consume(buf, kt, diagonal):
        s = buf[...]
        if diagonal:
            qc = lax.broadcasted_iota(jnp.int32, (ATT_T, ATT_T), 0) // CHUNK
            kc = lax.broadcasted_iota(jnp.int32, (ATT_T, ATT_T), 1) // CHUNK
            s = jnp.where(kc <= qc, s, NEG_INF)
        _softmax_step(s, v_ref[key_rows(kt), :], m_ref, acc_ref)

    a_sc[...] = scores(0)

    s = _dot_nt(q_ref[...], mk_ref[...])
    m0 = jnp.max(s, axis=-1, keepdims=True)
    p = jnp.exp(s - m0)
    m_ref[...] = jnp.broadcast_to(m0, m_ref.shape)
    acc_ref[...] = _dot(p.astype(BF16), _with_ones(mv_ref[...]))

    def pair(pi, carry):
        kt = 2 * pi
        b_sc[...] = scores(kt + 1)
        consume(a_sc, kt, False)
        a_sc[...] = scores(kt + 2)
        consume(b_sc, kt + 1, False)
        return carry

    lax.fori_loop(0, lax.shift_right_logical(i, 1), pair, 0)

    @pl.when((i & 1) == 1)
    def _():
        b_sc[...] = scores(i)
        consume(a_sc, i - 1, False)
        consume(b_sc, i, True)

    @pl.when((i & 1) == 0)
    def _():
        consume(a_sc, i, True)

    o_ref[...] = (acc_ref[:, :V_DIM] / acc_ref[:, V_DIM:]).astype(BF16)


def _attn_prompt(q, k, v, mk, mv, bsz, seq):
    nq = seq // ATT_T
    n = bsz * seq
    qmap = lambda b, h, i: (b * nq + i, h)
    return pl.pallas_call(
        _attn_prompt_kernel, grid=(bsz, N_HEADS, nq),
        in_specs=[pl.BlockSpec((ATT_T, HEAD_PAD), qmap),
                  pl.BlockSpec((seq, HEAD_PAD), lambda b, h, i: (b, h)),
                  pl.BlockSpec((seq, V_DIM), lambda b, h, i: (b, h)),
                  pl.BlockSpec((N_META, HEAD_PAD), lambda b, h, i: (0, h)),
                  pl.BlockSpec((N_META, V_DIM), lambda b, h, i: (0, h))],
        out_specs=pl.BlockSpec((ATT_T, V_DIM), qmap),
        out_shape=jax.ShapeDtypeStruct((n, N_HEADS * V_DIM), BF16),
        scratch_shapes=[pltpu.VMEM((ATT_T, ATT_T), F32), pltpu.VMEM((ATT_T, ATT_T), F32),
                        pltpu.VMEM((ATT_T, LANES), F32), pltpu.VMEM((ATT_T, 2 * V_DIM), F32)],
        compiler_params=_cp(("arbitrary", "arbitrary", "arbitrary")), name="attn_prompt",
    )(q, k, v, mk, mv)


def _attn_sample_kernel(q_ref, ck_ref, cv_ref, mk_ref, mv_ref, nk_ref, nv_ref, o_ref):
    q = q_ref[...]
    s_c = _dot_nt(q, ck_ref[...])
    s_m = _dot_nt(q, mk_ref[...])
    s_n = _dot_nt(q, nk_ref[...])
    m = jnp.maximum(jnp.max(s_c, axis=-1, keepdims=True),
                    jnp.maximum(jnp.max(s_m, axis=-1, keepdims=True), jnp.max(s_n, axis=-1, keepdims=True)))
    p_c, p_m, p_n = jnp.exp(s_c - m), jnp.exp(s_m - m), jnp.exp(s_n - m)
    l = (jnp.sum(p_c, axis=-1, keepdims=True) + jnp.sum(p_m, axis=-1, keepdims=True)
         + jnp.sum(p_n, axis=-1, keepdims=True))
    acc = (_dot(p_c.astype(BF16), cv_ref[...]) + _dot(p_m.astype(BF16), mv_ref[...])
           + _dot(p_n.astype(BF16), nv_ref[...]))
    o_ref[...] = (acc / l).astype(BF16)


def _attn_sample(q, ck, cv, mk, mv, nk, nv, bsz, seq, past):
    bh = lambda b, h: (b, h)
    return pl.pallas_call(
        _attn_sample_kernel, grid=(bsz, N_HEADS),
        in_specs=[pl.BlockSpec((seq, HEAD_PAD), bh),
                  pl.BlockSpec((past, HEAD_PAD), bh), pl.BlockSpec((past, V_DIM), bh),
                  pl.BlockSpec((N_META, HEAD_PAD), bh), pl.BlockSpec((N_META, V_DIM), bh),
                  pl.BlockSpec((seq, HEAD_PAD), bh), pl.BlockSpec((seq, V_DIM), bh)],
        out_specs=pl.BlockSpec((seq, V_DIM), bh),
        out_shape=jax.ShapeDtypeStruct((bsz * seq, N_HEADS * V_DIM), BF16),
        compiler_params=_cp(("arbitrary", "arbitrary")), name="attn_sample",
    )(q, ck, cv, mk, mv, nk, nv)


def _ssm_scan_rows(s_sc, hin, lam, nb, steps):
    lr = lam[:, :SSM_SW]
    li = lam[:, SSM_SW:]

    def step(r, h):
        s = jnp.concatenate([s_sc[b, pl.ds(r, 1), :] for b in range(nb)], axis=0)
        for b in range(nb):
            s_sc[b, pl.ds(r, 1), :] = h[b:b + 1, :]
        hre, him = h[:, :SSM_SW], h[:, SSM_SW:]
        nre = lr * hre - li * him + s[:, :SSM_SW]
        nim = lr * him + li * hre + s[:, SSM_SW:]
        return jnp.concatenate([nre, nim], axis=1)

    return lax.fori_loop(0, steps, step, hin)


def _ssm_prompt_kernel(u_ref, m_ref, p_ref, q_ref, lam_ref, d_ref, h0_ref, s_ref, hfin_ref, s_sc, h_sc,
                       *, nb, rt):
    t = pl.program_id(1)

    @pl.when(t == 0)
    def _():
        h_sc[...] = h0_ref[...]

    for b in range(nb):
        s_sc[b] = _dot(u_ref[b].astype(BF16), p_ref[...])
    h_sc[...] = _ssm_scan_rows(s_sc, h_sc[...], lam_ref[...], nb, rt)
    for b in range(nb):
        ub = u_ref[b]
        y = _dot(ub.astype(BF16), m_ref[...]) + _dot(s_sc[b].astype(BF16), q_ref[...]) + ub * d_ref[...]
        s_ref[b] = jax.nn.gelu(y).astype(BF16)

    @pl.when(t == pl.num_programs(1) - 1)
    def _():
        hfin_ref[...] = h_sc[...]


def _ssm_prompt(u, sw, h0, nb, lc, rt):
    tw = SSM_T * LANES
    u4 = u.reshape(SSM_NJ, nb, lc, tw)
    sq = 2 * SSM_SW
    kern = functools.partial(_ssm_prompt_kernel, nb=nb, rt=rt)
    s, hfin = pl.pallas_call(
        kern, grid=(SSM_NJ, lc // rt),
        in_specs=[pl.BlockSpec((None, nb, rt, tw), lambda j, t: (j, 0, t, 0)),
                  pl.BlockSpec((None, tw, tw), lambda j, t: (j, 0, 0)),
                  pl.BlockSpec((None, tw, sq), lambda j, t: (j, 0, 0)),
                  pl.BlockSpec((None, sq, tw), lambda j, t: (j, 0, 0)),
                  pl.BlockSpec((None, 1, sq), lambda j, t: (j, 0, 0)),
                  pl.BlockSpec((None, 1, tw), lambda j, t: (j, 0, 0)),
                  pl.BlockSpec((None, nb, sq), lambda j, t: (j, 0, 0))],
        out_specs=[pl.BlockSpec((None, nb, rt, tw), lambda j, t: (j, 0, t, 0)),
                   pl.BlockSpec((None, nb, sq), lambda j, t: (j, 0, 0))],
        out_shape=[jax.ShapeDtypeStruct((SSM_NJ, nb, lc, tw), BF16),
                   jax.ShapeDtypeStruct((SSM_NJ, nb, sq), F32)],
        scratch_shapes=[pltpu.VMEM((nb, rt, sq), F32), pltpu.VMEM((nb, sq), F32)],
        compiler_params=_cp(("arbitrary", "arbitrary")), name="ssm_prompt",
    )(u4, sw["M"], sw["P"], sw["Q"], sw["lam"], sw["D"], h0)
    return s.reshape(SSM_NJ, nb * lc, tw), hfin


def _ssm_sample_kernel(u_ref, m_ref, p_ref, q_ref, lam_ref, d_ref, h0_ref, s_ref, hfin_ref, s_sc,
                       *, nb, lc):
    u = u_ref[...]
    u16 = u.astype(BF16)
    s_sc[...] = _dot(u16, p_ref[...])
    lam = lam_ref[...]
    lr, li = lam[:, :SSM_SW], lam[:, SSM_SW:]
    h = h0_ref[...]
    for r in range(lc):
        rows = slice(r * nb, (r + 1) * nb)
        s = s_sc[rows, :]
        s_sc[rows, :] = h
        hre, him = h[:, :SSM_SW], h[:, SSM_SW:]
        h = jnp.concatenate([lr * hre - li * him + s[:, :SSM_SW], lr * him + li * hre + s[:, SSM_SW:]], axis=1)
    hfin_ref[...] = h
    y = _dot(u16, m_ref[...]) + _dot(s_sc[...].astype(BF16), q_ref[...]) + u * d_ref[...]
    s_ref[...] = jax.nn.gelu(y).astype(BF16)


def _ssm_sample(u, sw, h0, nb, lc):
    tw = SSM_T * LANES
    sq = 2 * SSM_SW
    u2 = u.reshape(SSM_NJ, nb, lc, tw).transpose(0, 2, 1, 3).reshape(SSM_NJ, lc * nb, tw)
    kern = functools.partial(_ssm_sample_kernel, nb=nb, lc=lc)
    blk = lambda j: (j, 0, 0)
    s, hfin = pl.pallas_call(
        kern, grid=(SSM_NJ,),
        in_specs=[pl.BlockSpec((None, lc * nb, tw), blk), pl.BlockSpec((None, tw, tw), blk),
                  pl.BlockSpec((None, tw, sq), blk), pl.BlockSpec((None, sq, tw), blk),
                  pl.BlockSpec((None, 1, sq), blk), pl.BlockSpec((None, 1, tw), blk),
                  pl.BlockSpec((None, nb, sq), blk)],
        out_specs=[pl.BlockSpec((None, lc * nb, tw), blk), pl.BlockSpec((None, nb, sq), blk)],
        out_shape=[jax.ShapeDtypeStruct((SSM_NJ, lc * nb, tw), BF16),
                   jax.ShapeDtypeStruct((SSM_NJ, nb, sq), F32)],
        scratch_shapes=[pltpu.VMEM((lc * nb, sq), F32)],
        compiler_params=_cp(("arbitrary",)), name="ssm_sample",
    )(u2, sw["M"], sw["P"], sw["Q"], sw["lam"], sw["D"], h0)
    s = s.reshape(SSM_NJ, lc, nb, tw).transpose(0, 2, 1, 3).reshape(SSM_NJ, nb * lc, tw)
    return s, hfin


def _ssm_weights(a_re, a_im, log_dt, b_re, b_im, c_re, c_im, d):
    hi = lax.Precision.HIGHEST
    t_ = SSM_T
    dt = jnp.exp(log_dt)[:, None]
    mag = jnp.exp(a_re * dt)
    lam_re, lam_im = mag * jnp.cos(a_im * dt), mag * jnp.sin(a_im * dt)
    den = a_re * a_re + a_im * a_im
    f_re = ((lam_re - 1.0) * a_re + lam_im * a_im) / den
    f_im = (lam_im * a_re - (lam_re - 1.0) * a_im) / den
    w_re = f_re[:, :, None] * b_re - f_im[:, :, None] * b_im
    w_im = f_re[:, :, None] * b_im + f_im[:, :, None] * b_re
    k = jnp.arange(t_ + 1, dtype=F32)[None, :, None]
    pmag = jnp.exp(a_re[:, None, :] * dt[:, :, None] * k)
    pang = a_im[:, None, :] * dt[:, :, None] * k
    pw_re, pw_im = pmag * jnp.cos(pang), pmag * jnp.sin(pang)
    v_re = pw_re[:, :t_, :, None] * w_re[:, None] - pw_im[:, :t_, :, None] * w_im[:, None]
    v_im = pw_re[:, :t_, :, None] * w_im[:, None] + pw_im[:, :t_, :, None] * w_re[:, None]
    kmat = (jnp.einsum("gcn,gjnd->gjcd", c_re, v_re, precision=hi)
            - jnp.einsum("gcn,gjnd->gjcd", c_im, v_im, precision=hi))
    eye = jnp.eye(SSM_GPB, dtype=F32)

    def blockdiag(x):
        x = jnp.moveaxis(x, 1, -3)
        y = x[..., :, :, None, :] * eye[:, None, :, None]
        return y.reshape(x.shape[:-3] + (SSM_GPB * x.shape[-2], SSM_GPB * x.shape[-1]))

    def per_block(x):
        return x.reshape((SSM_NJ, SSM_GPB) + x.shape[1:])

    bd_k = blockdiag(per_block(jnp.swapaxes(kmat, 2, 3))).astype(BF16)
    zero = jnp.zeros((SSM_NJ, LANES, LANES), BF16)
    m = jnp.concatenate(
        [jnp.concatenate([zero] * s + [bd_k[:, t - s] for t in range(s, t_)], axis=2) for s in range(t_)],
        axis=1)
    pl_re = jnp.swapaxes(per_block(pw_re), 1, 2).reshape(SSM_NJ, t_ + 1, SSM_SW)
    pl_im = jnp.swapaxes(per_block(pw_im), 1, 2).reshape(SSM_NJ, t_ + 1, SSM_SW)
    wb_re = blockdiag(per_block(jnp.swapaxes(w_re, 1, 2)))
    wb_im = blockdiag(per_block(jnp.swapaxes(w_im, 1, 2)))
    p_rows = []
    for s in range(t_):
        lr, li = pl_re[:, t_ - 1 - s][:, None, :], pl_im[:, t_ - 1 - s][:, None, :]
        p_rows.append(jnp.concatenate([lr * wb_re - li * wb_im, lr * wb_im + li * wb_re], axis=2).astype(BF16))
    p = jnp.concatenate(p_rows, axis=1)
    cb_re = blockdiag(per_block(jnp.swapaxes(c_re, 1, 2)))
    cb_im = blockdiag(per_block(jnp.swapaxes(c_im, 1, 2)))
    q_cols = []
    for t in range(t_):
        lr, li = pl_re[:, t + 1][:, :, None], pl_im[:, t + 1][:, :, None]
        q_cols.append(jnp.concatenate([cb_re * lr - cb_im * li, -(cb_re * li + cb_im * lr)], axis=1).astype(BF16))
    q = jnp.concatenate(q_cols, axis=2)
    lam_t = jnp.concatenate([pl_re[:, t_][:, None, :], pl_im[:, t_][:, None, :]], axis=2)
    dvec = jnp.tile(d.reshape(SSM_NJ, 1, LANES), (1, 1, t_))
    return {"M": m, "P": p, "Q": q, "lam": lam_t, "D": dvec}


def _mixer_kernel(o_ref, s_ref, ga_ref, gb_ref, x_ref, wo_ref, wv_ref, wg_ref, wout_ref, h_ref, s_sc):
    o_a = _dot(o_ref[...], wo_ref[...])
    nchunk = s_sc.shape[1] // SSM_T
    for j in range(SSM_NJ):
        sj = s_ref[j].astype(F32)
        for t in range(SSM_T):
            s_sc[j, pl.ds(t, nchunk, stride=SSM_T), :] = sj[:, t * LANES:(t + 1) * LANES]
    s = jnp.concatenate([s_sc[j] for j in range(SSM_NJ)], axis=1).astype(BF16)
    o_b = _dot(s, wv_ref[...]) * jax.nn.sigmoid(_dot(s, wg_ref[...]))
    merged = ga_ref[...] * o_a + gb_ref[...] * o_b
    h_ref[...] = x_ref[...] + _dot(merged.astype(BF16), wout_ref[...])


def _mixer(o, s, ga, gb, x, w, tm):
    n = x.shape[0]
    row = lambda i: (i, 0)
    const = lambda i: (0, 0)
    wspec = pl.BlockSpec((D_MODEL, D_MODEL), const)
    act = pl.BlockSpec((tm, D_MODEL), row)
    return pl.pallas_call(
        _mixer_kernel, grid=(n // tm,),
        in_specs=[act, pl.BlockSpec((SSM_NJ, tm // SSM_T, SSM_T * LANES), lambda i: (0, i, 0)), act, act, act,
                  wspec, wspec, wspec, wspec],
        out_specs=act, out_shape=jax.ShapeDtypeStruct((n, D_MODEL), F32),
        scratch_shapes=[pltpu.VMEM((SSM_NJ, tm, LANES), F32)],
        compiler_params=_cp(("arbitrary",)), name="mixer_out",
    )(o, s, ga, gb, x, w["wo"], w["wv"], w["wg"], w["wout"])


def _mlp_kernel(h_ref, g_ref, wup_ref, wdn_ref, y_ref):
    h = h_ref[...]
    hn = _rms(h, g_ref[...]).astype(BF16)
    a = jnp.maximum(_dot(hn, wup_ref[...]), 0.0)
    y_ref[...] = h + _dot((a * a).astype(BF16), wdn_ref[...])


def _mlp(h, w, tm):
    n = h.shape[0]
    row = lambda i: (i, 0)
    const = lambda i: (0, 0)
    act = pl.BlockSpec((tm, D_MODEL), row)
    return pl.pallas_call(
        _mlp_kernel, grid=(n // tm,),
        in_specs=[act, pl.BlockSpec((1, D_MODEL), const), pl.BlockSpec((D_MODEL, D_FF), const),
                  pl.BlockSpec((D_FF, D_MODEL), const)],
        out_specs=act, out_shape=jax.ShapeDtypeStruct((n, D_MODEL), F32),
        compiler_params=_cp(("arbitrary",)), name="mlp",
    )(h, w["gmlp"], w["wup"], w["wdn"])


def _rope_tables(pos):
    half = QK_ROPE // 2
    inv = ROPE_THETA ** (-jnp.arange(half, dtype=F32) / half)
    ang = pos.astype(F32)[:, None] * inv[None, :]
    cos, sin = jnp.cos(ang), jnp.sin(ang)
    z = jnp.zeros_like(cos)
    pad = jnp.zeros((pos.shape[0], LANES - QK_ROPE), F32)
    ct = jnp.concatenate([cos, cos, pad], axis=1)
    s1 = jnp.concatenate([-sin, z, pad], axis=1)
    s2 = jnp.concatenate([z, sin, pad], axis=1)
    return ct, s1, s2


def _pad_lanes(a, width):
    return jnp.pad(a, [(0, 0)] * (a.ndim - 1) + [(0, width - a.shape[-1])])


def _layer_weights(norm_mix, w_in, q_lora_norm, w_uq, q_nope_norm, q_rope_norm, kv_lora_norm, k_rope_norm,
                   w_ukv, k_nope_norm, w_o_attn, w_glu_v, w_glu_g, w_out, norm_mlp, w_mlp_up, w_mlp_down):
    o_kv = Q_LORA
    o_kr = o_kv + KV_LORA
    o_ssm = o_kr + QK_ROPE
    o_ga = o_ssm + D_MODEL
    o_gb = o_ga + D_MODEL
    bf = lambda a: a.astype(BF16)
    r2 = lambda a: a.reshape(1, -1).astype(F32)
    return {
        "nmix": r2(norm_mix),
        "wq": bf(w_in[:, :o_kv]), "wkv": bf(w_in[:, o_kv:o_kr]),
        "wkr": bf(_pad_lanes(w_in[:, o_kr:o_ssm], LANES)),
        "wu": bf(w_in[:, o_ssm:o_ga]), "wga": bf(w_in[:, o_ga:o_gb]), "wgb": bf(w_in[:, o_gb:]),
        "gq": r2(q_lora_norm),
        "wuqn": bf(w_uq[:, :, :QK_NOPE].reshape(Q_LORA, N_HEADS * QK_NOPE)),
        "wuqr": bf(_pad_lanes(w_uq[:, :, QK_NOPE:], LANES).reshape(Q_LORA, N_HEADS * LANES)),
        "gqn": r2(q_nope_norm), "gqr": _pad_lanes(r2(q_rope_norm), LANES),
        "gkv": r2(kv_lora_norm), "gkr": _pad_lanes(r2(k_rope_norm), LANES),
        "wuk": bf(w_ukv[:, :, :QK_NOPE].reshape(KV_LORA, N_HEADS * QK_NOPE)),
        "wuv": bf(w_ukv[:, :, QK_NOPE:].reshape(KV_LORA, N_HEADS * V_DIM)),
        "gkn": r2(k_nope_norm),
        "wo": bf(w_o_attn.reshape(N_HEADS * V_DIM, D_MODEL)),
        "wv": bf(w_glu_v), "wg": bf(w_glu_g), "wout": bf(w_out),
        "gmlp": r2(norm_mlp), "wup": bf(w_mlp_up), "wdn": bf(w_mlp_down),
    }


def kernel(x_prompt, x_sample, cache_latent, cache_krope, cache_meta_latent, cache_meta_krope, state_ssm_re, state_ssm_im, meta_tokens, norm_mix, w_in, q_lora_norm, w_uq, q_nope_norm, q_rope_norm, kv_lora_norm, k_rope_norm, w_ukv, k_nope_norm, w_o_attn, ssm_a_re, ssm_a_im, ssm_log_dt, ssm_b_re, ssm_b_im, ssm_c_re, ssm_c_im, ssm_d, w_glu_v, w_glu_g, w_out, norm_mlp, w_mlp_up, w_mlp_down):
    bsz_p, seq_p = x_prompt.shape[0], x_prompt.shape[1]
    bsz_s, seq_s = x_sample.shape[0], x_sample.shape[1]
    past = cache_latent.shape[2]
    depth = w_in.shape[0]
    assert depth == 1, "single-layer step"
    l = 0
    w = _layer_weights(norm_mix[l], w_in[l], q_lora_norm[l], w_uq[l], q_nope_norm[l], q_rope_norm[l],
                       kv_lora_norm[l], k_rope_norm[l], w_ukv[l], k_nope_norm[l], w_o_attn[l], w_glu_v[l],
                       w_glu_g[l], w_out[l], norm_mlp[l], w_mlp_up[l], w_mlp_down[l])
    sw = _ssm_weights(ssm_a_re[l].astype(F32), ssm_a_im[l].astype(F32), ssm_log_dt[l].astype(F32),
                      ssm_b_re[l].astype(F32), ssm_b_im[l].astype(F32), ssm_c_re[l].astype(F32),
                      ssm_c_im[l].astype(F32), ssm_d[l].astype(F32))
    sq = 2 * SSM_SW

    tabs_m = _rope_tables(jnp.arange(N_META, dtype=jnp.int32) - N_META)
    (_, mk, mv, mckv, mkr, mu, _, _) = _front(meta_tokens.astype(F32), tabs_m, 1, w, N_META)
    h0_zero = jnp.zeros((SSM_NJ, 1, sq), F32)
    _, h_meta = _ssm_prompt(mu, sw, h0_zero, 1, N_META // SSM_T, N_META // SSM_T)

    n_p = bsz_p * seq_p
    tm = 256
    xp = x_prompt.reshape(n_p, D_MODEL)
    tabs_p = _rope_tables(jnp.arange(seq_p, dtype=jnp.int32))
    q, k, v, ckv_p, kr_p, u_p, ga, gb = _front(xp, tabs_p, seq_p // tm, w, tm)
    o_p = _attn_prompt(q, k, v, mk, mv, bsz_p, seq_p)
    h0_p = jnp.broadcast_to(h_meta, (SSM_NJ, bsz_p, sq))
    s_p, hfin_p = _ssm_prompt(u_p, sw, h0_p, bsz_p, seq_p // SSM_T, 128)
    h1_p = _mixer(o_p, s_p, ga, gb, xp, w, tm)
    y_p = _mlp(h1_p, w, tm)

    n_s = bsz_s * seq_s
    xs = x_sample.reshape(n_s, D_MODEL)
    pos_s = past + jnp.arange(seq_s, dtype=jnp.int32)
    tabs_s = tuple(jnp.tile(t, (bsz_s, 1)) for t in _rope_tables(pos_s))
    tm_s = 256
    q_s, k_s, v_s, ckv_s, kr_s, u_s, ga_s, gb_s = _front(xs, tabs_s, n_s // tm_s, w, tm_s)
    ckrb = _pad_lanes(cache_krope[l].reshape(bsz_s * past, QK_ROPE), LANES).astype(BF16)
    cmkrb = _pad_lanes(cache_meta_krope[l].reshape(bsz_s * N_META, QK_ROPE), LANES).astype(BF16)
    ck, cv = _expand(cache_latent[l].reshape(bsz_s * past, KV_LORA).astype(F32), ckrb, w, 512)
    cmk, cmv = _expand(cache_meta_latent[l].reshape(bsz_s * N_META, KV_LORA).astype(F32), cmkrb, w,
                       bsz_s * N_META)
    o_s = _attn_sample(q_s, ck, cv, cmk, cmv, k_s, v_s, bsz_s, seq_s, past)

    def to_blocks(st):
        return st.astype(F32).reshape(bsz_s, SSM_NJ, SSM_SW).transpose(1, 0, 2)

    h0_s = jnp.concatenate([to_blocks(state_ssm_re[l]), to_blocks(state_ssm_im[l])], axis=2)
    s_s, hfin_s = _ssm_sample(u_s, sw, h0_s, bsz_s, seq_s // SSM_T)
    h1_s = _mixer(o_s, s_s, ga_s, gb_s, xs, w, tm_s)
    y_s = _mlp(h1_s, w, tm_s)

    def from_blocks(hf, nb):
        re = hf[:, :, :SSM_SW].transpose(1, 0, 2).reshape(1, nb, N_GROUPS, SSM_STATE)
        im = hf[:, :, SSM_SW:].transpose(1, 0, 2).reshape(1, nb, N_GROUPS, SSM_STATE)
        return re, im

    sre_p, sim_p = from_blocks(hfin_p, bsz_p)
    sre_s, sim_s = from_blocks(hfin_s, bsz_s)
    return (y_p.reshape(bsz_p, seq_p, D_MODEL), y_s.reshape(bsz_s, seq_s, D_MODEL),
            ckv_p.reshape(1, bsz_p, seq_p, KV_LORA), kr_p.reshape(1, bsz_p, seq_p, QK_ROPE),
            jnp.broadcast_to(mckv[None, None], (1, bsz_p, N_META, KV_LORA)),
            jnp.broadcast_to(mkr[None, None], (1, bsz_p, N_META, QK_ROPE)),
            sre_p, sim_p,
            ckv_s.reshape(1, bsz_s, seq_s, KV_LORA), kr_s.reshape(1, bsz_s, seq_s, QK_ROPE),
            sre_s, sim_s)
```

```python
import functools
import math

import jax
import jax.numpy as jnp
from jax import lax
from jax.experimental import pallas as pl
from jax.experimental.pallas import tpu as pltpu

F32 = jnp.float32
BF16 = jnp.bfloat16

D_MODEL = 1024
CHUNK = 64
N_META = 16
N_HEADS = 8
QK_NOPE = 128
QK_ROPE = 64
V_DIM = 128
QK_DIM = QK_NOPE + QK_ROPE
Q_LORA = 384
KV_LORA = 256
SSM_GROUP = 16
N_GROUPS = D_MODEL // SSM_GROUP
SSM_STATE = 64
D_FF = 4 * D_MODEL
ROPE_THETA = 10000.0
EPS = 1e-6
ATTN_SCALE = QK_DIM ** -0.5
Q_SCALE = ATTN_SCALE * math.log2(math.e)
NEG_INF = -1e30

LANES = 128
HEAD_PAD = 2 * LANES
SSM_T = 8
SSM_NJ = D_MODEL // LANES
SSM_GPB = LANES // SSM_GROUP
SSM_SW = SSM_GPB * SSM_STATE
VMEM_LIMIT = 56 * 1024 * 1024


def _cp(sem):
    return pltpu.CompilerParams(dimension_semantics=sem, vmem_limit_bytes=VMEM_LIMIT)


def _dot(a, b):
    return jnp.dot(a, b, preferred_element_type=F32)


def _dot_nt(a, b):
    return lax.dot_general(a, b, (((1,), (1,)), ((), ())), preferred_element_type=F32)


def _rms(x, g, n=None):
    n = x.shape[-1] if n is None else n
    ms = jnp.sum(x * x, axis=-1, keepdims=True) * (1.0 / n)
    return x * lax.rsqrt(ms + EPS) * g


def _rope128(b, ct, s1, s2):
    return b * ct + pltpu.roll(b, LANES - QK_ROPE // 2, 1) * s1 + pltpu.roll(b, QK_ROPE // 2, 1) * s2


def _front_kernel(x_ref, ct_ref, s1_ref, s2_ref, nmix_ref, wq_ref, wkv_ref, wkr_ref, wu_ref, wga_ref,
                  wgb_ref, gq_ref, wuqn_ref, wuqr_ref, gqn_ref, gqr_ref, gkv_ref, gkr_ref, wuk_ref,
                  wuv_ref, gkn_ref,
                  q_ref, k_ref, v_ref, ckv_ref, kr_ref, u_ref, ga_ref, gb_ref, u_sc):
    x = x_ref[...]
    xn = _rms(x, nmix_ref[...]).astype(BF16)
    ct, s1, s2 = ct_ref[...], s1_ref[...], s2_ref[...]

    u = _dot(xn, wu_ref[...])
    nchunk = u_sc.shape[1] // SSM_T
    for j in range(SSM_NJ):
        u_sc[j] = u[:, j * LANES:(j + 1) * LANES]
        for t in range(SSM_T):
            u_ref[j, :, t * LANES:(t + 1) * LANES] = u_sc[j, pl.ds(t, nchunk, stride=SSM_T), :]
    ga_ref[...] = jax.nn.sigmoid(_dot(xn, wga_ref[...]))
    gb_ref[...] = jax.nn.sigmoid(_dot(xn, wgb_ref[...]))

    c_kv = _rms(_dot(xn, wkv_ref[...]), gkv_ref[...])
    ckv_ref[...] = c_kv
    kr = _rms(_dot(xn, wkr_ref[...]), gkr_ref[...], QK_ROPE)
    kr = _rope128(kr, ct, s1, s2)
    kr_ref[...] = kr[:, :QK_ROPE]
    kr16 = kr.astype(BF16)
    c16 = c_kv.astype(BF16)
    kn = _dot(c16, wuk_ref[...])
    for h in range(N_HEADS):
        sl = slice(h * LANES, (h + 1) * LANES)
        k_ref[:, h * HEAD_PAD:h * HEAD_PAD + LANES] = _rms(kn[:, sl], gkn_ref[...]).astype(BF16)
        k_ref[:, h * HEAD_PAD + LANES:(h + 1) * HEAD_PAD] = kr16
    v_ref[...] = _dot(c16, wuv_ref[...]).astype(BF16)

    c_q = _rms(_dot(xn, wq_ref[...]), gq_ref[...]).astype(BF16)
    qn = _dot(c_q, wuqn_ref[...])
    qr = _dot(c_q, wuqr_ref[...])
    for h in range(N_HEADS):
        sl = slice(h * LANES, (h + 1) * LANES)
        q_ref[:, h * HEAD_PAD:h * HEAD_PAD + LANES] = (_rms(qn[:, sl], gqn_ref[...]) * Q_SCALE).astype(BF16)
        r = _rope128(_rms(qr[:, sl], gqr_ref[...], QK_ROPE), ct, s1, s2)
        q_ref[:, h * HEAD_PAD + LANES:(h + 1) * HEAD_PAD] = (r * Q_SCALE).astype(BF16)


def _front(x, tabs, tab_blocks, w, tm):
    n = x.shape[0]
    nt = n // tm
    ct, s1, s2 = tabs

    def row(i):
        return (i, 0)

    def tab(i):
        return (i % tab_blocks, 0)

    def const(i):
        return (0, 0)

    def full(a):
        return pl.BlockSpec(a.shape, const)

    weights = [w["nmix"], w["wq"], w["wkv"], w["wkr"], w["wu"], w["wga"], w["wgb"], w["gq"], w["wuqn"],
               w["wuqr"], w["gqn"], w["gqr"], w["gkv"], w["gkr"], w["wuk"], w["wuv"], w["gkn"]]
    wide = N_HEADS * LANES
    qk_wide = N_HEADS * HEAD_PAD
    out_shape = [
        jax.ShapeDtypeStruct((n, qk_wide), BF16),
        jax.ShapeDtypeStruct((n, qk_wide), BF16),
        jax.ShapeDtypeStruct((n, wide), BF16),
        jax.ShapeDtypeStruct((n, KV_LORA), F32),
        jax.ShapeDtypeStruct((n, QK_ROPE), F32),
        jax.ShapeDtypeStruct((SSM_NJ, n // SSM_T, SSM_T * LANES), F32),
        jax.ShapeDtypeStruct((n, D_MODEL), F32),
        jax.ShapeDtypeStruct((n, D_MODEL), F32),
    ]
    out_specs = [
        pl.BlockSpec((tm, qk_wide), row), pl.BlockSpec((tm, qk_wide), row),
        pl.BlockSpec((tm, wide), row), pl.BlockSpec((tm, KV_LORA), row),
        pl.BlockSpec((tm, QK_ROPE), row),
        pl.BlockSpec((SSM_NJ, tm // SSM_T, SSM_T * LANES), lambda i: (0, i, 0)),
        pl.BlockSpec((tm, D_MODEL), row), pl.BlockSpec((tm, D_MODEL), row),
    ]
    in_specs = ([pl.BlockSpec((tm, D_MODEL), row)] + [pl.BlockSpec((tm, LANES), tab)] * 3
                + [full(a) for a in weights])
    return pl.pallas_call(
        _front_kernel, grid=(nt,), in_specs=in_specs, out_specs=out_specs, out_shape=out_shape,
        scratch_shapes=[pltpu.VMEM((SSM_NJ, tm, LANES), F32)],
        compiler_params=_cp(("arbitrary",)), name="front",
    )(x, ct, s1, s2, *weights)


def _expand_kernel(c_ref, kr_ref, wuk_ref, wuv_ref, gkn_ref, k_ref, v_ref):
    c16 = c_ref[...].astype(BF16)
    kr16 = kr_ref[...]
    kn = _dot(c16, wuk_ref[...])
    for h in range(N_HEADS):
        sl = slice(h * LANES, (h + 1) * LANES)
        k_ref[:, h * HEAD_PAD:h * HEAD_PAD + LANES] = _rms(kn[:, sl], gkn_ref[...]).astype(BF16)
        k_ref[:, h * HEAD_PAD + LANES:(h + 1) * HEAD_PAD] = kr16
    v_ref[...] = _dot(c16, wuv_ref[...]).astype(BF16)


def _expand(c, krb, w, tm):
    n = c.shape[0]
    wide = N_HEADS * LANES
    qk_wide = N_HEADS * HEAD_PAD
    row = lambda i: (i, 0)
    const = lambda i: (0, 0)
    return pl.pallas_call(
        _expand_kernel, grid=(n // tm,),
        in_specs=[pl.BlockSpec((tm, KV_LORA), row), pl.BlockSpec((tm, LANES), row),
                  pl.BlockSpec(w["wuk"].shape, const),
                  pl.BlockSpec(w["wuv"].shape, const), pl.BlockSpec(w["gkn"].shape, const)],
        out_specs=[pl.BlockSpec((tm, qk_wide), row), pl.BlockSpec((tm, wide), row)],
        out_shape=[jax.ShapeDtypeStruct((n, qk_wide), BF16), jax.ShapeDtypeStruct((n, wide), BF16)],
        compiler_params=_cp(("arbitrary",)), name="expand",
    )(c, krb, w["wuk"], w["wuv"], w["gkn"])


ATT_T = 512


def _with_ones(v):
    return jnp.concatenate([v, jnp.ones_like(v)], axis=1)


def _softmax_step(s, v, m_ref, acc_ref):
    m_prev = m_ref[...]
    m_new = jnp.maximum(m_prev, jnp.max(s, axis=-1, keepdims=True))
    alpha = jnp.exp2(m_prev - m_new)
    p = jnp.exp2(s - jnp.tile(m_new, (1, s.shape[1] // LANES)))
    acc_ref[...] = jnp.tile(alpha, (1, 2)) * acc_ref[...] + _dot(p.astype(BF16), _with_ones(v))
    m_ref[...] = m_new


ATT_HG = 2


def _attn_prompt_kernel(q_ref, k_ref, v_ref, mk_ref, mv_ref, o_ref, a_sc, b_sc, m_ref, acc_ref):
    i = pl.program_id(2)
    heads = range(ATT_HG)

    def key_rows(kt):
        return pl.ds(pl.multiple_of(kt * ATT_T, ATT_T), ATT_T)

    def qk_lanes(g):
        return slice(g * HEAD_PAD, (g + 1) * HEAD_PAD)

    def v_lanes(g):
        return slice(g * V_DIM, (g + 1) * V_DIM)

    def scores(g, kt):
        return _dot_nt(q_ref[:, qk_lanes(g)], k_ref[key_rows(kt), qk_lanes(g)])

    def consume(g, buf, kt, diagonal):
        s = buf[g]
        if diagonal:
            qc = lax.broadcasted_iota(jnp.int32, (ATT_T, ATT_T), 0) // CHUNK
            kc = lax.broadcasted_iota(jnp.int32, (ATT_T, ATT_T), 1) // CHUNK
            s = jnp.where(kc <= qc, s, NEG_INF)
        _softmax_step(s, v_ref[key_rows(kt), v_lanes(g)], m_ref.at[g], acc_ref.at[g])

    for g in heads:
        a_sc[g] = scores(g, 0)

    for g in heads:
        s = _dot_nt(q_ref[:, qk_lanes(g)], mk_ref[:, qk_lanes(g)])
        m0 = jnp.max(s, axis=-1, keepdims=True)
        p = jnp.exp2(s - m0)
        m_ref[g] = jnp.broadcast_to(m0, (ATT_T, LANES))
        acc_ref[g] = _dot(p.astype(BF16), _with_ones(mv_ref[:, v_lanes(g)]))

    def pair(pi, carry):
        kt = 2 * pi
        for g in heads:
            b_sc[g] = scores(g, kt + 1)
            consume(g, a_sc, kt, False)
        for g in heads:
            a_sc[g] = scores(g, kt + 2)
            consume(g, b_sc, kt + 1, False)
        return carry

    lax.fori_loop(0, lax.shift_right_logical(i, 1), pair, 0)

    @pl.when((i & 1) == 1)
    def _():
        for g in heads:
            b_sc[g] = scores(g, i)
            consume(g, a_sc, i - 1, False)
        for g in heads:
            consume(g, b_sc, i, True)

    @pl.when((i & 1) == 0)
    def _():
        for g in heads:
            consume(g, a_sc, i, True)

    for g in heads:
        o_ref[:, v_lanes(g)] = (acc_ref[g, :, :V_DIM] / acc_ref[g, :, V_DIM:]).astype(BF16)


def _attn_prompt(q, k, v, mk, mv, bsz, seq):
    nq = seq // ATT_T
    n = bsz * seq
    qmap = lambda b, h, i: (b * nq + i, h)
    return pl.pallas_call(
        _attn_prompt_kernel, grid=(bsz, N_HEADS // ATT_HG, nq),
        in_specs=[pl.BlockSpec((ATT_T, ATT_HG * HEAD_PAD), qmap),
                  pl.BlockSpec((seq, ATT_HG * HEAD_PAD), lambda b, h, i: (b, h)),
                  pl.BlockSpec((seq, ATT_HG * V_DIM), lambda b, h, i: (b, h)),
                  pl.BlockSpec((N_META, ATT_HG * HEAD_PAD), lambda b, h, i: (0, h)),
                  pl.BlockSpec((N_META, ATT_HG * V_DIM), lambda b, h, i: (0, h))],
        out_specs=pl.BlockSpec((ATT_T, ATT_HG * V_DIM), qmap),
        out_shape=jax.ShapeDtypeStruct((n, N_HEADS * V_DIM), BF16),
        scratch_shapes=[pltpu.VMEM((ATT_HG, ATT_T, ATT_T), F32), pltpu.VMEM((ATT_HG, ATT_T, ATT_T), F32),
                        pltpu.VMEM((ATT_HG, ATT_T, LANES), F32), pltpu.VMEM((ATT_HG, ATT_T, 2 * V_DIM), F32)],
        compiler_params=_cp(("arbitrary", "arbitrary", "arbitrary")), name="attn_prompt",
    )(q, k, v, mk, mv)


def _attn_sample_kernel(q_ref, ck_ref, cv_ref, mk_ref, mv_ref, nk_ref, nv_ref, o_ref):
    q = q_ref[...]
    s_c = _dot_nt(q, ck_ref[...])
    s_m = _dot_nt(q, mk_ref[...])
    s_n = _dot_nt(q, nk_ref[...])
    m = jnp.maximum(jnp.max(s_c, axis=-1, keepdims=True),
                    jnp.maximum(jnp.max(s_m, axis=-1, keepdims=True), jnp.max(s_n, axis=-1, keepdims=True)))
    p_c, p_m, p_n = jnp.exp2(s_c - m), jnp.exp2(s_m - m), jnp.exp2(s_n - m)
    l = (jnp.sum(p_c, axis=-1, keepdims=True) + jnp.sum(p_m, axis=-1, keepdims=True)
         + jnp.sum(p_n, axis=-1, keepdims=True))
    acc = (_dot(p_c.astype(BF16), cv_ref[...]) + _dot(p_m.astype(BF16), mv_ref[...])
           + _dot(p_n.astype(BF16), nv_ref[...]))
    o_ref[...] = (acc / l).astype(BF16)


def _attn_sample(q, ck, cv, mk, mv, nk, nv, bsz, seq, past):
    bh = lambda b, h: (b, h)
    return pl.pallas_call(
        _attn_sample_kernel, grid=(bsz, N_HEADS),
        in_specs=[pl.BlockSpec((seq, HEAD_PAD), bh),
                  pl.BlockSpec((past, HEAD_PAD), bh), pl.BlockSpec((past, V_DIM), bh),
                  pl.BlockSpec((N_META, HEAD_PAD), bh), pl.BlockSpec((N_META, V_DIM), bh),
                  pl.BlockSpec((seq, HEAD_PAD), bh), pl.BlockSpec((seq, V_DIM), bh)],
        out_specs=pl.BlockSpec((seq, V_DIM), bh),
        out_shape=jax.ShapeDtypeStruct((bsz * seq, N_HEADS * V_DIM), BF16),
        compiler_params=_cp(("arbitrary", "arbitrary")), name="attn_sample",
    )(q, ck, cv, mk, mv, nk, nv)


MXU_DIM = 256


def _dot_causal(u16, m_ref):
    nblk = m_ref.shape[0] // MXU_DIM
    return jnp.concatenate(
        [_dot(u16[:, :(cb + 1) * MXU_DIM], m_ref[:(cb + 1) * MXU_DIM, cb * MXU_DIM:(cb + 1) * MXU_DIM])
         for cb in range(nblk)], axis=1)


SSM_PACK = 4
SSM_SC = SSM_SW // LANES
SUBLANES = 2 * SSM_PACK


def _ssm_prompt_kernel(u_ref, m_ref, p_ref, q_ref, lam_ref, d_ref, h0_ref, s_ref, hfin_ref, s_sc, h_sc,
                       *, nb, rt):
    t = pl.program_id(1)

    @pl.when(t == 0)
    def _():
        h_sc[...] = h0_ref[...]

    if nb < SSM_PACK:
        s_sc[...] = jnp.zeros_like(s_sc)
    for b in range(nb):
        sb = _dot(u_ref[b].astype(BF16), p_ref[...])
        for c in range(SSM_SC):
            s_sc[c, pl.ds(b, rt, stride=SUBLANES), :] = sb[:, c * LANES:(c + 1) * LANES]
            s_sc[c, pl.ds(SSM_PACK + b, rt, stride=SUBLANES), :] = sb[:, SSM_SW + c * LANES:SSM_SW + (c + 1) * LANES]

    lam = lam_ref[...]
    im_rows = lax.broadcasted_iota(jnp.int32, (SUBLANES, LANES), 0) >= SSM_PACK
    coef = []
    for c in range(SSM_SC):
        lr = jnp.broadcast_to(lam[:, c * LANES:(c + 1) * LANES], (SUBLANES, LANES))
        li = jnp.broadcast_to(lam[:, SSM_SW + c * LANES:SSM_SW + (c + 1) * LANES], (SUBLANES, LANES))
        coef.append((lr, jnp.where(im_rows, li, -li)))

    def step(r, hs):
        rows = pl.ds(pl.multiple_of(r * SUBLANES, SUBLANES), SUBLANES)
        out = []
        for c in range(SSM_SC):
            s = s_sc[c, rows, :]
            s_sc[c, rows, :] = hs[c]
            a, bb = coef[c]
            out.append(a * hs[c] + bb * pltpu.roll(hs[c], SSM_PACK, 0) + s)
        return tuple(out)

    hs = lax.fori_loop(0, rt, step, tuple(h_sc[c] for c in range(SSM_SC)))
    for c in range(SSM_SC):
        h_sc[c] = hs[c]

    for b in range(nb):
        ub = u_ref[b]
        hb = jnp.concatenate([s_sc[c, pl.ds(b, rt, stride=SUBLANES), :] for c in range(SSM_SC)]
                             + [s_sc[c, pl.ds(SSM_PACK + b, rt, stride=SUBLANES), :] for c in range(SSM_SC)],
                             axis=1)
        y = _dot_causal(ub.astype(BF16), m_ref) + _dot(hb.astype(BF16), q_ref[...]) + ub * d_ref[...]
        s_ref[b] = jax.nn.gelu(y).astype(BF16)

    @pl.when(t == pl.num_programs(1) - 1)
    def _():
        hfin_ref[...] = h_sc[...]


def _pack_state(h):
    nj, nb, _ = h.shape
    h = jnp.pad(h, ((0, 0), (0, SSM_PACK - nb), (0, 0)))
    re = h[:, :, :SSM_SW].reshape(nj, SSM_PACK, SSM_SC, LANES)
    im = h[:, :, SSM_SW:].reshape(nj, SSM_PACK, SSM_SC, LANES)
    return jnp.concatenate([re, im], axis=1).transpose(0, 2, 1, 3)


def _unpack_state(hp, nb):
    nj = hp.shape[0]
    x = hp.transpose(0, 2, 1, 3).reshape(nj, SUBLANES, SSM_SW)
    return jnp.concatenate([x[:, :nb], x[:, SSM_PACK:SSM_PACK + nb]], axis=2)


def _ssm_prompt(u, sw, h0, nb, lc, rt):
    assert nb <= SSM_PACK
    tw = SSM_T * LANES
    u4 = u.reshape(SSM_NJ, nb, lc, tw)
    sq = 2 * SSM_SW
    kern = functools.partial(_ssm_prompt_kernel, nb=nb, rt=rt)
    st_spec = pl.BlockSpec((None, SSM_SC, SUBLANES, LANES), lambda j, t: (j, 0, 0, 0))
    s, hfin = pl.pallas_call(
        kern, grid=(SSM_NJ, lc // rt),
        in_specs=[pl.BlockSpec((None, nb, rt, tw), lambda j, t: (j, 0, t, 0)),
                  pl.BlockSpec((None, tw, tw), lambda j, t: (j, 0, 0)),
                  pl.BlockSpec((None, tw, sq), lambda j, t: (j, 0, 0)),
                  pl.BlockSpec((None, sq, tw), lambda j, t: (j, 0, 0)),
                  pl.BlockSpec((None, 1, sq), lambda j, t: (j, 0, 0)),
                  pl.BlockSpec((None, 1, tw), lambda j, t: (j, 0, 0)),
                  st_spec],
        out_specs=[pl.BlockSpec((None, nb, rt, tw), lambda j, t: (j, 0, t, 0)), st_spec],
        out_shape=[jax.ShapeDtypeStruct((SSM_NJ, nb, lc, tw), BF16),
                   jax.ShapeDtypeStruct((SSM_NJ, SSM_SC, SUBLANES, LANES), F32)],
        scratch_shapes=[pltpu.VMEM((SSM_SC, rt * SUBLANES, LANES), F32),
                        pltpu.VMEM((SSM_SC, SUBLANES, LANES), F32)],
        compiler_params=_cp(("arbitrary", "arbitrary")), name="ssm_prompt",
    )(u4, sw["M"], sw["P"], sw["Q"], sw["lam"], sw["D"], _pack_state(h0))
    return s.reshape(SSM_NJ, nb * lc, tw), _unpack_state(hfin, nb)


def _ssm_sample_kernel(u_ref, m_ref, p_ref, q_ref, lam_ref, d_ref, h0_ref, s_ref, hfin_ref, s_sc,
                       *, nb, lc):
    u = u_ref[...]
    u16 = u.astype(BF16)
    s_sc[...] = _dot(u16, p_ref[...])
    lam = lam_ref[...]
    lr, li = lam[:, :SSM_SW], lam[:, SSM_SW:]
    h = h0_ref[...]
    for r in range(lc):
        rows = slice(r * nb, (r + 1) * nb)
        s = s_sc[rows, :]
        s_sc[rows, :] = h
        hre, him = h[:, :SSM_SW], h[:, SSM_SW:]
        h = jnp.concatenate([lr * hre - li * him + s[:, :SSM_SW], lr * him + li * hre + s[:, SSM_SW:]], axis=1)
    hfin_ref[...] = h
    y = _dot_causal(u16, m_ref) + _dot(s_sc[...].astype(BF16), q_ref[...]) + u * d_ref[...]
    s_ref[...] = jax.nn.gelu(y).astype(BF16)


def _ssm_sample(u, sw, h0, nb, lc):
    tw = SSM_T * LANES
    sq = 2 * SSM_SW
    u2 = u.reshape(SSM_NJ, nb, lc, tw).transpose(0, 2, 1, 3).reshape(SSM_NJ, lc * nb, tw)
    kern = functools.partial(_ssm_sample_kernel, nb=nb, lc=lc)
    blk = lambda j: (j, 0, 0)
    s, hfin = pl.pallas_call(
        kern, grid=(SSM_NJ,),
        in_specs=[pl.BlockSpec((None, lc * nb, tw), blk), pl.BlockSpec((None, tw, tw), blk),
                  pl.BlockSpec((None, tw, sq), blk), pl.BlockSpec((None, sq, tw), blk),
                  pl.BlockSpec((None, 1, sq), blk), pl.BlockSpec((None, 1, tw), blk),
                  pl.BlockSpec((None, nb, sq), blk)],
        out_specs=[pl.BlockSpec((None, lc * nb, tw), blk), pl.BlockSpec((None, nb, sq), blk)],
        out_shape=[jax.ShapeDtypeStruct((SSM_NJ, lc * nb, tw), BF16),
                   jax.ShapeDtypeStruct((SSM_NJ, nb, sq), F32)],
        scratch_shapes=[pltpu.VMEM((lc * nb, sq), F32)],
        compiler_params=_cp(("arbitrary",)), name="ssm_sample",
    )(u2, sw["M"], sw["P"], sw["Q"], sw["lam"], sw["D"], h0)
    s = s.reshape(SSM_NJ, lc, nb, tw).transpose(0, 2, 1, 3).reshape(SSM_NJ, nb * lc, tw)
    return s, hfin


def _ssm_weights(a_re, a_im, log_dt, b_re, b_im, c_re, c_im, d):
    hi = lax.Precision.HIGHEST
    t_ = SSM_T
    dt = jnp.exp(log_dt)[:, None]
    mag = jnp.exp(a_re * dt)
    lam_re, lam_im = mag * jnp.cos(a_im * dt), mag * jnp.sin(a_im * dt)
    den = a_re * a_re + a_im * a_im
    f_re = ((lam_re - 1.0) * a_re + lam_im * a_im) / den
    f_im = (lam_im * a_re - (lam_re - 1.0) * a_im) / den
    w_re = f_re[:, :, None] * b_re - f_im[:, :, None] * b_im
    w_im = f_re[:, :, None] * b_im + f_im[:, :, None] * b_re
    k = jnp.arange(t_ + 1, dtype=F32)[None, :, None]
    pmag = jnp.exp(a_re[:, None, :] * dt[:, :, None] * k)
    pang = a_im[:, None, :] * dt[:, :, None] * k
    pw_re, pw_im = pmag * jnp.cos(pang), pmag * jnp.sin(pang)
    v_re = pw_re[:, :t_, :, None] * w_re[:, None] - pw_im[:, :t_, :, None] * w_im[:, None]
    v_im = pw_re[:, :t_, :, None] * w_im[:, None] + pw_im[:, :t_, :, None] * w_re[:, None]
    kmat = (jnp.einsum("gcn,gjnd->gjcd", c_re, v_re, precision=hi)
            - jnp.einsum("gcn,gjnd->gjcd", c_im, v_im, precision=hi))
    eye = jnp.eye(SSM_GPB, dtype=F32)

    def blockdiag(x):
        x = jnp.moveaxis(x, 1, -3)
        y = x[..., :, :, None, :] * eye[:, None, :, None]
        return y.reshape(x.shape[:-3] + (SSM_GPB * x.shape[-2], SSM_GPB * x.shape[-1]))

    def per_block(x):
        return x.reshape((SSM_NJ, SSM_GPB) + x.shape[1:])

    bd_k = blockdiag(per_block(jnp.swapaxes(kmat, 2, 3))).astype(BF16)
    zero = jnp.zeros((SSM_NJ, LANES, LANES), BF16)
    m = jnp.concatenate(
        [jnp.concatenate([zero] * s + [bd_k[:, t - s] for t in range(s, t_)], axis=2) for s in range(t_)],
        axis=1)
    pl_re = jnp.swapaxes(per_block(pw_re), 1, 2).reshape(SSM_NJ, t_ + 1, SSM_SW)
    pl_im = jnp.swapaxes(per_block(pw_im), 1, 2).reshape(SSM_NJ, t_ + 1, SSM_SW)
    wb_re = blockdiag(per_block(jnp.swapaxes(w_re, 1, 2)))
    wb_im = blockdiag(per_block(jnp.swapaxes(w_im, 1, 2)))
    p_rows = []
    for s in range(t_):
        lr, li = pl_re[:, t_ - 1 - s][:, None, :], pl_im[:, t_ - 1 - s][:, None, :]
        p_rows.append(jnp.concatenate([lr * wb_re - li * wb_im, lr * wb_im + li * wb_re], axis=2).astype(BF16))
    p = jnp.concatenate(p_rows, axis=1)
    cb_re = blockdiag(per_block(jnp.swapaxes(c_re, 1, 2)))
    cb_im = blockdiag(per_block(jnp.swapaxes(c_im, 1, 2)))
    q_cols = []
    for t in range(t_):
        lr, li = pl_re[:, t + 1][:, :, None], pl_im[:, t + 1][:, :, None]
        q_cols.append(jnp.concatenate([cb_re * lr - cb_im * li, -(cb_re * li + cb_im * lr)], axis=1).astype(BF16))
    q = jnp.concatenate(q_cols, axis=2)
    lam_t = jnp.concatenate([pl_re[:, t_][:, None, :], pl_im[:, t_][:, None, :]], axis=2)
    dvec = jnp.tile(d.reshape(SSM_NJ, 1, LANES), (1, 1, t_))
    return {"M": m, "P": p, "Q": q, "lam": lam_t, "D": dvec}


def _mixer_kernel(o_ref, s_ref, ga_ref, gb_ref, x_ref, wo_ref, wv_ref, wg_ref, wout_ref, h_ref, s_sc):
    o_a = _dot(o_ref[...], wo_ref[...])
    nchunk = s_sc.shape[1] // SSM_T
    for j in range(SSM_NJ):
        sj = s_ref[j].astype(F32)
        for t in range(SSM_T):
            s_sc[j, pl.ds(t, nchunk, stride=SSM_T), :] = sj[:, t * LANES:(t + 1) * LANES]
    s = jnp.concatenate([s_sc[j] for j in range(SSM_NJ)], axis=1).astype(BF16)
    o_b = _dot(s, wv_ref[...]) * jax.nn.sigmoid(_dot(s, wg_ref[...]))
    merged = ga_ref[...] * o_a + gb_ref[...] * o_b
    h_ref[...] = x_ref[...] + _dot(merged.astype(BF16), wout_ref[...])


def _mixer(o, s, ga, gb, x, w, tm):
    n = x.shape[0]
    row = lambda i: (i, 0)
    const = lambda i: (0, 0)
    wspec = pl.BlockSpec((D_MODEL, D_MODEL), const)
    act = pl.BlockSpec((tm, D_MODEL), row)
    return pl.pallas_call(
        _mixer_kernel, grid=(n // tm,),
        in_specs=[act, pl.BlockSpec((SSM_NJ, tm // SSM_T, SSM_T * LANES), lambda i: (0, i, 0)), act, act, act,
                  wspec, wspec, wspec, wspec],
        out_specs=act, out_shape=jax.ShapeDtypeStruct((n, D_MODEL), F32),
        scratch_shapes=[pltpu.VMEM((SSM_NJ, tm, LANES), F32)],
        compiler_params=_cp(("arbitrary",)), name="mixer_out",
    )(o, s, ga, gb, x, w["wo"], w["wv"], w["wg"], w["wout"])


def _mlp_kernel(h_ref, g_ref, wup_ref, wdn_ref, y_ref):
    h = h_ref[...]
    hn = _rms(h, g_ref[...]).astype(BF16)
    a = jnp.maximum(_dot(hn, wup_ref[...]), 0.0)
    y_ref[...] = h + _dot((a * a).astype(BF16), wdn_ref[...])


def _mlp(h, w, tm):
    n = h.shape[0]
    row = lambda i: (i, 0)
    const = lambda i: (0, 0)
    act = pl.BlockSpec((tm, D_MODEL), row)
    return pl.pallas_call(
        _mlp_kernel, grid=(n // tm,),
        in_specs=[act, pl.BlockSpec((1, D_MODEL), const), pl.BlockSpec((D_MODEL, D_FF), const),
                  pl.BlockSpec((D_FF, D_MODEL), const)],
        out_specs=act, out_shape=jax.ShapeDtypeStruct((n, D_MODEL), F32),
        compiler_params=_cp(("arbitrary",)), name="mlp",
    )(h, w["gmlp"], w["wup"], w["wdn"])


def _rope_tables(pos):
    half = QK_ROPE // 2
    inv = ROPE_THETA ** (-jnp.arange(half, dtype=F32) / half)
    ang = pos.astype(F32)[:, None] * inv[None, :]
    cos, sin = jnp.cos(ang), jnp.sin(ang)
    z = jnp.zeros_like(cos)
    pad = jnp.zeros((pos.shape[0], LANES - QK_ROPE), F32)
    ct = jnp.concatenate([cos, cos, pad], axis=1)
    s1 = jnp.concatenate([-sin, z, pad], axis=1)
    s2 = jnp.concatenate([z, sin, pad], axis=1)
    return ct, s1, s2


def _pad_lanes(a, width):
    return jnp.pad(a, [(0, 0)] * (a.ndim - 1) + [(0, width - a.shape[-1])])


def _layer_weights(norm_mix, w_in, q_lora_norm, w_uq, q_nope_norm, q_rope_norm, kv_lora_norm, k_rope_norm,
                   w_ukv, k_nope_norm, w_o_attn, w_glu_v, w_glu_g, w_out, norm_mlp, w_mlp_up, w_mlp_down):
    o_kv = Q_LORA
    o_kr = o_kv + KV_LORA
    o_ssm = o_kr + QK_ROPE
    o_ga = o_ssm + D_MODEL
    o_gb = o_ga + D_MODEL
    bf = lambda a: a.astype(BF16)
    r2 = lambda a: a.reshape(1, -1).astype(F32)
    return {
        "nmix": r2(norm_mix),
        "wq": bf(w_in[:, :o_kv]), "wkv": bf(w_in[:, o_kv:o_kr]),
        "wkr": bf(_pad_lanes(w_in[:, o_kr:o_ssm], LANES)),
        "wu": bf(w_in[:, o_ssm:o_ga]), "wga": bf(w_in[:, o_ga:o_gb]), "wgb": bf(w_in[:, o_gb:]),
        "gq": r2(q_lora_norm),
        "wuqn": bf(w_uq[:, :, :QK_NOPE].reshape(Q_LORA, N_HEADS * QK_NOPE)),
        "wuqr": bf(_pad_lanes(w_uq[:, :, QK_NOPE:], LANES).reshape(Q_LORA, N_HEADS * LANES)),
        "gqn": r2(q_nope_norm), "gqr": _pad_lanes(r2(q_rope_norm), LANES),
        "gkv": r2(kv_lora_norm), "gkr": _pad_lanes(r2(k_rope_norm), LANES),
        "wuk": bf(w_ukv[:, :, :QK_NOPE].reshape(KV_LORA, N_HEADS * QK_NOPE)),
        "wuv": bf(w_ukv[:, :, QK_NOPE:].reshape(KV_LORA, N_HEADS * V_DIM)),
        "gkn": r2(k_nope_norm),
        "wo": bf(w_o_attn.reshape(N_HEADS * V_DIM, D_MODEL)),
        "wv": bf(w_glu_v), "wg": bf(w_glu_g), "wout": bf(w_out),
        "gmlp": r2(norm_mlp), "wup": bf(w_mlp_up), "wdn": bf(w_mlp_down),
    }


def kernel(x_prompt, x_sample, cache_latent, cache_krope, cache_meta_latent, cache_meta_krope, state_ssm_re, state_ssm_im, meta_tokens, norm_mix, w_in, q_lora_norm, w_uq, q_nope_norm, q_rope_norm, kv_lora_norm, k_rope_norm, w_ukv, k_nope_norm, w_o_attn, ssm_a_re, ssm_a_im, ssm_log_dt, ssm_b_re, ssm_b_im, ssm_c_re, ssm_c_im, ssm_d, w_glu_v, w_glu_g, w_out, norm_mlp, w_mlp_up, w_mlp_down):
    bsz_p, seq_p = x_prompt.shape[0], x_prompt.shape[1]
    bsz_s, seq_s = x_sample.shape[0], x_sample.shape[1]
    past = cache_latent.shape[2]
    depth = w_in.shape[0]
    assert depth == 1, "single-layer step"
    l = 0
    w = _layer_weights(norm_mix[l], w_in[l], q_lora_norm[l], w_uq[l], q_nope_norm[l], q_rope_norm[l],
                       kv_lora_norm[l], k_rope_norm[l], w_ukv[l], k_nope_norm[l], w_o_attn[l], w_glu_v[l],
                       w_glu_g[l], w_out[l], norm_mlp[l], w_mlp_up[l], w_mlp_down[l])
    sw = _ssm_weights(ssm_a_re[l].astype(F32), ssm_a_im[l].astype(F32), ssm_log_dt[l].astype(F32),
                      ssm_b_re[l].astype(F32), ssm_b_im[l].astype(F32), ssm_c_re[l].astype(F32),
                      ssm_c_im[l].astype(F32), ssm_d[l].astype(F32))
    sq = 2 * SSM_SW

    tabs_m = _rope_tables(jnp.arange(N_META, dtype=jnp.int32) - N_META)
    (_, mk, mv, mckv, mkr, mu, _, _) = _front(meta_tokens.astype(F32), tabs_m, 1, w, N_META)
    h0_zero = jnp.zeros((SSM_NJ, 1, sq), F32)
    _, h_meta = _ssm_prompt(mu, sw, h0_zero, 1, N_META // SSM_T, N_META // SSM_T)

    n_p = bsz_p * seq_p
    tm = 256
    xp = x_prompt.reshape(n_p, D_MODEL)
    tabs_p = _rope_tables(jnp.arange(seq_p, dtype=jnp.int32))
    q, k, v, ckv_p, kr_p, u_p, ga, gb = _front(xp, tabs_p, seq_p // tm, w, tm)
    o_p = _attn_prompt(q, k, v, mk, mv, bsz_p, seq_p)
    h0_p = jnp.broadcast_to(h_meta, (SSM_NJ, bsz_p, sq))
    s_p, hfin_p = _ssm_prompt(u_p, sw, h0_p, bsz_p, seq_p // SSM_T, 128)
    h1_p = _mixer(o_p, s_p, ga, gb, xp, w, tm)
    y_p = _mlp(h1_p, w, tm)

    n_s = bsz_s * seq_s
    xs = x_sample.reshape(n_s, D_MODEL)
    pos_s = past + jnp.arange(seq_s, dtype=jnp.int32)
    tabs_s = tuple(jnp.tile(t, (bsz_s, 1)) for t in _rope_tables(pos_s))
    tm_s = 256
    q_s, k_s, v_s, ckv_s, kr_s, u_s, ga_s, gb_s = _front(xs, tabs_s, n_s // tm_s, w, tm_s)
    ckrb = _pad_lanes(cache_krope[l].reshape(bsz_s * past, QK_ROPE), LANES).astype(BF16)
    cmkrb = _pad_lanes(cache_meta_krope[l].reshape(bsz_s * N_META, QK_ROPE), LANES).astype(BF16)
    ck, cv = _expand(cache_latent[l].reshape(bsz_s * past, KV_LORA).astype(F32), ckrb, w, 512)
    cmk, cmv = _expand(cache_meta_latent[l].reshape(bsz_s * N_META, KV_LORA).astype(F32), cmkrb, w,
                       bsz_s * N_META)
    o_s = _attn_sample(q_s, ck, cv, cmk, cmv, k_s, v_s, bsz_s, seq_s, past)

    def to_blocks(st):
        return st.astype(F32).reshape(bsz_s, SSM_NJ, SSM_SW).transpose(1, 0, 2)

    h0_s = jnp.concatenate([to_blocks(state_ssm_re[l]), to_blocks(state_ssm_im[l])], axis=2)
    s_s, hfin_s = _ssm_sample(u_s, sw, h0_s, bsz_s, seq_s // SSM_T)
    h1_s = _mixer(o_s, s_s, ga_s, gb_s, xs, w, tm_s)
    y_s = _mlp(h1_s, w, tm_s)

    def from_blocks(hf, nb):
        re = hf[:, :, :SSM_SW].transpose(1, 0, 2).reshape(1, nb, N_GROUPS, SSM_STATE)
        im = hf[:, :, SSM_SW:].transpose(1, 0, 2).reshape(1, nb, N_GROUPS, SSM_STATE)
        return re, im

    sre_p, sim_p = from_blocks(hfin_p, bsz_p)
    sre_s, sim_s = from_blocks(hfin_s, bsz_s)
    return (y_p.reshape(bsz_p, seq_p, D_MODEL), y_s.reshape(bsz_s, seq_s, D_MODEL),
            ckv_p.reshape(1, bsz_p, seq_p, KV_LORA), kr_p.reshape(1, bsz_p, seq_p, QK_ROPE),
            jnp.broadcast_to(mckv[None, None], (1, bsz_p, N_META, KV_LORA)),
            jnp.broadcast_to(mkr[None, None], (1, bsz_p, N_META, QK_ROPE)),
            sre_p, sim_p,
            ckv_s.reshape(1, bsz_s, seq_s, KV_LORA), kr_s.reshape(1, bsz_s, seq_s, QK_ROPE),
            sre_s, sim_s)
```

```python
import functools
import math

import jax
import jax.numpy as jnp
from jax import lax
from jax.experimental import pallas as pl
from jax.experimental.pallas import tpu as pltpu

F32 = jnp.float32
BF16 = jnp.bfloat16

D_MODEL = 1024
CHUNK = 64
N_META = 16
N_HEADS = 8
QK_NOPE = 128
QK_ROPE = 64
V_DIM = 128
QK_DIM = QK_NOPE + QK_ROPE
Q_LORA = 384
KV_LORA = 256
SSM_GROUP = 16
N_GROUPS = D_MODEL // SSM_GROUP
SSM_STATE = 64
D_FF = 4 * D_MODEL
ROPE_THETA = 10000.0
EPS = 1e-6
ATTN_SCALE = QK_DIM ** -0.5
Q_SCALE = ATTN_SCALE * math.log2(math.e)
NEG_INF = -1e30

LANES = 128
HEAD_PAD = 2 * LANES
SSM_T = 8
SSM_NJ = D_MODEL // LANES
SSM_GPB = LANES // SSM_GROUP
SSM_SW = SSM_GPB * SSM_STATE
VMEM_LIMIT = 56 * 1024 * 1024


def _cp(sem):
    return pltpu.CompilerParams(dimension_semantics=sem, vmem_limit_bytes=VMEM_LIMIT)


def _dot(a, b):
    return jnp.dot(a, b, preferred_element_type=F32)


def _dot_nt(a, b):
    return lax.dot_general(a, b, (((1,), (1,)), ((), ())), preferred_element_type=F32)


def _rms(x, g, n=None):
    n = x.shape[-1] if n is None else n
    ms = jnp.sum(x * x, axis=-1, keepdims=True) * (1.0 / n)
    return x * lax.rsqrt(ms + EPS) * g


def _rope128(b, ct, s1, s2):
    return b * ct + pltpu.roll(b, LANES - QK_ROPE // 2, 1) * s1 + pltpu.roll(b, QK_ROPE // 2, 1) * s2


def _front_kernel(x_ref, ct_ref, s1_ref, s2_ref, nmix_ref, wq_ref, wkv_ref, wkr_ref, wu_ref, wga_ref,
                  wgb_ref, gq_ref, wuqn_ref, wuqr_ref, gqn_ref, gqr_ref, gkv_ref, gkr_ref, wuk_ref,
                  wuv_ref, gkn_ref,
                  q_ref, k_ref, v_ref, ckv_ref, kr_ref, u_ref, ga_ref, gb_ref, u_sc):
    x = x_ref[...]
    xn = _rms(x, nmix_ref[...]).astype(BF16)
    ct, s1, s2 = ct_ref[...], s1_ref[...], s2_ref[...]


    c_q = _rms(_dot(xn, wq_ref[...]), gq_ref[...]).astype(BF16)
    qn = _dot(c_q, wuqn_ref[...])
    qr = _dot(c_q, wuqr_ref[...])
    for h in range(N_HEADS):
        sl = slice(h * LANES, (h + 1) * LANES)
        q_ref[:, h * HEAD_PAD:h * HEAD_PAD + LANES] = (_rms(qn[:, sl], gqn_ref[...]) * Q_SCALE).astype(BF16)
        r = _rope128(_rms(qr[:, sl], gqr_ref[...], QK_ROPE), ct, s1, s2)
        q_ref[:, h * HEAD_PAD + LANES:(h + 1) * HEAD_PAD] = (r * Q_SCALE).astype(BF16)

    c_kv = _rms(_dot(xn, wkv_ref[...]), gkv_ref[...])
    ckv_ref[...] = c_kv
    kr = _rms(_dot(xn, wkr_ref[...]), gkr_ref[...], QK_ROPE)
    kr = _rope128(kr, ct, s1, s2)
    kr_ref[...] = kr[:, :QK_ROPE]
    kr16 = kr.astype(BF16)
    c16 = c_kv.astype(BF16)
    kn = _dot(c16, wuk_ref[...])
    for h in range(N_HEADS):
        sl = slice(h * LANES, (h + 1) * LANES)
        k_ref[:, h * HEAD_PAD:h * HEAD_PAD + LANES] = _rms(kn[:, sl], gkn_ref[...]).astype(BF16)
        k_ref[:, h * HEAD_PAD + LANES:(h + 1) * HEAD_PAD] = kr16
    v_ref[...] = _dot(c16, wuv_ref[...]).astype(BF16)

    u = _dot(xn, wu_ref[...])
    nchunk = u_sc.shape[1] // SSM_T
    for j in range(SSM_NJ):
        u_sc[j] = u[:, j * LANES:(j + 1) * LANES]
        for t in range(SSM_T):
            u_ref[j, :, t * LANES:(t + 1) * LANES] = u_sc[j, pl.ds(t, nchunk, stride=SSM_T), :]
    ga_ref[...] = jax.nn.sigmoid(_dot(xn, wga_ref[...]))
    gb_ref[...] = jax.nn.sigmoid(_dot(xn, wgb_ref[...]))


def _front(x, tabs, tab_blocks, w, tm):
    n = x.shape[0]
    nt = n // tm
    ct, s1, s2 = tabs

    def row(i):
        return (i, 0)

    def tab(i):
        return (i % tab_blocks, 0)

    def const(i):
        return (0, 0)

    def full(a):
        return pl.BlockSpec(a.shape, const)

    weights = [w["nmix"], w["wq"], w["wkv"], w["wkr"], w["wu"], w["wga"], w["wgb"], w["gq"], w["wuqn"],
               w["wuqr"], w["gqn"], w["gqr"], w["gkv"], w["gkr"], w["wuk"], w["wuv"], w["gkn"]]
    wide = N_HEADS * LANES
    qk_wide = N_HEADS * HEAD_PAD
    out_shape = [
        jax.ShapeDtypeStruct((n, qk_wide), BF16),
        jax.ShapeDtypeStruct((n, qk_wide), BF16),
        jax.ShapeDtypeStruct((n, wide), BF16),
        jax.ShapeDtypeStruct((n, KV_LORA), F32),
        jax.ShapeDtypeStruct((n, QK_ROPE), F32),
        jax.ShapeDtypeStruct((SSM_NJ, n // SSM_T, SSM_T * LANES), F32),
        jax.ShapeDtypeStruct((n, D_MODEL), F32),
        jax.ShapeDtypeStruct((n, D_MODEL), F32),
    ]
    out_specs = [
        pl.BlockSpec((tm, qk_wide), row), pl.BlockSpec((tm, qk_wide), row),
        pl.BlockSpec((tm, wide), row), pl.BlockSpec((tm, KV_LORA), row),
        pl.BlockSpec((tm, QK_ROPE), row),
        pl.BlockSpec((SSM_NJ, tm // SSM_T, SSM_T * LANES), lambda i: (0, i, 0)),
        pl.BlockSpec((tm, D_MODEL), row), pl.BlockSpec((tm, D_MODEL), row),
    ]
    in_specs = ([pl.BlockSpec((tm, D_MODEL), row)] + [pl.BlockSpec((tm, LANES), tab)] * 3
                + [full(a) for a in weights])
    return pl.pallas_call(
        _front_kernel, grid=(nt,), in_specs=in_specs, out_specs=out_specs, out_shape=out_shape,
        scratch_shapes=[pltpu.VMEM((SSM_NJ, tm, LANES), F32)],
        compiler_params=_cp(("arbitrary",)), name="front",
    )(x, ct, s1, s2, *weights)


ATT_T = 512


def _with_ones(v):
    return jnp.concatenate([v, jnp.ones_like(v)], axis=1)


def _softmax_step(s, v, m_ref, acc_ref):
    m_prev = m_ref[...]
    m_new = jnp.maximum(m_prev, jnp.max(s, axis=-1, keepdims=True))
    alpha = jnp.exp2(m_prev - m_new)
    p = jnp.exp2(s - jnp.tile(m_new, (1, s.shape[1] // LANES)))
    acc_ref[...] = jnp.tile(alpha, (1, 2)) * acc_ref[...] + _dot(p.astype(BF16), _with_ones(v))
    m_ref[...] = m_new


ATT_HG = 2


def _attn_prompt_kernel(q_ref, k_ref, v_ref, mk_ref, mv_ref, o_ref, a_sc, b_sc, m_ref, acc_ref):
    i = pl.program_id(2)
    heads = range(ATT_HG)

    def key_rows(kt):
        return pl.ds(pl.multiple_of(kt * ATT_T, ATT_T), ATT_T)

    def qk_lanes(g):
        return slice(g * HEAD_PAD, (g + 1) * HEAD_PAD)

    def v_lanes(g):
        return slice(g * V_DIM, (g + 1) * V_DIM)

    def scores(g, kt):
        return _dot_nt(q_ref[:, qk_lanes(g)], k_ref[key_rows(kt), qk_lanes(g)])

    def consume(g, buf, kt, diagonal):
        s = buf[g]
        if diagonal:
            qc = lax.broadcasted_iota(jnp.int32, (ATT_T, ATT_T), 0) // CHUNK
            kc = lax.broadcasted_iota(jnp.int32, (ATT_T, ATT_T), 1) // CHUNK
            s = jnp.where(kc <= qc, s, NEG_INF)
        _softmax_step(s, v_ref[key_rows(kt), v_lanes(g)], m_ref.at[g], acc_ref.at[g])

    for g in heads:
        a_sc[g] = scores(g, 0)

    for g in heads:
        s = _dot_nt(q_ref[:, qk_lanes(g)], mk_ref[:, qk_lanes(g)])
        m0 = jnp.max(s, axis=-1, keepdims=True)
        p = jnp.exp2(s - m0)
        m_ref[g] = jnp.broadcast_to(m0, (ATT_T, LANES))
        acc_ref[g] = _dot(p.astype(BF16), _with_ones(mv_ref[:, v_lanes(g)]))

    def pair(pi, carry):
        kt = 2 * pi
        for g in heads:
            b_sc[g] = scores(g, kt + 1)
            consume(g, a_sc, kt, False)
        for g in heads:
            a_sc[g] = scores(g, kt + 2)
            consume(g, b_sc, kt + 1, False)
        return carry

    lax.fori_loop(0, lax.shift_right_logical(i, 1), pair, 0)

    @pl.when((i & 1) == 1)
    def _():
        for g in heads:
            b_sc[g] = scores(g, i)
            consume(g, a_sc, i - 1, False)
        for g in heads:
            consume(g, b_sc, i, True)

    @pl.when((i & 1) == 0)
    def _():
        for g in heads:
            consume(g, a_sc, i, True)

    for g in heads:
        o_ref[:, v_lanes(g)] = (acc_ref[g, :, :V_DIM] / acc_ref[g, :, V_DIM:]).astype(BF16)


def _attn_prompt(q, k, v, mk, mv, bsz, seq):
    nq = seq // ATT_T
    n = bsz * seq
    qmap = lambda b, h, i: (b * nq + i, h)
    return pl.pallas_call(
        _attn_prompt_kernel, grid=(bsz, N_HEADS // ATT_HG, nq),
        in_specs=[pl.BlockSpec((ATT_T, ATT_HG * HEAD_PAD), qmap),
                  pl.BlockSpec((seq, ATT_HG * HEAD_PAD), lambda b, h, i: (b, h)),
                  pl.BlockSpec((seq, ATT_HG * V_DIM), lambda b, h, i: (b, h)),
                  pl.BlockSpec((N_META, ATT_HG * HEAD_PAD), lambda b, h, i: (0, h)),
                  pl.BlockSpec((N_META, ATT_HG * V_DIM), lambda b, h, i: (0, h))],
        out_specs=pl.BlockSpec((ATT_T, ATT_HG * V_DIM), qmap),
        out_shape=jax.ShapeDtypeStruct((n, N_HEADS * V_DIM), BF16),
        scratch_shapes=[pltpu.VMEM((ATT_HG, ATT_T, ATT_T), F32), pltpu.VMEM((ATT_HG, ATT_T, ATT_T), F32),
                        pltpu.VMEM((ATT_HG, ATT_T, LANES), F32), pltpu.VMEM((ATT_HG, ATT_T, 2 * V_DIM), F32)],
        compiler_params=_cp(("arbitrary", "arbitrary", "arbitrary")), name="attn_prompt",
    )(q, k, v, mk, mv)


SAMPLE_KEY_CHUNK = 512


def _attn_sample_kernel(q_ref, cm_ref, cn_ref, cc_ref, krm_ref, krn_ref, krc_ref, wuk_ref, wukg_ref, wuv_ref,
                        hsum_ref, o_ref, c_sc, kr_sc, st_sc, pt_sc, *, seq, past):
    n_small = N_META + seq
    n_keys = n_small + past

    c_sc[:, KV_LORA:] = jnp.ones((n_keys, LANES), BF16)
    c_sc[0:N_META, :KV_LORA] = cm_ref[...].astype(BF16)
    c_sc[N_META:n_small, :KV_LORA] = cn_ref[...].astype(BF16)
    c_sc[n_small:, :KV_LORA] = cc_ref[...].astype(BF16)
    kr_sc[:, QK_ROPE:] = jnp.zeros((n_keys, LANES - QK_ROPE), BF16)
    kr_sc[0:N_META, :QK_ROPE] = krm_ref[...].astype(BF16)
    kr_sc[N_META:n_small, :] = krn_ref[...]
    kr_sc[n_small:, :QK_ROPE] = krc_ref[...].astype(BF16)

    qa, qr = [], []
    for h in range(N_HEADS):
        qn_h = q_ref[:, h * HEAD_PAD:h * HEAD_PAD + LANES]
        qa.append(_dot_nt(qn_h, wukg_ref[:, h * LANES:(h + 1) * LANES]))
        qr.append(q_ref[:, h * HEAD_PAD + LANES:(h + 1) * HEAD_PAD])
    qa = jnp.concatenate(qa, axis=0).astype(BF16)
    qr = jnp.concatenate(qr, axis=0)

    bounds = [(0, n_small)] + [(n_small + i, n_small + i + SAMPLE_KEY_CHUNK)
                               for i in range(0, past, SAMPLE_KEY_CHUNK)]
    m = None
    for lo, hi in bounds:
        c16 = c_sc[lo:hi, :KV_LORA]
        kn = _dot(c16, wuk_ref[...])
        ssq = _dot((kn * kn).astype(BF16), hsum_ref[...])
        r = lax.rsqrt(ssq * (1.0 / QK_NOPE) + EPS)
        st = _dot_nt(c16, qa) * r + _dot_nt(kr_sc[lo:hi, :], qr)
        st_sc[lo:hi, :] = st
        cm = jnp.max(st, axis=0, keepdims=True)
        m = cm if m is None else jnp.maximum(m, cm)

    for lo, hi in bounds:
        pt_sc[lo:hi, :] = jnp.exp2(st_sc[lo:hi, :] - m).astype(BF16)
    pc = lax.dot_general(pt_sc[...], c_sc[...], (((0,), (0,)), ((), ())), preferred_element_type=F32)
    pcn = (pc[:, :KV_LORA] / jnp.tile(pc[:, KV_LORA:], (1, KV_LORA // LANES))).astype(BF16)
    for h in range(N_HEADS):
        o_ref[:, h * V_DIM:(h + 1) * V_DIM] = _dot(pcn[h * seq:(h + 1) * seq, :],
                                                   wuv_ref[:, h * V_DIM:(h + 1) * V_DIM]).astype(BF16)


def _attn_sample(q, k_new, c_new, cache_c, cache_kr, meta_c, meta_kr, w, bsz, seq, past):
    n_keys = N_META + seq + past
    hq = N_HEADS * seq
    hsum = jnp.repeat(jnp.repeat(jnp.eye(N_HEADS, dtype=BF16), QK_NOPE, axis=0), seq, axis=1)
    row = lambda b: (b, 0)
    b3 = lambda b: (b, 0, 0)
    const = lambda b: (0, 0)
    kern = functools.partial(_attn_sample_kernel, seq=seq, past=past)
    return pl.pallas_call(
        kern, grid=(bsz,),
        in_specs=[pl.BlockSpec((seq, N_HEADS * HEAD_PAD), row),
                  pl.BlockSpec((None, N_META, KV_LORA), b3), pl.BlockSpec((seq, KV_LORA), row),
                  pl.BlockSpec((None, past, KV_LORA), b3),
                  pl.BlockSpec((None, N_META, QK_ROPE), b3), pl.BlockSpec((seq, LANES), lambda b: (b, 1)),
                  pl.BlockSpec((None, past, QK_ROPE), b3),
                  pl.BlockSpec(w["wuk"].shape, const), pl.BlockSpec(w["wukg"].shape, const),
                  pl.BlockSpec(w["wuv"].shape, const), pl.BlockSpec(hsum.shape, const)],
        out_specs=pl.BlockSpec((seq, N_HEADS * V_DIM), row),
        out_shape=jax.ShapeDtypeStruct((bsz * seq, N_HEADS * V_DIM), BF16),
        scratch_shapes=[pltpu.VMEM((n_keys, KV_LORA + LANES), BF16), pltpu.VMEM((n_keys, LANES), BF16),
                        pltpu.VMEM((n_keys, hq), F32), pltpu.VMEM((n_keys, hq), BF16)],
        compiler_params=_cp(("arbitrary",)), name="attn_sample",
    )(q, meta_c, c_new, cache_c, meta_kr, k_new, cache_kr, w["wuk"], w["wukg"], w["wuv"], hsum)


MXU_DIM = 256


def _dot_causal(u16, m_ref):
    nblk = m_ref.shape[0] // MXU_DIM
    return jnp.concatenate(
        [_dot(u16[:, :(cb + 1) * MXU_DIM], m_ref[:(cb + 1) * MXU_DIM, cb * MXU_DIM:(cb + 1) * MXU_DIM])
         for cb in range(nblk)], axis=1)


SSM_PACK = 4
SSM_SC = SSM_SW // LANES
SUBLANES = 2 * SSM_PACK


def _ssm_prompt_kernel(u_ref, m_ref, p_ref, q_ref, lam_ref, d_ref, h0_ref, s_ref, hfin_ref, s_sc, h_sc,
                       *, nb, rt):
    t = pl.program_id(1)

    @pl.when(t == 0)
    def _():
        h_sc[...] = h0_ref[...]

    if nb < SSM_PACK:
        s_sc[...] = jnp.zeros_like(s_sc)
    for b in range(nb):
        sb = _dot(u_ref[b].astype(BF16), p_ref[...])
        for c in range(SSM_SC):
            s_sc[c, pl.ds(b, rt, stride=SUBLANES), :] = sb[:, c * LANES:(c + 1) * LANES]
            s_sc[c, pl.ds(SSM_PACK + b, rt, stride=SUBLANES), :] = sb[:, SSM_SW + c * LANES:SSM_SW + (c + 1) * LANES]

    lam = lam_ref[...]
    im_rows = lax.broadcasted_iota(jnp.int32, (SUBLANES, LANES), 0) >= SSM_PACK
    coef = []
    for c in range(SSM_SC):
        lr = jnp.broadcast_to(lam[:, c * LANES:(c + 1) * LANES], (SUBLANES, LANES))
        li = jnp.broadcast_to(lam[:, SSM_SW + c * LANES:SSM_SW + (c + 1) * LANES], (SUBLANES, LANES))
        coef.append((lr, jnp.where(im_rows, li, -li)))

    def step(r, hs):
        rows = pl.ds(pl.multiple_of(r * SUBLANES, SUBLANES), SUBLANES)
        out = []
        for c in range(SSM_SC):
            s = s_sc[c, rows, :]
            s_sc[c, rows, :] = hs[c]
            a, bb = coef[c]
            out.append(a * hs[c] + bb * pltpu.roll(hs[c], SSM_PACK, 0) + s)
        return tuple(out)

    hs = lax.fori_loop(0, rt, step, tuple(h_sc[c] for c in range(SSM_SC)))
    for c in range(SSM_SC):
        h_sc[c] = hs[c]

    for b in range(nb):
        ub = u_ref[b]
        hb = jnp.concatenate([s_sc[c, pl.ds(b, rt, stride=SUBLANES), :] for c in range(SSM_SC)]
                             + [s_sc[c, pl.ds(SSM_PACK + b, rt, stride=SUBLANES), :] for c in range(SSM_SC)],
                             axis=1)
        y = _dot_causal(ub.astype(BF16), m_ref) + _dot(hb.astype(BF16), q_ref[...]) + ub * d_ref[...]
        s_ref[b] = jax.nn.gelu(y).astype(BF16)

    @pl.when(t == pl.num_programs(1) - 1)
    def _():
        hfin_ref[...] = h_sc[...]


def _pack_state(h):
    nj, nb, _ = h.shape
    h = jnp.pad(h, ((0, 0), (0, SSM_PACK - nb), (0, 0)))
    re = h[:, :, :SSM_SW].reshape(nj, SSM_PACK, SSM_SC, LANES)
    im = h[:, :, SSM_SW:].reshape(nj, SSM_PACK, SSM_SC, LANES)
    return jnp.concatenate([re, im], axis=1).transpose(0, 2, 1, 3)


def _unpack_state(hp, nb):
    nj = hp.shape[0]
    x = hp.transpose(0, 2, 1, 3).reshape(nj, SUBLANES, SSM_SW)
    return jnp.concatenate([x[:, :nb], x[:, SSM_PACK:SSM_PACK + nb]], axis=2)


def _ssm_prompt(u, sw, h0, nb, lc, rt):
    assert nb <= SSM_PACK
    tw = SSM_T * LANES
    u4 = u.reshape(SSM_NJ, nb, lc, tw)
    sq = 2 * SSM_SW
    kern = functools.partial(_ssm_prompt_kernel, nb=nb, rt=rt)
    st_spec = pl.BlockSpec((None, SSM_SC, SUBLANES, LANES), lambda j, t: (j, 0, 0, 0))
    s, hfin = pl.pallas_call(
        kern, grid=(SSM_NJ, lc // rt),
        in_specs=[pl.BlockSpec((None, nb, rt, tw), lambda j, t: (j, 0, t, 0)),
                  pl.BlockSpec((None, tw, tw), lambda j, t: (j, 0, 0)),
                  pl.BlockSpec((None, tw, sq), lambda j, t: (j, 0, 0)),
                  pl.BlockSpec((None, sq, tw), lambda j, t: (j, 0, 0)),
                  pl.BlockSpec((None, 1, sq), lambda j, t: (j, 0, 0)),
                  pl.BlockSpec((None, 1, tw), lambda j, t: (j, 0, 0)),
                  st_spec],
        out_specs=[pl.BlockSpec((None, nb, rt, tw), lambda j, t: (j, 0, t, 0)), st_spec],
        out_shape=[jax.ShapeDtypeStruct((SSM_NJ, nb, lc, tw), BF16),
                   jax.ShapeDtypeStruct((SSM_NJ, SSM_SC, SUBLANES, LANES), F32)],
        scratch_shapes=[pltpu.VMEM((SSM_SC, rt * SUBLANES, LANES), F32),
                        pltpu.VMEM((SSM_SC, SUBLANES, LANES), F32)],
        compiler_params=_cp(("arbitrary", "arbitrary")), name="ssm_prompt",
    )(u4, sw["M"], sw["P"], sw["Q"], sw["lam"], sw["D"], _pack_state(h0))
    return s.reshape(SSM_NJ, nb * lc, tw), _unpack_state(hfin, nb)


def _ssm_sample_kernel(u_ref, m_ref, p_ref, q_ref, lam_ref, d_ref, h0_ref, s_ref, hfin_ref, s_sc,
                       *, nb, lc):
    u = u_ref[...]
    u16 = u.astype(BF16)
    s_sc[...] = _dot(u16, p_ref[...])
    lam = lam_ref[...]
    lr, li = lam[:, :SSM_SW], lam[:, SSM_SW:]
    h = h0_ref[...]
    for r in range(lc):
        rows = slice(r * nb, (r + 1) * nb)
        s = s_sc[rows, :]
        s_sc[rows, :] = h
        hre, him = h[:, :SSM_SW], h[:, SSM_SW:]
        h = jnp.concatenate([lr * hre - li * him + s[:, :SSM_SW], lr * him + li * hre + s[:, SSM_SW:]], axis=1)
    hfin_ref[...] = h
    y = _dot_causal(u16, m_ref) + _dot(s_sc[...].astype(BF16), q_ref[...]) + u * d_ref[...]
    s_ref[...] = jax.nn.gelu(y).astype(BF16)


def _ssm_sample(u, sw, h0, nb, lc):
    tw = SSM_T * LANES
    sq = 2 * SSM_SW
    u2 = u.reshape(SSM_NJ, nb, lc, tw).transpose(0, 2, 1, 3).reshape(SSM_NJ, lc * nb, tw)
    kern = functools.partial(_ssm_sample_kernel, nb=nb, lc=lc)
    blk = lambda j: (j, 0, 0)
    s, hfin = pl.pallas_call(
        kern, grid=(SSM_NJ,),
        in_specs=[pl.BlockSpec((None, lc * nb, tw), blk), pl.BlockSpec((None, tw, tw), blk),
                  pl.BlockSpec((None, tw, sq), blk), pl.BlockSpec((None, sq, tw), blk),
                  pl.BlockSpec((None, 1, sq), blk), pl.BlockSpec((None, 1, tw), blk),
                  pl.BlockSpec((None, nb, sq), blk)],
        out_specs=[pl.BlockSpec((None, lc * nb, tw), blk), pl.BlockSpec((None, nb, sq), blk)],
        out_shape=[jax.ShapeDtypeStruct((SSM_NJ, lc * nb, tw), BF16),
                   jax.ShapeDtypeStruct((SSM_NJ, nb, sq), F32)],
        scratch_shapes=[pltpu.VMEM((lc * nb, sq), F32)],
        compiler_params=_cp(("arbitrary",)), name="ssm_sample",
    )(u2, sw["M"], sw["P"], sw["Q"], sw["lam"], sw["D"], h0)
    s = s.reshape(SSM_NJ, lc, nb, tw).transpose(0, 2, 1, 3).reshape(SSM_NJ, nb * lc, tw)
    return s, hfin


def _ssm_weights(a_re, a_im, log_dt, b_re, b_im, c_re, c_im, d):
    hi = lax.Precision.HIGHEST
    t_ = SSM_T
    dt = jnp.exp(log_dt)[:, None]
    mag = jnp.exp(a_re * dt)
    lam_re, lam_im = mag * jnp.cos(a_im * dt), mag * jnp.sin(a_im * dt)
    den = a_re * a_re + a_im * a_im
    f_re = ((lam_re - 1.0) * a_re + lam_im * a_im) / den
    f_im = (lam_im * a_re - (lam_re - 1.0) * a_im) / den
    w_re = f_re[:, :, None] * b_re - f_im[:, :, None] * b_im
    w_im = f_re[:, :, None] * b_im + f_im[:, :, None] * b_re
    k = jnp.arange(t_ + 1, dtype=F32)[None, :, None]
    pmag = jnp.exp(a_re[:, None, :] * dt[:, :, None] * k)
    pang = a_im[:, None, :] * dt[:, :, None] * k
    pw_re, pw_im = pmag * jnp.cos(pang), pmag * jnp.sin(pang)
    v_re = pw_re[:, :t_, :, None] * w_re[:, None] - pw_im[:, :t_, :, None] * w_im[:, None]
    v_im = pw_re[:, :t_, :, None] * w_im[:, None] + pw_im[:, :t_, :, None] * w_re[:, None]
    kmat = (jnp.einsum("gcn,gjnd->gjcd", c_re, v_re, precision=hi)
            - jnp.einsum("gcn,gjnd->gjcd", c_im, v_im, precision=hi))
    eye = jnp.eye(SSM_GPB, dtype=F32)

    def blockdiag(x):
        x = jnp.moveaxis(x, 1, -3)
        y = x[..., :, :, None, :] * eye[:, None, :, None]
        return y.reshape(x.shape[:-3] + (SSM_GPB * x.shape[-2], SSM_GPB * x.shape[-1]))

    def per_block(x):
        return x.reshape((SSM_NJ, SSM_GPB) + x.shape[1:])

    bd_k = blockdiag(per_block(jnp.swapaxes(kmat, 2, 3))).astype(BF16)
    zero = jnp.zeros((SSM_NJ, LANES, LANES), BF16)
    m = jnp.concatenate(
        [jnp.concatenate([zero] * s + [bd_k[:, t - s] for t in range(s, t_)], axis=2) for s in range(t_)],
        axis=1)
    pl_re = jnp.swapaxes(per_block(pw_re), 1, 2).reshape(SSM_NJ, t_ + 1, SSM_SW)
    pl_im = jnp.swapaxes(per_block(pw_im), 1, 2).reshape(SSM_NJ, t_ + 1, SSM_SW)
    wb_re = blockdiag(per_block(jnp.swapaxes(w_re, 1, 2)))
    wb_im = blockdiag(per_block(jnp.swapaxes(w_im, 1, 2)))
    p_rows = []
    for s in range(t_):
        lr, li = pl_re[:, t_ - 1 - s][:, None, :], pl_im[:, t_ - 1 - s][:, None, :]
        p_rows.append(jnp.concatenate([lr * wb_re - li * wb_im, lr * wb_im + li * wb_re], axis=2).astype(BF16))
    p = jnp.concatenate(p_rows, axis=1)
    cb_re = blockdiag(per_block(jnp.swapaxes(c_re, 1, 2)))
    cb_im = blockdiag(per_block(jnp.swapaxes(c_im, 1, 2)))
    q_cols = []
    for t in range(t_):
        lr, li = pl_re[:, t + 1][:, :, None], pl_im[:, t + 1][:, :, None]
        q_cols.append(jnp.concatenate([cb_re * lr - cb_im * li, -(cb_re * li + cb_im * lr)], axis=1).astype(BF16))
    q = jnp.concatenate(q_cols, axis=2)
    lam_t = jnp.concatenate([pl_re[:, t_][:, None, :], pl_im[:, t_][:, None, :]], axis=2)
    dvec = jnp.tile(d.reshape(SSM_NJ, 1, LANES), (1, 1, t_))
    return {"M": m, "P": p, "Q": q, "lam": lam_t, "D": dvec}


def _mixer_kernel(o_ref, s_ref, ga_ref, gb_ref, x_ref, wo_ref, wv_ref, wg_ref, wout_ref, h_ref, s_sc):
    o_a = _dot(o_ref[...], wo_ref[...])
    nchunk = s_sc.shape[1] // SSM_T
    for j in range(SSM_NJ):
        sj = s_ref[j].astype(F32)
        for t in range(SSM_T):
            s_sc[j, pl.ds(t, nchunk, stride=SSM_T), :] = sj[:, t * LANES:(t + 1) * LANES]
    s = jnp.concatenate([s_sc[j] for j in range(SSM_NJ)], axis=1).astype(BF16)
    o_b = _dot(s, wv_ref[...]) * jax.nn.sigmoid(_dot(s, wg_ref[...]))
    merged = ga_ref[...] * o_a + gb_ref[...] * o_b
    h_ref[...] = x_ref[...] + _dot(merged.astype(BF16), wout_ref[...])


def _mixer(o, s, ga, gb, x, w, tm):
    n = x.shape[0]
    row = lambda i: (i, 0)
    const = lambda i: (0, 0)
    wspec = pl.BlockSpec((D_MODEL, D_MODEL), const)
    act = pl.BlockSpec((tm, D_MODEL), row)
    return pl.pallas_call(
        _mixer_kernel, grid=(n // tm,),
        in_specs=[act, pl.BlockSpec((SSM_NJ, tm // SSM_T, SSM_T * LANES), lambda i: (0, i, 0)), act, act, act,
                  wspec, wspec, wspec, wspec],
        out_specs=act, out_shape=jax.ShapeDtypeStruct((n, D_MODEL), F32),
        scratch_shapes=[pltpu.VMEM((SSM_NJ, tm, LANES), F32)],
        compiler_params=_cp(("arbitrary",)), name="mixer_out",
    )(o, s, ga, gb, x, w["wo"], w["wv"], w["wg"], w["wout"])


def _mlp_kernel(h_ref, g_ref, wup_ref, wdn_ref, y_ref):
    h = h_ref[...]
    hn = _rms(h, g_ref[...]).astype(BF16)
    a = jnp.maximum(_dot(hn, wup_ref[...]), 0.0)
    y_ref[...] = h + _dot((a * a).astype(BF16), wdn_ref[...])


def _mlp(h, w, tm):
    n = h.shape[0]
    row = lambda i: (i, 0)
    const = lambda i: (0, 0)
    act = pl.BlockSpec((tm, D_MODEL), row)
    return pl.pallas_call(
        _mlp_kernel, grid=(n // tm,),
        in_specs=[act, pl.BlockSpec((1, D_MODEL), const), pl.BlockSpec((D_MODEL, D_FF), const),
                  pl.BlockSpec((D_FF, D_MODEL), const)],
        out_specs=act, out_shape=jax.ShapeDtypeStruct((n, D_MODEL), F32),
        compiler_params=_cp(("arbitrary",)), name="mlp",
    )(h, w["gmlp"], w["wup"], w["wdn"])


def _rope_tables(pos):
    half = QK_ROPE // 2
    inv = ROPE_THETA ** (-jnp.arange(half, dtype=F32) / half)
    ang = pos.astype(F32)[:, None] * inv[None, :]
    cos, sin = jnp.cos(ang), jnp.sin(ang)
    z = jnp.zeros_like(cos)
    pad = jnp.zeros((pos.shape[0], LANES - QK_ROPE), F32)
    ct = jnp.concatenate([cos, cos, pad], axis=1)
    s1 = jnp.concatenate([-sin, z, pad], axis=1)
    s2 = jnp.concatenate([z, sin, pad], axis=1)
    return ct, s1, s2


def _pad_lanes(a, width):
    return jnp.pad(a, [(0, 0)] * (a.ndim - 1) + [(0, width - a.shape[-1])])


def _layer_weights(norm_mix, w_in, q_lora_norm, w_uq, q_nope_norm, q_rope_norm, kv_lora_norm, k_rope_norm,
                   w_ukv, k_nope_norm, w_o_attn, w_glu_v, w_glu_g, w_out, norm_mlp, w_mlp_up, w_mlp_down):
    o_kv = Q_LORA
    o_kr = o_kv + KV_LORA
    o_ssm = o_kr + QK_ROPE
    o_ga = o_ssm + D_MODEL
    o_gb = o_ga + D_MODEL
    bf = lambda a: a.astype(BF16)
    r2 = lambda a: a.reshape(1, -1).astype(F32)
    return {
        "nmix": r2(norm_mix),
        "wq": bf(w_in[:, :o_kv]), "wkv": bf(w_in[:, o_kv:o_kr]),
        "wkr": bf(_pad_lanes(w_in[:, o_kr:o_ssm], LANES)),
        "wu": bf(w_in[:, o_ssm:o_ga]), "wga": bf(w_in[:, o_ga:o_gb]), "wgb": bf(w_in[:, o_gb:]),
        "gq": r2(q_lora_norm),
        "wuqn": bf(w_uq[:, :, :QK_NOPE].reshape(Q_LORA, N_HEADS * QK_NOPE)),
        "wuqr": bf(_pad_lanes(w_uq[:, :, QK_NOPE:], LANES).reshape(Q_LORA, N_HEADS * LANES)),
        "gqn": r2(q_nope_norm), "gqr": _pad_lanes(r2(q_rope_norm), LANES),
        "gkv": r2(kv_lora_norm), "gkr": _pad_lanes(r2(k_rope_norm), LANES),
        "wuk": bf(w_ukv[:, :, :QK_NOPE].reshape(KV_LORA, N_HEADS * QK_NOPE)),
        "wukg": bf((w_ukv[:, :, :QK_NOPE] * k_nope_norm.astype(F32)).reshape(KV_LORA, N_HEADS * QK_NOPE)),
        "wuv": bf(w_ukv[:, :, QK_NOPE:].reshape(KV_LORA, N_HEADS * V_DIM)),
        "gkn": r2(k_nope_norm),
        "wo": bf(w_o_attn.reshape(N_HEADS * V_DIM, D_MODEL)),
        "wv": bf(w_glu_v), "wg": bf(w_glu_g), "wout": bf(w_out),
        "gmlp": r2(norm_mlp), "wup": bf(w_mlp_up), "wdn": bf(w_mlp_down),
    }


def kernel(x_prompt, x_sample, cache_latent, cache_krope, cache_meta_latent, cache_meta_krope, state_ssm_re, state_ssm_im, meta_tokens, norm_mix, w_in, q_lora_norm, w_uq, q_nope_norm, q_rope_norm, kv_lora_norm, k_rope_norm, w_ukv, k_nope_norm, w_o_attn, ssm_a_re, ssm_a_im, ssm_log_dt, ssm_b_re, ssm_b_im, ssm_c_re, ssm_c_im, ssm_d, w_glu_v, w_glu_g, w_out, norm_mlp, w_mlp_up, w_mlp_down):
    bsz_p, seq_p = x_prompt.shape[0], x_prompt.shape[1]
    bsz_s, seq_s = x_sample.shape[0], x_sample.shape[1]
    past = cache_latent.shape[2]
    depth = w_in.shape[0]
    assert depth == 1, "single-layer step"
    l = 0
    w = _layer_weights(norm_mix[l], w_in[l], q_lora_norm[l], w_uq[l], q_nope_norm[l], q_rope_norm[l],
                       kv_lora_norm[l], k_rope_norm[l], w_ukv[l], k_nope_norm[l], w_o_attn[l], w_glu_v[l],
                       w_glu_g[l], w_out[l], norm_mlp[l], w_mlp_up[l], w_mlp_down[l])
    sw = _ssm_weights(ssm_a_re[l].astype(F32), ssm_a_im[l].astype(F32), ssm_log_dt[l].astype(F32),
                      ssm_b_re[l].astype(F32), ssm_b_im[l].astype(F32), ssm_c_re[l].astype(F32),
                      ssm_c_im[l].astype(F32), ssm_d[l].astype(F32))
    sq = 2 * SSM_SW

    tabs_m = _rope_tables(jnp.arange(N_META, dtype=jnp.int32) - N_META)
    (_, mk, mv, mckv, mkr, mu, _, _) = _front(meta_tokens.astype(F32), tabs_m, 1, w, N_META)
    h0_zero = jnp.zeros((SSM_NJ, 1, sq), F32)
    _, h_meta = _ssm_prompt(mu, sw, h0_zero, 1, N_META // SSM_T, N_META // SSM_T)

    n_p = bsz_p * seq_p
    tm = 256
    xp = x_prompt.reshape(n_p, D_MODEL)
    tabs_p = _rope_tables(jnp.arange(seq_p, dtype=jnp.int32))
    q, k, v, ckv_p, kr_p, u_p, ga, gb = _front(xp, tabs_p, seq_p // tm, w, tm)
    o_p = _attn_prompt(q, k, v, mk, mv, bsz_p, seq_p)
    h0_p = jnp.broadcast_to(h_meta, (SSM_NJ, bsz_p, sq))
    s_p, hfin_p = _ssm_prompt(u_p, sw, h0_p, bsz_p, seq_p // SSM_T, 128)
    h1_p = _mixer(o_p, s_p, ga, gb, xp, w, tm)
    y_p = _mlp(h1_p, w, tm)

    n_s = bsz_s * seq_s
    xs = x_sample.reshape(n_s, D_MODEL)
    pos_s = past + jnp.arange(seq_s, dtype=jnp.int32)
    tabs_s = tuple(jnp.tile(t, (bsz_s, 1)) for t in _rope_tables(pos_s))
    tm_s = 256
    q_s, k_s, _, ckv_s, kr_s, u_s, ga_s, gb_s = _front(xs, tabs_s, n_s // tm_s, w, tm_s)
    o_s = _attn_sample(q_s, k_s, ckv_s, cache_latent[l].astype(F32), cache_krope[l].astype(F32),
                       cache_meta_latent[l].astype(F32), cache_meta_krope[l].astype(F32), w, bsz_s, seq_s, past)

    def to_blocks(st):
        return st.astype(F32).reshape(bsz_s, SSM_NJ, SSM_SW).transpose(1, 0, 2)

    h0_s = jnp.concatenate([to_blocks(state_ssm_re[l]), to_blocks(state_ssm_im[l])], axis=2)
    s_s, hfin_s = _ssm_sample(u_s, sw, h0_s, bsz_s, seq_s // SSM_T)
    h1_s = _mixer(o_s, s_s, ga_s, gb_s, xs, w, tm_s)
    y_s = _mlp(h1_s, w, tm_s)

    def from_blocks(hf, nb):
        re = hf[:, :, :SSM_SW].transpose(1, 0, 2).reshape(1, nb, N_GROUPS, SSM_STATE)
        im = hf[:, :, SSM_SW:].transpose(1, 0, 2).reshape(1, nb, N_GROUPS, SSM_STATE)
        return re, im

    sre_p, sim_p = from_blocks(hfin_p, bsz_p)
    sre_s, sim_s = from_blocks(hfin_s, bsz_s)
    return (y_p.reshape(bsz_p, seq_p, D_MODEL), y_s.reshape(bsz_s, seq_s, D_MODEL),
            ckv_p.reshape(1, bsz_p, seq_p, KV_LORA), kr_p.reshape(1, bsz_p, seq_p, QK_ROPE),
            jnp.broadcast_to(mckv[None, None], (1, bsz_p, N_META, KV_LORA)),
            jnp.broadcast_to(mkr[None, None], (1, bsz_p, N_META, QK_ROPE)),
            sre_p, sim_p,
            ckv_s.reshape(1, bsz_s, seq_s, KV_LORA), kr_s.reshape(1, bsz_s, seq_s, QK_ROPE),
            sre_s, sim_s)
```

```python
import functools
import math

import jax
import jax.numpy as jnp
from jax import lax
from jax.experimental import pallas as pl
from jax.experimental.pallas import tpu as pltpu

F32 = jnp.float32
BF16 = jnp.bfloat16

D_MODEL = 1024
CHUNK = 64
N_META = 16
N_HEADS = 8
QK_NOPE = 128
QK_ROPE = 64
V_DIM = 128
QK_DIM = QK_NOPE + QK_ROPE
Q_LORA = 384
KV_LORA = 256
SSM_GROUP = 16
N_GROUPS = D_MODEL // SSM_GROUP
SSM_STATE = 64
D_FF = 4 * D_MODEL
ROPE_THETA = 10000.0
EPS = 1e-6
ATTN_SCALE = QK_DIM ** -0.5
Q_SCALE = ATTN_SCALE * math.log2(math.e)
NEG_INF = -1e30

LANES = 128
HEAD_PAD = 2 * LANES
SSM_T = 8
SSM_NJ = D_MODEL // LANES
SSM_GPB = LANES // SSM_GROUP
SSM_SW = SSM_GPB * SSM_STATE
VMEM_LIMIT = 56 * 1024 * 1024


def _cp(sem):
    return pltpu.CompilerParams(dimension_semantics=sem, vmem_limit_bytes=VMEM_LIMIT)


def _dot(a, b):
    return jnp.dot(a, b, preferred_element_type=F32)


def _dot_nt(a, b):
    return lax.dot_general(a, b, (((1,), (1,)), ((), ())), preferred_element_type=F32)


def _rms(x, g, n=None):
    n = x.shape[-1] if n is None else n
    ms = jnp.sum(x * x, axis=-1, keepdims=True) * (1.0 / n)
    return x * lax.rsqrt(ms + EPS) * g


def _rope128(b, ct, s1, s2):
    return b * ct + pltpu.roll(b, LANES - QK_ROPE // 2, 1) * s1 + pltpu.roll(b, QK_ROPE // 2, 1) * s2


def _front_kernel(x_ref, ct_ref, s1_ref, s2_ref, nmix_ref, wq_ref, wkv_ref, wkr_ref, wu_ref, wga_ref,
                  wgb_ref, gq_ref, wuqn_ref, wuqr_ref, gqn_ref, gqr_ref, gkv_ref, gkr_ref, wuk_ref,
                  wuv_ref, gkn_ref,
                  q_ref, k_ref, v_ref, ckv_ref, kr_ref, u_ref, ga_ref, gb_ref, u_sc):
    x = x_ref[...]
    xn = _rms(x, nmix_ref[...]).astype(BF16)
    ct, s1, s2 = ct_ref[...], s1_ref[...], s2_ref[...]


    c_q = _rms(_dot(xn, wq_ref[...]), gq_ref[...]).astype(BF16)
    qn = _dot(c_q, wuqn_ref[...])
    qr = _dot(c_q, wuqr_ref[...])
    for h in range(N_HEADS):
        sl = slice(h * LANES, (h + 1) * LANES)
        q_ref[:, h * HEAD_PAD:h * HEAD_PAD + LANES] = (_rms(qn[:, sl], gqn_ref[...]) * Q_SCALE).astype(BF16)
        r = _rope128(_rms(qr[:, sl], gqr_ref[...], QK_ROPE), ct, s1, s2)
        q_ref[:, h * HEAD_PAD + LANES:(h + 1) * HEAD_PAD] = (r * Q_SCALE).astype(BF16)

    c_kv = _rms(_dot(xn, wkv_ref[...]), gkv_ref[...])
    ckv_ref[...] = c_kv
    kr = _rms(_dot(xn, wkr_ref[...]), gkr_ref[...], QK_ROPE)
    kr = _rope128(kr, ct, s1, s2)
    kr_ref[...] = kr[:, :QK_ROPE]
    kr16 = kr.astype(BF16)
    c16 = c_kv.astype(BF16)
    kn = _dot(c16, wuk_ref[...])
    for h in range(N_HEADS):
        sl = slice(h * LANES, (h + 1) * LANES)
        k_ref[:, h * HEAD_PAD:h * HEAD_PAD + LANES] = _rms(kn[:, sl], gkn_ref[...]).astype(BF16)
        k_ref[:, h * HEAD_PAD + LANES:(h + 1) * HEAD_PAD] = kr16
    v_ref[...] = _dot(c16, wuv_ref[...]).astype(BF16)

    u = _dot(xn, wu_ref[...])
    nchunk = u_sc.shape[1] // SSM_T
    for j in range(SSM_NJ):
        u_sc[j] = u[:, j * LANES:(j + 1) * LANES]
        for t in range(SSM_T):
            u_ref[j, :, t * LANES:(t + 1) * LANES] = u_sc[j, pl.ds(t, nchunk, stride=SSM_T), :]
    ga_ref[...] = jax.nn.sigmoid(_dot(xn, wga_ref[...]))
    gb_ref[...] = jax.nn.sigmoid(_dot(xn, wgb_ref[...]))


def _front(x, tabs, tab_blocks, w, tm):
    n = x.shape[0]
    nt = n // tm
    ct, s1, s2 = tabs

    def row(i):
        return (i, 0)

    def tab(i):
        return (i % tab_blocks, 0)

    def const(i):
        return (0, 0)

    def full(a):
        return pl.BlockSpec(a.shape, const)

    weights = [w["nmix"], w["wq"], w["wkv"], w["wkr"], w["wu"], w["wga"], w["wgb"], w["gq"], w["wuqn"],
               w["wuqr"], w["gqn"], w["gqr"], w["gkv"], w["gkr"], w["wuk"], w["wuv"], w["gkn"]]
    wide = N_HEADS * LANES
    qk_wide = N_HEADS * HEAD_PAD
    out_shape = [
        jax.ShapeDtypeStruct((n, qk_wide), BF16),
        jax.ShapeDtypeStruct((n, qk_wide), BF16),
        jax.ShapeDtypeStruct((n, wide), BF16),
        jax.ShapeDtypeStruct((n, KV_LORA), F32),
        jax.ShapeDtypeStruct((n, QK_ROPE), F32),
        jax.ShapeDtypeStruct((SSM_NJ, n // SSM_T, SSM_T * LANES), F32),
        jax.ShapeDtypeStruct((n, D_MODEL), F32),
        jax.ShapeDtypeStruct((n, D_MODEL), F32),
    ]
    out_specs = [
        pl.BlockSpec((tm, qk_wide), row), pl.BlockSpec((tm, qk_wide), row),
        pl.BlockSpec((tm, wide), row), pl.BlockSpec((tm, KV_LORA), row),
        pl.BlockSpec((tm, QK_ROPE), row),
        pl.BlockSpec((SSM_NJ, tm // SSM_T, SSM_T * LANES), lambda i: (0, i, 0)),
        pl.BlockSpec((tm, D_MODEL), row), pl.BlockSpec((tm, D_MODEL), row),
    ]
    in_specs = ([pl.BlockSpec((tm, D_MODEL), row)] + [pl.BlockSpec((tm, LANES), tab)] * 3
                + [full(a) for a in weights])
    return pl.pallas_call(
        _front_kernel, grid=(nt,), in_specs=in_specs, out_specs=out_specs, out_shape=out_shape,
        scratch_shapes=[pltpu.VMEM((SSM_NJ, tm, LANES), F32)],
        compiler_params=_cp(("arbitrary",)), name="front",
    )(x, ct, s1, s2, *weights)


ATT_T = 512


def _with_ones(v):
    return jnp.concatenate([v, jnp.ones_like(v)], axis=1)


def _softmax_step(s, v, m_ref, acc_ref):
    m_prev = m_ref[...]
    m_new = jnp.maximum(m_prev, jnp.max(s, axis=-1, keepdims=True))
    alpha = jnp.exp2(m_prev - m_new)
    p = jnp.exp2(s - jnp.tile(m_new, (1, s.shape[1] // LANES)))
    acc_ref[...] = jnp.tile(alpha, (1, 2)) * acc_ref[...] + _dot(p.astype(BF16), _with_ones(v))
    m_ref[...] = m_new


ATT_HG = 4


def _attn_prompt_kernel(q_ref, k_ref, v_ref, mk_ref, mv_ref, o_ref, a_sc, b_sc, m_ref, acc_ref):
    i = pl.program_id(2)
    heads = range(ATT_HG)

    def key_rows(kt):
        return pl.ds(pl.multiple_of(kt * ATT_T, ATT_T), ATT_T)

    def qk_lanes(g):
        return slice(g * HEAD_PAD, (g + 1) * HEAD_PAD)

    def v_lanes(g):
        return slice(g * V_DIM, (g + 1) * V_DIM)

    def scores(g, kt):
        return _dot_nt(q_ref[:, qk_lanes(g)], k_ref[key_rows(kt), qk_lanes(g)])

    def consume(g, buf, kt, diagonal):
        s = buf[g]
        if diagonal:
            qc = lax.broadcasted_iota(jnp.int32, (ATT_T, ATT_T), 0) // CHUNK
            kc = lax.broadcasted_iota(jnp.int32, (ATT_T, ATT_T), 1) // CHUNK
            s = jnp.where(kc <= qc, s, NEG_INF)
        _softmax_step(s, v_ref[key_rows(kt), v_lanes(g)], m_ref.at[g], acc_ref.at[g])

    for g in heads:
        a_sc[g] = scores(g, 0)

    for g in heads:
        s = _dot_nt(q_ref[:, qk_lanes(g)], mk_ref[:, qk_lanes(g)])
        m0 = jnp.max(s, axis=-1, keepdims=True)
        p = jnp.exp2(s - m0)
        m_ref[g] = jnp.broadcast_to(m0, (ATT_T, LANES))
        acc_ref[g] = _dot(p.astype(BF16), _with_ones(mv_ref[:, v_lanes(g)]))

    def pair(pi, carry):
        kt = 2 * pi
        for g in heads:
            b_sc[g] = scores(g, kt + 1)
            consume(g, a_sc, kt, False)
        for g in heads:
            a_sc[g] = scores(g, kt + 2)
            consume(g, b_sc, kt + 1, False)
        return carry

    lax.fori_loop(0, lax.shift_right_logical(i, 1), pair, 0)

    @pl.when((i & 1) == 1)
    def _():
        for g in heads:
            b_sc[g] = scores(g, i)
            consume(g, a_sc, i - 1, False)
        for g in heads:
            consume(g, b_sc, i, True)

    @pl.when((i & 1) == 0)
    def _():
        for g in heads:
            consume(g, a_sc, i, True)

    for g in heads:
        o_ref[:, v_lanes(g)] = (acc_ref[g, :, :V_DIM] / acc_ref[g, :, V_DIM:]).astype(BF16)


def _attn_prompt(q, k, v, mk, mv, bsz, seq):
    nq = seq // ATT_T
    n = bsz * seq
    qmap = lambda b, h, i: (b * nq + i, h)
    return pl.pallas_call(
        _attn_prompt_kernel, grid=(bsz, N_HEADS // ATT_HG, nq),
        in_specs=[pl.BlockSpec((ATT_T, ATT_HG * HEAD_PAD), qmap),
                  pl.BlockSpec((seq, ATT_HG * HEAD_PAD), lambda b, h, i: (b, h)),
                  pl.BlockSpec((seq, ATT_HG * V_DIM), lambda b, h, i: (b, h)),
                  pl.BlockSpec((N_META, ATT_HG * HEAD_PAD), lambda b, h, i: (0, h)),
                  pl.BlockSpec((N_META, ATT_HG * V_DIM), lambda b, h, i: (0, h))],
        out_specs=pl.BlockSpec((ATT_T, ATT_HG * V_DIM), qmap),
        out_shape=jax.ShapeDtypeStruct((n, N_HEADS * V_DIM), BF16),
        scratch_shapes=[pltpu.VMEM((ATT_HG, ATT_T, ATT_T), F32), pltpu.VMEM((ATT_HG, ATT_T, ATT_T), F32),
                        pltpu.VMEM((ATT_HG, ATT_T, LANES), F32), pltpu.VMEM((ATT_HG, ATT_T, 2 * V_DIM), F32)],
        compiler_params=_cp(("arbitrary", "arbitrary", "arbitrary")), name="attn_prompt",
    )(q, k, v, mk, mv)


SAMPLE_KEY_CHUNK = 512


def _attn_sample_kernel(q_ref, cm_ref, cn_ref, cc_ref, krm_ref, krn_ref, krc_ref, wuk_ref, wukg_ref, wuv_ref,
                        hsum_ref, o_ref, c_sc, kr_sc, st_sc, pt_sc, *, seq, past):
    n_small = N_META + seq
    n_keys = n_small + past

    c_sc[:, KV_LORA:] = jnp.ones((n_keys, LANES), BF16)
    c_sc[0:N_META, :KV_LORA] = cm_ref[...].astype(BF16)
    c_sc[N_META:n_small, :KV_LORA] = cn_ref[...].astype(BF16)
    c_sc[n_small:, :KV_LORA] = cc_ref[...].astype(BF16)
    kr_sc[:, QK_ROPE:] = jnp.zeros((n_keys, LANES - QK_ROPE), BF16)
    kr_sc[0:N_META, :QK_ROPE] = krm_ref[...].astype(BF16)
    kr_sc[N_META:n_small, :] = krn_ref[...]
    kr_sc[n_small:, :QK_ROPE] = krc_ref[...].astype(BF16)

    qa, qr = [], []
    for h in range(N_HEADS):
        qn_h = q_ref[:, h * HEAD_PAD:h * HEAD_PAD + LANES]
        qa.append(_dot_nt(qn_h, wukg_ref[:, h * LANES:(h + 1) * LANES]))
        qr.append(q_ref[:, h * HEAD_PAD + LANES:(h + 1) * HEAD_PAD])
    qa = jnp.concatenate(qa, axis=0).astype(BF16)
    qr = jnp.concatenate(qr, axis=0)

    bounds = [(0, n_small)] + [(n_small + i, n_small + i + SAMPLE_KEY_CHUNK)
                               for i in range(0, past, SAMPLE_KEY_CHUNK)]
    m = None
    for lo, hi in bounds:
        c16 = c_sc[lo:hi, :KV_LORA]
        kn = _dot(c16, wuk_ref[...])
        ssq = _dot((kn * kn).astype(BF16), hsum_ref[...])
        r = lax.rsqrt(ssq * (1.0 / QK_NOPE) + EPS)
        st = _dot_nt(c16, qa) * r + _dot_nt(kr_sc[lo:hi, :], qr)
        st_sc[lo:hi, :] = st
        cm = jnp.max(st, axis=0, keepdims=True)
        m = cm if m is None else jnp.maximum(m, cm)

    for lo, hi in bounds:
        pt_sc[lo:hi, :] = jnp.exp2(st_sc[lo:hi, :] - m).astype(BF16)
    pc = lax.dot_general(pt_sc[...], c_sc[...], (((0,), (0,)), ((), ())), preferred_element_type=F32)
    pcn = (pc[:, :KV_LORA] / jnp.tile(pc[:, KV_LORA:], (1, KV_LORA // LANES))).astype(BF16)
    for h in range(N_HEADS):
        o_ref[:, h * V_DIM:(h + 1) * V_DIM] = _dot(pcn[h * seq:(h + 1) * seq, :],
                                                   wuv_ref[:, h * V_DIM:(h + 1) * V_DIM]).astype(BF16)


def _attn_sample(q, k_new, c_new, cache_c, cache_kr, meta_c, meta_kr, w, bsz, seq, past):
    n_keys = N_META + seq + past
    hq = N_HEADS * seq
    hsum = jnp.repeat(jnp.repeat(jnp.eye(N_HEADS, dtype=BF16), QK_NOPE, axis=0), seq, axis=1)
    row = lambda b: (b, 0)
    b3 = lambda b: (b, 0, 0)
    const = lambda b: (0, 0)
    kern = functools.partial(_attn_sample_kernel, seq=seq, past=past)
    return pl.pallas_call(
        kern, grid=(bsz,),
        in_specs=[pl.BlockSpec((seq, N_HEADS * HEAD_PAD), row),
                  pl.BlockSpec((None, N_META, KV_LORA), b3), pl.BlockSpec((seq, KV_LORA), row),
                  pl.BlockSpec((None, past, KV_LORA), b3),
                  pl.BlockSpec((None, N_META, QK_ROPE), b3), pl.BlockSpec((seq, LANES), lambda b: (b, 1)),
                  pl.BlockSpec((None, past, QK_ROPE), b3),
                  pl.BlockSpec(w["wuk"].shape, const), pl.BlockSpec(w["wukg"].shape, const),
                  pl.BlockSpec(w["wuv"].shape, const), pl.BlockSpec(hsum.shape, const)],
        out_specs=pl.BlockSpec((seq, N_HEADS * V_DIM), row),
        out_shape=jax.ShapeDtypeStruct((bsz * seq, N_HEADS * V_DIM), BF16),
        scratch_shapes=[pltpu.VMEM((n_keys, KV_LORA + LANES), BF16), pltpu.VMEM((n_keys, LANES), BF16),
                        pltpu.VMEM((n_keys, hq), F32), pltpu.VMEM((n_keys, hq), BF16)],
        compiler_params=_cp(("arbitrary",)), name="attn_sample",
    )(q, meta_c, c_new, cache_c, meta_kr, k_new, cache_kr, w["wuk"], w["wukg"], w["wuv"], hsum)


MXU_DIM = 256


def _dot_causal(u16, m_ref):
    nblk = m_ref.shape[0] // MXU_DIM
    return jnp.concatenate(
        [_dot(u16[:, :(cb + 1) * MXU_DIM], m_ref[:(cb + 1) * MXU_DIM, cb * MXU_DIM:(cb + 1) * MXU_DIM])
         for cb in range(nblk)], axis=1)


SSM_PACK = 4
SSM_SC = SSM_SW // LANES
SUBLANES = 2 * SSM_PACK


def _ssm_prompt_kernel(u_ref, m_ref, p_ref, q_ref, lam_ref, d_ref, h0_ref, s_ref, hfin_ref, s_sc, h_sc,
                       *, nb, rt):
    t = pl.program_id(1)

    @pl.when(t == 0)
    def _():
        h_sc[...] = h0_ref[...]

    if nb < SSM_PACK:
        s_sc[...] = jnp.zeros_like(s_sc)
    for b in range(nb):
        sb = _dot(u_ref[b].astype(BF16), p_ref[...])
        for c in range(SSM_SC):
            s_sc[c, pl.ds(b, rt, stride=SUBLANES), :] = sb[:, c * LANES:(c + 1) * LANES]
            s_sc[c, pl.ds(SSM_PACK + b, rt, stride=SUBLANES), :] = sb[:, SSM_SW + c * LANES:SSM_SW + (c + 1) * LANES]

    lam = lam_ref[...]
    im_rows = lax.broadcasted_iota(jnp.int32, (SUBLANES, LANES), 0) >= SSM_PACK
    coef = []
    for c in range(SSM_SC):
        lr = jnp.broadcast_to(lam[:, c * LANES:(c + 1) * LANES], (SUBLANES, LANES))
        li = jnp.broadcast_to(lam[:, SSM_SW + c * LANES:SSM_SW + (c + 1) * LANES], (SUBLANES, LANES))
        coef.append((lr, jnp.where(im_rows, li, -li)))

    def step(r, hs):
        rows = pl.ds(pl.multiple_of(r * SUBLANES, SUBLANES), SUBLANES)
        out = []
        for c in range(SSM_SC):
            s = s_sc[c, rows, :]
            s_sc[c, rows, :] = hs[c]
            a, bb = coef[c]
            out.append(a * hs[c] + bb * pltpu.roll(hs[c], SSM_PACK, 0) + s)
        return tuple(out)

    hs = lax.fori_loop(0, rt, step, tuple(h_sc[c] for c in range(SSM_SC)))
    for c in range(SSM_SC):
        h_sc[c] = hs[c]

    for b in range(nb):
        ub = u_ref[b]
        hb = jnp.concatenate([s_sc[c, pl.ds(b, rt, stride=SUBLANES), :] for c in range(SSM_SC)]
                             + [s_sc[c, pl.ds(SSM_PACK + b, rt, stride=SUBLANES), :] for c in range(SSM_SC)],
                             axis=1)
        y = _dot_causal(ub.astype(BF16), m_ref) + _dot(hb.astype(BF16), q_ref[...]) + ub * d_ref[...]
        s_ref[b] = jax.nn.gelu(y).astype(BF16)

    @pl.when(t == pl.num_programs(1) - 1)
    def _():
        hfin_ref[...] = h_sc[...]


def _pack_state(h):
    nj, nb, _ = h.shape
    h = jnp.pad(h, ((0, 0), (0, SSM_PACK - nb), (0, 0)))
    re = h[:, :, :SSM_SW].reshape(nj, SSM_PACK, SSM_SC, LANES)
    im = h[:, :, SSM_SW:].reshape(nj, SSM_PACK, SSM_SC, LANES)
    return jnp.concatenate([re, im], axis=1).transpose(0, 2, 1, 3)


def _unpack_state(hp, nb):
    nj = hp.shape[0]
    x = hp.transpose(0, 2, 1, 3).reshape(nj, SUBLANES, SSM_SW)
    return jnp.concatenate([x[:, :nb], x[:, SSM_PACK:SSM_PACK + nb]], axis=2)


def _ssm_prompt(u, sw, h0, nb, lc, rt):
    assert nb <= SSM_PACK
    tw = SSM_T * LANES
    u4 = u.reshape(SSM_NJ, nb, lc, tw)
    sq = 2 * SSM_SW
    kern = functools.partial(_ssm_prompt_kernel, nb=nb, rt=rt)
    st_spec = pl.BlockSpec((None, SSM_SC, SUBLANES, LANES), lambda j, t: (j, 0, 0, 0))
    s, hfin = pl.pallas_call(
        kern, grid=(SSM_NJ, lc // rt),
        in_specs=[pl.BlockSpec((None, nb, rt, tw), lambda j, t: (j, 0, t, 0)),
                  pl.BlockSpec((None, tw, tw), lambda j, t: (j, 0, 0)),
                  pl.BlockSpec((None, tw, sq), lambda j, t: (j, 0, 0)),
                  pl.BlockSpec((None, sq, tw), lambda j, t: (j, 0, 0)),
                  pl.BlockSpec((None, 1, sq), lambda j, t: (j, 0, 0)),
                  pl.BlockSpec((None, 1, tw), lambda j, t: (j, 0, 0)),
                  st_spec],
        out_specs=[pl.BlockSpec((None, nb, rt, tw), lambda j, t: (j, 0, t, 0)), st_spec],
        out_shape=[jax.ShapeDtypeStruct((SSM_NJ, nb, lc, tw), BF16),
                   jax.ShapeDtypeStruct((SSM_NJ, SSM_SC, SUBLANES, LANES), F32)],
        scratch_shapes=[pltpu.VMEM((SSM_SC, rt * SUBLANES, LANES), F32),
                        pltpu.VMEM((SSM_SC, SUBLANES, LANES), F32)],
        compiler_params=_cp(("arbitrary", "arbitrary")), name="ssm_prompt",
    )(u4, sw["M"], sw["P"], sw["Q"], sw["lam"], sw["D"], _pack_state(h0))
    return s.reshape(SSM_NJ, nb * lc, tw), _unpack_state(hfin, nb)


def _ssm_sample_kernel(u_ref, m_ref, p_ref, q_ref, lam_ref, d_ref, h0_ref, s_ref, hfin_ref, s_sc,
                       *, nb, lc):
    u = u_ref[...]
    u16 = u.astype(BF16)
    s_sc[...] = _dot(u16, p_ref[...])
    lam = lam_ref[...]
    lr, li = lam[:, :SSM_SW], lam[:, SSM_SW:]
    h = h0_ref[...]
    for r in range(lc):
        rows = slice(r * nb, (r + 1) * nb)
        s = s_sc[rows, :]
        s_sc[rows, :] = h
        hre, him = h[:, :SSM_SW], h[:, SSM_SW:]
        h = jnp.concatenate([lr * hre - li * him + s[:, :SSM_SW], lr * him + li * hre + s[:, SSM_SW:]], axis=1)
    hfin_ref[...] = h
    y = _dot_causal(u16, m_ref) + _dot(s_sc[...].astype(BF16), q_ref[...]) + u * d_ref[...]
    s_ref[...] = jax.nn.gelu(y).astype(BF16)


def _ssm_sample(u, sw, h0, nb, lc):
    tw = SSM_T * LANES
    sq = 2 * SSM_SW
    u2 = u.reshape(SSM_NJ, nb, lc, tw).transpose(0, 2, 1, 3).reshape(SSM_NJ, lc * nb, tw)
    kern = functools.partial(_ssm_sample_kernel, nb=nb, lc=lc)
    blk = lambda j: (j, 0, 0)
    s, hfin = pl.pallas_call(
        kern, grid=(SSM_NJ,),
        in_specs=[pl.BlockSpec((None, lc * nb, tw), blk), pl.BlockSpec((None, tw, tw), blk),
                  pl.BlockSpec((None, tw, sq), blk), pl.BlockSpec((None, sq, tw), blk),
                  pl.BlockSpec((None, 1, sq), blk), pl.BlockSpec((None, 1, tw), blk),
                  pl.BlockSpec((None, nb, sq), blk)],
        out_specs=[pl.BlockSpec((None, lc * nb, tw), blk), pl.BlockSpec((None, nb, sq), blk)],
        out_shape=[jax.ShapeDtypeStruct((SSM_NJ, lc * nb, tw), BF16),
                   jax.ShapeDtypeStruct((SSM_NJ, nb, sq), F32)],
        scratch_shapes=[pltpu.VMEM((lc * nb, sq), F32)],
        compiler_params=_cp(("arbitrary",)), name="ssm_sample",
    )(u2, sw["M"], sw["P"], sw["Q"], sw["lam"], sw["D"], h0)
    s = s.reshape(SSM_NJ, lc, nb, tw).transpose(0, 2, 1, 3).reshape(SSM_NJ, nb * lc, tw)
    return s, hfin


def _ssm_fold_kernel(wre_ref, wim_ref, cre_ref, cim_ref, plr_ref, pli_ref, plrt_ref, plit_ref, repw_ref, repc_ref,
                     m_ref, p_ref, q_ref):
    hi = lax.Precision.HIGHEST
    f32dot = lambda a, b: jnp.dot(a, b, precision=hi, preferred_element_type=F32)
    same_group_w = (lax.broadcasted_iota(jnp.int32, (LANES, SSM_SW), 0) // SSM_GROUP
                    == lax.broadcasted_iota(jnp.int32, (LANES, SSM_SW), 1) // SSM_STATE)
    same_group_c = (lax.broadcasted_iota(jnp.int32, (SSM_SW, LANES), 0) // SSM_STATE
                    == lax.broadcasted_iota(jnp.int32, (SSM_SW, LANES), 1) // SSM_GROUP)
    wb_re = jnp.where(same_group_w, f32dot(wre_ref[...], repw_ref[...]), 0.0)
    wb_im = jnp.where(same_group_w, f32dot(wim_ref[...], repw_ref[...]), 0.0)
    cb_re = jnp.where(same_group_c, f32dot(cre_ref[...], repc_ref[...]), 0.0)
    cb_im = jnp.where(same_group_c, f32dot(cim_ref[...], repc_ref[...]), 0.0)

    m_ref[...] = jnp.zeros_like(m_ref)
    for k in range(SSM_T):
        lr, li = plr_ref[k:k + 1, :], pli_ref[k:k + 1, :]
        v_re = lr * wb_re - li * wb_im
        v_im = lr * wb_im + li * wb_re
        s = SSM_T - 1 - k
        p_ref[s * LANES:(s + 1) * LANES, :SSM_SW] = v_re.astype(BF16)
        p_ref[s * LANES:(s + 1) * LANES, SSM_SW:] = v_im.astype(BF16)
        bd = (f32dot(v_re, cb_re) - f32dot(v_im, cb_im)).astype(BF16)
        for s in range(SSM_T - k):
            t = s + k
            m_ref[s * LANES:(s + 1) * LANES, t * LANES:(t + 1) * LANES] = bd
        lrc, lic = plrt_ref[:, k + 1:k + 2], plit_ref[:, k + 1:k + 2]
        q_ref[:SSM_SW, k * LANES:(k + 1) * LANES] = (cb_re * lrc - cb_im * lic).astype(BF16)
        q_ref[SSM_SW:, k * LANES:(k + 1) * LANES] = (-(cb_re * lic + cb_im * lrc)).astype(BF16)


def _ssm_weights(a_re, a_im, log_dt, b_re, b_im, c_re, c_im, d):
    t_ = SSM_T
    dt = jnp.exp(log_dt)[:, None]
    mag = jnp.exp(a_re * dt)
    lam_re, lam_im = mag * jnp.cos(a_im * dt), mag * jnp.sin(a_im * dt)
    den = a_re * a_re + a_im * a_im
    f_re = ((lam_re - 1.0) * a_re + lam_im * a_im) / den
    f_im = (lam_im * a_re - (lam_re - 1.0) * a_im) / den
    w_re = f_re[:, :, None] * b_re - f_im[:, :, None] * b_im
    w_im = f_re[:, :, None] * b_im + f_im[:, :, None] * b_re
    k = jnp.arange(t_ + 1, dtype=F32)[None, :, None]
    pmag = jnp.exp(a_re[:, None, :] * dt[:, :, None] * k)
    pang = a_im[:, None, :] * dt[:, :, None] * k
    pw_re, pw_im = pmag * jnp.cos(pang), pmag * jnp.sin(pang)

    def lanes_gn(x):
        x = x.reshape(SSM_NJ, SSM_GPB, t_ + 1, SSM_STATE)
        return jnp.swapaxes(x, 1, 2).reshape(SSM_NJ, t_ + 1, SSM_SW)

    pl_re, pl_im = lanes_gn(pw_re), lanes_gn(pw_im)
    wcat = [jnp.swapaxes(x, 1, 2).reshape(SSM_NJ, LANES, SSM_STATE) for x in (w_re, w_im)]
    ccat = [jnp.swapaxes(x, 1, 2).reshape(SSM_NJ, SSM_SW, SSM_GROUP) for x in (c_re, c_im)]
    rep_w = jnp.tile(jnp.eye(SSM_STATE, dtype=F32), (1, SSM_GPB))
    rep_c = jnp.tile(jnp.eye(SSM_GROUP, dtype=F32), (1, SSM_GPB))
    tw = t_ * LANES
    sq = 2 * SSM_SW
    blk = lambda j: (j, 0, 0)
    const = lambda j: (0, 0)
    per_j = lambda a: pl.BlockSpec((None,) + a.shape[1:], blk)
    ins = [wcat[0], wcat[1], ccat[0], ccat[1], pl_re, pl_im, jnp.swapaxes(pl_re, 1, 2), jnp.swapaxes(pl_im, 1, 2)]
    m, p, q = pl.pallas_call(
        _ssm_fold_kernel, grid=(SSM_NJ,),
        in_specs=[per_j(a) for a in ins] + [pl.BlockSpec(rep_w.shape, const), pl.BlockSpec(rep_c.shape, const)],
        out_specs=[pl.BlockSpec((None, tw, tw), blk), pl.BlockSpec((None, tw, sq), blk),
                   pl.BlockSpec((None, sq, tw), blk)],
        out_shape=[jax.ShapeDtypeStruct((SSM_NJ, tw, tw), BF16), jax.ShapeDtypeStruct((SSM_NJ, tw, sq), BF16),
                   jax.ShapeDtypeStruct((SSM_NJ, sq, tw), BF16)],
        compiler_params=_cp(("arbitrary",)), name="ssm_fold",
    )(*ins, rep_w, rep_c)
    lam_t = jnp.concatenate([pl_re[:, t_][:, None, :], pl_im[:, t_][:, None, :]], axis=2)
    dvec = jnp.tile(d.reshape(SSM_NJ, 1, LANES), (1, 1, t_))
    return {"M": m, "P": p, "Q": q, "lam": lam_t, "D": dvec}


def _mixer_kernel(o_ref, s_ref, ga_ref, gb_ref, x_ref, wo_ref, wv_ref, wg_ref, wout_ref, h_ref, s_sc):
    o_a = _dot(o_ref[...], wo_ref[...])
    nchunk = s_sc.shape[1] // SSM_T
    for j in range(SSM_NJ):
        sj = s_ref[j].astype(F32)
        for t in range(SSM_T):
            s_sc[j, pl.ds(t, nchunk, stride=SSM_T), :] = sj[:, t * LANES:(t + 1) * LANES]
    s = jnp.concatenate([s_sc[j] for j in range(SSM_NJ)], axis=1).astype(BF16)
    o_b = _dot(s, wv_ref[...]) * jax.nn.sigmoid(_dot(s, wg_ref[...]))
    merged = ga_ref[...] * o_a + gb_ref[...] * o_b
    h_ref[...] = x_ref[...] + _dot(merged.astype(BF16), wout_ref[...])


def _mixer(o, s, ga, gb, x, w, tm):
    n = x.shape[0]
    row = lambda i: (i, 0)
    const = lambda i: (0, 0)
    wspec = pl.BlockSpec((D_MODEL, D_MODEL), const)
    act = pl.BlockSpec((tm, D_MODEL), row)
    return pl.pallas_call(
        _mixer_kernel, grid=(n // tm,),
        in_specs=[act, pl.BlockSpec((SSM_NJ, tm // SSM_T, SSM_T * LANES), lambda i: (0, i, 0)), act, act, act,
                  wspec, wspec, wspec, wspec],
        out_specs=act, out_shape=jax.ShapeDtypeStruct((n, D_MODEL), F32),
        scratch_shapes=[pltpu.VMEM((SSM_NJ, tm, LANES), F32)],
        compiler_params=_cp(("arbitrary",)), name="mixer_out",
    )(o, s, ga, gb, x, w["wo"], w["wv"], w["wg"], w["wout"])


def _mlp_kernel(h_ref, g_ref, wup_ref, wdn_ref, y_ref):
    h = h_ref[...]
    hn = _rms(h, g_ref[...]).astype(BF16)
    a = jnp.maximum(_dot(hn, wup_ref[...]), 0.0)
    y_ref[...] = h + _dot((a * a).astype(BF16), wdn_ref[...])


def _mlp(h, w, tm):
    n = h.shape[0]
    row = lambda i: (i, 0)
    const = lambda i: (0, 0)
    act = pl.BlockSpec((tm, D_MODEL), row)
    return pl.pallas_call(
        _mlp_kernel, grid=(n // tm,),
        in_specs=[act, pl.BlockSpec((1, D_MODEL), const), pl.BlockSpec((D_MODEL, D_FF), const),
                  pl.BlockSpec((D_FF, D_MODEL), const)],
        out_specs=act, out_shape=jax.ShapeDtypeStruct((n, D_MODEL), F32),
        compiler_params=_cp(("arbitrary",)), name="mlp",
    )(h, w["gmlp"], w["wup"], w["wdn"])


def _rope_tables(pos):
    half = QK_ROPE // 2
    inv = ROPE_THETA ** (-jnp.arange(half, dtype=F32) / half)
    ang = pos.astype(F32)[:, None] * inv[None, :]
    cos, sin = jnp.cos(ang), jnp.sin(ang)
    z = jnp.zeros_like(cos)
    pad = jnp.zeros((pos.shape[0], LANES - QK_ROPE), F32)
    ct = jnp.concatenate([cos, cos, pad], axis=1)
    s1 = jnp.concatenate([-sin, z, pad], axis=1)
    s2 = jnp.concatenate([z, sin, pad], axis=1)
    return ct, s1, s2


def _pad_lanes(a, width):
    return jnp.pad(a, [(0, 0)] * (a.ndim - 1) + [(0, width - a.shape[-1])])


def _layer_weights(norm_mix, w_in, q_lora_norm, w_uq, q_nope_norm, q_rope_norm, kv_lora_norm, k_rope_norm,
                   w_ukv, k_nope_norm, w_o_attn, w_glu_v, w_glu_g, w_out, norm_mlp, w_mlp_up, w_mlp_down):
    o_kv = Q_LORA
    o_kr = o_kv + KV_LORA
    o_ssm = o_kr + QK_ROPE
    o_ga = o_ssm + D_MODEL
    o_gb = o_ga + D_MODEL
    bf = lambda a: a.astype(BF16)
    r2 = lambda a: a.reshape(1, -1).astype(F32)
    return {
        "nmix": r2(norm_mix),
        "wq": bf(w_in[:, :o_kv]), "wkv": bf(w_in[:, o_kv:o_kr]),
        "wkr": bf(_pad_lanes(w_in[:, o_kr:o_ssm], LANES)),
        "wu": bf(w_in[:, o_ssm:o_ga]), "wga": bf(w_in[:, o_ga:o_gb]), "wgb": bf(w_in[:, o_gb:]),
        "gq": r2(q_lora_norm),
        "wuqn": bf(w_uq[:, :, :QK_NOPE].reshape(Q_LORA, N_HEADS * QK_NOPE)),
        "wuqr": bf(_pad_lanes(w_uq[:, :, QK_NOPE:], LANES).reshape(Q_LORA, N_HEADS * LANES)),
        "gqn": r2(q_nope_norm), "gqr": _pad_lanes(r2(q_rope_norm), LANES),
        "gkv": r2(kv_lora_norm), "gkr": _pad_lanes(r2(k_rope_norm), LANES),
        "wuk": bf(w_ukv[:, :, :QK_NOPE].reshape(KV_LORA, N_HEADS * QK_NOPE)),
        "wukg": bf((w_ukv[:, :, :QK_NOPE] * k_nope_norm.astype(F32)).reshape(KV_LORA, N_HEADS * QK_NOPE)),
        "wuv": bf(w_ukv[:, :, QK_NOPE:].reshape(KV_LORA, N_HEADS * V_DIM)),
        "gkn": r2(k_nope_norm),
        "wo": bf(w_o_attn.reshape(N_HEADS * V_DIM, D_MODEL)),
        "wv": bf(w_glu_v), "wg": bf(w_glu_g), "wout": bf(w_out),
        "gmlp": r2(norm_mlp), "wup": bf(w_mlp_up), "wdn": bf(w_mlp_down),
    }


def kernel(x_prompt, x_sample, cache_latent, cache_krope, cache_meta_latent, cache_meta_krope, state_ssm_re, state_ssm_im, meta_tokens, norm_mix, w_in, q_lora_norm, w_uq, q_nope_norm, q_rope_norm, kv_lora_norm, k_rope_norm, w_ukv, k_nope_norm, w_o_attn, ssm_a_re, ssm_a_im, ssm_log_dt, ssm_b_re, ssm_b_im, ssm_c_re, ssm_c_im, ssm_d, w_glu_v, w_glu_g, w_out, norm_mlp, w_mlp_up, w_mlp_down):
    bsz_p, seq_p = x_prompt.shape[0], x_prompt.shape[1]
    bsz_s, seq_s = x_sample.shape[0], x_sample.shape[1]
    past = cache_latent.shape[2]
    depth = w_in.shape[0]
    assert depth == 1, "single-layer step"
    l = 0
    w = _layer_weights(norm_mix[l], w_in[l], q_lora_norm[l], w_uq[l], q_nope_norm[l], q_rope_norm[l],
                       kv_lora_norm[l], k_rope_norm[l], w_ukv[l], k_nope_norm[l], w_o_attn[l], w_glu_v[l],
                       w_glu_g[l], w_out[l], norm_mlp[l], w_mlp_up[l], w_mlp_down[l])
    sw = _ssm_weights(ssm_a_re[l].astype(F32), ssm_a_im[l].astype(F32), ssm_log_dt[l].astype(F32),
                      ssm_b_re[l].astype(F32), ssm_b_im[l].astype(F32), ssm_c_re[l].astype(F32),
                      ssm_c_im[l].astype(F32), ssm_d[l].astype(F32))
    sq = 2 * SSM_SW

    tabs_m = _rope_tables(jnp.arange(N_META, dtype=jnp.int32) - N_META)
    (_, mk, mv, mckv, mkr, mu, _, _) = _front(meta_tokens.astype(F32), tabs_m, 1, w, N_META)
    h0_zero = jnp.zeros((SSM_NJ, 1, sq), F32)
    _, h_meta = _ssm_prompt(mu, sw, h0_zero, 1, N_META // SSM_T, N_META // SSM_T)

    n_p = bsz_p * seq_p
    tm = 256
    xp = x_prompt.reshape(n_p, D_MODEL)
    tabs_p = _rope_tables(jnp.arange(seq_p, dtype=jnp.int32))
    q, k, v, ckv_p, kr_p, u_p, ga, gb = _front(xp, tabs_p, seq_p // tm, w, tm)
    o_p = _attn_prompt(q, k, v, mk, mv, bsz_p, seq_p)
    h0_p = jnp.broadcast_to(h_meta, (SSM_NJ, bsz_p, sq))
    s_p, hfin_p = _ssm_prompt(u_p, sw, h0_p, bsz_p, seq_p // SSM_T, 128)
    h1_p = _mixer(o_p, s_p, ga, gb, xp, w, tm)
    y_p = _mlp(h1_p, w, tm)

    n_s = bsz_s * seq_s
    xs = x_sample.reshape(n_s, D_MODEL)
    pos_s = past + jnp.arange(seq_s, dtype=jnp.int32)
    tabs_s = tuple(jnp.tile(t, (bsz_s, 1)) for t in _rope_tables(pos_s))
    tm_s = 256
    q_s, k_s, _, ckv_s, kr_s, u_s, ga_s, gb_s = _front(xs, tabs_s, n_s // tm_s, w, tm_s)
    o_s = _attn_sample(q_s, k_s, ckv_s, cache_latent[l].astype(F32), cache_krope[l].astype(F32),
                       cache_meta_latent[l].astype(F32), cache_meta_krope[l].astype(F32), w, bsz_s, seq_s, past)

    def to_blocks(st):
        return st.astype(F32).reshape(bsz_s, SSM_NJ, SSM_SW).transpose(1, 0, 2)

    h0_s = jnp.concatenate([to_blocks(state_ssm_re[l]), to_blocks(state_ssm_im[l])], axis=2)
    s_s, hfin_s = _ssm_sample(u_s, sw, h0_s, bsz_s, seq_s // SSM_T)
    h1_s = _mixer(o_s, s_s, ga_s, gb_s, xs, w, tm_s)
    y_s = _mlp(h1_s, w, tm_s)

    def from_blocks(hf, nb):
        re = hf[:, :, :SSM_SW].transpose(1, 0, 2).reshape(1, nb, N_GROUPS, SSM_STATE)
        im = hf[:, :, SSM_SW:].transpose(1, 0, 2).reshape(1, nb, N_GROUPS, SSM_STATE)
        return re, im

    sre_p, sim_p = from_blocks(hfin_p, bsz_p)
    sre_s, sim_s = from_blocks(hfin_s, bsz_s)
    return (y_p.reshape(bsz_p, seq_p, D_MODEL), y_s.reshape(bsz_s, seq_s, D_MODEL),
            ckv_p.reshape(1, bsz_p, seq_p, KV_LORA), kr_p.reshape(1, bsz_p, seq_p, QK_ROPE),
            jnp.broadcast_to(mckv[None, None], (1, bsz_p, N_META, KV_LORA)),
            jnp.broadcast_to(mkr[None, None], (1, bsz_p, N_META, QK_ROPE)),
            sre_p, sim_p,
            ckv_s.reshape(1, bsz_s, seq_s, KV_LORA), kr_s.reshape(1, bsz_s, seq_s, QK_ROPE),
            sre_s, sim_s)
```

```python
import functools
import math

import jax
import jax.numpy as jnp
from jax import lax
from jax.experimental import pallas as pl
from jax.experimental.pallas import tpu as pltpu

F32 = jnp.float32
BF16 = jnp.bfloat16

D_MODEL = 1024
CHUNK = 64
N_META = 16
N_HEADS = 8
QK_NOPE = 128
QK_ROPE = 64
V_DIM = 128
QK_DIM = QK_NOPE + QK_ROPE
Q_LORA = 384
KV_LORA = 256
SSM_GROUP = 16
N_GROUPS = D_MODEL // SSM_GROUP
SSM_STATE = 64
D_FF = 4 * D_MODEL
ROPE_THETA = 10000.0
EPS = 1e-6
ATTN_SCALE = QK_DIM ** -0.5
Q_SCALE = ATTN_SCALE * math.log2(math.e)
NEG_INF = -1e30

LANES = 128
HEAD_PAD = 2 * LANES
SSM_T = 8
SSM_NJ = D_MODEL // LANES
SSM_GPB = LANES // SSM_GROUP
SSM_SW = SSM_GPB * SSM_STATE
VMEM_LIMIT = 56 * 1024 * 1024


def _cp(sem):
    return pltpu.CompilerParams(dimension_semantics=sem, vmem_limit_bytes=VMEM_LIMIT)


def _dot(a, b):
    return jnp.dot(a, b, preferred_element_type=F32)


def _dot_nt(a, b):
    return lax.dot_general(a, b, (((1,), (1,)), ((), ())), preferred_element_type=F32)


def _rms(x, g, n=None):
    n = x.shape[-1] if n is None else n
    ms = jnp.sum(x * x, axis=-1, keepdims=True) * (1.0 / n)
    return x * lax.rsqrt(ms + EPS) * g


def _rope128(b, ct, s1, s2):
    return b * ct + pltpu.roll(b, LANES - QK_ROPE // 2, 1) * s1 + pltpu.roll(b, QK_ROPE // 2, 1) * s2


def _front_kernel(x_ref, ct_ref, s1_ref, s2_ref, nmix_ref, wq_ref, wkv_ref, wkr_ref, wu_ref, wga_ref,
                  wgb_ref, gq_ref, wuqn_ref, wuqr_ref, gqn_ref, gqr_ref, gkv_ref, gkr_ref, wuk_ref,
                  wuv_ref, gkn_ref,
                  q_ref, k_ref, v_ref, ckv_ref, kr_ref, u_ref, ga_ref, gb_ref, u_sc):
    x = x_ref[...]
    xn = _rms(x, nmix_ref[...]).astype(BF16)
    ct, s1, s2 = ct_ref[...], s1_ref[...], s2_ref[...]


    c_q = _rms(_dot(xn, wq_ref[...]), gq_ref[...]).astype(BF16)
    qn = _dot(c_q, wuqn_ref[...])
    qr = _dot(c_q, wuqr_ref[...])
    for h in range(N_HEADS):
        sl = slice(h * LANES, (h + 1) * LANES)
        q_ref[:, h * HEAD_PAD:h * HEAD_PAD + LANES] = (_rms(qn[:, sl], gqn_ref[...]) * Q_SCALE).astype(BF16)
        r = _rope128(_rms(qr[:, sl], gqr_ref[...], QK_ROPE), ct, s1, s2)
        q_ref[:, h * HEAD_PAD + LANES:(h + 1) * HEAD_PAD] = (r * Q_SCALE).astype(BF16)

    c_kv = _rms(_dot(xn, wkv_ref[...]), gkv_ref[...])
    ckv_ref[...] = c_kv
    kr = _rms(_dot(xn, wkr_ref[...]), gkr_ref[...], QK_ROPE)
    kr = _rope128(kr, ct, s1, s2)
    kr_ref[...] = kr[:, :QK_ROPE]
    kr16 = kr.astype(BF16)
    c16 = c_kv.astype(BF16)
    kn = _dot(c16, wuk_ref[...])
    for h in range(N_HEADS):
        sl = slice(h * LANES, (h + 1) * LANES)
        k_ref[:, h * HEAD_PAD:h * HEAD_PAD + LANES] = _rms(kn[:, sl], gkn_ref[...]).astype(BF16)
        k_ref[:, h * HEAD_PAD + LANES:(h + 1) * HEAD_PAD] = kr16
    v_ref[...] = _dot(c16, wuv_ref[...]).astype(BF16)

    u = _dot(xn, wu_ref[...])
    nchunk = u_sc.shape[1] // SSM_T
    for j in range(SSM_NJ):
        u_sc[j] = u[:, j * LANES:(j + 1) * LANES]
        for t in range(SSM_T):
            u_ref[j, :, t * LANES:(t + 1) * LANES] = u_sc[j, pl.ds(t, nchunk, stride=SSM_T), :]
    ga_ref[...] = jax.nn.sigmoid(_dot(xn, wga_ref[...]))
    gb_ref[...] = jax.nn.sigmoid(_dot(xn, wgb_ref[...]))


def _front(x, tabs, tab_blocks, w, tm):
    n = x.shape[0]
    nt = n // tm
    ct, s1, s2 = tabs

    def row(i):
        return (i, 0)

    def tab(i):
        return (i % tab_blocks, 0)

    def const(i):
        return (0, 0)

    def full(a):
        return pl.BlockSpec(a.shape, const, pipeline_mode=pl.Buffered(1))

    weights = [w["nmix"], w["wq"], w["wkv"], w["wkr"], w["wu"], w["wga"], w["wgb"], w["gq"], w["wuqn"],
               w["wuqr"], w["gqn"], w["gqr"], w["gkv"], w["gkr"], w["wuk"], w["wuv"], w["gkn"]]
    wide = N_HEADS * LANES
    qk_wide = N_HEADS * HEAD_PAD
    out_shape = [
        jax.ShapeDtypeStruct((n, qk_wide), BF16),
        jax.ShapeDtypeStruct((n, qk_wide), BF16),
        jax.ShapeDtypeStruct((n, wide), BF16),
        jax.ShapeDtypeStruct((n, KV_LORA), F32),
        jax.ShapeDtypeStruct((n, QK_ROPE), F32),
        jax.ShapeDtypeStruct((SSM_NJ, n // SSM_T, SSM_T * LANES), F32),
        jax.ShapeDtypeStruct((n, D_MODEL), F32),
        jax.ShapeDtypeStruct((n, D_MODEL), F32),
    ]
    out_specs = [
        pl.BlockSpec((tm, qk_wide), row), pl.BlockSpec((tm, qk_wide), row),
        pl.BlockSpec((tm, wide), row), pl.BlockSpec((tm, KV_LORA), row),
        pl.BlockSpec((tm, QK_ROPE), row),
        pl.BlockSpec((SSM_NJ, tm // SSM_T, SSM_T * LANES), lambda i: (0, i, 0)),
        pl.BlockSpec((tm, D_MODEL), row), pl.BlockSpec((tm, D_MODEL), row),
    ]
    in_specs = ([pl.BlockSpec((tm, D_MODEL), row)] + [pl.BlockSpec((tm, LANES), tab)] * 3
                + [full(a) for a in weights])
    return pl.pallas_call(
        _front_kernel, grid=(nt,), in_specs=in_specs, out_specs=out_specs, out_shape=out_shape,
        scratch_shapes=[pltpu.VMEM((SSM_NJ, tm, LANES), F32)],
        compiler_params=_cp(("arbitrary",)), name="front",
    )(x, ct, s1, s2, *weights)


ATT_T = 512


def _with_ones(v):
    return jnp.concatenate([v, jnp.ones_like(v)], axis=1)


def _softmax_step(s, v, m_ref, acc_ref):
    m_prev = m_ref[...]
    m_new = jnp.maximum(m_prev, jnp.max(s, axis=-1, keepdims=True))
    alpha = jnp.exp2(m_prev - m_new)
    p = jnp.exp2(s - jnp.tile(m_new, (1, s.shape[1] // LANES)))
    acc_ref[...] = jnp.tile(alpha, (1, 2)) * acc_ref[...] + _dot(p.astype(BF16), _with_ones(v))
    m_ref[...] = m_new


ATT_HG = 4


def _attn_prompt_kernel(q_ref, k_ref, v_ref, mk_ref, mv_ref, o_ref, a_sc, b_sc, m_ref, acc_ref):
    i = pl.program_id(2)
    heads = range(ATT_HG)

    def key_rows(kt):
        return pl.ds(pl.multiple_of(kt * ATT_T, ATT_T), ATT_T)

    def qk_lanes(g):
        return slice(g * HEAD_PAD, (g + 1) * HEAD_PAD)

    def v_lanes(g):
        return slice(g * V_DIM, (g + 1) * V_DIM)

    def scores(g, kt):
        return _dot_nt(q_ref[:, qk_lanes(g)], k_ref[key_rows(kt), qk_lanes(g)])

    def consume(g, buf, kt, diagonal):
        s = buf[g]
        if diagonal:
            qc = lax.broadcasted_iota(jnp.int32, (ATT_T, ATT_T), 0) // CHUNK
            kc = lax.broadcasted_iota(jnp.int32, (ATT_T, ATT_T), 1) // CHUNK
            s = jnp.where(kc <= qc, s, NEG_INF)
        _softmax_step(s, v_ref[key_rows(kt), v_lanes(g)], m_ref.at[g], acc_ref.at[g])

    for g in heads:
        a_sc[g] = scores(g, 0)

    for g in heads:
        s = _dot_nt(q_ref[:, qk_lanes(g)], mk_ref[:, qk_lanes(g)])
        m0 = jnp.max(s, axis=-1, keepdims=True)
        p = jnp.exp2(s - m0)
        m_ref[g] = jnp.broadcast_to(m0, (ATT_T, LANES))
        acc_ref[g] = _dot(p.astype(BF16), _with_ones(mv_ref[:, v_lanes(g)]))

    def pair(pi, carry):
        kt = 2 * pi
        for g in heads:
            b_sc[g] = scores(g, kt + 1)
            consume(g, a_sc, kt, False)
        for g in heads:
            a_sc[g] = scores(g, kt + 2)
            consume(g, b_sc, kt + 1, False)
        return carry

    lax.fori_loop(0, lax.shift_right_logical(i, 1), pair, 0)

    @pl.when((i & 1) == 1)
    def _():
        for g in heads:
            b_sc[g] = scores(g, i)
            consume(g, a_sc, i - 1, False)
        for g in heads:
            consume(g, b_sc, i, True)

    @pl.when((i & 1) == 0)
    def _():
        for g in heads:
            consume(g, a_sc, i, True)

    for g in heads:
        o_ref[:, v_lanes(g)] = (acc_ref[g, :, :V_DIM] / acc_ref[g, :, V_DIM:]).astype(BF16)


def _attn_prompt(q, k, v, mk, mv, bsz, seq):
    nq = seq // ATT_T
    n = bsz * seq
    qmap = lambda b, h, i: (b * nq + i, h)
    return pl.pallas_call(
        _attn_prompt_kernel, grid=(bsz, N_HEADS // ATT_HG, nq),
        in_specs=[pl.BlockSpec((ATT_T, ATT_HG * HEAD_PAD), qmap),
                  pl.BlockSpec((seq, ATT_HG * HEAD_PAD), lambda b, h, i: (b, h)),
                  pl.BlockSpec((seq, ATT_HG * V_DIM), lambda b, h, i: (b, h)),
                  pl.BlockSpec((N_META, ATT_HG * HEAD_PAD), lambda b, h, i: (0, h)),
                  pl.BlockSpec((N_META, ATT_HG * V_DIM), lambda b, h, i: (0, h))],
        out_specs=pl.BlockSpec((ATT_T, ATT_HG * V_DIM), qmap),
        out_shape=jax.ShapeDtypeStruct((n, N_HEADS * V_DIM), BF16),
        scratch_shapes=[pltpu.VMEM((ATT_HG, ATT_T, ATT_T), F32), pltpu.VMEM((ATT_HG, ATT_T, ATT_T), F32),
                        pltpu.VMEM((ATT_HG, ATT_T, LANES), F32), pltpu.VMEM((ATT_HG, ATT_T, 2 * V_DIM), F32)],
        compiler_params=_cp(("arbitrary", "arbitrary", "arbitrary")), name="attn_prompt",
    )(q, k, v, mk, mv)


SAMPLE_KEY_CHUNK = 512


def _attn_sample_kernel(q_ref, cm_ref, cn_ref, cc_ref, krm_ref, krn_ref, krc_ref, wuk_ref, wukg_ref, wuv_ref,
                        hsum_ref, o_ref, c_sc, kr_sc, st_sc, pt_sc, *, seq, past):
    n_small = N_META + seq
    n_keys = n_small + past

    c_sc[:, KV_LORA:] = jnp.ones((n_keys, LANES), BF16)
    c_sc[0:N_META, :KV_LORA] = cm_ref[...].astype(BF16)
    c_sc[N_META:n_small, :KV_LORA] = cn_ref[...].astype(BF16)
    c_sc[n_small:, :KV_LORA] = cc_ref[...].astype(BF16)
    kr_sc[:, QK_ROPE:] = jnp.zeros((n_keys, LANES - QK_ROPE), BF16)
    kr_sc[0:N_META, :QK_ROPE] = krm_ref[...].astype(BF16)
    kr_sc[N_META:n_small, :] = krn_ref[...]
    kr_sc[n_small:, :QK_ROPE] = krc_ref[...].astype(BF16)

    qa, qr = [], []
    for h in range(N_HEADS):
        qn_h = q_ref[:, h * HEAD_PAD:h * HEAD_PAD + LANES]
        qa.append(_dot_nt(qn_h, wukg_ref[:, h * LANES:(h + 1) * LANES]))
        qr.append(q_ref[:, h * HEAD_PAD + LANES:(h + 1) * HEAD_PAD])
    qa = jnp.concatenate(qa, axis=0).astype(BF16)
    qr = jnp.concatenate(qr, axis=0)

    bounds = [(0, n_small)] + [(n_small + i, n_small + i + SAMPLE_KEY_CHUNK)
                               for i in range(0, past, SAMPLE_KEY_CHUNK)]
    m = None
    for lo, hi in bounds:
        c16 = c_sc[lo:hi, :KV_LORA]
        kn = _dot(c16, wuk_ref[...])
        ssq = _dot((kn * kn).astype(BF16), hsum_ref[...])
        r = lax.rsqrt(ssq * (1.0 / QK_NOPE) + EPS)
        st = _dot_nt(c16, qa) * r + _dot_nt(kr_sc[lo:hi, :], qr)
        st_sc[lo:hi, :] = st
        cm = jnp.max(st, axis=0, keepdims=True)
        m = cm if m is None else jnp.maximum(m, cm)

    for lo, hi in bounds:
        pt_sc[lo:hi, :] = jnp.exp2(st_sc[lo:hi, :] - m).astype(BF16)
    pc = lax.dot_general(pt_sc[...], c_sc[...], (((0,), (0,)), ((), ())), preferred_element_type=F32)
    pcn = (pc[:, :KV_LORA] / jnp.tile(pc[:, KV_LORA:], (1, KV_LORA // LANES))).astype(BF16)
    for h in range(N_HEADS):
        o_ref[:, h * V_DIM:(h + 1) * V_DIM] = _dot(pcn[h * seq:(h + 1) * seq, :],
                                                   wuv_ref[:, h * V_DIM:(h + 1) * V_DIM]).astype(BF16)


def _attn_sample(q, k_new, c_new, cache_c, cache_kr, meta_c, meta_kr, w, bsz, seq, past):
    n_keys = N_META + seq + past
    hq = N_HEADS * seq
    hsum = jnp.repeat(jnp.repeat(jnp.eye(N_HEADS, dtype=BF16), QK_NOPE, axis=0), seq, axis=1)
    row = lambda b: (b, 0)
    b3 = lambda b: (b, 0, 0)
    const = lambda b: (0, 0)
    kern = functools.partial(_attn_sample_kernel, seq=seq, past=past)
    return pl.pallas_call(
        kern, grid=(bsz,),
        in_specs=[pl.BlockSpec((seq, N_HEADS * HEAD_PAD), row),
                  pl.BlockSpec((None, N_META, KV_LORA), b3), pl.BlockSpec((seq, KV_LORA), row),
                  pl.BlockSpec((None, past, KV_LORA), b3),
                  pl.BlockSpec((None, N_META, QK_ROPE), b3), pl.BlockSpec((seq, LANES), lambda b: (b, 1)),
                  pl.BlockSpec((None, past, QK_ROPE), b3),
                  pl.BlockSpec(w["wuk"].shape, const), pl.BlockSpec(w["wukg"].shape, const),
                  pl.BlockSpec(w["wuv"].shape, const), pl.BlockSpec(hsum.shape, const)],
        out_specs=pl.BlockSpec((seq, N_HEADS * V_DIM), row),
        out_shape=jax.ShapeDtypeStruct((bsz * seq, N_HEADS * V_DIM), BF16),
        scratch_shapes=[pltpu.VMEM((n_keys, KV_LORA + LANES), BF16), pltpu.VMEM((n_keys, LANES), BF16),
                        pltpu.VMEM((n_keys, hq), F32), pltpu.VMEM((n_keys, hq), BF16)],
        compiler_params=_cp(("arbitrary",)), name="attn_sample",
    )(q, meta_c, c_new, cache_c, meta_kr, k_new, cache_kr, w["wuk"], w["wukg"], w["wuv"], hsum)


MXU_DIM = 256


def _dot_causal(u16, m_ref):
    nblk = m_ref.shape[0] // MXU_DIM
    return jnp.concatenate(
        [_dot(u16[:, :(cb + 1) * MXU_DIM], m_ref[:(cb + 1) * MXU_DIM, cb * MXU_DIM:(cb + 1) * MXU_DIM])
         for cb in range(nblk)], axis=1)


SSM_PACK = 4
SSM_SC = SSM_SW // LANES
SUBLANES = 2 * SSM_PACK


def _ssm_prompt_kernel(u_ref, m_ref, p_ref, q_ref, lam_ref, d_ref, h0_ref, s_ref, hfin_ref, s_sc, h_sc,
                       *, nb, rt):
    t = pl.program_id(1)

    @pl.when(t == 0)
    def _():
        h_sc[...] = h0_ref[...]

    if nb < SSM_PACK:
        s_sc[...] = jnp.zeros_like(s_sc)
    for b in range(nb):
        sb = _dot(u_ref[b].astype(BF16), p_ref[...])
        for c in range(SSM_SC):
            s_sc[c, pl.ds(b, rt, stride=SUBLANES), :] = sb[:, c * LANES:(c + 1) * LANES]
            s_sc[c, pl.ds(SSM_PACK + b, rt, stride=SUBLANES), :] = sb[:, SSM_SW + c * LANES:SSM_SW + (c + 1) * LANES]

    lam = lam_ref[...]
    im_rows = lax.broadcasted_iota(jnp.int32, (SUBLANES, LANES), 0) >= SSM_PACK
    coef = []
    for c in range(SSM_SC):
        lr = jnp.broadcast_to(lam[:, c * LANES:(c + 1) * LANES], (SUBLANES, LANES))
        li = jnp.broadcast_to(lam[:, SSM_SW + c * LANES:SSM_SW + (c + 1) * LANES], (SUBLANES, LANES))
        coef.append((lr, jnp.where(im_rows, li, -li)))

    def step(r, hs):
        rows = pl.ds(pl.multiple_of(r * SUBLANES, SUBLANES), SUBLANES)
        out = []
        for c in range(SSM_SC):
            s = s_sc[c, rows, :]
            s_sc[c, rows, :] = hs[c]
            a, bb = coef[c]
            out.append(a * hs[c] + bb * pltpu.roll(hs[c], SSM_PACK, 0) + s)
        return tuple(out)

    hs = lax.fori_loop(0, rt, step, tuple(h_sc[c] for c in range(SSM_SC)))
    for c in range(SSM_SC):
        h_sc[c] = hs[c]

    for b in range(nb):
        ub = u_ref[b]
        hb = jnp.concatenate([s_sc[c, pl.ds(b, rt, stride=SUBLANES), :] for c in range(SSM_SC)]
                             + [s_sc[c, pl.ds(SSM_PACK + b, rt, stride=SUBLANES), :] for c in range(SSM_SC)],
                             axis=1)
        y = _dot_causal(ub.astype(BF16), m_ref) + _dot(hb.astype(BF16), q_ref[...]) + ub * d_ref[...]
        s_ref[b] = jax.nn.gelu(y).astype(BF16)

    @pl.when(t == pl.num_programs(1) - 1)
    def _():
        hfin_ref[...] = h_sc[...]


def _pack_state(h):
    nj, nb, _ = h.shape
    h = jnp.pad(h, ((0, 0), (0, SSM_PACK - nb), (0, 0)))
    re = h[:, :, :SSM_SW].reshape(nj, SSM_PACK, SSM_SC, LANES)
    im = h[:, :, SSM_SW:].reshape(nj, SSM_PACK, SSM_SC, LANES)
    return jnp.concatenate([re, im], axis=1).transpose(0, 2, 1, 3)


def _unpack_state(hp, nb):
    nj = hp.shape[0]
    x = hp.transpose(0, 2, 1, 3).reshape(nj, SUBLANES, SSM_SW)
    return jnp.concatenate([x[:, :nb], x[:, SSM_PACK:SSM_PACK + nb]], axis=2)


def _ssm_prompt(u, sw, h0, nb, lc, rt):
    assert nb <= SSM_PACK
    tw = SSM_T * LANES
    u4 = u.reshape(SSM_NJ, nb, lc, tw)
    sq = 2 * SSM_SW
    kern = functools.partial(_ssm_prompt_kernel, nb=nb, rt=rt)
    st_spec = pl.BlockSpec((None, SSM_SC, SUBLANES, LANES), lambda j, t: (j, 0, 0, 0))
    s, hfin = pl.pallas_call(
        kern, grid=(SSM_NJ, lc // rt),
        in_specs=[pl.BlockSpec((None, nb, rt, tw), lambda j, t: (j, 0, t, 0)),
                  pl.BlockSpec((None, tw, tw), lambda j, t: (j, 0, 0)),
                  pl.BlockSpec((None, tw, sq), lambda j, t: (j, 0, 0)),
                  pl.BlockSpec((None, sq, tw), lambda j, t: (j, 0, 0)),
                  pl.BlockSpec((None, 1, sq), lambda j, t: (j, 0, 0)),
                  pl.BlockSpec((None, 1, tw), lambda j, t: (j, 0, 0)),
                  st_spec],
        out_specs=[pl.BlockSpec((None, nb, rt, tw), lambda j, t: (j, 0, t, 0)), st_spec],
        out_shape=[jax.ShapeDtypeStruct((SSM_NJ, nb, lc, tw), BF16),
                   jax.ShapeDtypeStruct((SSM_NJ, SSM_SC, SUBLANES, LANES), F32)],
        scratch_shapes=[pltpu.VMEM((SSM_SC, rt * SUBLANES, LANES), F32),
                        pltpu.VMEM((SSM_SC, SUBLANES, LANES), F32)],
        compiler_params=_cp(("arbitrary", "arbitrary")), name="ssm_prompt",
    )(u4, sw["M"], sw["P"], sw["Q"], sw["lam"], sw["D"], _pack_state(h0))
    return s.reshape(SSM_NJ, nb * lc, tw), _unpack_state(hfin, nb)


def _ssm_state_kernel(u_ref, p_ref, lam_ref, h_ref):
    s = _dot(u_ref[...].astype(BF16), p_ref[...])
    lam = lam_ref[...]
    lr, li = lam[:, :SSM_SW], lam[:, SSM_SW:]
    h = s[0:1, :]
    for r in range(1, s.shape[0]):
        hre, him = h[:, :SSM_SW], h[:, SSM_SW:]
        h = jnp.concatenate([lr * hre - li * him, lr * him + li * hre], axis=1) + s[r:r + 1, :]
    h_ref[...] = h


def _ssm_state(u, sw, lc):
    tw = SSM_T * LANES
    sq = 2 * SSM_SW
    blk = lambda j: (j, 0, 0)
    return pl.pallas_call(
        _ssm_state_kernel, grid=(SSM_NJ,),
        in_specs=[pl.BlockSpec((None, lc, tw), blk), pl.BlockSpec((None, tw, sq), blk),
                  pl.BlockSpec((None, 1, sq), blk)],
        out_specs=pl.BlockSpec((None, 1, sq), blk),
        out_shape=jax.ShapeDtypeStruct((SSM_NJ, 1, sq), F32),
        compiler_params=_cp(("arbitrary",)), name="ssm_state",
    )(u, sw["P"], sw["lam"])


def _ssm_sample_kernel(u_ref, m_ref, p_ref, q_ref, lam_ref, d_ref, h0_ref, s_ref, hfin_ref, s_sc,
                       *, nb, lc):
    u = u_ref[...]
    u16 = u.astype(BF16)
    s_sc[...] = _dot(u16, p_ref[...])
    lam = lam_ref[...]
    lr, li = lam[:, :SSM_SW], lam[:, SSM_SW:]
    h = h0_ref[...]
    for r in range(lc):
        rows = slice(r * nb, (r + 1) * nb)
        s = s_sc[rows, :]
        s_sc[rows, :] = h
        hre, him = h[:, :SSM_SW], h[:, SSM_SW:]
        h = jnp.concatenate([lr * hre - li * him + s[:, :SSM_SW], lr * him + li * hre + s[:, SSM_SW:]], axis=1)
    hfin_ref[...] = h
    y = _dot_causal(u16, m_ref) + _dot(s_sc[...].astype(BF16), q_ref[...]) + u * d_ref[...]
    s_ref[...] = jax.nn.gelu(y).astype(BF16)


def _ssm_sample(u, sw, h0, nb, lc):
    tw = SSM_T * LANES
    sq = 2 * SSM_SW
    u2 = u.reshape(SSM_NJ, nb, lc, tw).transpose(0, 2, 1, 3).reshape(SSM_NJ, lc * nb, tw)
    kern = functools.partial(_ssm_sample_kernel, nb=nb, lc=lc)
    blk = lambda j: (j, 0, 0)
    s, hfin = pl.pallas_call(
        kern, grid=(SSM_NJ,),
        in_specs=[pl.BlockSpec((None, lc * nb, tw), blk), pl.BlockSpec((None, tw, tw), blk),
                  pl.BlockSpec((None, tw, sq), blk), pl.BlockSpec((None, sq, tw), blk),
                  pl.BlockSpec((None, 1, sq), blk), pl.BlockSpec((None, 1, tw), blk),
                  pl.BlockSpec((None, nb, sq), blk)],
        out_specs=[pl.BlockSpec((None, lc * nb, tw), blk), pl.BlockSpec((None, nb, sq), blk)],
        out_shape=[jax.ShapeDtypeStruct((SSM_NJ, lc * nb, tw), BF16),
                   jax.ShapeDtypeStruct((SSM_NJ, nb, sq), F32)],
        scratch_shapes=[pltpu.VMEM((lc * nb, sq), F32)],
        compiler_params=_cp(("arbitrary",)), name="ssm_sample",
    )(u2, sw["M"], sw["P"], sw["Q"], sw["lam"], sw["D"], h0)
    s = s.reshape(SSM_NJ, lc, nb, tw).transpose(0, 2, 1, 3).reshape(SSM_NJ, nb * lc, tw)
    return s, hfin


def _ssm_fold_kernel(wre_ref, wim_ref, cre_ref, cim_ref, plr_ref, pli_ref, plrt_ref, plit_ref, repw_ref, repc_ref,
                     m_ref, p_ref, q_ref):
    def split(a, terms):
        out = []
        for _ in range(terms):
            t = a.astype(BF16)
            out.append(t)
            a = a - t.astype(F32)
        return out

    def repeat_lanes(a, rep):
        return sum(_dot(t, rep) for t in split(a, 3))

    def dot_hi_lo(a, b_parts):
        a_hi, a_lo = split(a, 2)
        b_hi, b_lo = b_parts
        return _dot(a_hi, b_hi) + (_dot(a_hi, b_lo) + _dot(a_lo, b_hi))

    same_group_w = (lax.broadcasted_iota(jnp.int32, (LANES, SSM_SW), 0) // SSM_GROUP
                    == lax.broadcasted_iota(jnp.int32, (LANES, SSM_SW), 1) // SSM_STATE)
    same_group_c = (lax.broadcasted_iota(jnp.int32, (SSM_SW, LANES), 0) // SSM_STATE
                    == lax.broadcasted_iota(jnp.int32, (SSM_SW, LANES), 1) // SSM_GROUP)
    repw, repc = repw_ref[...], repc_ref[...]
    wb_re = jnp.where(same_group_w, repeat_lanes(wre_ref[...], repw), 0.0)
    wb_im = jnp.where(same_group_w, repeat_lanes(wim_ref[...], repw), 0.0)
    cb_re = jnp.where(same_group_c, repeat_lanes(cre_ref[...], repc), 0.0)
    cb_im = jnp.where(same_group_c, repeat_lanes(cim_ref[...], repc), 0.0)
    cb_re_parts, cb_im_parts = split(cb_re, 2), split(cb_im, 2)

    m_ref[...] = jnp.zeros_like(m_ref)
    for k in range(SSM_T):
        lr, li = plr_ref[k:k + 1, :], pli_ref[k:k + 1, :]
        v_re = lr * wb_re - li * wb_im
        v_im = lr * wb_im + li * wb_re
        s = SSM_T - 1 - k
        p_ref[s * LANES:(s + 1) * LANES, :SSM_SW] = v_re.astype(BF16)
        p_ref[s * LANES:(s + 1) * LANES, SSM_SW:] = v_im.astype(BF16)
        bd = (dot_hi_lo(v_re, cb_re_parts) - dot_hi_lo(v_im, cb_im_parts)).astype(BF16)
        for s in range(SSM_T - k):
            t = s + k
            m_ref[s * LANES:(s + 1) * LANES, t * LANES:(t + 1) * LANES] = bd
        lrc, lic = plrt_ref[:, k + 1:k + 2], plit_ref[:, k + 1:k + 2]
        q_ref[:SSM_SW, k * LANES:(k + 1) * LANES] = (cb_re * lrc - cb_im * lic).astype(BF16)
        q_ref[SSM_SW:, k * LANES:(k + 1) * LANES] = (-(cb_re * lic + cb_im * lrc)).astype(BF16)


def _ssm_weights(a_re, a_im, log_dt, b_re, b_im, c_re, c_im, d):
    t_ = SSM_T
    dt = jnp.exp(log_dt)[:, None]
    mag = jnp.exp(a_re * dt)
    lam_re, lam_im = mag * jnp.cos(a_im * dt), mag * jnp.sin(a_im * dt)
    den = a_re * a_re + a_im * a_im
    f_re = ((lam_re - 1.0) * a_re + lam_im * a_im) / den
    f_im = (lam_im * a_re - (lam_re - 1.0) * a_im) / den
    w_re = f_re[:, :, None] * b_re - f_im[:, :, None] * b_im
    w_im = f_re[:, :, None] * b_im + f_im[:, :, None] * b_re
    k = jnp.arange(t_ + 1, dtype=F32)[None, :, None]
    pmag = jnp.exp(a_re[:, None, :] * dt[:, :, None] * k)
    pang = a_im[:, None, :] * dt[:, :, None] * k
    pw_re, pw_im = pmag * jnp.cos(pang), pmag * jnp.sin(pang)

    def lanes_gn(x):
        x = x.reshape(SSM_NJ, SSM_GPB, t_ + 1, SSM_STATE)
        return jnp.swapaxes(x, 1, 2).reshape(SSM_NJ, t_ + 1, SSM_SW)

    pl_re, pl_im = lanes_gn(pw_re), lanes_gn(pw_im)
    wcat = [jnp.swapaxes(x, 1, 2).reshape(SSM_NJ, LANES, SSM_STATE) for x in (w_re, w_im)]
    ccat = [jnp.swapaxes(x, 1, 2).reshape(SSM_NJ, SSM_SW, SSM_GROUP) for x in (c_re, c_im)]
    rep_w = jnp.tile(jnp.eye(SSM_STATE, dtype=BF16), (1, SSM_GPB))
    rep_c = jnp.tile(jnp.eye(SSM_GROUP, dtype=BF16), (1, SSM_GPB))
    tw = t_ * LANES
    sq = 2 * SSM_SW
    blk = lambda j: (j, 0, 0)
    const = lambda j: (0, 0)
    per_j = lambda a: pl.BlockSpec((None,) + a.shape[1:], blk)
    ins = [wcat[0], wcat[1], ccat[0], ccat[1], pl_re, pl_im, jnp.swapaxes(pl_re, 1, 2), jnp.swapaxes(pl_im, 1, 2)]
    m, p, q = pl.pallas_call(
        _ssm_fold_kernel, grid=(SSM_NJ,),
        in_specs=[per_j(a) for a in ins] + [pl.BlockSpec(rep_w.shape, const), pl.BlockSpec(rep_c.shape, const)],
        out_specs=[pl.BlockSpec((None, tw, tw), blk), pl.BlockSpec((None, tw, sq), blk),
                   pl.BlockSpec((None, sq, tw), blk)],
        out_shape=[jax.ShapeDtypeStruct((SSM_NJ, tw, tw), BF16), jax.ShapeDtypeStruct((SSM_NJ, tw, sq), BF16),
                   jax.ShapeDtypeStruct((SSM_NJ, sq, tw), BF16)],
        compiler_params=_cp(("arbitrary",)), name="ssm_fold",
    )(*ins, rep_w, rep_c)
    lam_t = jnp.concatenate([pl_re[:, t_][:, None, :], pl_im[:, t_][:, None, :]], axis=2)
    dvec = jnp.tile(d.reshape(SSM_NJ, 1, LANES), (1, 1, t_))
    return {"M": m, "P": p, "Q": q, "lam": lam_t, "D": dvec}


def _mixer_kernel(o_ref, s_ref, ga_ref, gb_ref, x_ref, wo_ref, wv_ref, wg_ref, wout_ref, h_ref, s_sc):
    o_a = _dot(o_ref[...], wo_ref[...])
    nchunk = s_sc.shape[1] // SSM_T
    for j in range(SSM_NJ):
        sj = s_ref[j].astype(F32)
        for t in range(SSM_T):
            s_sc[j, pl.ds(t, nchunk, stride=SSM_T), :] = sj[:, t * LANES:(t + 1) * LANES]
    s = jnp.concatenate([s_sc[j] for j in range(SSM_NJ)], axis=1).astype(BF16)
    o_b = _dot(s, wv_ref[...]) * jax.nn.sigmoid(_dot(s, wg_ref[...]))
    merged = ga_ref[...] * o_a + gb_ref[...] * o_b
    h_ref[...] = x_ref[...] + _dot(merged.astype(BF16), wout_ref[...])


def _mixer(o, s, ga, gb, x, w, tm):
    n = x.shape[0]
    row = lambda i: (i, 0)
    const = lambda i: (0, 0)
    wspec = pl.BlockSpec((D_MODEL, D_MODEL), const, pipeline_mode=pl.Buffered(1))
    act = pl.BlockSpec((tm, D_MODEL), row)
    return pl.pallas_call(
        _mixer_kernel, grid=(n // tm,),
        in_specs=[act, pl.BlockSpec((SSM_NJ, tm // SSM_T, SSM_T * LANES), lambda i: (0, i, 0)), act, act, act,
                  wspec, wspec, wspec, wspec],
        out_specs=act, out_shape=jax.ShapeDtypeStruct((n, D_MODEL), F32),
        scratch_shapes=[pltpu.VMEM((SSM_NJ, tm, LANES), F32)],
        compiler_params=_cp(("arbitrary",)), name="mixer_out",
    )(o, s, ga, gb, x, w["wo"], w["wv"], w["wg"], w["wout"])


def _mlp_kernel(h_ref, g_ref, wup_ref, wdn_ref, y_ref):
    h = h_ref[...]
    hn = _rms(h, g_ref[...]).astype(BF16)
    a = jnp.maximum(_dot(hn, wup_ref[...]), 0.0)
    y_ref[...] = h + _dot((a * a).astype(BF16), wdn_ref[...])


def _mlp(h, w, tm):
    n = h.shape[0]
    row = lambda i: (i, 0)
    const = lambda i: (0, 0)
    act = pl.BlockSpec((tm, D_MODEL), row)
    return pl.pallas_call(
        _mlp_kernel, grid=(n // tm,),
        in_specs=[act, pl.BlockSpec((1, D_MODEL), const),
                  pl.BlockSpec((D_MODEL, D_FF), const, pipeline_mode=pl.Buffered(1)),
                  pl.BlockSpec((D_FF, D_MODEL), const, pipeline_mode=pl.Buffered(1))],
        out_specs=act, out_shape=jax.ShapeDtypeStruct((n, D_MODEL), F32),
        compiler_params=_cp(("arbitrary",)), name="mlp",
    )(h, w["gmlp"], w["wup"], w["wdn"])


def _rope_tables(pos):
    half = QK_ROPE // 2
    inv = ROPE_THETA ** (-jnp.arange(half, dtype=F32) / half)
    ang = pos.astype(F32)[:, None] * inv[None, :]
    cos, sin = jnp.cos(ang), jnp.sin(ang)
    z = jnp.zeros_like(cos)
    pad = jnp.zeros((pos.shape[0], LANES - QK_ROPE), F32)
    ct = jnp.concatenate([cos, cos, pad], axis=1)
    s1 = jnp.concatenate([-sin, z, pad], axis=1)
    s2 = jnp.concatenate([z, sin, pad], axis=1)
    return ct, s1, s2


def _pad_lanes(a, width):
    return jnp.pad(a, [(0, 0)] * (a.ndim - 1) + [(0, width - a.shape[-1])])


def _layer_weights(norm_mix, w_in, q_lora_norm, w_uq, q_nope_norm, q_rope_norm, kv_lora_norm, k_rope_norm,
                   w_ukv, k_nope_norm, w_o_attn, w_glu_v, w_glu_g, w_out, norm_mlp, w_mlp_up, w_mlp_down):
    o_kv = Q_LORA
    o_kr = o_kv + KV_LORA
    o_ssm = o_kr + QK_ROPE
    o_ga = o_ssm + D_MODEL
    o_gb = o_ga + D_MODEL
    bf = lambda a: a.astype(BF16)
    r2 = lambda a: a.reshape(1, -1).astype(F32)
    return {
        "nmix": r2(norm_mix),
        "wq": bf(w_in[:, :o_kv]), "wkv": bf(w_in[:, o_kv:o_kr]),
        "wkr": bf(_pad_lanes(w_in[:, o_kr:o_ssm], LANES)),
        "wu": bf(w_in[:, o_ssm:o_ga]), "wga": bf(w_in[:, o_ga:o_gb]), "wgb": bf(w_in[:, o_gb:]),
        "gq": r2(q_lora_norm),
        "wuqn": bf(w_uq[:, :, :QK_NOPE].reshape(Q_LORA, N_HEADS * QK_NOPE)),
        "wuqr": bf(_pad_lanes(w_uq[:, :, QK_NOPE:], LANES).reshape(Q_LORA, N_HEADS * LANES)),
        "gqn": r2(q_nope_norm), "gqr": _pad_lanes(r2(q_rope_norm), LANES),
        "gkv": r2(kv_lora_norm), "gkr": _pad_lanes(r2(k_rope_norm), LANES),
        "wuk": bf(w_ukv[:, :, :QK_NOPE].reshape(KV_LORA, N_HEADS * QK_NOPE)),
        "wukg": bf((w_ukv[:, :, :QK_NOPE] * k_nope_norm.astype(F32)).reshape(KV_LORA, N_HEADS * QK_NOPE)),
        "wuv": bf(w_ukv[:, :, QK_NOPE:].reshape(KV_LORA, N_HEADS * V_DIM)),
        "gkn": r2(k_nope_norm),
        "wo": bf(w_o_attn.reshape(N_HEADS * V_DIM, D_MODEL)),
        "wv": bf(w_glu_v), "wg": bf(w_glu_g), "wout": bf(w_out),
        "gmlp": r2(norm_mlp), "wup": bf(w_mlp_up), "wdn": bf(w_mlp_down),
    }


def kernel(x_prompt, x_sample, cache_latent, cache_krope, cache_meta_latent, cache_meta_krope, state_ssm_re, state_ssm_im, meta_tokens, norm_mix, w_in, q_lora_norm, w_uq, q_nope_norm, q_rope_norm, kv_lora_norm, k_rope_norm, w_ukv, k_nope_norm, w_o_attn, ssm_a_re, ssm_a_im, ssm_log_dt, ssm_b_re, ssm_b_im, ssm_c_re, ssm_c_im, ssm_d, w_glu_v, w_glu_g, w_out, norm_mlp, w_mlp_up, w_mlp_down):
    bsz_p, seq_p = x_prompt.shape[0], x_prompt.shape[1]
    bsz_s, seq_s = x_sample.shape[0], x_sample.shape[1]
    past = cache_latent.shape[2]
    depth = w_in.shape[0]
    assert depth == 1, "single-layer step"
    l = 0
    w = _layer_weights(norm_mix[l], w_in[l], q_lora_norm[l], w_uq[l], q_nope_norm[l], q_rope_norm[l],
                       kv_lora_norm[l], k_rope_norm[l], w_ukv[l], k_nope_norm[l], w_o_attn[l], w_glu_v[l],
                       w_glu_g[l], w_out[l], norm_mlp[l], w_mlp_up[l], w_mlp_down[l])
    sw = _ssm_weights(ssm_a_re[l].astype(F32), ssm_a_im[l].astype(F32), ssm_log_dt[l].astype(F32),
                      ssm_b_re[l].astype(F32), ssm_b_im[l].astype(F32), ssm_c_re[l].astype(F32),
                      ssm_c_im[l].astype(F32), ssm_d[l].astype(F32))
    sq = 2 * SSM_SW

    tabs_m = _rope_tables(jnp.arange(N_META, dtype=jnp.int32) - N_META)
    (_, mk, mv, mckv, mkr, mu, _, _) = _front(meta_tokens.astype(F32), tabs_m, 1, w, N_META)
    h_meta = _ssm_state(mu, sw, N_META // SSM_T)

    n_p = bsz_p * seq_p
    tm = 256
    xp = x_prompt.reshape(n_p, D_MODEL)
    tabs_p = _rope_tables(jnp.arange(seq_p, dtype=jnp.int32))
    tm_f = 512
    q, k, v, ckv_p, kr_p, u_p, ga, gb = _front(xp, tabs_p, seq_p // tm_f, w, tm_f)
    o_p = _attn_prompt(q, k, v, mk, mv, bsz_p, seq_p)
    h0_p = jnp.broadcast_to(h_meta, (SSM_NJ, bsz_p, sq))
    s_p, hfin_p = _ssm_prompt(u_p, sw, h0_p, bsz_p, seq_p // SSM_T, 256)
    h1_p = _mixer(o_p, s_p, ga, gb, xp, w, tm_f)
    y_p = _mlp(h1_p, w, tm_f)

    n_s = bsz_s * seq_s
    xs = x_sample.reshape(n_s, D_MODEL)
    pos_s = past + jnp.arange(seq_s, dtype=jnp.int32)
    tabs_s = tuple(jnp.tile(t, (bsz_s, 1)) for t in _rope_tables(pos_s))
    tm_s = 256
    q_s, k_s, _, ckv_s, kr_s, u_s, ga_s, gb_s = _front(xs, tabs_s, n_s // tm_s, w, tm_s)
    o_s = _attn_sample(q_s, k_s, ckv_s, cache_latent[l].astype(F32), cache_krope[l].astype(F32),
                       cache_meta_latent[l].astype(F32), cache_meta_krope[l].astype(F32), w, bsz_s, seq_s, past)

    def to_blocks(st):
        return st.astype(F32).reshape(bsz_s, SSM_NJ, SSM_SW).transpose(1, 0, 2)

    h0_s = jnp.concatenate([to_blocks(state_ssm_re[l]), to_blocks(state_ssm_im[l])], axis=2)
    s_s, hfin_s = _ssm_sample(u_s, sw, h0_s, bsz_s, seq_s // SSM_T)
    h1_s = _mixer(o_s, s_s, ga_s, gb_s, xs, w, tm_s)
    y_s = _mlp(h1_s, w, tm_s)

    def from_blocks(hf, nb):
        re = hf[:, :, :SSM_SW].transpose(1, 0, 2).reshape(1, nb, N_GROUPS, SSM_STATE)
        im = hf[:, :, SSM_SW:].transpose(1, 0, 2).reshape(1, nb, N_GROUPS, SSM_STATE)
        return re, im

    sre_p, sim_p = from_blocks(hfin_p, bsz_p)
    sre_s, sim_s = from_blocks(hfin_s, bsz_s)
    return (y_p.reshape(bsz_p, seq_p, D_MODEL), y_s.reshape(bsz_s, seq_s, D_MODEL),
            ckv_p.reshape(1, bsz_p, seq_p, KV_LORA), kr_p.reshape(1, bsz_p, seq_p, QK_ROPE),
            jnp.broadcast_to(mckv[None, None], (1, bsz_p, N_META, KV_LORA)),
            jnp.broadcast_to(mkr[None, None], (1, bsz_p, N_META, QK_ROPE)),
            sre_p, sim_p,
            ckv_s.reshape(1, bsz_s, seq_s, KV_LORA), kr_s.reshape(1, bsz_s, seq_s, QK_ROPE),
            sre_s, sim_s)
```

```python
import functools
import math

import jax
import jax.numpy as jnp
from jax import lax
from jax.experimental import pallas as pl
from jax.experimental.pallas import tpu as pltpu

F32 = jnp.float32
BF16 = jnp.bfloat16

D_MODEL = 1024
CHUNK = 64
N_META = 16
N_HEADS = 8
QK_NOPE = 128
QK_ROPE = 64
V_DIM = 128
QK_DIM = QK_NOPE + QK_ROPE
Q_LORA = 384
KV_LORA = 256
SSM_GROUP = 16
N_GROUPS = D_MODEL // SSM_GROUP
SSM_STATE = 64
D_FF = 4 * D_MODEL
ROPE_THETA = 10000.0
EPS = 1e-6
ATTN_SCALE = QK_DIM ** -0.5
Q_SCALE = ATTN_SCALE * math.log2(math.e)
NEG_INF = -1e30

LANES = 128
HEAD_PAD = 2 * LANES
SSM_T = 8
SSM_NJ = D_MODEL // LANES
SSM_GPB = LANES // SSM_GROUP
SSM_SW = SSM_GPB * SSM_STATE
VMEM_LIMIT = 56 * 1024 * 1024


def _cp(sem):
    return pltpu.CompilerParams(dimension_semantics=sem, vmem_limit_bytes=VMEM_LIMIT)


def _dot(a, b):
    return jnp.dot(a, b, preferred_element_type=F32)


def _dot_nt(a, b):
    return lax.dot_general(a, b, (((1,), (1,)), ((), ())), preferred_element_type=F32)


def _rms(x, g, n=None):
    n = x.shape[-1] if n is None else n
    ms = jnp.sum(x * x, axis=-1, keepdims=True) * (1.0 / n)
    return x * lax.rsqrt(ms + EPS) * g


def _rope128(b, ct, s1, s2):
    return b * ct + pltpu.roll(b, LANES - QK_ROPE // 2, 1) * s1 + pltpu.roll(b, QK_ROPE // 2, 1) * s2


def _front_kernel(x_ref, ct_ref, s1_ref, s2_ref, nmix_ref, wq_ref, wkv_ref, wkr_ref, wu_ref, wga_ref,
                  wgb_ref, gq_ref, wuqn_ref, wuqr_ref, gqn_ref, gqr_ref, gkv_ref, gkr_ref, wuk_ref,
                  wuv_ref, gkn_ref,
                  q_ref, k_ref, v_ref, ckv_ref, kr_ref, u_ref, ga_ref, gb_ref, u_sc):
    x = x_ref[...]
    xn = _rms(x, nmix_ref[...]).astype(BF16)
    ct, s1, s2 = ct_ref[...], s1_ref[...], s2_ref[...]


    c_q = _rms(_dot(xn, wq_ref[...]), gq_ref[...]).astype(BF16)
    qn = _dot(c_q, wuqn_ref[...])
    qr = _dot(c_q, wuqr_ref[...])
    for h in range(N_HEADS):
        sl = slice(h * LANES, (h + 1) * LANES)
        q_ref[:, h * HEAD_PAD:h * HEAD_PAD + LANES] = (_rms(qn[:, sl], gqn_ref[...]) * Q_SCALE).astype(BF16)
        r = _rope128(_rms(qr[:, sl], gqr_ref[...], QK_ROPE), ct, s1, s2)
        q_ref[:, h * HEAD_PAD + LANES:(h + 1) * HEAD_PAD] = (r * Q_SCALE).astype(BF16)

    c_kv = _rms(_dot(xn, wkv_ref[...]), gkv_ref[...])
    ckv_ref[...] = c_kv
    kr = _rms(_dot(xn, wkr_ref[...]), gkr_ref[...], QK_ROPE)
    kr = _rope128(kr, ct, s1, s2)
    kr_ref[...] = kr[:, :QK_ROPE]
    kr16 = kr.astype(BF16)
    c16 = c_kv.astype(BF16)
    kn = _dot(c16, wuk_ref[...])
    for h in range(N_HEADS):
        sl = slice(h * LANES, (h + 1) * LANES)
        k_ref[:, h * HEAD_PAD:h * HEAD_PAD + LANES] = _rms(kn[:, sl], gkn_ref[...]).astype(BF16)
        k_ref[:, h * HEAD_PAD + LANES:(h + 1) * HEAD_PAD] = kr16
    v_ref[...] = _dot(c16, wuv_ref[...]).astype(BF16)

    u = _dot(xn, wu_ref[...])
    nchunk = u_sc.shape[1] // SSM_T
    for j in range(SSM_NJ):
        u_sc[j] = u[:, j * LANES:(j + 1) * LANES]
        for t in range(SSM_T):
            u_ref[j, :, t * LANES:(t + 1) * LANES] = u_sc[j, pl.ds(t, nchunk, stride=SSM_T), :]
    ga_ref[...] = jax.nn.sigmoid(_dot(xn, wga_ref[...]))
    gb_ref[...] = jax.nn.sigmoid(_dot(xn, wgb_ref[...]))


def _front(x, tabs, tab_blocks, w, tm):
    n = x.shape[0]
    nt = n // tm
    ct, s1, s2 = tabs

    def row(i):
        return (i, 0)

    def tab(i):
        return (i % tab_blocks, 0)

    def const(i):
        return (0, 0)

    def full(a):
        return pl.BlockSpec(a.shape, const, pipeline_mode=pl.Buffered(1))

    weights = [w["nmix"], w["wq"], w["wkv"], w["wkr"], w["wu"], w["wga"], w["wgb"], w["gq"], w["wuqn"],
               w["wuqr"], w["gqn"], w["gqr"], w["gkv"], w["gkr"], w["wuk"], w["wuv"], w["gkn"]]
    wide = N_HEADS * LANES
    qk_wide = N_HEADS * HEAD_PAD
    out_shape = [
        jax.ShapeDtypeStruct((n, qk_wide), BF16),
        jax.ShapeDtypeStruct((n, qk_wide), BF16),
        jax.ShapeDtypeStruct((n, wide), BF16),
        jax.ShapeDtypeStruct((n, KV_LORA), F32),
        jax.ShapeDtypeStruct((n, QK_ROPE), F32),
        jax.ShapeDtypeStruct((SSM_NJ, n // SSM_T, SSM_T * LANES), F32),
        jax.ShapeDtypeStruct((n, D_MODEL), F32),
        jax.ShapeDtypeStruct((n, D_MODEL), F32),
    ]
    out_specs = [
        pl.BlockSpec((tm, qk_wide), row), pl.BlockSpec((tm, qk_wide), row),
        pl.BlockSpec((tm, wide), row), pl.BlockSpec((tm, KV_LORA), row),
        pl.BlockSpec((tm, QK_ROPE), row),
        pl.BlockSpec((SSM_NJ, tm // SSM_T, SSM_T * LANES), lambda i: (0, i, 0)),
        pl.BlockSpec((tm, D_MODEL), row), pl.BlockSpec((tm, D_MODEL), row),
    ]
    in_specs = ([pl.BlockSpec((tm, D_MODEL), row)] + [pl.BlockSpec((tm, LANES), tab)] * 3
                + [full(a) for a in weights])
    return pl.pallas_call(
        _front_kernel, grid=(nt,), in_specs=in_specs, out_specs=out_specs, out_shape=out_shape,
        scratch_shapes=[pltpu.VMEM((SSM_NJ, tm, LANES), F32)],
        compiler_params=_cp(("arbitrary",)), name="front",
    )(x, ct, s1, s2, *weights)


ATT_T = 512


def _with_ones(v):
    return jnp.concatenate([v, jnp.ones_like(v)], axis=1)


def _softmax_step(s, v, m_ref, acc_ref):
    m_prev = m_ref[...]
    m_new = jnp.maximum(m_prev, jnp.max(s, axis=-1, keepdims=True))
    alpha = jnp.exp2(m_prev - m_new)
    p = jnp.exp2(s - jnp.tile(m_new, (1, s.shape[1] // LANES)))
    acc_ref[...] = jnp.tile(alpha, (1, 2)) * acc_ref[...] + _dot(p.astype(BF16), _with_ones(v))
    m_ref[...] = m_new


ATT_HG = 4


def _attn_prompt_kernel(q_ref, k_ref, v_ref, mk_ref, mv_ref, o_ref, a_sc, b_sc, m_ref, acc_ref):
    step = pl.program_id(2)
    npairs = step
    heads = range(ATT_HG)

    def key_rows(kt):
        return pl.ds(pl.multiple_of(kt * ATT_T, ATT_T), ATT_T)

    def q_rows(slot):
        return slice(slot * ATT_T, (slot + 1) * ATT_T)

    def qk_lanes(g):
        return slice(g * HEAD_PAD, (g + 1) * HEAD_PAD)

    def v_lanes(g):
        return slice(g * V_DIM, (g + 1) * V_DIM)

    def scores(slot, g, kt):
        return _dot_nt(q_ref[q_rows(slot), qk_lanes(g)], k_ref[key_rows(kt), qk_lanes(g)])

    def consume(slot, g, s, kt, diagonal):
        if diagonal:
            qc = lax.broadcasted_iota(jnp.int32, (ATT_T, ATT_T), 0) // CHUNK
            kc = lax.broadcasted_iota(jnp.int32, (ATT_T, ATT_T), 1) // CHUNK
            s = jnp.where(kc <= qc, s, NEG_INF)
        _softmax_step(s, v_ref[key_rows(kt), v_lanes(g)], m_ref.at[slot, g], acc_ref.at[slot, g])

    def head(slot):
        for g in heads:
            a_sc[slot, g] = scores(slot, g, 0)
        for g in heads:
            s = _dot_nt(q_ref[q_rows(slot), qk_lanes(g)], mk_ref[:, qk_lanes(g)])
            m0 = jnp.max(s, axis=-1, keepdims=True)
            p = jnp.exp2(s - m0)
            m_ref[slot, g] = jnp.broadcast_to(m0, (ATT_T, LANES))
            acc_ref[slot, g] = _dot(p.astype(BF16), _with_ones(mv_ref[:, v_lanes(g)]))

    def full_pairs(slot):
        def pair(pi, carry):
            kt = 2 * pi
            for g in heads:
                b_sc[g] = scores(slot, g, kt + 1)
                consume(slot, g, a_sc[slot, g], kt, False)
            for g in heads:
                a_sc[slot, g] = scores(slot, g, kt + 2)
                consume(slot, g, b_sc[g], kt + 1, False)
            return carry

        lax.fori_loop(0, npairs, pair, 0)

    def tail(slot):
        i = 2 * step + slot
        if slot == 1:
            for g in heads:
                b_sc[g] = scores(slot, g, i)
                consume(slot, g, a_sc[slot, g], i - 1, False)
            for g in heads:
                consume(slot, g, b_sc[g], i, True)
        else:
            for g in heads:
                consume(slot, g, a_sc[slot, g], i, True)
        for g in heads:
            o_ref[q_rows(slot), v_lanes(g)] = (acc_ref[slot, g, :, :V_DIM]
                                               / acc_ref[slot, g, :, V_DIM:]).astype(BF16)

    head(0)
    full_pairs(0)
    tail(0)
    head(1)
    full_pairs(1)
    tail(1)


def _attn_prompt(q, k, v, mk, mv, bsz, seq):
    nstep = seq // (2 * ATT_T)
    n = bsz * seq
    qmap = lambda b, h, i: (b * nstep + i, h)
    return pl.pallas_call(
        _attn_prompt_kernel, grid=(bsz, N_HEADS // ATT_HG, nstep),
        in_specs=[pl.BlockSpec((2 * ATT_T, ATT_HG * HEAD_PAD), qmap),
                  pl.BlockSpec((seq, ATT_HG * HEAD_PAD), lambda b, h, i: (b, h)),
                  pl.BlockSpec((seq, ATT_HG * V_DIM), lambda b, h, i: (b, h)),
                  pl.BlockSpec((N_META, ATT_HG * HEAD_PAD), lambda b, h, i: (0, h)),
                  pl.BlockSpec((N_META, ATT_HG * V_DIM), lambda b, h, i: (0, h))],
        out_specs=pl.BlockSpec((2 * ATT_T, ATT_HG * V_DIM), qmap),
        out_shape=jax.ShapeDtypeStruct((n, N_HEADS * V_DIM), BF16),
        scratch_shapes=[pltpu.VMEM((2, ATT_HG, ATT_T, ATT_T), F32), pltpu.VMEM((ATT_HG, ATT_T, ATT_T), F32),
                        pltpu.VMEM((2, ATT_HG, ATT_T, LANES), F32),
                        pltpu.VMEM((2, ATT_HG, ATT_T, 2 * V_DIM), F32)],
        compiler_params=_cp(("arbitrary", "arbitrary", "arbitrary")), name="attn_prompt",
    )(q, k, v, mk, mv)


SAMPLE_KEY_CHUNK = 512


def _attn_sample_kernel(q_ref, cm_ref, cn_ref, cc_ref, krm_ref, krn_ref, krc_ref, wuk_ref, wukg_ref, wuv_ref,
                        hsum_ref, o_ref, c_sc, kr_sc, st_sc, pt_sc, *, seq, past):
    n_small = N_META + seq
    n_keys = n_small + past

    c_sc[:, KV_LORA:] = jnp.ones((n_keys, LANES), BF16)
    c_sc[0:N_META, :KV_LORA] = cm_ref[...].astype(BF16)
    c_sc[N_META:n_small, :KV_LORA] = cn_ref[...].astype(BF16)
    c_sc[n_small:, :KV_LORA] = cc_ref[...].astype(BF16)
    kr_sc[:, QK_ROPE:] = jnp.zeros((n_keys, LANES - QK_ROPE), BF16)
    kr_sc[0:N_META, :QK_ROPE] = krm_ref[...].astype(BF16)
    kr_sc[N_META:n_small, :] = krn_ref[...]
    kr_sc[n_small:, :QK_ROPE] = krc_ref[...].astype(BF16)

    qa, qr = [], []
    for h in range(N_HEADS):
        qn_h = q_ref[:, h * HEAD_PAD:h * HEAD_PAD + LANES]
        qa.append(_dot_nt(qn_h, wukg_ref[:, h * LANES:(h + 1) * LANES]))
        qr.append(q_ref[:, h * HEAD_PAD + LANES:(h + 1) * HEAD_PAD])
    qa = jnp.concatenate(qa, axis=0).astype(BF16)
    qr = jnp.concatenate(qr, axis=0)

    bounds = [(0, n_small)] + [(n_small + i, n_small + i + SAMPLE_KEY_CHUNK)
                               for i in range(0, past, SAMPLE_KEY_CHUNK)]
    m = None
    for lo, hi in bounds:
        c16 = c_sc[lo:hi, :KV_LORA]
        kn = _dot(c16, wuk_ref[...])
        ssq = _dot((kn * kn).astype(BF16), hsum_ref[...])
        r = lax.rsqrt(ssq * (1.0 / QK_NOPE) + EPS)
        st = _dot_nt(c16, qa) * r + _dot_nt(kr_sc[lo:hi, :], qr)
        st_sc[lo:hi, :] = st
        cm = jnp.max(st, axis=0, keepdims=True)
        m = cm if m is None else jnp.maximum(m, cm)

    for lo, hi in bounds:
        pt_sc[lo:hi, :] = jnp.exp2(st_sc[lo:hi, :] - m).astype(BF16)
    pc = lax.dot_general(pt_sc[...], c_sc[...], (((0,), (0,)), ((), ())), preferred_element_type=F32)
    pcn = (pc[:, :KV_LORA] / jnp.tile(pc[:, KV_LORA:], (1, KV_LORA // LANES))).astype(BF16)
    for h in range(N_HEADS):
        o_ref[:, h * V_DIM:(h + 1) * V_DIM] = _dot(pcn[h * seq:(h + 1) * seq, :],
                                                   wuv_ref[:, h * V_DIM:(h + 1) * V_DIM]).astype(BF16)


def _attn_sample(q, k_new, c_new, cache_c, cache_kr, meta_c, meta_kr, w, bsz, seq, past):
    n_keys = N_META + seq + past
    hq = N_HEADS * seq
    hsum = jnp.repeat(jnp.repeat(jnp.eye(N_HEADS, dtype=BF16), QK_NOPE, axis=0), seq, axis=1)
    row = lambda b: (b, 0)
    b3 = lambda b: (b, 0, 0)
    const = lambda b: (0, 0)
    kern = functools.partial(_attn_sample_kernel, seq=seq, past=past)
    return pl.pallas_call(
        kern, grid=(bsz,),
        in_specs=[pl.BlockSpec((seq, N_HEADS * HEAD_PAD), row),
                  pl.BlockSpec((None, N_META, KV_LORA), b3), pl.BlockSpec((seq, KV_LORA), row),
                  pl.BlockSpec((None, past, KV_LORA), b3),
                  pl.BlockSpec((None, N_META, QK_ROPE), b3), pl.BlockSpec((seq, LANES), lambda b: (b, 1)),
                  pl.BlockSpec((None, past, QK_ROPE), b3),
                  pl.BlockSpec(w["wuk"].shape, const), pl.BlockSpec(w["wukg"].shape, const),
                  pl.BlockSpec(w["wuv"].shape, const), pl.BlockSpec(hsum.shape, const)],
        out_specs=pl.BlockSpec((seq, N_HEADS * V_DIM), row),
        out_shape=jax.ShapeDtypeStruct((bsz * seq, N_HEADS * V_DIM), BF16),
        scratch_shapes=[pltpu.VMEM((n_keys, KV_LORA + LANES), BF16), pltpu.VMEM((n_keys, LANES), BF16),
                        pltpu.VMEM((n_keys, hq), F32), pltpu.VMEM((n_keys, hq), BF16)],
        compiler_params=_cp(("arbitrary",)), name="attn_sample",
    )(q, meta_c, c_new, cache_c, meta_kr, k_new, cache_kr, w["wuk"], w["wukg"], w["wuv"], hsum)


MXU_DIM = 256


def _dot_causal(u16, m_ref):
    nblk = m_ref.shape[0] // MXU_DIM
    return jnp.concatenate(
        [_dot(u16[:, :(cb + 1) * MXU_DIM], m_ref[:(cb + 1) * MXU_DIM, cb * MXU_DIM:(cb + 1) * MXU_DIM])
         for cb in range(nblk)], axis=1)


SSM_PACK = 4
SSM_SC = SSM_SW // LANES
SUBLANES = 2 * SSM_PACK


def _ssm_prompt_kernel(u_ref, m_ref, p_ref, q_ref, lam_ref, d_ref, h0_ref, s_ref, hfin_ref, s_sc, h_sc,
                       *, nb, rt):
    t = pl.program_id(1)

    @pl.when(t == 0)
    def _():
        h_sc[...] = h0_ref[...]

    if nb < SSM_PACK:
        s_sc[...] = jnp.zeros_like(s_sc)
    for b in range(nb):
        sb = _dot(u_ref[b].astype(BF16), p_ref[...])
        for c in range(SSM_SC):
            s_sc[c, pl.ds(b, rt, stride=SUBLANES), :] = sb[:, c * LANES:(c + 1) * LANES]
            s_sc[c, pl.ds(SSM_PACK + b, rt, stride=SUBLANES), :] = sb[:, SSM_SW + c * LANES:SSM_SW + (c + 1) * LANES]

    lam = lam_ref[...]
    im_rows = lax.broadcasted_iota(jnp.int32, (SUBLANES, LANES), 0) >= SSM_PACK
    coef = []
    for c in range(SSM_SC):
        lr = jnp.broadcast_to(lam[:, c * LANES:(c + 1) * LANES], (SUBLANES, LANES))
        li = jnp.broadcast_to(lam[:, SSM_SW + c * LANES:SSM_SW + (c + 1) * LANES], (SUBLANES, LANES))
        coef.append((lr, jnp.where(im_rows, li, -li)))

    def step(r, hs):
        rows = pl.ds(pl.multiple_of(r * SUBLANES, SUBLANES), SUBLANES)
        out = []
        for c in range(SSM_SC):
            s = s_sc[c, rows, :]
            s_sc[c, rows, :] = hs[c]
            a, bb = coef[c]
            out.append(a * hs[c] + bb * pltpu.roll(hs[c], SSM_PACK, 0) + s)
        return tuple(out)

    hs = lax.fori_loop(0, rt, step, tuple(h_sc[c] for c in range(SSM_SC)))
    for c in range(SSM_SC):
        h_sc[c] = hs[c]

    for b in range(nb):
        ub = u_ref[b]
        hb = jnp.concatenate([s_sc[c, pl.ds(b, rt, stride=SUBLANES), :] for c in range(SSM_SC)]
                             + [s_sc[c, pl.ds(SSM_PACK + b, rt, stride=SUBLANES), :] for c in range(SSM_SC)],
                             axis=1)
        y = _dot_causal(ub.astype(BF16), m_ref) + _dot(hb.astype(BF16), q_ref[...]) + ub * d_ref[...]
        s_ref[b] = jax.nn.gelu(y).astype(BF16)

    @pl.when(t == pl.num_programs(1) - 1)
    def _():
        hfin_ref[...] = h_sc[...]


def _pack_state(h):
    nj, nb, _ = h.shape
    h = jnp.pad(h, ((0, 0), (0, SSM_PACK - nb), (0, 0)))
    re = h[:, :, :SSM_SW].reshape(nj, SSM_PACK, SSM_SC, LANES)
    im = h[:, :, SSM_SW:].reshape(nj, SSM_PACK, SSM_SC, LANES)
    return jnp.concatenate([re, im], axis=1).transpose(0, 2, 1, 3)


def _unpack_state(hp, nb):
    nj = hp.shape[0]
    x = hp.transpose(0, 2, 1, 3).reshape(nj, SUBLANES, SSM_SW)
    return jnp.concatenate([x[:, :nb], x[:, SSM_PACK:SSM_PACK + nb]], axis=2)


def _ssm_prompt(u, sw, h0, nb, lc, rt):
    assert nb <= SSM_PACK
    tw = SSM_T * LANES
    u4 = u.reshape(SSM_NJ, nb, lc, tw)
    sq = 2 * SSM_SW
    kern = functools.partial(_ssm_prompt_kernel, nb=nb, rt=rt)
    st_spec = pl.BlockSpec((None, SSM_SC, SUBLANES, LANES), lambda j, t: (j, 0, 0, 0))
    s, hfin = pl.pallas_call(
        kern, grid=(SSM_NJ, lc // rt),
        in_specs=[pl.BlockSpec((None, nb, rt, tw), lambda j, t: (j, 0, t, 0)),
                  pl.BlockSpec((None, tw, tw), lambda j, t: (j, 0, 0)),
                  pl.BlockSpec((None, tw, sq), lambda j, t: (j, 0, 0)),
                  pl.BlockSpec((None, sq, tw), lambda j, t: (j, 0, 0)),
                  pl.BlockSpec((None, 1, sq), lambda j, t: (j, 0, 0)),
                  pl.BlockSpec((None, 1, tw), lambda j, t: (j, 0, 0)),
                  st_spec],
        out_specs=[pl.BlockSpec((None, nb, rt, tw), lambda j, t: (j, 0, t, 0)), st_spec],
        out_shape=[jax.ShapeDtypeStruct((SSM_NJ, nb, lc, tw), BF16),
                   jax.ShapeDtypeStruct((SSM_NJ, SSM_SC, SUBLANES, LANES), F32)],
        scratch_shapes=[pltpu.VMEM((SSM_SC, rt * SUBLANES, LANES), F32),
                        pltpu.VMEM((SSM_SC, SUBLANES, LANES), F32)],
        compiler_params=_cp(("arbitrary", "arbitrary")), name="ssm_prompt",
    )(u4, sw["M"], sw["P"], sw["Q"], sw["lam"], sw["D"], _pack_state(h0))
    return s.reshape(SSM_NJ, nb * lc, tw), _unpack_state(hfin, nb)


def _ssm_state_kernel(u_ref, p_ref, lam_ref, h_ref):
    s = _dot(u_ref[...].astype(BF16), p_ref[...])
    lam = lam_ref[...]
    lr, li = lam[:, :SSM_SW], lam[:, SSM_SW:]
    h = s[0:1, :]
    for r in range(1, s.shape[0]):
        hre, him = h[:, :SSM_SW], h[:, SSM_SW:]
        h = jnp.concatenate([lr * hre - li * him, lr * him + li * hre], axis=1) + s[r:r + 1, :]
    h_ref[...] = h


def _ssm_state(u, sw, lc):
    tw = SSM_T * LANES
    sq = 2 * SSM_SW
    blk = lambda j: (j, 0, 0)
    return pl.pallas_call(
        _ssm_state_kernel, grid=(SSM_NJ,),
        in_specs=[pl.BlockSpec((None, lc, tw), blk), pl.BlockSpec((None, tw, sq), blk),
                  pl.BlockSpec((None, 1, sq), blk)],
        out_specs=pl.BlockSpec((None, 1, sq), blk),
        out_shape=jax.ShapeDtypeStruct((SSM_NJ, 1, sq), F32),
        compiler_params=_cp(("arbitrary",)), name="ssm_state",
    )(u, sw["P"], sw["lam"])


def _ssm_sample_kernel(u_ref, m_ref, p_ref, q_ref, lam_ref, d_ref, h0_ref, s_ref, hfin_ref, s_sc,
                       *, nb, lc):
    u = u_ref[...]
    u16 = u.astype(BF16)
    s_sc[...] = _dot(u16, p_ref[...])
    lam = lam_ref[...]
    lr, li = lam[:, :SSM_SW], lam[:, SSM_SW:]
    h = h0_ref[...]
    for r in range(lc):
        rows = slice(r * nb, (r + 1) * nb)
        s = s_sc[rows, :]
        s_sc[rows, :] = h
        hre, him = h[:, :SSM_SW], h[:, SSM_SW:]
        h = jnp.concatenate([lr * hre - li * him + s[:, :SSM_SW], lr * him + li * hre + s[:, SSM_SW:]], axis=1)
    hfin_ref[...] = h
    y = _dot_causal(u16, m_ref) + _dot(s_sc[...].astype(BF16), q_ref[...]) + u * d_ref[...]
    s_ref[...] = jax.nn.gelu(y).astype(BF16)


def _ssm_sample(u, sw, h0, nb, lc):
    tw = SSM_T * LANES
    sq = 2 * SSM_SW
    u2 = u.reshape(SSM_NJ, nb, lc, tw).transpose(0, 2, 1, 3).reshape(SSM_NJ, lc * nb, tw)
    kern = functools.partial(_ssm_sample_kernel, nb=nb, lc=lc)
    blk = lambda j: (j, 0, 0)
    s, hfin = pl.pallas_call(
        kern, grid=(SSM_NJ,),
        in_specs=[pl.BlockSpec((None, lc * nb, tw), blk), pl.BlockSpec((None, tw, tw), blk),
                  pl.BlockSpec((None, tw, sq), blk), pl.BlockSpec((None, sq, tw), blk),
                  pl.BlockSpec((None, 1, sq), blk), pl.BlockSpec((None, 1, tw), blk),
                  pl.BlockSpec((None, nb, sq), blk)],
        out_specs=[pl.BlockSpec((None, lc * nb, tw), blk), pl.BlockSpec((None, nb, sq), blk)],
        out_shape=[jax.ShapeDtypeStruct((SSM_NJ, lc * nb, tw), BF16),
                   jax.ShapeDtypeStruct((SSM_NJ, nb, sq), F32)],
        scratch_shapes=[pltpu.VMEM((lc * nb, sq), F32)],
        compiler_params=_cp(("arbitrary",)), name="ssm_sample",
    )(u2, sw["M"], sw["P"], sw["Q"], sw["lam"], sw["D"], h0)
    s = s.reshape(SSM_NJ, lc, nb, tw).transpose(0, 2, 1, 3).reshape(SSM_NJ, nb * lc, tw)
    return s, hfin


def _ssm_fold_kernel(wre_ref, wim_ref, cre_ref, cim_ref, plr_ref, pli_ref, plrt_ref, plit_ref, repw_ref, repc_ref,
                     m_ref, p_ref, q_ref):
    def split(a, terms):
        out = []
        for _ in range(terms):
            t = a.astype(BF16)
            out.append(t)
            a = a - t.astype(F32)
        return out

    def repeat_lanes(a, rep):
        return sum(_dot(t, rep) for t in split(a, 3))

    def dot_hi_lo(a, b_parts):
        a_hi, a_lo = split(a, 2)
        b_hi, b_lo = b_parts
        return _dot(a_hi, b_hi) + (_dot(a_hi, b_lo) + _dot(a_lo, b_hi))

    same_group_w = (lax.broadcasted_iota(jnp.int32, (LANES, SSM_SW), 0) // SSM_GROUP
                    == lax.broadcasted_iota(jnp.int32, (LANES, SSM_SW), 1) // SSM_STATE)
    same_group_c = (lax.broadcasted_iota(jnp.int32, (SSM_SW, LANES), 0) // SSM_STATE
                    == lax.broadcasted_iota(jnp.int32, (SSM_SW, LANES), 1) // SSM_GROUP)
    repw, repc = repw_ref[...], repc_ref[...]
    wb_re = jnp.where(same_group_w, repeat_lanes(wre_ref[...], repw), 0.0)
    wb_im = jnp.where(same_group_w, repeat_lanes(wim_ref[...], repw), 0.0)
    cb_re = jnp.where(same_group_c, repeat_lanes(cre_ref[...], repc), 0.0)
    cb_im = jnp.where(same_group_c, repeat_lanes(cim_ref[...], repc), 0.0)
    cb_re_parts, cb_im_parts = split(cb_re, 2), split(cb_im, 2)

    m_ref[...] = jnp.zeros_like(m_ref)
    for k in range(SSM_T):
        lr, li = plr_ref[k:k + 1, :], pli_ref[k:k + 1, :]
        v_re = lr * wb_re - li * wb_im
        v_im = lr * wb_im + li * wb_re
        s = SSM_T - 1 - k
        p_ref[s * LANES:(s + 1) * LANES, :SSM_SW] = v_re.astype(BF16)
        p_ref[s * LANES:(s + 1) * LANES, SSM_SW:] = v_im.astype(BF16)
        bd = (dot_hi_lo(v_re, cb_re_parts) - dot_hi_lo(v_im, cb_im_parts)).astype(BF16)
        for s in range(SSM_T - k):
            t = s + k
            m_ref[s * LANES:(s + 1) * LANES, t * LANES:(t + 1) * LANES] = bd
        lrc, lic = plrt_ref[:, k + 1:k + 2], plit_ref[:, k + 1:k + 2]
        q_ref[:SSM_SW, k * LANES:(k + 1) * LANES] = (cb_re * lrc - cb_im * lic).astype(BF16)
        q_ref[SSM_SW:, k * LANES:(k + 1) * LANES] = (-(cb_re * lic + cb_im * lrc)).astype(BF16)


def _ssm_weights(a_re, a_im, log_dt, b_re, b_im, c_re, c_im, d):
    t_ = SSM_T
    dt = jnp.exp(log_dt)[:, None]
    mag = jnp.exp(a_re * dt)
    lam_re, lam_im = mag * jnp.cos(a_im * dt), mag * jnp.sin(a_im * dt)
    den = a_re * a_re + a_im * a_im
    f_re = ((lam_re - 1.0) * a_re + lam_im * a_im) / den
    f_im = (lam_im * a_re - (lam_re - 1.0) * a_im) / den
    w_re = f_re[:, :, None] * b_re - f_im[:, :, None] * b_im
    w_im = f_re[:, :, None] * b_im + f_im[:, :, None] * b_re
    k = jnp.arange(t_ + 1, dtype=F32)[None, :, None]
    pmag = jnp.exp(a_re[:, None, :] * dt[:, :, None] * k)
    pang = a_im[:, None, :] * dt[:, :, None] * k
    pw_re, pw_im = pmag * jnp.cos(pang), pmag * jnp.sin(pang)

    def lanes_gn(x):
        x = x.reshape(SSM_NJ, SSM_GPB, t_ + 1, SSM_STATE)
        return jnp.swapaxes(x, 1, 2).reshape(SSM_NJ, t_ + 1, SSM_SW)

    pl_re, pl_im = lanes_gn(pw_re), lanes_gn(pw_im)
    wcat = [jnp.swapaxes(x, 1, 2).reshape(SSM_NJ, LANES, SSM_STATE) for x in (w_re, w_im)]
    ccat = [jnp.swapaxes(x, 1, 2).reshape(SSM_NJ, SSM_SW, SSM_GROUP) for x in (c_re, c_im)]
    rep_w = jnp.tile(jnp.eye(SSM_STATE, dtype=BF16), (1, SSM_GPB))
    rep_c = jnp.tile(jnp.eye(SSM_GROUP, dtype=BF16), (1, SSM_GPB))
    tw = t_ * LANES
    sq = 2 * SSM_SW
    blk = lambda j: (j, 0, 0)
    const = lambda j: (0, 0)
    per_j = lambda a: pl.BlockSpec((None,) + a.shape[1:], blk)
    ins = [wcat[0], wcat[1], ccat[0], ccat[1], pl_re, pl_im, jnp.swapaxes(pl_re, 1, 2), jnp.swapaxes(pl_im, 1, 2)]
    m, p, q = pl.pallas_call(
        _ssm_fold_kernel, grid=(SSM_NJ,),
        in_specs=[per_j(a) for a in ins] + [pl.BlockSpec(rep_w.shape, const), pl.BlockSpec(rep_c.shape, const)],
        out_specs=[pl.BlockSpec((None, tw, tw), blk), pl.BlockSpec((None, tw, sq), blk),
                   pl.BlockSpec((None, sq, tw), blk)],
        out_shape=[jax.ShapeDtypeStruct((SSM_NJ, tw, tw), BF16), jax.ShapeDtypeStruct((SSM_NJ, tw, sq), BF16),
                   jax.ShapeDtypeStruct((SSM_NJ, sq, tw), BF16)],
        compiler_params=_cp(("arbitrary",)), name="ssm_fold",
    )(*ins, rep_w, rep_c)
    lam_t = jnp.concatenate([pl_re[:, t_][:, None, :], pl_im[:, t_][:, None, :]], axis=2)
    dvec = jnp.tile(d.reshape(SSM_NJ, 1, LANES), (1, 1, t_))
    return {"M": m, "P": p, "Q": q, "lam": lam_t, "D": dvec}


def _mixer_kernel(o_ref, s_ref, ga_ref, gb_ref, x_ref, wo_ref, wv_ref, wg_ref, wout_ref, h_ref, s_sc):
    o_a = _dot(o_ref[...], wo_ref[...])
    nchunk = s_sc.shape[1] // SSM_T
    for j in range(SSM_NJ):
        sj = s_ref[j].astype(F32)
        for t in range(SSM_T):
            s_sc[j, pl.ds(t, nchunk, stride=SSM_T), :] = sj[:, t * LANES:(t + 1) * LANES]
    s = jnp.concatenate([s_sc[j] for j in range(SSM_NJ)], axis=1).astype(BF16)
    o_b = _dot(s, wv_ref[...]) * jax.nn.sigmoid(_dot(s, wg_ref[...]))
    merged = ga_ref[...] * o_a + gb_ref[...] * o_b
    h_ref[...] = x_ref[...] + _dot(merged.astype(BF16), wout_ref[...])


def _mixer(o, s, ga, gb, x, w, tm):
    n = x.shape[0]
    row = lambda i: (i, 0)
    const = lambda i: (0, 0)
    wspec = pl.BlockSpec((D_MODEL, D_MODEL), const, pipeline_mode=pl.Buffered(1))
    act = pl.BlockSpec((tm, D_MODEL), row)
    return pl.pallas_call(
        _mixer_kernel, grid=(n // tm,),
        in_specs=[act, pl.BlockSpec((SSM_NJ, tm // SSM_T, SSM_T * LANES), lambda i: (0, i, 0)), act, act, act,
                  wspec, wspec, wspec, wspec],
        out_specs=act, out_shape=jax.ShapeDtypeStruct((n, D_MODEL), F32),
        scratch_shapes=[pltpu.VMEM((SSM_NJ, tm, LANES), F32)],
        compiler_params=_cp(("arbitrary",)), name="mixer_out",
    )(o, s, ga, gb, x, w["wo"], w["wv"], w["wg"], w["wout"])


def _mlp_kernel(h_ref, g_ref, wup_ref, wdn_ref, y_ref):
    h = h_ref[...]
    hn = _rms(h, g_ref[...]).astype(BF16)
    a = jnp.maximum(_dot(hn, wup_ref[...]), 0.0)
    y_ref[...] = h + _dot((a * a).astype(BF16), wdn_ref[...])


def _mlp(h, w, tm):
    n = h.shape[0]
    row = lambda i: (i, 0)
    const = lambda i: (0, 0)
    act = pl.BlockSpec((tm, D_MODEL), row)
    return pl.pallas_call(
        _mlp_kernel, grid=(n // tm,),
        in_specs=[act, pl.BlockSpec((1, D_MODEL), const),
                  pl.BlockSpec((D_MODEL, D_FF), const, pipeline_mode=pl.Buffered(1)),
                  pl.BlockSpec((D_FF, D_MODEL), const, pipeline_mode=pl.Buffered(1))],
        out_specs=act, out_shape=jax.ShapeDtypeStruct((n, D_MODEL), F32),
        compiler_params=_cp(("arbitrary",)), name="mlp",
    )(h, w["gmlp"], w["wup"], w["wdn"])


def _rope_tables(pos):
    half = QK_ROPE // 2
    inv = ROPE_THETA ** (-jnp.arange(half, dtype=F32) / half)
    ang = pos.astype(F32)[:, None] * inv[None, :]
    cos, sin = jnp.cos(ang), jnp.sin(ang)
    z = jnp.zeros_like(cos)
    pad = jnp.zeros((pos.shape[0], LANES - QK_ROPE), F32)
    ct = jnp.concatenate([cos, cos, pad], axis=1)
    s1 = jnp.concatenate([-sin, z, pad], axis=1)
    s2 = jnp.concatenate([z, sin, pad], axis=1)
    return ct, s1, s2


def _pad_lanes(a, width):
    return jnp.pad(a, [(0, 0)] * (a.ndim - 1) + [(0, width - a.shape[-1])])


def _layer_weights(norm_mix, w_in, q_lora_norm, w_uq, q_nope_norm, q_rope_norm, kv_lora_norm, k_rope_norm,
                   w_ukv, k_nope_norm, w_o_attn, w_glu_v, w_glu_g, w_out, norm_mlp, w_mlp_up, w_mlp_down):
    o_kv = Q_LORA
    o_kr = o_kv + KV_LORA
    o_ssm = o_kr + QK_ROPE
    o_ga = o_ssm + D_MODEL
    o_gb = o_ga + D_MODEL
    bf = lambda a: a.astype(BF16)
    r2 = lambda a: a.reshape(1, -1).astype(F32)
    return {
        "nmix": r2(norm_mix),
        "wq": bf(w_in[:, :o_kv]), "wkv": bf(w_in[:, o_kv:o_kr]),
        "wkr": bf(_pad_lanes(w_in[:, o_kr:o_ssm], LANES)),
        "wu": bf(w_in[:, o_ssm:o_ga]), "wga": bf(w_in[:, o_ga:o_gb]), "wgb": bf(w_in[:, o_gb:]),
        "gq": r2(q_lora_norm),
        "wuqn": bf(w_uq[:, :, :QK_NOPE].reshape(Q_LORA, N_HEADS * QK_NOPE)),
        "wuqr": bf(_pad_lanes(w_uq[:, :, QK_NOPE:], LANES).reshape(Q_LORA, N_HEADS * LANES)),
        "gqn": r2(q_nope_norm), "gqr": _pad_lanes(r2(q_rope_norm), LANES),
        "gkv": r2(kv_lora_norm), "gkr": _pad_lanes(r2(k_rope_norm), LANES),
        "wuk": bf(w_ukv[:, :, :QK_NOPE].reshape(KV_LORA, N_HEADS * QK_NOPE)),
        "wukg": bf((w_ukv[:, :, :QK_NOPE] * k_nope_norm.astype(F32)).reshape(KV_LORA, N_HEADS * QK_NOPE)),
        "wuv": bf(w_ukv[:, :, QK_NOPE:].reshape(KV_LORA, N_HEADS * V_DIM)),
        "gkn": r2(k_nope_norm),
        "wo": bf(w_o_attn.reshape(N_HEADS * V_DIM, D_MODEL)),
        "wv": bf(w_glu_v), "wg": bf(w_glu_g), "wout": bf(w_out),
        "gmlp": r2(norm_mlp), "wup": bf(w_mlp_up), "wdn": bf(w_mlp_down),
    }


def kernel(x_prompt, x_sample, cache_latent, cache_krope, cache_meta_latent, cache_meta_krope, state_ssm_re, state_ssm_im, meta_tokens, norm_mix, w_in, q_lora_norm, w_uq, q_nope_norm, q_rope_norm, kv_lora_norm, k_rope_norm, w_ukv, k_nope_norm, w_o_attn, ssm_a_re, ssm_a_im, ssm_log_dt, ssm_b_re, ssm_b_im, ssm_c_re, ssm_c_im, ssm_d, w_glu_v, w_glu_g, w_out, norm_mlp, w_mlp_up, w_mlp_down):
    bsz_p, seq_p = x_prompt.shape[0], x_prompt.shape[1]
    bsz_s, seq_s = x_sample.shape[0], x_sample.shape[1]
    past = cache_latent.shape[2]
    depth = w_in.shape[0]
    assert depth == 1, "single-layer step"
    l = 0
    w = _layer_weights(norm_mix[l], w_in[l], q_lora_norm[l], w_uq[l], q_nope_norm[l], q_rope_norm[l],
                       kv_lora_norm[l], k_rope_norm[l], w_ukv[l], k_nope_norm[l], w_o_attn[l], w_glu_v[l],
                       w_glu_g[l], w_out[l], norm_mlp[l], w_mlp_up[l], w_mlp_down[l])
    sw = _ssm_weights(ssm_a_re[l].astype(F32), ssm_a_im[l].astype(F32), ssm_log_dt[l].astype(F32),
                      ssm_b_re[l].astype(F32), ssm_b_im[l].astype(F32), ssm_c_re[l].astype(F32),
                      ssm_c_im[l].astype(F32), ssm_d[l].astype(F32))
    sq = 2 * SSM_SW

    tabs_m = _rope_tables(jnp.arange(N_META, dtype=jnp.int32) - N_META)
    (_, mk, mv, mckv, mkr, mu, _, _) = _front(meta_tokens.astype(F32), tabs_m, 1, w, N_META)
    h_meta = _ssm_state(mu, sw, N_META // SSM_T)

    n_p = bsz_p * seq_p
    tm = 256
    xp = x_prompt.reshape(n_p, D_MODEL)
    tabs_p = _rope_tables(jnp.arange(seq_p, dtype=jnp.int32))
    tm_f = 512
    q, k, v, ckv_p, kr_p, u_p, ga, gb = _front(xp, tabs_p, seq_p // tm_f, w, tm_f)
    o_p = _attn_prompt(q, k, v, mk, mv, bsz_p, seq_p)
    h0_p = jnp.broadcast_to(h_meta, (SSM_NJ, bsz_p, sq))
    s_p, hfin_p = _ssm_prompt(u_p, sw, h0_p, bsz_p, seq_p // SSM_T, 256)
    h1_p = _mixer(o_p, s_p, ga, gb, xp, w, tm_f)
    y_p = _mlp(h1_p, w, tm_f)

    n_s = bsz_s * seq_s
    xs = x_sample.reshape(n_s, D_MODEL)
    pos_s = past + jnp.arange(seq_s, dtype=jnp.int32)
    tabs_s = tuple(jnp.tile(t, (bsz_s, 1)) for t in _rope_tables(pos_s))
    tm_s = 256
    q_s, k_s, _, ckv_s, kr_s, u_s, ga_s, gb_s = _front(xs, tabs_s, n_s // tm_s, w, tm_s)
    o_s = _attn_sample(q_s, k_s, ckv_s, cache_latent[l].astype(F32), cache_krope[l].astype(F32),
                       cache_meta_latent[l].astype(F32), cache_meta_krope[l].astype(F32), w, bsz_s, seq_s, past)

    def to_blocks(st):
        return st.astype(F32).reshape(bsz_s, SSM_NJ, SSM_SW).transpose(1, 0, 2)

    h0_s = jnp.concatenate([to_blocks(state_ssm_re[l]), to_blocks(state_ssm_im[l])], axis=2)
    s_s, hfin_s = _ssm_sample(u_s, sw, h0_s, bsz_s, seq_s // SSM_T)
    h1_s = _mixer(o_s, s_s, ga_s, gb_s, xs, w, tm_s)
    y_s = _mlp(h1_s, w, tm_s)

    def from_blocks(hf, nb):
        re = hf[:, :, :SSM_SW].transpose(1, 0, 2).reshape(1, nb, N_GROUPS, SSM_STATE)
        im = hf[:, :, SSM_SW:].transpose(1, 0, 2).reshape(1, nb, N_GROUPS, SSM_STATE)
        return re, im

    sre_p, sim_p = from_blocks(hfin_p, bsz_p)
    sre_s, sim_s = from_blocks(hfin_s, bsz_s)
    return (y_p.reshape(bsz_p, seq_p, D_MODEL), y_s.reshape(bsz_s, seq_s, D_MODEL),
            ckv_p.reshape(1, bsz_p, seq_p, KV_LORA), kr_p.reshape(1, bsz_p, seq_p, QK_ROPE),
            jnp.broadcast_to(mckv[None, None], (1, bsz_p, N_META, KV_LORA)),
            jnp.broadcast_to(mkr[None, None], (1, bsz_p, N_META, QK_ROPE)),
            sre_p, sim_p,
            ckv_s.reshape(1, bsz_s, seq_s, KV_LORA), kr_s.reshape(1, bsz_s, seq_s, QK_ROPE),
            sre_s, sim_s)
```

```python
import functools
import math

import jax
import jax.numpy as jnp
from jax import lax
from jax.experimental import pallas as pl
from jax.experimental.pallas import tpu as pltpu

F32 = jnp.float32
BF16 = jnp.bfloat16

D_MODEL = 1024
CHUNK = 64
N_META = 16
N_HEADS = 8
QK_NOPE = 128
QK_ROPE = 64
V_DIM = 128
QK_DIM = QK_NOPE + QK_ROPE
Q_LORA = 384
KV_LORA = 256
SSM_GROUP = 16
N_GROUPS = D_MODEL // SSM_GROUP
SSM_STATE = 64
D_FF = 4 * D_MODEL
ROPE_THETA = 10000.0
EPS = 1e-6
ATTN_SCALE = QK_DIM ** -0.5
Q_SCALE = ATTN_SCALE * math.log2(math.e)
NEG_INF = -1e30

LANES = 128
HEAD_PAD = 2 * LANES
SSM_T = 8
SSM_NJ = D_MODEL // LANES
SSM_GPB = LANES // SSM_GROUP
SSM_SW = SSM_GPB * SSM_STATE
VMEM_LIMIT = 56 * 1024 * 1024


def _cp(sem):
    return pltpu.CompilerParams(dimension_semantics=sem, vmem_limit_bytes=VMEM_LIMIT)


def _dot(a, b):
    return jnp.dot(a, b, preferred_element_type=F32)


def _dot_nt(a, b):
    return lax.dot_general(a, b, (((1,), (1,)), ((), ())), preferred_element_type=F32)


def _rms(x, g, n=None):
    n = x.shape[-1] if n is None else n
    ms = jnp.sum(x * x, axis=-1, keepdims=True) * (1.0 / n)
    return x * lax.rsqrt(ms + EPS) * g


def _rope128(b, ct, s1, s2):
    return b * ct + pltpu.roll(b, LANES - QK_ROPE // 2, 1) * s1 + pltpu.roll(b, QK_ROPE // 2, 1) * s2


def _front_kernel(x_ref, ct_ref, s1_ref, s2_ref, nmix_ref, wq_ref, wkv_ref, wkr_ref, wu_ref, wga_ref,
                  wgb_ref, gq_ref, wuqn_ref, wuqr_ref, gqn_ref, gqr_ref, gkv_ref, gkr_ref, wuk_ref,
                  wuv_ref, gkn_ref,
                  q_ref, k_ref, v_ref, ckv_ref, kr_ref, u_ref, ga_ref, gb_ref, u_sc):
    x = x_ref[...]
    xn = _rms(x, nmix_ref[...]).astype(BF16)
    ct, s1, s2 = ct_ref[...], s1_ref[...], s2_ref[...]


    c_q = _rms(_dot(xn, wq_ref[...]), gq_ref[...]).astype(BF16)
    qn = _dot(c_q, wuqn_ref[...])
    qr = _dot(c_q, wuqr_ref[...])
    for h in range(N_HEADS):
        sl = slice(h * LANES, (h + 1) * LANES)
        q_ref[:, h * HEAD_PAD:h * HEAD_PAD + LANES] = (_rms(qn[:, sl], gqn_ref[...]) * Q_SCALE).astype(BF16)
        r = _rope128(_rms(qr[:, sl], gqr_ref[...], QK_ROPE), ct, s1, s2)
        q_ref[:, h * HEAD_PAD + LANES:(h + 1) * HEAD_PAD] = (r * Q_SCALE).astype(BF16)

    c_kv = _rms(_dot(xn, wkv_ref[...]), gkv_ref[...])
    ckv_ref[...] = c_kv
    kr = _rms(_dot(xn, wkr_ref[...]), gkr_ref[...], QK_ROPE)
    kr = _rope128(kr, ct, s1, s2)
    kr_ref[...] = kr[:, :QK_ROPE]
    kr16 = kr.astype(BF16)
    c16 = c_kv.astype(BF16)
    kn = _dot(c16, wuk_ref[...])
    for h in range(N_HEADS):
        sl = slice(h * LANES, (h + 1) * LANES)
        k_ref[:, h * HEAD_PAD:h * HEAD_PAD + LANES] = _rms(kn[:, sl], gkn_ref[...]).astype(BF16)
        k_ref[:, h * HEAD_PAD + LANES:(h + 1) * HEAD_PAD] = kr16
    v_ref[...] = _dot(c16, wuv_ref[...]).astype(BF16)

    u = _dot(xn, wu_ref[...])
    nchunk = u_sc.shape[1] // SSM_T
    for j in range(SSM_NJ):
        u_sc[j] = u[:, j * LANES:(j + 1) * LANES]
        for t in range(SSM_T):
            u_ref[j, :, t * LANES:(t + 1) * LANES] = u_sc[j, pl.ds(t, nchunk, stride=SSM_T), :]
    ga_ref[...] = jax.nn.sigmoid(_dot(xn, wga_ref[...]))
    gb_ref[...] = jax.nn.sigmoid(_dot(xn, wgb_ref[...]))


def _front(x, tabs, tab_blocks, w, tm):
    n = x.shape[0]
    nt = n // tm
    ct, s1, s2 = tabs

    def row(i):
        return (i, 0)

    def tab(i):
        return (i % tab_blocks, 0)

    def const(i):
        return (0, 0)

    def full(a):
        return pl.BlockSpec(a.shape, const, pipeline_mode=pl.Buffered(1))

    weights = [w["nmix"], w["wq"], w["wkv"], w["wkr"], w["wu"], w["wga"], w["wgb"], w["gq"], w["wuqn"],
               w["wuqr"], w["gqn"], w["gqr"], w["gkv"], w["gkr"], w["wuk"], w["wuv"], w["gkn"]]
    wide = N_HEADS * LANES
    qk_wide = N_HEADS * HEAD_PAD
    out_shape = [
        jax.ShapeDtypeStruct((n, qk_wide), BF16),
        jax.ShapeDtypeStruct((n, qk_wide), BF16),
        jax.ShapeDtypeStruct((n, wide), BF16),
        jax.ShapeDtypeStruct((n, KV_LORA), F32),
        jax.ShapeDtypeStruct((n, QK_ROPE), F32),
        jax.ShapeDtypeStruct((SSM_NJ, n // SSM_T, SSM_T * LANES), F32),
        jax.ShapeDtypeStruct((n, D_MODEL), F32),
        jax.ShapeDtypeStruct((n, D_MODEL), F32),
    ]
    out_specs = [
        pl.BlockSpec((tm, qk_wide), row), pl.BlockSpec((tm, qk_wide), row),
        pl.BlockSpec((tm, wide), row), pl.BlockSpec((tm, KV_LORA), row),
        pl.BlockSpec((tm, QK_ROPE), row),
        pl.BlockSpec((SSM_NJ, tm // SSM_T, SSM_T * LANES), lambda i: (0, i, 0)),
        pl.BlockSpec((tm, D_MODEL), row), pl.BlockSpec((tm, D_MODEL), row),
    ]
    in_specs = ([pl.BlockSpec((tm, D_MODEL), row)] + [pl.BlockSpec((tm, LANES), tab)] * 3
                + [full(a) for a in weights])
    return pl.pallas_call(
        _front_kernel, grid=(nt,), in_specs=in_specs, out_specs=out_specs, out_shape=out_shape,
        scratch_shapes=[pltpu.VMEM((SSM_NJ, tm, LANES), F32)],
        compiler_params=_cp(("arbitrary",)), name="front",
    )(x, ct, s1, s2, *weights)


ATT_T = 512


def _with_ones(v):
    return jnp.concatenate([v, jnp.ones_like(v)], axis=1)


def _softmax_step(s, v, m_ref, acc_ref):
    m_prev = m_ref[...]
    m_new = jnp.maximum(m_prev, jnp.max(s, axis=-1, keepdims=True))
    alpha = jnp.exp2(m_prev - m_new)
    p = jnp.exp2(s - jnp.tile(m_new, (1, s.shape[1] // LANES)))
    acc_ref[...] = jnp.tile(alpha, (1, 2)) * acc_ref[...] + _dot(p.astype(BF16), _with_ones(v))
    m_ref[...] = m_new


ATT_HG = 4


def _attn_prompt_kernel(q_ref, k_ref, v_ref, mk_ref, mv_ref, o_ref, a_sc, b_sc, m_ref, acc_ref):
    step = pl.program_id(2)
    npairs = step
    heads = range(ATT_HG)

    def key_rows(kt):
        return pl.ds(pl.multiple_of(kt * ATT_T, ATT_T), ATT_T)

    def q_rows(slot):
        return slice(slot * ATT_T, (slot + 1) * ATT_T)

    def qk_lanes(g):
        return slice(g * HEAD_PAD, (g + 1) * HEAD_PAD)

    def v_lanes(g):
        return slice(g * V_DIM, (g + 1) * V_DIM)

    def scores(slot, g, kt):
        return _dot_nt(q_ref[q_rows(slot), qk_lanes(g)], k_ref[key_rows(kt), qk_lanes(g)])

    def consume(slot, g, s, kt, diagonal):
        if diagonal:
            qc = lax.broadcasted_iota(jnp.int32, (ATT_T, ATT_T), 0) // CHUNK
            kc = lax.broadcasted_iota(jnp.int32, (ATT_T, ATT_T), 1) // CHUNK
            s = jnp.where(kc <= qc, s, NEG_INF)
        _softmax_step(s, v_ref[key_rows(kt), v_lanes(g)], m_ref.at[slot, g], acc_ref.at[slot, g])

    def head(slot):
        for g in heads:
            a_sc[slot, g] = scores(slot, g, 0)
        for g in heads:
            s = _dot_nt(q_ref[q_rows(slot), qk_lanes(g)], mk_ref[:, qk_lanes(g)])
            m0 = jnp.max(s, axis=-1, keepdims=True)
            p = jnp.exp2(s - m0)
            m_ref[slot, g] = jnp.broadcast_to(m0, (ATT_T, LANES))
            acc_ref[slot, g] = _dot(p.astype(BF16), _with_ones(mv_ref[:, v_lanes(g)]))

    def full_pairs(slot):
        def pair(pi, carry):
            kt = 2 * pi
            for g in heads:
                b_sc[g] = scores(slot, g, kt + 1)
                consume(slot, g, a_sc[slot, g], kt, False)
            for g in heads:
                a_sc[slot, g] = scores(slot, g, kt + 2)
                consume(slot, g, b_sc[g], kt + 1, False)
            return carry

        lax.fori_loop(0, npairs, pair, 0)

    def tail(slot):
        i = 2 * step + slot
        if slot == 1:
            for g in heads:
                b_sc[g] = scores(slot, g, i)
                consume(slot, g, a_sc[slot, g], i - 1, False)
            for g in heads:
                consume(slot, g, b_sc[g], i, True)
        else:
            for g in heads:
                consume(slot, g, a_sc[slot, g], i, True)
        for g in heads:
            o_ref[q_rows(slot), v_lanes(g)] = (acc_ref[slot, g, :, :V_DIM]
                                               / acc_ref[slot, g, :, V_DIM:]).astype(BF16)

    head(0)
    full_pairs(0)
    tail(0)
    head(1)
    full_pairs(1)
    tail(1)


def _attn_prompt(q, k, v, mk, mv, bsz, seq):
    nstep = seq // (2 * ATT_T)
    n = bsz * seq
    qmap = lambda b, h, i: (b * nstep + i, h)
    return pl.pallas_call(
        _attn_prompt_kernel, grid=(bsz, N_HEADS // ATT_HG, nstep),
        in_specs=[pl.BlockSpec((2 * ATT_T, ATT_HG * HEAD_PAD), qmap),
                  pl.BlockSpec((seq, ATT_HG * HEAD_PAD), lambda b, h, i: (b, h)),
                  pl.BlockSpec((seq, ATT_HG * V_DIM), lambda b, h, i: (b, h)),
                  pl.BlockSpec((N_META, ATT_HG * HEAD_PAD), lambda b, h, i: (0, h)),
                  pl.BlockSpec((N_META, ATT_HG * V_DIM), lambda b, h, i: (0, h))],
        out_specs=pl.BlockSpec((2 * ATT_T, ATT_HG * V_DIM), qmap),
        out_shape=jax.ShapeDtypeStruct((n, N_HEADS * V_DIM), BF16),
        scratch_shapes=[pltpu.VMEM((2, ATT_HG, ATT_T, ATT_T), F32), pltpu.VMEM((ATT_HG, ATT_T, ATT_T), F32),
                        pltpu.VMEM((2, ATT_HG, ATT_T, LANES), F32),
                        pltpu.VMEM((2, ATT_HG, ATT_T, 2 * V_DIM), F32)],
        compiler_params=_cp(("arbitrary", "arbitrary", "arbitrary")), name="attn_prompt",
    )(q, k, v, mk, mv)


SAMPLE_KEY_CHUNK = 512


def _attn_sample_kernel(q_ref, cm_ref, cn_ref, cc_ref, krm_ref, krn_ref, krc_ref, wuk_ref, wukg_ref, wuv_ref,
                        hsum_ref, o_ref, c_sc, kr_sc, st_sc, pt_sc, *, seq, past):
    n_small = N_META + seq
    n_keys = n_small + past

    c_sc[:, KV_LORA:] = jnp.ones((n_keys, LANES), BF16)
    c_sc[0:N_META, :KV_LORA] = cm_ref[...].astype(BF16)
    c_sc[N_META:n_small, :KV_LORA] = cn_ref[...].astype(BF16)
    c_sc[n_small:, :KV_LORA] = cc_ref[...].astype(BF16)
    kr_sc[:, QK_ROPE:] = jnp.zeros((n_small, LANES - QK_ROPE), BF16)
    kr_sc[0:N_META, :QK_ROPE] = krm_ref[...].astype(BF16)
    kr_sc[N_META:n_small, :] = krn_ref[...]

    qa, qr = [], []
    for h in range(N_HEADS):
        qn_h = q_ref[:, h * HEAD_PAD:h * HEAD_PAD + LANES]
        qa.append(_dot_nt(qn_h, wukg_ref[:, h * LANES:(h + 1) * LANES]))
        qr.append(q_ref[:, h * HEAD_PAD + LANES:(h + 1) * HEAD_PAD])
    qa = jnp.concatenate(qa, axis=0).astype(BF16)
    qr = jnp.concatenate(qr, axis=0)

    bounds = [(0, n_small)] + [(n_small + i, n_small + i + SAMPLE_KEY_CHUNK)
                               for i in range(0, past, SAMPLE_KEY_CHUNK)]
    m = None
    for lo, hi in bounds:
        c16 = c_sc[lo:hi, :KV_LORA]
        kn = _dot(c16, wuk_ref[...])
        ssq = _dot((kn * kn).astype(BF16), hsum_ref[...])
        r = lax.rsqrt(ssq * (1.0 / QK_NOPE) + EPS)
        if lo == 0:
            s_rope = _dot_nt(kr_sc[...], qr)
        else:
            kr_t = krc_ref[:, lo - n_small:hi - n_small].astype(BF16)
            s_rope = lax.dot_general(kr_t, qr[:, :QK_ROPE], (((0,), (1,)), ((), ())),
                                     preferred_element_type=F32)
        st = _dot_nt(c16, qa) * r + s_rope
        st_sc[lo:hi, :] = st
        cm = jnp.max(st, axis=0, keepdims=True)
        m = cm if m is None else jnp.maximum(m, cm)

    for lo, hi in bounds:
        pt_sc[lo:hi, :] = jnp.exp2(st_sc[lo:hi, :] - m).astype(BF16)
    pc = lax.dot_general(pt_sc[...], c_sc[...], (((0,), (0,)), ((), ())), preferred_element_type=F32)
    pcn = (pc[:, :KV_LORA] / jnp.tile(pc[:, KV_LORA:], (1, KV_LORA // LANES))).astype(BF16)
    for h in range(N_HEADS):
        o_ref[:, h * V_DIM:(h + 1) * V_DIM] = _dot(pcn[h * seq:(h + 1) * seq, :],
                                                   wuv_ref[:, h * V_DIM:(h + 1) * V_DIM]).astype(BF16)


def _attn_sample(q, k_new, c_new, cache_c, cache_kr, meta_c, meta_kr, w, bsz, seq, past):
    n_keys = N_META + seq + past
    hq = N_HEADS * seq
    hsum = jnp.repeat(jnp.repeat(jnp.eye(N_HEADS, dtype=BF16), QK_NOPE, axis=0), seq, axis=1)
    row = lambda b: (b, 0)
    b3 = lambda b: (b, 0, 0)
    const = lambda b: (0, 0)
    kern = functools.partial(_attn_sample_kernel, seq=seq, past=past)
    return pl.pallas_call(
        kern, grid=(bsz,),
        in_specs=[pl.BlockSpec((seq, N_HEADS * HEAD_PAD), row),
                  pl.BlockSpec((None, N_META, KV_LORA), b3), pl.BlockSpec((seq, KV_LORA), row),
                  pl.BlockSpec((None, past, KV_LORA), b3),
                  pl.BlockSpec((None, N_META, QK_ROPE), b3), pl.BlockSpec((seq, LANES), lambda b: (b, 1)),
                  pl.BlockSpec((None, QK_ROPE, past), b3),
                  pl.BlockSpec(w["wuk"].shape, const), pl.BlockSpec(w["wukg"].shape, const),
                  pl.BlockSpec(w["wuv"].shape, const), pl.BlockSpec(hsum.shape, const)],
        out_specs=pl.BlockSpec((seq, N_HEADS * V_DIM), row),
        out_shape=jax.ShapeDtypeStruct((bsz * seq, N_HEADS * V_DIM), BF16),
        scratch_shapes=[pltpu.VMEM((n_keys, KV_LORA + LANES), BF16), pltpu.VMEM((N_META + seq, LANES), BF16),
                        pltpu.VMEM((n_keys, hq), F32), pltpu.VMEM((n_keys, hq), BF16)],
        compiler_params=_cp(("arbitrary",)), name="attn_sample",
    )(q, meta_c, c_new, cache_c, meta_kr, k_new, jnp.swapaxes(cache_kr, 1, 2), w["wuk"], w["wukg"], w["wuv"],
      hsum)


MXU_DIM = 256


def _dot_causal(u16, m_ref):
    nblk = m_ref.shape[0] // MXU_DIM
    return jnp.concatenate(
        [_dot(u16[:, :(cb + 1) * MXU_DIM], m_ref[:(cb + 1) * MXU_DIM, cb * MXU_DIM:(cb + 1) * MXU_DIM])
         for cb in range(nblk)], axis=1)


SSM_PACK = 4
SSM_SC = SSM_SW // LANES
SUBLANES = 2 * SSM_PACK


def _ssm_prompt_kernel(u_ref, m_ref, p_ref, q_ref, lam_ref, d_ref, h0_ref, s_ref, hfin_ref, s_sc, hp_sc, h_sc,
                       *, nb, rt):
    t = pl.program_id(1)

    @pl.when(t == 0)
    def _():
        h_sc[...] = h0_ref[...]

    if nb < SSM_PACK:
        s_sc[...] = jnp.zeros_like(s_sc)
    for b in range(nb):
        sb = _dot(u_ref[b].astype(BF16), p_ref[...])
        for c in range(SSM_SC):
            s_sc[c, pl.ds(b, rt, stride=SUBLANES), :] = sb[:, c * LANES:(c + 1) * LANES]
            s_sc[c, pl.ds(SSM_PACK + b, rt, stride=SUBLANES), :] = sb[:, SSM_SW + c * LANES:SSM_SW + (c + 1) * LANES]

    lam = lam_ref[...]
    im_rows = lax.broadcasted_iota(jnp.int32, (SUBLANES, LANES), 0) >= SSM_PACK
    coef = []
    for c in range(SSM_SC):
        lr = jnp.broadcast_to(lam[:, c * LANES:(c + 1) * LANES], (SUBLANES, LANES))
        li = jnp.broadcast_to(lam[:, SSM_SW + c * LANES:SSM_SW + (c + 1) * LANES], (SUBLANES, LANES))
        coef.append((lr, jnp.where(im_rows, li, -li)))

    hs = [h_sc[c] for c in range(SSM_SC)]
    for r in range(rt):
        rows = slice(r * SUBLANES, (r + 1) * SUBLANES)
        for c in range(SSM_SC):
            hp_sc[c, rows, :] = hs[c]
            a, bb = coef[c]
            hs[c] = a * hs[c] + bb * pltpu.roll(hs[c], SSM_PACK, 0) + s_sc[c, rows, :]
    for c in range(SSM_SC):
        h_sc[c] = hs[c]

    for b in range(nb):
        ub = u_ref[b]
        hb = jnp.concatenate([hp_sc[c, pl.ds(b, rt, stride=SUBLANES), :] for c in range(SSM_SC)]
                             + [hp_sc[c, pl.ds(SSM_PACK + b, rt, stride=SUBLANES), :] for c in range(SSM_SC)],
                             axis=1)
        y = _dot_causal(ub.astype(BF16), m_ref) + _dot(hb.astype(BF16), q_ref[...]) + ub * d_ref[...]
        s_ref[b] = jax.nn.gelu(y).astype(BF16)

    @pl.when(t == pl.num_programs(1) - 1)
    def _():
        hfin_ref[...] = h_sc[...]


def _pack_state(h):
    nj, nb, _ = h.shape
    h = jnp.pad(h, ((0, 0), (0, SSM_PACK - nb), (0, 0)))
    re = h[:, :, :SSM_SW].reshape(nj, SSM_PACK, SSM_SC, LANES)
    im = h[:, :, SSM_SW:].reshape(nj, SSM_PACK, SSM_SC, LANES)
    return jnp.concatenate([re, im], axis=1).transpose(0, 2, 1, 3)


def _unpack_state(hp, nb):
    nj = hp.shape[0]
    x = hp.transpose(0, 2, 1, 3).reshape(nj, SUBLANES, SSM_SW)
    return jnp.concatenate([x[:, :nb], x[:, SSM_PACK:SSM_PACK + nb]], axis=2)


def _ssm_prompt(u, sw, h0, nb, lc, rt):
    assert nb <= SSM_PACK
    tw = SSM_T * LANES
    u4 = u.reshape(SSM_NJ, nb, lc, tw)
    sq = 2 * SSM_SW
    kern = functools.partial(_ssm_prompt_kernel, nb=nb, rt=rt)
    st_spec = pl.BlockSpec((None, SSM_SC, SUBLANES, LANES), lambda j, t: (j, 0, 0, 0))
    s, hfin = pl.pallas_call(
        kern, grid=(SSM_NJ, lc // rt),
        in_specs=[pl.BlockSpec((None, nb, rt, tw), lambda j, t: (j, 0, t, 0)),
                  pl.BlockSpec((None, tw, tw), lambda j, t: (j, 0, 0)),
                  pl.BlockSpec((None, tw, sq), lambda j, t: (j, 0, 0)),
                  pl.BlockSpec((None, sq, tw), lambda j, t: (j, 0, 0)),
                  pl.BlockSpec((None, 1, sq), lambda j, t: (j, 0, 0)),
                  pl.BlockSpec((None, 1, tw), lambda j, t: (j, 0, 0)),
                  st_spec],
        out_specs=[pl.BlockSpec((None, nb, rt, tw), lambda j, t: (j, 0, t, 0)), st_spec],
        out_shape=[jax.ShapeDtypeStruct((SSM_NJ, nb, lc, tw), BF16),
                   jax.ShapeDtypeStruct((SSM_NJ, SSM_SC, SUBLANES, LANES), F32)],
        scratch_shapes=[pltpu.VMEM((SSM_SC, rt * SUBLANES, LANES), F32),
                        pltpu.VMEM((SSM_SC, rt * SUBLANES, LANES), F32),
                        pltpu.VMEM((SSM_SC, SUBLANES, LANES), F32)],
        compiler_params=_cp(("arbitrary", "arbitrary")), name="ssm_prompt",
    )(u4, sw["M"], sw["P"], sw["Q"], sw["lam"], sw["D"], _pack_state(h0))
    return s.reshape(SSM_NJ, nb * lc, tw), _unpack_state(hfin, nb)


def _ssm_state_kernel(u_ref, p_ref, lam_ref, h_ref):
    s = _dot(u_ref[...].astype(BF16), p_ref[...])
    lam = lam_ref[...]
    lr, li = lam[:, :SSM_SW], lam[:, SSM_SW:]
    h = s[0:1, :]
    for r in range(1, s.shape[0]):
        hre, him = h[:, :SSM_SW], h[:, SSM_SW:]
        h = jnp.concatenate([lr * hre - li * him, lr * him + li * hre], axis=1) + s[r:r + 1, :]
    h_ref[...] = h


def _ssm_state(u, sw, lc):
    tw = SSM_T * LANES
    sq = 2 * SSM_SW
    blk = lambda j: (j, 0, 0)
    return pl.pallas_call(
        _ssm_state_kernel, grid=(SSM_NJ,),
        in_specs=[pl.BlockSpec((None, lc, tw), blk), pl.BlockSpec((None, tw, sq), blk),
                  pl.BlockSpec((None, 1, sq), blk)],
        out_specs=pl.BlockSpec((None, 1, sq), blk),
        out_shape=jax.ShapeDtypeStruct((SSM_NJ, 1, sq), F32),
        compiler_params=_cp(("arbitrary",)), name="ssm_state",
    )(u, sw["P"], sw["lam"])


def _ssm_sample_kernel(u_ref, m_ref, p_ref, q_ref, lam_ref, d_ref, h0_ref, s_ref, hfin_ref, s_sc,
                       *, nb, lc):
    u = u_ref[...]
    u16 = u.astype(BF16)
    s_sc[...] = _dot(u16, p_ref[...])
    lam = lam_ref[...]
    lr, li = lam[:, :SSM_SW], lam[:, SSM_SW:]
    h = h0_ref[...]
    for r in range(lc):
        rows = slice(r * nb, (r + 1) * nb)
        s = s_sc[rows, :]
        s_sc[rows, :] = h
        hre, him = h[:, :SSM_SW], h[:, SSM_SW:]
        h = jnp.concatenate([lr * hre - li * him + s[:, :SSM_SW], lr * him + li * hre + s[:, SSM_SW:]], axis=1)
    hfin_ref[...] = h
    y = _dot_causal(u16, m_ref) + _dot(s_sc[...].astype(BF16), q_ref[...]) + u * d_ref[...]
    s_ref[...] = jax.nn.gelu(y).astype(BF16)


def _ssm_sample(u, sw, h0, nb, lc):
    tw = SSM_T * LANES
    sq = 2 * SSM_SW
    u2 = u.reshape(SSM_NJ, nb, lc, tw).transpose(0, 2, 1, 3).reshape(SSM_NJ, lc * nb, tw)
    kern = functools.partial(_ssm_sample_kernel, nb=nb, lc=lc)
    blk = lambda j: (j, 0, 0)
    s, hfin = pl.pallas_call(
        kern, grid=(SSM_NJ,),
        in_specs=[pl.BlockSpec((None, lc * nb, tw), blk), pl.BlockSpec((None, tw, tw), blk),
                  pl.BlockSpec((None, tw, sq), blk), pl.BlockSpec((None, sq, tw), blk),
                  pl.BlockSpec((None, 1, sq), blk), pl.BlockSpec((None, 1, tw), blk),
                  pl.BlockSpec((None, nb, sq), blk)],
        out_specs=[pl.BlockSpec((None, lc * nb, tw), blk), pl.BlockSpec((None, nb, sq), blk)],
        out_shape=[jax.ShapeDtypeStruct((SSM_NJ, lc * nb, tw), BF16),
                   jax.ShapeDtypeStruct((SSM_NJ, nb, sq), F32)],
        scratch_shapes=[pltpu.VMEM((lc * nb, sq), F32)],
        compiler_params=_cp(("arbitrary",)), name="ssm_sample",
    )(u2, sw["M"], sw["P"], sw["Q"], sw["lam"], sw["D"], h0)
    s = s.reshape(SSM_NJ, lc, nb, tw).transpose(0, 2, 1, 3).reshape(SSM_NJ, nb * lc, tw)
    return s, hfin


def _ssm_fold_kernel(wre_ref, wim_ref, cre_ref, cim_ref, plr_ref, pli_ref, rep_ref, m_ref, p_ref, q_ref):
    def split(a, terms):
        out = []
        for _ in range(terms):
            t = a.astype(BF16)
            out.append(t)
            a = a - t.astype(F32)
        return out

    def repeat_lanes(a, rep):
        return sum(_dot(t, rep) for t in split(a, 3))

    def dot_nt_hi_lo(a, b_parts):
        a_hi, a_lo = split(a, 2)
        b_hi, b_lo = b_parts
        return _dot_nt(a_hi, b_hi) + (_dot_nt(a_hi, b_lo) + _dot_nt(a_lo, b_hi))

    same_group = (lax.broadcasted_iota(jnp.int32, (LANES, SSM_SW), 0) // SSM_GROUP
                  == lax.broadcasted_iota(jnp.int32, (LANES, SSM_SW), 1) // SSM_STATE)
    rep = rep_ref[...]
    wb_re = jnp.where(same_group, repeat_lanes(wre_ref[...], rep), 0.0)
    wb_im = jnp.where(same_group, repeat_lanes(wim_ref[...], rep), 0.0)
    cb_re = jnp.where(same_group, repeat_lanes(cre_ref[...], rep), 0.0)
    cb_im = jnp.where(same_group, repeat_lanes(cim_ref[...], rep), 0.0)
    cb_re_parts, cb_im_parts = split(cb_re, 2), split(cb_im, 2)

    m_ref[...] = jnp.zeros_like(m_ref)
    for k in range(SSM_T):
        lr, li = plr_ref[k:k + 1, :], pli_ref[k:k + 1, :]
        v_re = lr * wb_re - li * wb_im
        v_im = lr * wb_im + li * wb_re
        s = SSM_T - 1 - k
        p_ref[s * LANES:(s + 1) * LANES, :SSM_SW] = v_re.astype(BF16)
        p_ref[s * LANES:(s + 1) * LANES, SSM_SW:] = v_im.astype(BF16)
        bd = (dot_nt_hi_lo(v_re, cb_re_parts) - dot_nt_hi_lo(v_im, cb_im_parts)).astype(BF16)
        for s in range(SSM_T - k):
            t = s + k
            m_ref[s * LANES:(s + 1) * LANES, t * LANES:(t + 1) * LANES] = bd
        lr1, li1 = plr_ref[k + 1:k + 2, :], pli_ref[k + 1:k + 2, :]
        q_ref[:SSM_SW, k * LANES:(k + 1) * LANES] = (cb_re * lr1 - cb_im * li1).T.astype(BF16)
        q_ref[SSM_SW:, k * LANES:(k + 1) * LANES] = (-(cb_re * li1 + cb_im * lr1)).T.astype(BF16)


def _ssm_weights(a_re, a_im, log_dt, b_re, b_im, c_re, c_im, d):
    t_ = SSM_T
    dt = jnp.exp(log_dt)[:, None]
    mag = jnp.exp(a_re * dt)
    lam_re, lam_im = mag * jnp.cos(a_im * dt), mag * jnp.sin(a_im * dt)
    den = a_re * a_re + a_im * a_im
    f_re = ((lam_re - 1.0) * a_re + lam_im * a_im) / den
    f_im = (lam_im * a_re - (lam_re - 1.0) * a_im) / den
    w_re = f_re[:, :, None] * b_re - f_im[:, :, None] * b_im
    w_im = f_re[:, :, None] * b_im + f_im[:, :, None] * b_re
    k = jnp.arange(t_ + 1, dtype=F32)[None, :, None]
    pmag = jnp.exp(a_re[:, None, :] * dt[:, :, None] * k)
    pang = a_im[:, None, :] * dt[:, :, None] * k
    pw_re, pw_im = pmag * jnp.cos(pang), pmag * jnp.sin(pang)

    def lanes_gn(x):
        x = x.reshape(SSM_NJ, SSM_GPB, t_ + 1, SSM_STATE)
        return jnp.swapaxes(x, 1, 2).reshape(SSM_NJ, t_ + 1, SSM_SW)

    pl_re, pl_im = lanes_gn(pw_re), lanes_gn(pw_im)
    wcat = [jnp.swapaxes(x, 1, 2).reshape(SSM_NJ, LANES, SSM_STATE) for x in (w_re, w_im)]
    ccat = [x.reshape(SSM_NJ, LANES, SSM_STATE) for x in (c_re, c_im)]
    rep = jnp.tile(jnp.eye(SSM_STATE, dtype=BF16), (1, SSM_GPB))
    tw = t_ * LANES
    sq = 2 * SSM_SW
    blk = lambda j: (j, 0, 0)
    const = lambda j: (0, 0)
    per_j = lambda a: pl.BlockSpec((None,) + a.shape[1:], blk)
    ins = [wcat[0], wcat[1], ccat[0], ccat[1], pl_re, pl_im]
    m, p, q = pl.pallas_call(
        _ssm_fold_kernel, grid=(SSM_NJ,),
        in_specs=[per_j(a) for a in ins] + [pl.BlockSpec(rep.shape, const)],
        out_specs=[pl.BlockSpec((None, tw, tw), blk), pl.BlockSpec((None, tw, sq), blk),
                   pl.BlockSpec((None, sq, tw), blk)],
        out_shape=[jax.ShapeDtypeStruct((SSM_NJ, tw, tw), BF16), jax.ShapeDtypeStruct((SSM_NJ, tw, sq), BF16),
                   jax.ShapeDtypeStruct((SSM_NJ, sq, tw), BF16)],
        compiler_params=_cp(("arbitrary",)), name="ssm_fold",
    )(*ins, rep)
    lam_t = jnp.concatenate([pl_re[:, t_][:, None, :], pl_im[:, t_][:, None, :]], axis=2)
    dvec = jnp.tile(d.reshape(SSM_NJ, 1, LANES), (1, 1, t_))
    return {"M": m, "P": p, "Q": q, "lam": lam_t, "D": dvec}


def _mixer_kernel(o_ref, s_ref, ga_ref, gb_ref, x_ref, wo_ref, wv_ref, wg_ref, wout_ref, h_ref, s_sc):
    o_a = _dot(o_ref[...], wo_ref[...])
    nchunk = s_sc.shape[1] // SSM_T
    for j in range(SSM_NJ):
        sj = s_ref[j].astype(F32)
        for t in range(SSM_T):
            s_sc[j, pl.ds(t, nchunk, stride=SSM_T), :] = sj[:, t * LANES:(t + 1) * LANES]
    s = jnp.concatenate([s_sc[j] for j in range(SSM_NJ)], axis=1).astype(BF16)
    o_b = _dot(s, wv_ref[...]) * jax.nn.sigmoid(_dot(s, wg_ref[...]))
    merged = ga_ref[...] * o_a + gb_ref[...] * o_b
    h_ref[...] = x_ref[...] + _dot(merged.astype(BF16), wout_ref[...])


def _mixer(o, s, ga, gb, x, w, tm):
    n = x.shape[0]
    row = lambda i: (i, 0)
    const = lambda i: (0, 0)
    wspec = pl.BlockSpec((D_MODEL, D_MODEL), const, pipeline_mode=pl.Buffered(1))
    act = pl.BlockSpec((tm, D_MODEL), row)
    return pl.pallas_call(
        _mixer_kernel, grid=(n // tm,),
        in_specs=[act, pl.BlockSpec((SSM_NJ, tm // SSM_T, SSM_T * LANES), lambda i: (0, i, 0)), act, act, act,
                  wspec, wspec, wspec, wspec],
        out_specs=act, out_shape=jax.ShapeDtypeStruct((n, D_MODEL), F32),
        scratch_shapes=[pltpu.VMEM((SSM_NJ, tm, LANES), F32)],
        compiler_params=_cp(("arbitrary",)), name="mixer_out",
    )(o, s, ga, gb, x, w["wo"], w["wv"], w["wg"], w["wout"])


def _mlp_kernel(h_ref, g_ref, wup_ref, wdn_ref, y_ref):
    h = h_ref[...]
    hn = _rms(h, g_ref[...]).astype(BF16)
    a = jnp.maximum(_dot(hn, wup_ref[...]), 0.0)
    y_ref[...] = h + _dot((a * a).astype(BF16), wdn_ref[...])


def _mlp(h, w, tm):
    n = h.shape[0]
    row = lambda i: (i, 0)
    const = lambda i: (0, 0)
    act = pl.BlockSpec((tm, D_MODEL), row)
    return pl.pallas_call(
        _mlp_kernel, grid=(n // tm,),
        in_specs=[act, pl.BlockSpec((1, D_MODEL), const),
                  pl.BlockSpec((D_MODEL, D_FF), const, pipeline_mode=pl.Buffered(1)),
                  pl.BlockSpec((D_FF, D_MODEL), const, pipeline_mode=pl.Buffered(1))],
        out_specs=act, out_shape=jax.ShapeDtypeStruct((n, D_MODEL), F32),
        compiler_params=_cp(("arbitrary",)), name="mlp",
    )(h, w["gmlp"], w["wup"], w["wdn"])


def _rope_tables(pos):
    half = QK_ROPE // 2
    inv = ROPE_THETA ** (-jnp.arange(half, dtype=F32) / half)
    ang = pos.astype(F32)[:, None] * inv[None, :]
    cos, sin = jnp.cos(ang), jnp.sin(ang)
    z = jnp.zeros_like(cos)
    pad = jnp.zeros((pos.shape[0], LANES - QK_ROPE), F32)
    ct = jnp.concatenate([cos, cos, pad], axis=1)
    s1 = jnp.concatenate([-sin, z, pad], axis=1)
    s2 = jnp.concatenate([z, sin, pad], axis=1)
    return ct, s1, s2


def _pad_lanes(a, width):
    return jnp.pad(a, [(0, 0)] * (a.ndim - 1) + [(0, width - a.shape[-1])])


def _layer_weights(norm_mix, w_in, q_lora_norm, w_uq, q_nope_norm, q_rope_norm, kv_lora_norm, k_rope_norm,
                   w_ukv, k_nope_norm, w_o_attn, w_glu_v, w_glu_g, w_out, norm_mlp, w_mlp_up, w_mlp_down):
    o_kv = Q_LORA
    o_kr = o_kv + KV_LORA
    o_ssm = o_kr + QK_ROPE
    o_ga = o_ssm + D_MODEL
    o_gb = o_ga + D_MODEL
    bf = lambda a: a.astype(BF16)
    r2 = lambda a: a.reshape(1, -1).astype(F32)
    return {
        "nmix": r2(norm_mix),
        "wq": bf(w_in[:, :o_kv]), "wkv": bf(w_in[:, o_kv:o_kr]),
        "wkr": bf(_pad_lanes(w_in[:, o_kr:o_ssm], LANES)),
        "wu": bf(w_in[:, o_ssm:o_ga]), "wga": bf(w_in[:, o_ga:o_gb]), "wgb": bf(w_in[:, o_gb:]),
        "gq": r2(q_lora_norm),
        "wuqn": bf(w_uq[:, :, :QK_NOPE].reshape(Q_LORA, N_HEADS * QK_NOPE)),
        "wuqr": bf(_pad_lanes(w_uq[:, :, QK_NOPE:], LANES).reshape(Q_LORA, N_HEADS * LANES)),
        "gqn": r2(q_nope_norm), "gqr": _pad_lanes(r2(q_rope_norm), LANES),
        "gkv": r2(kv_lora_norm), "gkr": _pad_lanes(r2(k_rope_norm), LANES),
        "wuk": bf(w_ukv[:, :, :QK_NOPE].reshape(KV_LORA, N_HEADS * QK_NOPE)),
        "wukg": bf((w_ukv[:, :, :QK_NOPE] * k_nope_norm.astype(F32)).reshape(KV_LORA, N_HEADS * QK_NOPE)),
        "wuv": bf(w_ukv[:, :, QK_NOPE:].reshape(KV_LORA, N_HEADS * V_DIM)),
        "gkn": r2(k_nope_norm),
        "wo": bf(w_o_attn.reshape(N_HEADS * V_DIM, D_MODEL)),
        "wv": bf(w_glu_v), "wg": bf(w_glu_g), "wout": bf(w_out),
        "gmlp": r2(norm_mlp), "wup": bf(w_mlp_up), "wdn": bf(w_mlp_down),
    }


def kernel(x_prompt, x_sample, cache_latent, cache_krope, cache_meta_latent, cache_meta_krope, state_ssm_re, state_ssm_im, meta_tokens, norm_mix, w_in, q_lora_norm, w_uq, q_nope_norm, q_rope_norm, kv_lora_norm, k_rope_norm, w_ukv, k_nope_norm, w_o_attn, ssm_a_re, ssm_a_im, ssm_log_dt, ssm_b_re, ssm_b_im, ssm_c_re, ssm_c_im, ssm_d, w_glu_v, w_glu_g, w_out, norm_mlp, w_mlp_up, w_mlp_down):
    bsz_p, seq_p = x_prompt.shape[0], x_prompt.shape[1]
    bsz_s, seq_s = x_sample.shape[0], x_sample.shape[1]
    past = cache_latent.shape[2]
    depth = w_in.shape[0]
    assert depth == 1, "single-layer step"
    l = 0
    w = _layer_weights(norm_mix[l], w_in[l], q_lora_norm[l], w_uq[l], q_nope_norm[l], q_rope_norm[l],
                       kv_lora_norm[l], k_rope_norm[l], w_ukv[l], k_nope_norm[l], w_o_attn[l], w_glu_v[l],
                       w_glu_g[l], w_out[l], norm_mlp[l], w_mlp_up[l], w_mlp_down[l])
    sw = _ssm_weights(ssm_a_re[l].astype(F32), ssm_a_im[l].astype(F32), ssm_log_dt[l].astype(F32),
                      ssm_b_re[l].astype(F32), ssm_b_im[l].astype(F32), ssm_c_re[l].astype(F32),
                      ssm_c_im[l].astype(F32), ssm_d[l].astype(F32))
    sq = 2 * SSM_SW

    tabs_m = _rope_tables(jnp.arange(N_META, dtype=jnp.int32) - N_META)
    (_, mk, mv, mckv, mkr, mu, _, _) = _front(meta_tokens.astype(F32), tabs_m, 1, w, N_META)
    h_meta = _ssm_state(mu, sw, N_META // SSM_T)

    n_p = bsz_p * seq_p
    tm = 256
    xp = x_prompt.reshape(n_p, D_MODEL)
    tabs_p = _rope_tables(jnp.arange(seq_p, dtype=jnp.int32))
    tm_f = 512
    q, k, v, ckv_p, kr_p, u_p, ga, gb = _front(xp, tabs_p, seq_p // tm_f, w, tm_f)
    o_p = _attn_prompt(q, k, v, mk, mv, bsz_p, seq_p)
    h0_p = jnp.broadcast_to(h_meta, (SSM_NJ, bsz_p, sq))
    s_p, hfin_p = _ssm_prompt(u_p, sw, h0_p, bsz_p, seq_p // SSM_T, 256)
    h1_p = _mixer(o_p, s_p, ga, gb, xp, w, tm_f)
    y_p = _mlp(h1_p, w, tm_f)

    n_s = bsz_s * seq_s
    xs = x_sample.reshape(n_s, D_MODEL)
    pos_s = past + jnp.arange(seq_s, dtype=jnp.int32)
    tabs_s = tuple(jnp.tile(t, (bsz_s, 1)) for t in _rope_tables(pos_s))
    tm_s = 256
    q_s, k_s, _, ckv_s, kr_s, u_s, ga_s, gb_s = _front(xs, tabs_s, n_s // tm_s, w, tm_s)
    o_s = _attn_sample(q_s, k_s, ckv_s, cache_latent[l].astype(F32), cache_krope[l].astype(F32),
                       cache_meta_latent[l].astype(F32), cache_meta_krope[l].astype(F32), w, bsz_s, seq_s, past)

    def to_blocks(st):
        return st.astype(F32).reshape(bsz_s, SSM_NJ, SSM_SW).transpose(1, 0, 2)

    h0_s = jnp.concatenate([to_blocks(state_ssm_re[l]), to_blocks(state_ssm_im[l])], axis=2)
    s_s, hfin_s = _ssm_sample(u_s, sw, h0_s, bsz_s, seq_s // SSM_T)
    h1_s = _mixer(o_s, s_s, ga_s, gb_s, xs, w, tm_s)
    y_s = _mlp(h1_s, w, tm_s)

    def from_blocks(hf, nb):
        re = hf[:, :, :SSM_SW].transpose(1, 0, 2).reshape(1, nb, N_GROUPS, SSM_STATE)
        im = hf[:, :, SSM_SW:].transpose(1, 0, 2).reshape(1, nb, N_GROUPS, SSM_STATE)
        return re, im

    sre_p, sim_p = from_blocks(hfin_p, bsz_p)
    sre_s, sim_s = from_blocks(hfin_s, bsz_s)
    return (y_p.reshape(bsz_p, seq_p, D_MODEL), y_s.reshape(bsz_s, seq_s, D_MODEL),
            ckv_p.reshape(1, bsz_p, seq_p, KV_LORA), kr_p.reshape(1, bsz_p, seq_p, QK_ROPE),
            jnp.broadcast_to(mckv[None, None], (1, bsz_p, N_META, KV_LORA)),
            jnp.broadcast_to(mkr[None, None], (1, bsz_p, N_META, QK_ROPE)),
            sre_p, sim_p,
            ckv_s.reshape(1, bsz_s, seq_s, KV_LORA), kr_s.reshape(1, bsz_s, seq_s, QK_ROPE),
            sre_s, sim_s)
```

```python
import functools
import math

import jax
import jax.numpy as jnp
from jax import lax
from jax.experimental import pallas as pl
from jax.experimental.pallas import tpu as pltpu

F32 = jnp.float32
BF16 = jnp.bfloat16

D_MODEL = 1024
CHUNK = 64
N_META = 16
N_HEADS = 8
QK_NOPE = 128
QK_ROPE = 64
V_DIM = 128
QK_DIM = QK_NOPE + QK_ROPE
Q_LORA = 384
KV_LORA = 256
SSM_GROUP = 16
N_GROUPS = D_MODEL // SSM_GROUP
SSM_STATE = 64
D_FF = 4 * D_MODEL
ROPE_THETA = 10000.0
EPS = 1e-6
ATTN_SCALE = QK_DIM ** -0.5
Q_SCALE = ATTN_SCALE * math.log2(math.e)
NEG_INF = -1e30

LANES = 128
HEAD_PAD = 2 * LANES
SSM_T = 8
SSM_NJ = D_MODEL // LANES
SSM_GPB = LANES // SSM_GROUP
SSM_SW = SSM_GPB * SSM_STATE
VMEM_LIMIT = 56 * 1024 * 1024
TM_PROMPT = 512
TM_SAMPLE = 256
SSM_RT = 256


def _cp(sem):
    return pltpu.CompilerParams(dimension_semantics=sem, vmem_limit_bytes=VMEM_LIMIT)


def _dot(a, b):
    return jnp.dot(a, b, preferred_element_type=F32)


def _dot_nt(a, b):
    return lax.dot_general(a, b, (((1,), (1,)), ((), ())), preferred_element_type=F32)


def _rms(x, g, n=None):
    n = x.shape[-1] if n is None else n
    ms = jnp.sum(x * x, axis=-1, keepdims=True) * (1.0 / n)
    return x * lax.rsqrt(ms + EPS) * g


def _rope128(b, ct, s1, s2):
    return b * ct + pltpu.roll(b, LANES - QK_ROPE // 2, 1) * s1 + pltpu.roll(b, QK_ROPE // 2, 1) * s2


IN_Q = 0
IN_KV = IN_Q + Q_LORA
IN_U = IN_KV + KV_LORA
IN_GA = IN_U + D_MODEL
IN_GB = IN_GA + D_MODEL
IN_KR = IN_GB + D_MODEL
IN_END = IN_KR + LANES


def _front_kernel(x_ref, ct_ref, s1_ref, s2_ref, nmix_ref, win_ref, gq_ref, wuqn_ref, wuqr_ref, gqn_ref,
                  gqr_ref, gkv_ref, gkr_ref, wuk_ref, wuv_ref, gkn_ref,
                  q_ref, k_ref, v_ref, ckv_ref, kr_ref, u_ref, ga_ref, gb_ref, u_sc, *, kr_transposed):
    x = x_ref[...]
    xn = _rms(x, nmix_ref[...]).astype(BF16)
    ct, s1, s2 = ct_ref[...], s1_ref[...], s2_ref[...]


    c_q = _rms(_dot(xn, win_ref[:, IN_Q:IN_KV]), gq_ref[...]).astype(BF16)
    qn = _dot(c_q, wuqn_ref[...])
    qr = _dot(c_q, wuqr_ref[...])
    for h in range(N_HEADS):
        sl = slice(h * LANES, (h + 1) * LANES)
        q_ref[:, h * HEAD_PAD:h * HEAD_PAD + LANES] = (_rms(qn[:, sl], gqn_ref[...]) * Q_SCALE).astype(BF16)
        r = _rope128(_rms(qr[:, sl], gqr_ref[...], QK_ROPE), ct, s1, s2)
        q_ref[:, h * HEAD_PAD + LANES:(h + 1) * HEAD_PAD] = (r * Q_SCALE).astype(BF16)

    c_kv = _rms(_dot(xn, win_ref[:, IN_KV:IN_U]), gkv_ref[...])
    ckv_ref[...] = c_kv
    kr = _rms(_dot(xn, win_ref[:, IN_KR:IN_END]), gkr_ref[...], QK_ROPE)
    kr = _rope128(kr, ct, s1, s2)
    kr_ref[...] = kr.T[:QK_ROPE, :] if kr_transposed else kr[:, :QK_ROPE]
    kr16 = kr.astype(BF16)
    c16 = c_kv.astype(BF16)
    kn = _dot(c16, wuk_ref[...])
    for h in range(N_HEADS):
        sl = slice(h * LANES, (h + 1) * LANES)
        k_ref[:, h * HEAD_PAD:h * HEAD_PAD + LANES] = _rms(kn[:, sl], gkn_ref[...]).astype(BF16)
        k_ref[:, h * HEAD_PAD + LANES:(h + 1) * HEAD_PAD] = kr16
    v_ref[...] = _dot(c16, wuv_ref[...]).astype(BF16)

    u = _dot(xn, win_ref[:, IN_U:IN_GA])
    nchunk = u_sc.shape[1] // SSM_T
    for j in range(SSM_NJ):
        u_sc[j] = u[:, j * LANES:(j + 1) * LANES]
        for t in range(SSM_T):
            u_ref[j, :, t * LANES:(t + 1) * LANES] = u_sc[j, pl.ds(t, nchunk, stride=SSM_T), :]
    ga_ref[...] = _dot(xn, win_ref[:, IN_GA:IN_GB])
    gb_ref[...] = _dot(xn, win_ref[:, IN_GB:IN_KR])


def _front(x, tabs, tab_blocks, w, tm, kr_seq=None):
    n = x.shape[0]
    nt = n // tm
    ct, s1, s2 = tabs

    def row(i):
        return (i, 0)

    def tab(i):
        return (i % tab_blocks, 0)

    def const(i):
        return (0, 0)

    def full(a):
        return pl.BlockSpec(a.shape, const, pipeline_mode=pl.Buffered(1))

    weights = [w["nmix"], w["win"], w["gq"], w["wuqn"], w["wuqr"], w["gqn"], w["gqr"], w["gkv"], w["gkr"],
               w["wuk"], w["wuv"], w["gkn"]]
    wide = N_HEADS * LANES
    qk_wide = N_HEADS * HEAD_PAD
    out_shape = [
        jax.ShapeDtypeStruct((n, qk_wide), BF16),
        jax.ShapeDtypeStruct((n, qk_wide), BF16),
        jax.ShapeDtypeStruct((n, wide), BF16),
        jax.ShapeDtypeStruct((n, KV_LORA), F32),
        (jax.ShapeDtypeStruct((n, QK_ROPE), F32) if kr_seq is None
         else jax.ShapeDtypeStruct((n // kr_seq, QK_ROPE, kr_seq), F32)),
        jax.ShapeDtypeStruct((SSM_NJ, n // SSM_T, SSM_T * LANES), F32),
        jax.ShapeDtypeStruct((n, D_MODEL), F32),
        jax.ShapeDtypeStruct((n, D_MODEL), F32),
    ]
    out_specs = [
        pl.BlockSpec((tm, qk_wide), row), pl.BlockSpec((tm, qk_wide), row),
        pl.BlockSpec((tm, wide), row), pl.BlockSpec((tm, KV_LORA), row),
        (pl.BlockSpec((tm, QK_ROPE), row) if kr_seq is None else
         pl.BlockSpec((None, QK_ROPE, tm), lambda i: (i // (kr_seq // tm), 0, i % (kr_seq // tm)))),
        pl.BlockSpec((SSM_NJ, tm // SSM_T, SSM_T * LANES), lambda i: (0, i, 0)),
        pl.BlockSpec((tm, D_MODEL), row), pl.BlockSpec((tm, D_MODEL), row),
    ]
    in_specs = ([pl.BlockSpec((tm, D_MODEL), row)] + [pl.BlockSpec((tm, LANES), tab)] * 3
                + [full(a) for a in weights])
    return pl.pallas_call(
        functools.partial(_front_kernel, kr_transposed=kr_seq is not None), grid=(nt,), in_specs=in_specs,
        out_specs=out_specs, out_shape=out_shape,
        scratch_shapes=[pltpu.VMEM((SSM_NJ, tm, LANES), F32)],
        compiler_params=_cp(("arbitrary",)), name="front",
    )(x, ct, s1, s2, *weights)


ATT_T = 512


def _with_ones(v):
    return jnp.concatenate([v, jnp.ones_like(v)], axis=1)


def _softmax_step(s, v, m_ref, acc_ref):
    m_prev = m_ref[...]
    m_new = jnp.maximum(m_prev, jnp.max(s, axis=-1, keepdims=True))
    alpha = jnp.exp2(m_prev - m_new)
    p = jnp.exp2(s - jnp.tile(m_new, (1, s.shape[1] // LANES)))
    acc_ref[...] = jnp.tile(alpha, (1, 2)) * acc_ref[...] + _dot(p.astype(BF16), _with_ones(v))
    m_ref[...] = m_new


ATT_HG = 4


def _attn_prompt_kernel(q_ref, k_ref, v_ref, mk_ref, mv_ref, o_ref, a_sc, b_sc, m_ref, acc_ref):
    step = pl.program_id(2)
    npairs = step
    heads = range(ATT_HG)

    def key_rows(kt):
        return pl.ds(pl.multiple_of(kt * ATT_T, ATT_T), ATT_T)

    def q_rows(slot):
        return slice(slot * ATT_T, (slot + 1) * ATT_T)

    def qk_lanes(g):
        return slice(g * HEAD_PAD, (g + 1) * HEAD_PAD)

    def v_lanes(g):
        return slice(g * V_DIM, (g + 1) * V_DIM)

    def scores(slot, g, kt):
        return _dot_nt(q_ref[q_rows(slot), qk_lanes(g)], k_ref[key_rows(kt), qk_lanes(g)])

    def consume(slot, g, s, kt, diagonal):
        if diagonal:
            qc = lax.broadcasted_iota(jnp.int32, (ATT_T, ATT_T), 0) // CHUNK
            kc = lax.broadcasted_iota(jnp.int32, (ATT_T, ATT_T), 1) // CHUNK
            s = jnp.where(kc <= qc, s, NEG_INF)
        _softmax_step(s, v_ref[key_rows(kt), v_lanes(g)], m_ref.at[slot, g], acc_ref.at[slot, g])

    def head(slot):
        for g in heads:
            a_sc[slot, g] = scores(slot, g, 0)
        for g in heads:
            s = _dot_nt(q_ref[q_rows(slot), qk_lanes(g)], mk_ref[:, qk_lanes(g)])
            m0 = jnp.max(s, axis=-1, keepdims=True)
            p = jnp.exp2(s - m0)
            m_ref[slot, g] = jnp.broadcast_to(m0, (ATT_T, LANES))
            acc_ref[slot, g] = _dot(p.astype(BF16), _with_ones(mv_ref[:, v_lanes(g)]))

    def full_pairs(slot):
        def pair(pi, carry):
            kt = 2 * pi
            for g in heads:
                b_sc[g] = scores(slot, g, kt + 1)
                consume(slot, g, a_sc[slot, g], kt, False)
            for g in heads:
                a_sc[slot, g] = scores(slot, g, kt + 2)
                consume(slot, g, b_sc[g], kt + 1, False)
            return carry

        lax.fori_loop(0, npairs, pair, 0)

    def tail(slot):
        i = 2 * step + slot
        if slot == 1:
            for g in heads:
                b_sc[g] = scores(slot, g, i)
                consume(slot, g, a_sc[slot, g], i - 1, False)
            for g in heads:
                consume(slot, g, b_sc[g], i, True)
        else:
            for g in heads:
                consume(slot, g, a_sc[slot, g], i, True)
        for g in heads:
            o_ref[q_rows(slot), v_lanes(g)] = (acc_ref[slot, g, :, :V_DIM]
                                               / acc_ref[slot, g, :, V_DIM:]).astype(BF16)

    head(0)
    full_pairs(0)
    tail(0)
    head(1)
    full_pairs(1)
    tail(1)


def _attn_prompt(q, k, v, mk, mv, bsz, seq):
    nstep = seq // (2 * ATT_T)
    n = bsz * seq
    qmap = lambda b, h, i: (b * nstep + i, h)
    return pl.pallas_call(
        _attn_prompt_kernel, grid=(bsz, N_HEADS // ATT_HG, nstep),
        in_specs=[pl.BlockSpec((2 * ATT_T, ATT_HG * HEAD_PAD), qmap),
                  pl.BlockSpec((seq, ATT_HG * HEAD_PAD), lambda b, h, i: (b, h)),
                  pl.BlockSpec((seq, ATT_HG * V_DIM), lambda b, h, i: (b, h)),
                  pl.BlockSpec((N_META, ATT_HG * HEAD_PAD), lambda b, h, i: (0, h)),
                  pl.BlockSpec((N_META, ATT_HG * V_DIM), lambda b, h, i: (0, h))],
        out_specs=pl.BlockSpec((2 * ATT_T, ATT_HG * V_DIM), qmap),
        out_shape=jax.ShapeDtypeStruct((n, N_HEADS * V_DIM), BF16),
        scratch_shapes=[pltpu.VMEM((2, ATT_HG, ATT_T, ATT_T), F32), pltpu.VMEM((ATT_HG, ATT_T, ATT_T), F32),
                        pltpu.VMEM((2, ATT_HG, ATT_T, LANES), F32),
                        pltpu.VMEM((2, ATT_HG, ATT_T, 2 * V_DIM), F32)],
        compiler_params=_cp(("arbitrary", "arbitrary", "arbitrary")), name="attn_prompt",
    )(q, k, v, mk, mv)


SAMPLE_KEY_CHUNK = 512


def _attn_sample_kernel(q_ref, cm_ref, cn_ref, cc_ref, krm_ref, krn_ref, krc_ref, wuk_ref, wukg_ref, wuv_ref,
                        hsum_ref, o_ref, c_sc, kr_sc, st_sc, pt_sc, *, seq, past):
    n_small = N_META + seq
    n_keys = n_small + past

    c_sc[:, KV_LORA:] = jnp.ones((n_keys, LANES), BF16)
    c_sc[0:N_META, :KV_LORA] = cm_ref[...].astype(BF16)
    c_sc[N_META:n_small, :KV_LORA] = cn_ref[...].astype(BF16)
    c_sc[n_small:, :KV_LORA] = cc_ref[...].astype(BF16)
    kr_sc[:, QK_ROPE:] = jnp.zeros((n_small, LANES - QK_ROPE), BF16)
    kr_sc[0:N_META, :QK_ROPE] = krm_ref[...].astype(BF16)
    kr_sc[N_META:n_small, :] = krn_ref[...]

    qa, qr = [], []
    for h in range(N_HEADS):
        qn_h = q_ref[:, h * HEAD_PAD:h * HEAD_PAD + LANES]
        qa.append(_dot_nt(qn_h, wukg_ref[:, h * LANES:(h + 1) * LANES]))
        qr.append(q_ref[:, h * HEAD_PAD + LANES:(h + 1) * HEAD_PAD])
    qa = jnp.concatenate(qa, axis=0).astype(BF16)
    qr = jnp.concatenate(qr, axis=0)

    bounds = [(0, n_small)] + [(n_small + i, n_small + i + SAMPLE_KEY_CHUNK)
                               for i in range(0, past, SAMPLE_KEY_CHUNK)]
    m = None
    for lo, hi in bounds:
        c16 = c_sc[lo:hi, :KV_LORA]
        kn = _dot(c16, wuk_ref[...])
        ssq = _dot((kn * kn).astype(BF16), hsum_ref[...])
        r = lax.rsqrt(ssq * (1.0 / QK_NOPE) + EPS)
        if lo == 0:
            s_rope = _dot_nt(kr_sc[...], qr)
        else:
            kr_t = krc_ref[:, lo - n_small:hi - n_small].astype(BF16)
            s_rope = lax.dot_general(kr_t, qr[:, :QK_ROPE], (((0,), (1,)), ((), ())),
                                     preferred_element_type=F32)
        st = _dot_nt(c16, qa) * r + s_rope
        st_sc[lo:hi, :] = st
        cm = jnp.max(st, axis=0, keepdims=True)
        m = cm if m is None else jnp.maximum(m, cm)

    for lo, hi in bounds:
        pt_sc[lo:hi, :] = jnp.exp2(st_sc[lo:hi, :] - m).astype(BF16)
    pc = lax.dot_general(pt_sc[...], c_sc[...], (((0,), (0,)), ((), ())), preferred_element_type=F32)
    pcn = (pc[:, :KV_LORA] / jnp.tile(pc[:, KV_LORA:], (1, KV_LORA // LANES))).astype(BF16)
    for h in range(N_HEADS):
        o_ref[:, h * V_DIM:(h + 1) * V_DIM] = _dot(pcn[h * seq:(h + 1) * seq, :],
                                                   wuv_ref[:, h * V_DIM:(h + 1) * V_DIM]).astype(BF16)


def _attn_sample(q, k_new, c_new, cache_c, cache_kr, meta_c, meta_kr, w, bsz, seq, past):
    n_keys = N_META + seq + past
    hq = N_HEADS * seq
    hsum = jnp.repeat(jnp.repeat(jnp.eye(N_HEADS, dtype=BF16), QK_NOPE, axis=0), seq, axis=1)
    row = lambda b: (b, 0)
    b3 = lambda b: (b, 0, 0)
    const = lambda b: (0, 0)
    kern = functools.partial(_attn_sample_kernel, seq=seq, past=past)
    return pl.pallas_call(
        kern, grid=(bsz,),
        in_specs=[pl.BlockSpec((seq, N_HEADS * HEAD_PAD), row),
                  pl.BlockSpec((None, N_META, KV_LORA), b3), pl.BlockSpec((seq, KV_LORA), row),
                  pl.BlockSpec((None, past, KV_LORA), b3),
                  pl.BlockSpec((None, N_META, QK_ROPE), b3), pl.BlockSpec((seq, LANES), lambda b: (b, 1)),
                  pl.BlockSpec((None, QK_ROPE, past), b3),
                  pl.BlockSpec(w["wuk"].shape, const), pl.BlockSpec(w["wukg"].shape, const),
                  pl.BlockSpec(w["wuv"].shape, const), pl.BlockSpec(hsum.shape, const)],
        out_specs=pl.BlockSpec((seq, N_HEADS * V_DIM), row),
        out_shape=jax.ShapeDtypeStruct((bsz * seq, N_HEADS * V_DIM), BF16),
        scratch_shapes=[pltpu.VMEM((n_keys, KV_LORA + LANES), BF16), pltpu.VMEM((N_META + seq, LANES), BF16),
                        pltpu.VMEM((n_keys, hq), F32), pltpu.VMEM((n_keys, hq), BF16)],
        compiler_params=_cp(("arbitrary",)), name="attn_sample",
    )(q, meta_c, c_new, cache_c, meta_kr, k_new, jnp.swapaxes(cache_kr, 1, 2), w["wuk"], w["wukg"], w["wuv"],
      hsum)


MXU_DIM = 256


def _dot_causal(u16, m_ref):
    nblk = m_ref.shape[0] // MXU_DIM
    return jnp.concatenate(
        [_dot(u16[:, :(cb + 1) * MXU_DIM], m_ref[:(cb + 1) * MXU_DIM, cb * MXU_DIM:(cb + 1) * MXU_DIM])
         for cb in range(nblk)], axis=1)


SSM_PACK = 4
SSM_SC = SSM_SW // LANES
SUBLANES = 2 * SSM_PACK


def _ssm_prompt_kernel(u_ref, m_ref, p_ref, q_ref, lam_ref, d_ref, h0_ref, s_ref, hfin_ref, s_sc, hp_sc, h_sc,
                       *, nb, rt):
    t = pl.program_id(1)

    @pl.when(t == 0)
    def _():
        h_sc[...] = h0_ref[...]

    for b in range(nb):
        sb = _dot(u_ref[b].astype(BF16), p_ref[...])
        for c in range(SSM_SC):
            s_sc[c, pl.ds(b, rt, stride=SUBLANES), :] = sb[:, c * LANES:(c + 1) * LANES]
            s_sc[c, pl.ds(SSM_PACK + b, rt, stride=SUBLANES), :] = sb[:, SSM_SW + c * LANES:SSM_SW + (c + 1) * LANES]

    lam = lam_ref[...]
    im_rows = lax.broadcasted_iota(jnp.int32, (SUBLANES, LANES), 0) >= SSM_PACK
    coef = []
    for c in range(SSM_SC):
        lr = jnp.broadcast_to(lam[:, c * LANES:(c + 1) * LANES], (SUBLANES, LANES))
        li = jnp.broadcast_to(lam[:, SSM_SW + c * LANES:SSM_SW + (c + 1) * LANES], (SUBLANES, LANES))
        coef.append((lr, jnp.where(im_rows, li, -li)))

    hs = [h_sc[c] for c in range(SSM_SC)]
    for r in range(rt):
        rows = slice(r * SUBLANES, (r + 1) * SUBLANES)
        for c in range(SSM_SC):
            hp_sc[c, rows, :] = hs[c]
            a, bb = coef[c]
            hs[c] = a * hs[c] + bb * pltpu.roll(hs[c], SSM_PACK, 0) + s_sc[c, rows, :]
    for c in range(SSM_SC):
        h_sc[c] = hs[c]

    for b in range(nb):
        ub = u_ref[b]
        hb = jnp.concatenate([hp_sc[c, pl.ds(b, rt, stride=SUBLANES), :] for c in range(SSM_SC)]
                             + [hp_sc[c, pl.ds(SSM_PACK + b, rt, stride=SUBLANES), :] for c in range(SSM_SC)],
                             axis=1)
        y = _dot_causal(ub.astype(BF16), m_ref) + _dot(hb.astype(BF16), q_ref[...]) + ub * d_ref[...]
        s_ref[b] = jax.nn.gelu(y).astype(BF16)

    @pl.when(t == pl.num_programs(1) - 1)
    def _():
        hfin_ref[...] = h_sc[...]


def _pack_state(h):
    nj, nb, _ = h.shape
    h = jnp.pad(h, ((0, 0), (0, SSM_PACK - nb), (0, 0)))
    re = h[:, :, :SSM_SW].reshape(nj, SSM_PACK, SSM_SC, LANES)
    im = h[:, :, SSM_SW:].reshape(nj, SSM_PACK, SSM_SC, LANES)
    return jnp.concatenate([re, im], axis=1).transpose(0, 2, 1, 3)


def _unpack_state(hp, nb):
    nj = hp.shape[0]
    x = hp.transpose(0, 2, 1, 3).reshape(nj, SUBLANES, SSM_SW)
    return jnp.concatenate([x[:, :nb], x[:, SSM_PACK:SSM_PACK + nb]], axis=2)


def _ssm_prompt(u, sw, h0, nb, lc, rt):
    assert nb == SSM_PACK and lc % rt == 0
    tw = SSM_T * LANES
    u4 = u.reshape(SSM_NJ, nb, lc, tw)
    sq = 2 * SSM_SW
    kern = functools.partial(_ssm_prompt_kernel, nb=nb, rt=rt)
    st_spec = pl.BlockSpec((None, SSM_SC, SUBLANES, LANES), lambda j, t: (j, 0, 0, 0))
    s, hfin = pl.pallas_call(
        kern, grid=(SSM_NJ, lc // rt),
        in_specs=[pl.BlockSpec((None, nb, rt, tw), lambda j, t: (j, 0, t, 0)),
                  pl.BlockSpec((None, tw, tw), lambda j, t: (j, 0, 0)),
                  pl.BlockSpec((None, tw, sq), lambda j, t: (j, 0, 0)),
                  pl.BlockSpec((None, sq, tw), lambda j, t: (j, 0, 0)),
                  pl.BlockSpec((None, 1, sq), lambda j, t: (j, 0, 0)),
                  pl.BlockSpec((None, 1, tw), lambda j, t: (j, 0, 0)),
                  st_spec],
        out_specs=[pl.BlockSpec((None, nb, rt, tw), lambda j, t: (j, 0, t, 0)), st_spec],
        out_shape=[jax.ShapeDtypeStruct((SSM_NJ, nb, lc, tw), BF16),
                   jax.ShapeDtypeStruct((SSM_NJ, SSM_SC, SUBLANES, LANES), F32)],
        scratch_shapes=[pltpu.VMEM((SSM_SC, rt * SUBLANES, LANES), F32),
                        pltpu.VMEM((SSM_SC, rt * SUBLANES, LANES), F32),
                        pltpu.VMEM((SSM_SC, SUBLANES, LANES), F32)],
        compiler_params=_cp(("arbitrary", "arbitrary")), name="ssm_prompt",
    )(u4, sw["M"], sw["P"], sw["Q"], sw["lam"], sw["D"], _pack_state(h0))
    return s.reshape(SSM_NJ, nb * lc, tw), _unpack_state(hfin, nb)


def _ssm_state_kernel(u_ref, p_ref, lam_ref, h_ref):
    s = _dot(u_ref[...].astype(BF16), p_ref[...])
    lam = lam_ref[...]
    lr, li = lam[:, :SSM_SW], lam[:, SSM_SW:]
    h = s[0:1, :]
    for r in range(1, s.shape[0]):
        hre, him = h[:, :SSM_SW], h[:, SSM_SW:]
        h = jnp.concatenate([lr * hre - li * him, lr * him + li * hre], axis=1) + s[r:r + 1, :]
    h_ref[...] = h


def _ssm_state(u, sw, lc):
    tw = SSM_T * LANES
    sq = 2 * SSM_SW
    blk = lambda j: (j, 0, 0)
    return pl.pallas_call(
        _ssm_state_kernel, grid=(SSM_NJ,),
        in_specs=[pl.BlockSpec((None, lc, tw), blk), pl.BlockSpec((None, tw, sq), blk),
                  pl.BlockSpec((None, 1, sq), blk)],
        out_specs=pl.BlockSpec((None, 1, sq), blk),
        out_shape=jax.ShapeDtypeStruct((SSM_NJ, 1, sq), F32),
        compiler_params=_cp(("arbitrary",)), name="ssm_state",
    )(u, sw["P"], sw["lam"])


def _ssm_sample_kernel(u_ref, m_ref, p_ref, q_ref, lam_ref, d_ref, h0_ref, s_ref, hfin_ref, s_sc,
                       *, nb, lc):
    u = u_ref[...]
    u16 = u.astype(BF16)
    s_sc[...] = _dot(u16, p_ref[...])
    lam = lam_ref[...]
    lr, li = lam[:, :SSM_SW], lam[:, SSM_SW:]
    h = h0_ref[...]
    for r in range(lc):
        rows = slice(r * nb, (r + 1) * nb)
        s = s_sc[rows, :]
        s_sc[rows, :] = h
        hre, him = h[:, :SSM_SW], h[:, SSM_SW:]
        h = jnp.concatenate([lr * hre - li * him + s[:, :SSM_SW], lr * him + li * hre + s[:, SSM_SW:]], axis=1)
    hfin_ref[...] = h
    y = _dot_causal(u16, m_ref) + _dot(s_sc[...].astype(BF16), q_ref[...]) + u * d_ref[...]
    s_ref[...] = jax.nn.gelu(y).astype(BF16)


def _ssm_sample(u, sw, h0, nb, lc):
    tw = SSM_T * LANES
    sq = 2 * SSM_SW
    u2 = u.reshape(SSM_NJ, nb, lc, tw).transpose(0, 2, 1, 3).reshape(SSM_NJ, lc * nb, tw)
    kern = functools.partial(_ssm_sample_kernel, nb=nb, lc=lc)
    blk = lambda j: (j, 0, 0)
    s, hfin = pl.pallas_call(
        kern, grid=(SSM_NJ,),
        in_specs=[pl.BlockSpec((None, lc * nb, tw), blk), pl.BlockSpec((None, tw, tw), blk),
                  pl.BlockSpec((None, tw, sq), blk), pl.BlockSpec((None, sq, tw), blk),
                  pl.BlockSpec((None, 1, sq), blk), pl.BlockSpec((None, 1, tw), blk),
                  pl.BlockSpec((None, nb, sq), blk)],
        out_specs=[pl.BlockSpec((None, lc * nb, tw), blk), pl.BlockSpec((None, nb, sq), blk)],
        out_shape=[jax.ShapeDtypeStruct((SSM_NJ, lc * nb, tw), BF16),
                   jax.ShapeDtypeStruct((SSM_NJ, nb, sq), F32)],
        scratch_shapes=[pltpu.VMEM((lc * nb, sq), F32)],
        compiler_params=_cp(("arbitrary",)), name="ssm_sample",
    )(u2, sw["M"], sw["P"], sw["Q"], sw["lam"], sw["D"], h0)
    s = s.reshape(SSM_NJ, lc, nb, tw).transpose(0, 2, 1, 3).reshape(SSM_NJ, nb * lc, tw)
    return s, hfin


def _ssm_fold_kernel(wre_ref, wim_ref, cre_ref, cim_ref, plr_ref, pli_ref, rep_ref, m_ref, p_ref, q_ref):
    def split(a, terms):
        out = []
        for _ in range(terms):
            t = a.astype(BF16)
            out.append(t)
            a = a - t.astype(F32)
        return out

    def repeat_lanes(a, rep):
        return sum(_dot(t, rep) for t in split(a, 3))

    def dot_nt_hi_lo(a, b_parts):
        a_hi, a_lo = split(a, 2)
        b_hi, b_lo = b_parts
        return _dot_nt(a_hi, b_hi) + (_dot_nt(a_hi, b_lo) + _dot_nt(a_lo, b_hi))

    same_group = (lax.broadcasted_iota(jnp.int32, (LANES, SSM_SW), 0) // SSM_GROUP
                  == lax.broadcasted_iota(jnp.int32, (LANES, SSM_SW), 1) // SSM_STATE)
    rep = rep_ref[...]
    wb_re = jnp.where(same_group, repeat_lanes(wre_ref[...], rep), 0.0)
    wb_im = jnp.where(same_group, repeat_lanes(wim_ref[...], rep), 0.0)
    cb_re = jnp.where(same_group, repeat_lanes(cre_ref[...], rep), 0.0)
    cb_im = jnp.where(same_group, repeat_lanes(cim_ref[...], rep), 0.0)
    cb_re_parts, cb_im_parts = split(cb_re, 2), split(cb_im, 2)

    m_ref[...] = jnp.zeros_like(m_ref)
    for k in range(SSM_T):
        lr, li = plr_ref[k:k + 1, :], pli_ref[k:k + 1, :]
        v_re = lr * wb_re - li * wb_im
        v_im = lr * wb_im + li * wb_re
        s = SSM_T - 1 - k
        p_ref[s * LANES:(s + 1) * LANES, :SSM_SW] = v_re.astype(BF16)
        p_ref[s * LANES:(s + 1) * LANES, SSM_SW:] = v_im.astype(BF16)
        bd = (dot_nt_hi_lo(v_re, cb_re_parts) - dot_nt_hi_lo(v_im, cb_im_parts)).astype(BF16)
        for s in range(SSM_T - k):
            t = s + k
            m_ref[s * LANES:(s + 1) * LANES, t * LANES:(t + 1) * LANES] = bd
        lr1, li1 = plr_ref[k + 1:k + 2, :], pli_ref[k + 1:k + 2, :]
        q_ref[:SSM_SW, k * LANES:(k + 1) * LANES] = (cb_re * lr1 - cb_im * li1).T.astype(BF16)
        q_ref[SSM_SW:, k * LANES:(k + 1) * LANES] = (-(cb_re * li1 + cb_im * lr1)).T.astype(BF16)


def _ssm_weights(a_re, a_im, log_dt, b_re, b_im, c_re, c_im, d):
    t_ = SSM_T
    dt = jnp.exp(log_dt)[:, None]
    mag = jnp.exp(a_re * dt)
    lam_re, lam_im = mag * jnp.cos(a_im * dt), mag * jnp.sin(a_im * dt)
    den = a_re * a_re + a_im * a_im
    f_re = ((lam_re - 1.0) * a_re + lam_im * a_im) / den
    f_im = (lam_im * a_re - (lam_re - 1.0) * a_im) / den
    w_re = f_re[:, :, None] * b_re - f_im[:, :, None] * b_im
    w_im = f_re[:, :, None] * b_im + f_im[:, :, None] * b_re
    k = jnp.arange(t_ + 1, dtype=F32)[None, :, None]
    pmag = jnp.exp(a_re[:, None, :] * dt[:, :, None] * k)
    pang = a_im[:, None, :] * dt[:, :, None] * k
    pw_re, pw_im = pmag * jnp.cos(pang), pmag * jnp.sin(pang)

    def lanes_gn(x):
        x = x.reshape(SSM_NJ, SSM_GPB, t_ + 1, SSM_STATE)
        return jnp.swapaxes(x, 1, 2).reshape(SSM_NJ, t_ + 1, SSM_SW)

    pl_re, pl_im = lanes_gn(pw_re), lanes_gn(pw_im)
    wcat = [jnp.swapaxes(x, 1, 2).reshape(SSM_NJ, LANES, SSM_STATE) for x in (w_re, w_im)]
    ccat = [x.reshape(SSM_NJ, LANES, SSM_STATE) for x in (c_re, c_im)]
    rep = jnp.tile(jnp.eye(SSM_STATE, dtype=BF16), (1, SSM_GPB))
    tw = t_ * LANES
    sq = 2 * SSM_SW
    blk = lambda j: (j, 0, 0)
    const = lambda j: (0, 0)
    per_j = lambda a: pl.BlockSpec((None,) + a.shape[1:], blk)
    ins = [wcat[0], wcat[1], ccat[0], ccat[1], pl_re, pl_im]
    m, p, q = pl.pallas_call(
        _ssm_fold_kernel, grid=(SSM_NJ,),
        in_specs=[per_j(a) for a in ins] + [pl.BlockSpec(rep.shape, const)],
        out_specs=[pl.BlockSpec((None, tw, tw), blk), pl.BlockSpec((None, tw, sq), blk),
                   pl.BlockSpec((None, sq, tw), blk)],
        out_shape=[jax.ShapeDtypeStruct((SSM_NJ, tw, tw), BF16), jax.ShapeDtypeStruct((SSM_NJ, tw, sq), BF16),
                   jax.ShapeDtypeStruct((SSM_NJ, sq, tw), BF16)],
        compiler_params=_cp(("arbitrary",)), name="ssm_fold",
    )(*ins, rep)
    lam_t = jnp.concatenate([pl_re[:, t_][:, None, :], pl_im[:, t_][:, None, :]], axis=2)
    dvec = jnp.tile(d.reshape(SSM_NJ, 1, LANES), (1, 1, t_))
    return {"M": m, "P": p, "Q": q, "lam": lam_t, "D": dvec}


def _mixer_kernel(o_ref, s_ref, ga_ref, gb_ref, x_ref, wo_ref, wv_ref, wg_ref, wout_ref, h_ref, s_sc):
    o_a = _dot(o_ref[...], wo_ref[...])
    nchunk = s_sc.shape[1] // SSM_T
    for j in range(SSM_NJ):
        sj = s_ref[j].astype(F32)
        for t in range(SSM_T):
            s_sc[j, pl.ds(t, nchunk, stride=SSM_T), :] = sj[:, t * LANES:(t + 1) * LANES]
    s = jnp.concatenate([s_sc[j] for j in range(SSM_NJ)], axis=1).astype(BF16)
    o_b = _dot(s, wv_ref[...]) * jax.nn.sigmoid(_dot(s, wg_ref[...]))
    merged = jax.nn.sigmoid(ga_ref[...]) * o_a + jax.nn.sigmoid(gb_ref[...]) * o_b
    h_ref[...] = x_ref[...] + _dot(merged.astype(BF16), wout_ref[...])


def _mixer(o, s, ga, gb, x, w, tm):
    n = x.shape[0]
    row = lambda i: (i, 0)
    const = lambda i: (0, 0)
    wspec = pl.BlockSpec((D_MODEL, D_MODEL), const, pipeline_mode=pl.Buffered(1))
    act = pl.BlockSpec((tm, D_MODEL), row)
    return pl.pallas_call(
        _mixer_kernel, grid=(n // tm,),
        in_specs=[act, pl.BlockSpec((SSM_NJ, tm // SSM_T, SSM_T * LANES), lambda i: (0, i, 0)), act, act, act,
                  wspec, wspec, wspec, wspec],
        out_specs=act, out_shape=jax.ShapeDtypeStruct((n, D_MODEL), F32),
        scratch_shapes=[pltpu.VMEM((SSM_NJ, tm, LANES), F32)],
        compiler_params=_cp(("arbitrary",)), name="mixer_out",
    )(o, s, ga, gb, x, w["wo"], w["wv"], w["wg"], w["wout"])


def _mlp_kernel(h_ref, g_ref, wup_ref, wdn_ref, y_ref):
    h = h_ref[...]
    hn = _rms(h, g_ref[...]).astype(BF16)
    a = jnp.maximum(_dot(hn, wup_ref[...]), 0.0)
    y_ref[...] = h + _dot((a * a).astype(BF16), wdn_ref[...])


def _mlp(h, w, tm):
    n = h.shape[0]
    row = lambda i: (i, 0)
    const = lambda i: (0, 0)
    act = pl.BlockSpec((tm, D_MODEL), row)
    return pl.pallas_call(
        _mlp_kernel, grid=(n // tm,),
        in_specs=[act, pl.BlockSpec((1, D_MODEL), const),
                  pl.BlockSpec((D_MODEL, D_FF), const, pipeline_mode=pl.Buffered(1)),
                  pl.BlockSpec((D_FF, D_MODEL), const, pipeline_mode=pl.Buffered(1))],
        out_specs=act, out_shape=jax.ShapeDtypeStruct((n, D_MODEL), F32),
        compiler_params=_cp(("arbitrary",)), name="mlp",
    )(h, w["gmlp"], w["wup"], w["wdn"])


def _cos_sin(pos):
    half = QK_ROPE // 2
    inv = ROPE_THETA ** (-jnp.arange(half, dtype=F32) / half)
    ang = pos.astype(F32)[:, None] * inv[None, :]
    return jnp.cos(ang), jnp.sin(ang)


def _rope_tables(pos=None, length=None):
    if pos is not None:
        cos, sin = _cos_sin(pos)
    else:
        half = QK_ROPE // 2
        ch, sh = _cos_sin(jnp.arange(length // CHUNK, dtype=jnp.int32) * CHUNK)
        cl, sl = _cos_sin(jnp.arange(CHUNK, dtype=jnp.int32))
        cos = (ch[:, None, :] * cl[None] - sh[:, None, :] * sl[None]).reshape(length, half)
        sin = (sh[:, None, :] * cl[None] + ch[:, None, :] * sl[None]).reshape(length, half)
    z = jnp.zeros_like(cos)
    pad = jnp.zeros((cos.shape[0], LANES - QK_ROPE), F32)
    ct = jnp.concatenate([cos, cos, pad], axis=1)
    s1 = jnp.concatenate([-sin, z, pad], axis=1)
    s2 = jnp.concatenate([z, sin, pad], axis=1)
    return ct, s1, s2


def _pad_lanes(a, width):
    return jnp.pad(a, [(0, 0)] * (a.ndim - 1) + [(0, width - a.shape[-1])])


def _layer_weights(norm_mix, w_in, q_lora_norm, w_uq, q_nope_norm, q_rope_norm, kv_lora_norm, k_rope_norm,
                   w_ukv, k_nope_norm, w_o_attn, w_glu_v, w_glu_g, w_out, norm_mlp, w_mlp_up, w_mlp_down):
    o_kv = Q_LORA
    o_kr = o_kv + KV_LORA
    o_ssm = o_kr + QK_ROPE
    bf = lambda a: a.astype(BF16)
    r2 = lambda a: a.reshape(1, -1).astype(F32)
    return {
        "nmix": r2(norm_mix),
        "win": bf(jnp.concatenate([w_in[:, :o_kr], w_in[:, o_ssm:],
                                   _pad_lanes(w_in[:, o_kr:o_ssm], LANES)], axis=1)),
        "gq": r2(q_lora_norm),
        "wuqn": bf(w_uq[:, :, :QK_NOPE].reshape(Q_LORA, N_HEADS * QK_NOPE)),
        "wuqr": bf(_pad_lanes(w_uq[:, :, QK_NOPE:], LANES).reshape(Q_LORA, N_HEADS * LANES)),
        "gqn": r2(q_nope_norm), "gqr": _pad_lanes(r2(q_rope_norm), LANES),
        "gkv": r2(kv_lora_norm), "gkr": _pad_lanes(r2(k_rope_norm), LANES),
        "wuk": bf(w_ukv[:, :, :QK_NOPE].reshape(KV_LORA, N_HEADS * QK_NOPE)),
        "wukg": bf((w_ukv[:, :, :QK_NOPE] * k_nope_norm.astype(F32)).reshape(KV_LORA, N_HEADS * QK_NOPE)),
        "wuv": bf(w_ukv[:, :, QK_NOPE:].reshape(KV_LORA, N_HEADS * V_DIM)),
        "gkn": r2(k_nope_norm),
        "wo": bf(w_o_attn.reshape(N_HEADS * V_DIM, D_MODEL)),
        "wv": bf(w_glu_v), "wg": bf(w_glu_g), "wout": bf(w_out),
        "gmlp": r2(norm_mlp), "wup": bf(w_mlp_up), "wdn": bf(w_mlp_down),
    }


def kernel(x_prompt, x_sample, cache_latent, cache_krope, cache_meta_latent, cache_meta_krope, state_ssm_re, state_ssm_im, meta_tokens, norm_mix, w_in, q_lora_norm, w_uq, q_nope_norm, q_rope_norm, kv_lora_norm, k_rope_norm, w_ukv, k_nope_norm, w_o_attn, ssm_a_re, ssm_a_im, ssm_log_dt, ssm_b_re, ssm_b_im, ssm_c_re, ssm_c_im, ssm_d, w_glu_v, w_glu_g, w_out, norm_mlp, w_mlp_up, w_mlp_down):
    bsz_p, seq_p = x_prompt.shape[0], x_prompt.shape[1]
    bsz_s, seq_s = x_sample.shape[0], x_sample.shape[1]
    past = cache_latent.shape[2]
    depth = w_in.shape[0]
    assert depth == 1, "single-layer step"
    l = 0
    w = _layer_weights(norm_mix[l], w_in[l], q_lora_norm[l], w_uq[l], q_nope_norm[l], q_rope_norm[l],
                       kv_lora_norm[l], k_rope_norm[l], w_ukv[l], k_nope_norm[l], w_o_attn[l], w_glu_v[l],
                       w_glu_g[l], w_out[l], norm_mlp[l], w_mlp_up[l], w_mlp_down[l])
    sw = _ssm_weights(ssm_a_re[l].astype(F32), ssm_a_im[l].astype(F32), ssm_log_dt[l].astype(F32),
                      ssm_b_re[l].astype(F32), ssm_b_im[l].astype(F32), ssm_c_re[l].astype(F32),
                      ssm_c_im[l].astype(F32), ssm_d[l].astype(F32))
    sq = 2 * SSM_SW

    tabs_m = _rope_tables(jnp.arange(N_META, dtype=jnp.int32) - N_META)
    (_, mk, mv, mckv, mkr, mu, _, _) = _front(meta_tokens.astype(F32), tabs_m, 1, w, N_META)
    h_meta = _ssm_state(mu, sw, N_META // SSM_T)

    n_p = bsz_p * seq_p
    xp = x_prompt.reshape(n_p, D_MODEL)
    tabs_p = _rope_tables(length=seq_p)
    q, k, v, ckv_p, kr_p, u_p, ga, gb = _front(xp, tabs_p, seq_p // TM_PROMPT, w, TM_PROMPT, kr_seq=seq_p)
    o_p = _attn_prompt(q, k, v, mk, mv, bsz_p, seq_p)
    h0_p = jnp.broadcast_to(h_meta, (SSM_NJ, bsz_p, sq))
    s_p, hfin_p = _ssm_prompt(u_p, sw, h0_p, bsz_p, seq_p // SSM_T, SSM_RT)
    h1_p = _mixer(o_p, s_p, ga, gb, xp, w, TM_PROMPT)
    y_p = _mlp(h1_p, w, TM_PROMPT)

    n_s = bsz_s * seq_s
    xs = x_sample.reshape(n_s, D_MODEL)
    pos_s = past + jnp.arange(seq_s, dtype=jnp.int32)
    tabs_s = tuple(jnp.tile(t, (bsz_s, 1)) for t in _rope_tables(pos_s))
    q_s, k_s, _, ckv_s, kr_s, u_s, ga_s, gb_s = _front(xs, tabs_s, n_s // TM_SAMPLE, w, TM_SAMPLE)
    o_s = _attn_sample(q_s, k_s, ckv_s, cache_latent[l].astype(F32), cache_krope[l].astype(F32),
                       cache_meta_latent[l].astype(F32), cache_meta_krope[l].astype(F32), w, bsz_s, seq_s, past)

    def to_blocks(st):
        return st.astype(F32).reshape(bsz_s, SSM_NJ, SSM_SW).transpose(1, 0, 2)

    h0_s = jnp.concatenate([to_blocks(state_ssm_re[l]), to_blocks(state_ssm_im[l])], axis=2)
    s_s, hfin_s = _ssm_sample(u_s, sw, h0_s, bsz_s, seq_s // SSM_T)
    h1_s = _mixer(o_s, s_s, ga_s, gb_s, xs, w, TM_SAMPLE)
    y_s = _mlp(h1_s, w, TM_SAMPLE)

    def from_blocks(hf, nb):
        re = hf[:, :, :SSM_SW].transpose(1, 0, 2).reshape(1, nb, N_GROUPS, SSM_STATE)
        im = hf[:, :, SSM_SW:].transpose(1, 0, 2).reshape(1, nb, N_GROUPS, SSM_STATE)
        return re, im

    sre_p, sim_p = from_blocks(hfin_p, bsz_p)
    sre_s, sim_s = from_blocks(hfin_s, bsz_s)
    return (y_p.reshape(bsz_p, seq_p, D_MODEL), y_s.reshape(bsz_s, seq_s, D_MODEL),
            ckv_p.reshape(1, bsz_p, seq_p, KV_LORA), jnp.swapaxes(kr_p, 1, 2)[None],
            jnp.broadcast_to(mckv[None, None], (1, bsz_p, N_META, KV_LORA)),
            jnp.broadcast_to(mkr[None, None], (1, bsz_p, N_META, QK_ROPE)),
            sre_p, sim_p,
            ckv_s.reshape(1, bsz_s, seq_s, KV_LORA), kr_s.reshape(1, bsz_s, seq_s, QK_ROPE),
            sre_s, sim_s)
```

```python
import functools
import math

import jax
import jax.numpy as jnp
from jax import lax
from jax.experimental import pallas as pl
from jax.experimental.pallas import tpu as pltpu

F32 = jnp.float32
BF16 = jnp.bfloat16

D_MODEL = 1024
CHUNK = 64
N_META = 16
N_HEADS = 8
QK_NOPE = 128
QK_ROPE = 64
V_DIM = 128
QK_DIM = QK_NOPE + QK_ROPE
Q_LORA = 384
KV_LORA = 256
SSM_GROUP = 16
N_GROUPS = D_MODEL // SSM_GROUP
SSM_STATE = 64
D_FF = 4 * D_MODEL
ROPE_THETA = 10000.0
EPS = 1e-6
ATTN_SCALE = QK_DIM ** -0.5
Q_SCALE = ATTN_SCALE * math.log2(math.e)
NEG_INF = -1e30

LANES = 128
HEAD_PAD = 2 * LANES
SSM_T = 8
SSM_NJ = D_MODEL // LANES
SSM_GPB = LANES // SSM_GROUP
SSM_SW = SSM_GPB * SSM_STATE
VMEM_LIMIT = 56 * 1024 * 1024
TM_PROMPT = 512
TM_SAMPLE = 512
SSM_RT = 256


def _cp(sem):
    return pltpu.CompilerParams(dimension_semantics=sem, vmem_limit_bytes=VMEM_LIMIT)


def _dot(a, b):
    return jnp.dot(a, b, preferred_element_type=F32)


def _dot_nt(a, b):
    return lax.dot_general(a, b, (((1,), (1,)), ((), ())), preferred_element_type=F32)


def _rms(x, g, n=None):
    n = x.shape[-1] if n is None else n
    ms = jnp.sum(x * x, axis=-1, keepdims=True) * (1.0 / n)
    return x * lax.rsqrt(ms + EPS) * g


def _rope128(b, ct, s1, s2):
    return b * ct + pltpu.roll(b, LANES - QK_ROPE // 2, 1) * s1 + pltpu.roll(b, QK_ROPE // 2, 1) * s2


IN_Q = 0
IN_KV = IN_Q + Q_LORA
IN_U = IN_KV + KV_LORA
IN_GA = IN_U + D_MODEL
IN_GB = IN_GA + D_MODEL
IN_KR = IN_GB + D_MODEL
IN_END = IN_KR + LANES


def _front_kernel(x_ref, ct_ref, s1_ref, s2_ref, nmix_ref, win_ref, gq_ref, wuqn_ref, wuqr_ref, gqn_ref,
                  gqr_ref, gkv_ref, gkr_ref, wuk_ref, wuv_ref, gkn_ref,
                  q_ref, k_ref, v_ref, ckv_ref, kr_ref, u_ref, ga_ref, gb_ref, u_sc, *, kr_transposed):
    x = x_ref[...]
    xn = _rms(x, nmix_ref[...]).astype(BF16)
    ct, s1, s2 = ct_ref[...], s1_ref[...], s2_ref[...]


    c_q = _rms(_dot(xn, win_ref[:, IN_Q:IN_KV]), gq_ref[...]).astype(BF16)
    qn = _dot(c_q, wuqn_ref[...])
    qr = _dot(c_q, wuqr_ref[...])
    for h in range(N_HEADS):
        sl = slice(h * LANES, (h + 1) * LANES)
        q_ref[:, h * HEAD_PAD:h * HEAD_PAD + LANES] = _rms(qn[:, sl], gqn_ref[...]).astype(BF16)
        r = _rope128(_rms(qr[:, sl], gqr_ref[...], QK_ROPE), ct, s1, s2)
        q_ref[:, h * HEAD_PAD + LANES:(h + 1) * HEAD_PAD] = r.astype(BF16)

    c_kv = _rms(_dot(xn, win_ref[:, IN_KV:IN_U]), gkv_ref[...])
    ckv_ref[...] = c_kv
    kr = _rms(_dot(xn, win_ref[:, IN_KR:IN_END]), gkr_ref[...], QK_ROPE)
    kr = _rope128(kr, ct, s1, s2)
    kr_ref[...] = kr.T[:QK_ROPE, :] if kr_transposed else kr[:, :QK_ROPE]
    kr16 = kr.astype(BF16)
    c16 = c_kv.astype(BF16)
    kn = _dot(c16, wuk_ref[...])
    for h in range(N_HEADS):
        sl = slice(h * LANES, (h + 1) * LANES)
        k_ref[:, h * HEAD_PAD:h * HEAD_PAD + LANES] = _rms(kn[:, sl], gkn_ref[...]).astype(BF16)
        k_ref[:, h * HEAD_PAD + LANES:(h + 1) * HEAD_PAD] = kr16
    v_ref[...] = _dot(c16, wuv_ref[...]).astype(BF16)

    u = _dot(xn, win_ref[:, IN_U:IN_GA])
    nchunk = u_sc.shape[1] // SSM_T
    for j in range(SSM_NJ):
        u_sc[j] = u[:, j * LANES:(j + 1) * LANES]
        for t in range(SSM_T):
            u_ref[j, :, t * LANES:(t + 1) * LANES] = u_sc[j, pl.ds(t, nchunk, stride=SSM_T), :]
    ga_ref[...] = _dot(xn, win_ref[:, IN_GA:IN_GB])
    gb_ref[...] = _dot(xn, win_ref[:, IN_GB:IN_KR])


def _front(x, tabs, tab_blocks, w, tm, kr_seq=None):
    n = x.shape[0]
    nt = n // tm
    ct, s1, s2 = tabs

    def row(i):
        return (i, 0)

    def tab(i):
        return (i % tab_blocks, 0)

    def const(i):
        return (0, 0)

    def full(a):
        return pl.BlockSpec(a.shape, const, pipeline_mode=pl.Buffered(1))

    weights = [w["nmix"], w["win"], w["gq"], w["wuqn"], w["wuqr"], w["gqn"], w["gqr"], w["gkv"], w["gkr"],
               w["wuk"], w["wuv"], w["gkn"]]
    wide = N_HEADS * LANES
    qk_wide = N_HEADS * HEAD_PAD
    out_shape = [
        jax.ShapeDtypeStruct((n, qk_wide), BF16),
        jax.ShapeDtypeStruct((n, qk_wide), BF16),
        jax.ShapeDtypeStruct((n, wide), BF16),
        jax.ShapeDtypeStruct((n, KV_LORA), F32),
        (jax.ShapeDtypeStruct((n, QK_ROPE), F32) if kr_seq is None
         else jax.ShapeDtypeStruct((n // kr_seq, QK_ROPE, kr_seq), F32)),
        jax.ShapeDtypeStruct((SSM_NJ, n // SSM_T, SSM_T * LANES), F32),
        jax.ShapeDtypeStruct((n, D_MODEL), F32),
        jax.ShapeDtypeStruct((n, D_MODEL), F32),
    ]
    out_specs = [
        pl.BlockSpec((tm, qk_wide), row), pl.BlockSpec((tm, qk_wide), row),
        pl.BlockSpec((tm, wide), row), pl.BlockSpec((tm, KV_LORA), row),
        (pl.BlockSpec((tm, QK_ROPE), row) if kr_seq is None else
         pl.BlockSpec((None, QK_ROPE, tm), lambda i: (i // (kr_seq // tm), 0, i % (kr_seq // tm)))),
        pl.BlockSpec((SSM_NJ, tm // SSM_T, SSM_T * LANES), lambda i: (0, i, 0)),
        pl.BlockSpec((tm, D_MODEL), row), pl.BlockSpec((tm, D_MODEL), row),
    ]
    in_specs = ([pl.BlockSpec((tm, D_MODEL), row)] + [pl.BlockSpec((tm, LANES), tab)] * 3
                + [full(a) for a in weights])
    return pl.pallas_call(
        functools.partial(_front_kernel, kr_transposed=kr_seq is not None), grid=(nt,), in_specs=in_specs,
        out_specs=out_specs, out_shape=out_shape,
        scratch_shapes=[pltpu.VMEM((SSM_NJ, tm, LANES), F32)],
        compiler_params=_cp(("arbitrary",)), name="front",
    )(x, ct, s1, s2, *weights)


ATT_T = 512


def _with_ones(v):
    return jnp.concatenate([v, jnp.ones_like(v)], axis=1)


def _softmax_step(s, v, m_ref, acc_ref):
    m_prev = m_ref[...]
    m_new = jnp.maximum(m_prev, jnp.max(s, axis=-1, keepdims=True))
    alpha = jnp.exp2(m_prev - m_new)
    p = jnp.exp2(s - jnp.tile(m_new, (1, s.shape[1] // LANES)))
    acc_ref[...] = jnp.tile(alpha, (1, 2)) * acc_ref[...] + _dot(p.astype(BF16), _with_ones(v))
    m_ref[...] = m_new


ATT_HG = 4


def _attn_prompt_kernel(q_ref, k_ref, v_ref, mk_ref, mv_ref, o_ref, a_sc, b_sc, m_ref, acc_ref):
    step = pl.program_id(2)
    npairs = step
    heads = range(ATT_HG)

    def key_rows(kt):
        return pl.ds(pl.multiple_of(kt * ATT_T, ATT_T), ATT_T)

    def q_rows(slot):
        return slice(slot * ATT_T, (slot + 1) * ATT_T)

    def qk_lanes(g):
        return slice(g * HEAD_PAD, (g + 1) * HEAD_PAD)

    def v_lanes(g):
        return slice(g * V_DIM, (g + 1) * V_DIM)

    def scores(slot, g, kt):
        return _dot_nt(q_ref[q_rows(slot), qk_lanes(g)], k_ref[key_rows(kt), qk_lanes(g)])

    def consume(slot, g, s, kt, diagonal):
        if diagonal:
            qc = lax.broadcasted_iota(jnp.int32, (ATT_T, ATT_T), 0) // CHUNK
            kc = lax.broadcasted_iota(jnp.int32, (ATT_T, ATT_T), 1) // CHUNK
            s = jnp.where(kc <= qc, s, NEG_INF)
        _softmax_step(s, v_ref[key_rows(kt), v_lanes(g)], m_ref.at[slot, g], acc_ref.at[slot, g])

    def head(slot):
        for g in heads:
            a_sc[slot, g] = scores(slot, g, 0)
        for g in heads:
            s = _dot_nt(q_ref[q_rows(slot), qk_lanes(g)], mk_ref[:, qk_lanes(g)])
            m0 = jnp.max(s, axis=-1, keepdims=True)
            p = jnp.exp2(s - m0)
            m_ref[slot, g] = jnp.broadcast_to(m0, (ATT_T, LANES))
            acc_ref[slot, g] = _dot(p.astype(BF16), _with_ones(mv_ref[:, v_lanes(g)]))

    def full_pairs(slot):
        def pair(pi, carry):
            kt = 2 * pi
            for g in heads:
                b_sc[g] = scores(slot, g, kt + 1)
                consume(slot, g, a_sc[slot, g], kt, False)
            for g in heads:
                a_sc[slot, g] = scores(slot, g, kt + 2)
                consume(slot, g, b_sc[g], kt + 1, False)
            return carry

        lax.fori_loop(0, npairs, pair, 0)

    def tail(slot):
        i = 2 * step + slot
        if slot == 1:
            for g in heads:
                b_sc[g] = scores(slot, g, i)
                consume(slot, g, a_sc[slot, g], i - 1, False)
            for g in heads:
                consume(slot, g, b_sc[g], i, True)
        else:
            for g in heads:
                consume(slot, g, a_sc[slot, g], i, True)
        for g in heads:
            o_ref[q_rows(slot), v_lanes(g)] = (acc_ref[slot, g, :, :V_DIM]
                                               / acc_ref[slot, g, :, V_DIM:]).astype(BF16)

    head(0)
    full_pairs(0)
    tail(0)
    head(1)
    full_pairs(1)
    tail(1)


def _attn_prompt(q, k, v, mk, mv, bsz, seq):
    nstep = seq // (2 * ATT_T)
    n = bsz * seq
    qmap = lambda b, h, i: (b * nstep + i, h)
    return pl.pallas_call(
        _attn_prompt_kernel, grid=(bsz, N_HEADS // ATT_HG, nstep),
        in_specs=[pl.BlockSpec((2 * ATT_T, ATT_HG * HEAD_PAD), qmap),
                  pl.BlockSpec((seq, ATT_HG * HEAD_PAD), lambda b, h, i: (b, h)),
                  pl.BlockSpec((seq, ATT_HG * V_DIM), lambda b, h, i: (b, h)),
                  pl.BlockSpec((N_META, ATT_HG * HEAD_PAD), lambda b, h, i: (0, h)),
                  pl.BlockSpec((N_META, ATT_HG * V_DIM), lambda b, h, i: (0, h))],
        out_specs=pl.BlockSpec((2 * ATT_T, ATT_HG * V_DIM), qmap),
        out_shape=jax.ShapeDtypeStruct((n, N_HEADS * V_DIM), BF16),
        scratch_shapes=[pltpu.VMEM((2, ATT_HG, ATT_T, ATT_T), F32), pltpu.VMEM((ATT_HG, ATT_T, ATT_T), F32),
                        pltpu.VMEM((2, ATT_HG, ATT_T, LANES), F32),
                        pltpu.VMEM((2, ATT_HG, ATT_T, 2 * V_DIM), F32)],
        compiler_params=_cp(("arbitrary", "arbitrary", "arbitrary")), name="attn_prompt",
    )(q, k, v, mk, mv)


SAMPLE_KEY_CHUNK = 512


def _attn_sample_kernel(q_ref, cm_ref, cn_ref, cc_ref, krm_ref, krn_ref, krc_ref, wuk_ref, wukg_ref, wuv_ref,
                        hsum_ref, o_ref, c_sc, kr_sc, st_sc, pt_sc, *, seq, past):
    n_small = N_META + seq
    n_keys = n_small + past

    c_sc[:, KV_LORA:] = jnp.ones((n_keys, LANES), BF16)
    c_sc[0:N_META, :KV_LORA] = cm_ref[...].astype(BF16)
    c_sc[N_META:n_small, :KV_LORA] = cn_ref[...].astype(BF16)
    c_sc[n_small:, :KV_LORA] = cc_ref[...].astype(BF16)
    kr_sc[:, QK_ROPE:] = jnp.zeros((n_small, LANES - QK_ROPE), BF16)
    kr_sc[0:N_META, :QK_ROPE] = krm_ref[...].astype(BF16)
    kr_sc[N_META:n_small, :] = krn_ref[...]

    qa, qr = [], []
    for h in range(N_HEADS):
        qn_h = q_ref[:, h * HEAD_PAD:h * HEAD_PAD + LANES]
        qa.append(_dot_nt(qn_h, wukg_ref[:, h * LANES:(h + 1) * LANES]))
        qr.append(q_ref[:, h * HEAD_PAD + LANES:(h + 1) * HEAD_PAD])
    qa = jnp.concatenate(qa, axis=0).astype(BF16)
    qr = jnp.concatenate(qr, axis=0)

    bounds = [(0, n_small)] + [(n_small + i, n_small + i + SAMPLE_KEY_CHUNK)
                               for i in range(0, past, SAMPLE_KEY_CHUNK)]
    m = None
    for lo, hi in bounds:
        c16 = c_sc[lo:hi, :KV_LORA]
        kn = _dot(c16, wuk_ref[...])
        ssq = _dot((kn * kn).astype(BF16), hsum_ref[...])
        r = lax.rsqrt(ssq * (1.0 / QK_NOPE) + EPS)
        if lo == 0:
            s_rope = _dot_nt(kr_sc[...], qr)
        else:
            kr_t = krc_ref[:, lo - n_small:hi - n_small].astype(BF16)
            s_rope = lax.dot_general(kr_t, qr[:, :QK_ROPE], (((0,), (1,)), ((), ())),
                                     preferred_element_type=F32)
        st = _dot_nt(c16, qa) * r + s_rope
        st_sc[lo:hi, :] = st
        cm = jnp.max(st, axis=0, keepdims=True)
        m = cm if m is None else jnp.maximum(m, cm)

    for lo, hi in bounds:
        pt_sc[lo:hi, :] = jnp.exp2(st_sc[lo:hi, :] - m).astype(BF16)
    pc = lax.dot_general(pt_sc[...], c_sc[...], (((0,), (0,)), ((), ())), preferred_element_type=F32)
    pcn = (pc[:, :KV_LORA] / jnp.tile(pc[:, KV_LORA:], (1, KV_LORA // LANES))).astype(BF16)
    for h in range(N_HEADS):
        o_ref[:, h * V_DIM:(h + 1) * V_DIM] = _dot(pcn[h * seq:(h + 1) * seq, :],
                                                   wuv_ref[:, h * V_DIM:(h + 1) * V_DIM]).astype(BF16)


def _attn_sample(q, k_new, c_new, cache_c, cache_kr, meta_c, meta_kr, w, bsz, seq, past):
    n_keys = N_META + seq + past
    hq = N_HEADS * seq
    hsum = jnp.repeat(jnp.repeat(jnp.eye(N_HEADS, dtype=BF16), QK_NOPE, axis=0), seq, axis=1)
    row = lambda b: (b, 0)
    b3 = lambda b: (b, 0, 0)
    const = lambda b: (0, 0)
    kern = functools.partial(_attn_sample_kernel, seq=seq, past=past)
    return pl.pallas_call(
        kern, grid=(bsz,),
        in_specs=[pl.BlockSpec((seq, N_HEADS * HEAD_PAD), row),
                  pl.BlockSpec((None, N_META, KV_LORA), b3), pl.BlockSpec((seq, KV_LORA), row),
                  pl.BlockSpec((None, past, KV_LORA), b3),
                  pl.BlockSpec((None, N_META, QK_ROPE), b3), pl.BlockSpec((seq, LANES), lambda b: (b, 1)),
                  pl.BlockSpec((None, QK_ROPE, past), b3),
                  pl.BlockSpec(w["wuk"].shape, const), pl.BlockSpec(w["wukg"].shape, const),
                  pl.BlockSpec(w["wuv"].shape, const), pl.BlockSpec(hsum.shape, const)],
        out_specs=pl.BlockSpec((seq, N_HEADS * V_DIM), row),
        out_shape=jax.ShapeDtypeStruct((bsz * seq, N_HEADS * V_DIM), BF16),
        scratch_shapes=[pltpu.VMEM((n_keys, KV_LORA + LANES), BF16), pltpu.VMEM((N_META + seq, LANES), BF16),
                        pltpu.VMEM((n_keys, hq), F32), pltpu.VMEM((n_keys, hq), BF16)],
        compiler_params=_cp(("arbitrary",)), name="attn_sample",
    )(q, meta_c, c_new, cache_c, meta_kr, k_new, jnp.swapaxes(cache_kr, 1, 2), w["wuk"], w["wukg"], w["wuv"],
      hsum)


MXU_DIM = 256


def _dot_causal(u16, m_ref):
    nblk = m_ref.shape[0] // MXU_DIM
    return jnp.concatenate(
        [_dot(u16[:, :(cb + 1) * MXU_DIM], m_ref[:(cb + 1) * MXU_DIM, cb * MXU_DIM:(cb + 1) * MXU_DIM])
         for cb in range(nblk)], axis=1)


SSM_PACK = 4
SSM_SC = SSM_SW // LANES
SUBLANES = 2 * SSM_PACK


def _ssm_prompt_kernel(u_ref, m_ref, p_ref, q_ref, lam_ref, d_ref, h0_ref, s_ref, hfin_ref, s_sc, hp_sc, h_sc,
                       *, nb, rt):
    t = pl.program_id(1)

    @pl.when(t == 0)
    def _():
        h_sc[...] = h0_ref[...]

    for b in range(nb):
        sb = _dot(u_ref[b].astype(BF16), p_ref[...])
        for c in range(SSM_SC):
            s_sc[c, pl.ds(b, rt, stride=SUBLANES), :] = sb[:, c * LANES:(c + 1) * LANES]
            s_sc[c, pl.ds(SSM_PACK + b, rt, stride=SUBLANES), :] = sb[:, SSM_SW + c * LANES:SSM_SW + (c + 1) * LANES]

    lam = lam_ref[...]
    im_rows = lax.broadcasted_iota(jnp.int32, (SUBLANES, LANES), 0) >= SSM_PACK
    coef = []
    for c in range(SSM_SC):
        lr = jnp.broadcast_to(lam[:, c * LANES:(c + 1) * LANES], (SUBLANES, LANES))
        li = jnp.broadcast_to(lam[:, SSM_SW + c * LANES:SSM_SW + (c + 1) * LANES], (SUBLANES, LANES))
        coef.append((lr, jnp.where(im_rows, li, -li)))

    hs = [h_sc[c] for c in range(SSM_SC)]
    for r in range(rt):
        rows = slice(r * SUBLANES, (r + 1) * SUBLANES)
        for c in range(SSM_SC):
            hp_sc[c, rows, :] = hs[c]
            a, bb = coef[c]
            hs[c] = a * hs[c] + bb * pltpu.roll(hs[c], SSM_PACK, 0) + s_sc[c, rows, :]
    for c in range(SSM_SC):
        h_sc[c] = hs[c]

    for b in range(nb):
        ub = u_ref[b]
        hb = jnp.concatenate([hp_sc[c, pl.ds(b, rt, stride=SUBLANES), :] for c in range(SSM_SC)]
                             + [hp_sc[c, pl.ds(SSM_PACK + b, rt, stride=SUBLANES), :] for c in range(SSM_SC)],
                             axis=1)
        y = _dot_causal(ub.astype(BF16), m_ref) + _dot(hb.astype(BF16), q_ref[...]) + ub * d_ref[...]
        s_ref[b] = jax.nn.gelu(y).astype(BF16)

    @pl.when(t == pl.num_programs(1) - 1)
    def _():
        hfin_ref[...] = h_sc[...]


def _pack_state(h):
    nj, nb, _ = h.shape
    h = jnp.pad(h, ((0, 0), (0, SSM_PACK - nb), (0, 0)))
    re = h[:, :, :SSM_SW].reshape(nj, SSM_PACK, SSM_SC, LANES)
    im = h[:, :, SSM_SW:].reshape(nj, SSM_PACK, SSM_SC, LANES)
    return jnp.concatenate([re, im], axis=1).transpose(0, 2, 1, 3)


def _unpack_state(hp, nb):
    nj = hp.shape[0]
    x = hp.transpose(0, 2, 1, 3).reshape(nj, SUBLANES, SSM_SW)
    return jnp.concatenate([x[:, :nb], x[:, SSM_PACK:SSM_PACK + nb]], axis=2)


def _ssm_prompt(u, sw, h0, nb, lc, rt):
    assert nb == SSM_PACK and lc % rt == 0
    tw = SSM_T * LANES
    u4 = u.reshape(SSM_NJ, nb, lc, tw)
    sq = 2 * SSM_SW
    kern = functools.partial(_ssm_prompt_kernel, nb=nb, rt=rt)
    st_spec = pl.BlockSpec((None, SSM_SC, SUBLANES, LANES), lambda j, t: (j, 0, 0, 0))
    s, hfin = pl.pallas_call(
        kern, grid=(SSM_NJ, lc // rt),
        in_specs=[pl.BlockSpec((None, nb, rt, tw), lambda j, t: (j, 0, t, 0)),
                  pl.BlockSpec((None, tw, tw), lambda j, t: (j, 0, 0)),
                  pl.BlockSpec((None, tw, sq), lambda j, t: (j, 0, 0)),
                  pl.BlockSpec((None, sq, tw), lambda j, t: (j, 0, 0)),
                  pl.BlockSpec((None, 1, sq), lambda j, t: (j, 0, 0)),
                  pl.BlockSpec((None, 1, tw), lambda j, t: (j, 0, 0)),
                  st_spec],
        out_specs=[pl.BlockSpec((None, nb, rt, tw), lambda j, t: (j, 0, t, 0)), st_spec],
        out_shape=[jax.ShapeDtypeStruct((SSM_NJ, nb, lc, tw), BF16),
                   jax.ShapeDtypeStruct((SSM_NJ, SSM_SC, SUBLANES, LANES), F32)],
        scratch_shapes=[pltpu.VMEM((SSM_SC, rt * SUBLANES, LANES), F32),
                        pltpu.VMEM((SSM_SC, rt * SUBLANES, LANES), F32),
                        pltpu.VMEM((SSM_SC, SUBLANES, LANES), F32)],
        compiler_params=_cp(("arbitrary", "arbitrary")), name="ssm_prompt",
    )(u4, sw["M"], sw["P"], sw["Q"], sw["lam"], sw["D"], _pack_state(h0))
    return s.reshape(SSM_NJ, nb * lc, tw), _unpack_state(hfin, nb)


def _ssm_state_kernel(u_ref, p_ref, lam_ref, h_ref):
    s = _dot(u_ref[...].astype(BF16), p_ref[...])
    lam = lam_ref[...]
    lr, li = lam[:, :SSM_SW], lam[:, SSM_SW:]
    h = s[0:1, :]
    for r in range(1, s.shape[0]):
        hre, him = h[:, :SSM_SW], h[:, SSM_SW:]
        h = jnp.concatenate([lr * hre - li * him, lr * him + li * hre], axis=1) + s[r:r + 1, :]
    h_ref[...] = h


def _ssm_state(u, sw, lc):
    tw = SSM_T * LANES
    sq = 2 * SSM_SW
    blk = lambda j: (j, 0, 0)
    return pl.pallas_call(
        _ssm_state_kernel, grid=(SSM_NJ,),
        in_specs=[pl.BlockSpec((None, lc, tw), blk), pl.BlockSpec((None, tw, sq), blk),
                  pl.BlockSpec((None, 1, sq), blk)],
        out_specs=pl.BlockSpec((None, 1, sq), blk),
        out_shape=jax.ShapeDtypeStruct((SSM_NJ, 1, sq), F32),
        compiler_params=_cp(("arbitrary",)), name="ssm_state",
    )(u, sw["P"], sw["lam"])


def _ssm_sample_kernel(u_ref, m_ref, p_ref, q_ref, lam_ref, d_ref, h0_ref, s_ref, hfin_ref, s_sc,
                       *, nb, lc):
    u = u_ref[...]
    u16 = u.astype(BF16)
    s_sc[...] = _dot(u16, p_ref[...])
    lam = lam_ref[...]
    lr, li = lam[:, :SSM_SW], lam[:, SSM_SW:]
    h = h0_ref[...]
    for r in range(lc):
        rows = slice(r * nb, (r + 1) * nb)
        s = s_sc[rows, :]
        s_sc[rows, :] = h
        hre, him = h[:, :SSM_SW], h[:, SSM_SW:]
        h = jnp.concatenate([lr * hre - li * him + s[:, :SSM_SW], lr * him + li * hre + s[:, SSM_SW:]], axis=1)
    hfin_ref[...] = h
    y = _dot_causal(u16, m_ref) + _dot(s_sc[...].astype(BF16), q_ref[...]) + u * d_ref[...]
    s_ref[...] = jax.nn.gelu(y).astype(BF16)


def _ssm_sample(u, sw, h0, nb, lc):
    tw = SSM_T * LANES
    sq = 2 * SSM_SW
    u2 = u.reshape(SSM_NJ, nb, lc, tw).transpose(0, 2, 1, 3).reshape(SSM_NJ, lc * nb, tw)
    kern = functools.partial(_ssm_sample_kernel, nb=nb, lc=lc)
    blk = lambda j: (j, 0, 0)
    s, hfin = pl.pallas_call(
        kern, grid=(SSM_NJ,),
        in_specs=[pl.BlockSpec((None, lc * nb, tw), blk), pl.BlockSpec((None, tw, tw), blk),
                  pl.BlockSpec((None, tw, sq), blk), pl.BlockSpec((None, sq, tw), blk),
                  pl.BlockSpec((None, 1, sq), blk), pl.BlockSpec((None, 1, tw), blk),
                  pl.BlockSpec((None, nb, sq), blk)],
        out_specs=[pl.BlockSpec((None, lc * nb, tw), blk), pl.BlockSpec((None, nb, sq), blk)],
        out_shape=[jax.ShapeDtypeStruct((SSM_NJ, lc * nb, tw), BF16),
                   jax.ShapeDtypeStruct((SSM_NJ, nb, sq), F32)],
        scratch_shapes=[pltpu.VMEM((lc * nb, sq), F32)],
        compiler_params=_cp(("arbitrary",)), name="ssm_sample",
    )(u2, sw["M"], sw["P"], sw["Q"], sw["lam"], sw["D"], h0)
    s = s.reshape(SSM_NJ, lc, nb, tw).transpose(0, 2, 1, 3).reshape(SSM_NJ, nb * lc, tw)
    return s, hfin


def _ssm_fold_kernel(wre_ref, wim_ref, cre_ref, cim_ref, plr_ref, pli_ref, rep_ref, m_ref, p_ref, q_ref):
    def split(a, terms):
        out = []
        for _ in range(terms):
            t = a.astype(BF16)
            out.append(t)
            a = a - t.astype(F32)
        return out

    def repeat_lanes(a, rep):
        return sum(_dot(t, rep) for t in split(a, 3))

    def dot_nt_hi_lo(a, b_parts):
        a_hi, a_lo = split(a, 2)
        b_hi, b_lo = b_parts
        return _dot_nt(a_hi, b_hi) + (_dot_nt(a_hi, b_lo) + _dot_nt(a_lo, b_hi))

    same_group = (lax.broadcasted_iota(jnp.int32, (LANES, SSM_SW), 0) // SSM_GROUP
                  == lax.broadcasted_iota(jnp.int32, (LANES, SSM_SW), 1) // SSM_STATE)
    rep = rep_ref[...]
    wb_re = jnp.where(same_group, repeat_lanes(wre_ref[...], rep), 0.0)
    wb_im = jnp.where(same_group, repeat_lanes(wim_ref[...], rep), 0.0)
    cb_re = jnp.where(same_group, repeat_lanes(cre_ref[...], rep), 0.0)
    cb_im = jnp.where(same_group, repeat_lanes(cim_ref[...], rep), 0.0)
    cb_re_parts, cb_im_parts = split(cb_re, 2), split(cb_im, 2)

    m_ref[...] = jnp.zeros_like(m_ref)
    for k in range(SSM_T):
        lr, li = plr_ref[k:k + 1, :], pli_ref[k:k + 1, :]
        v_re = lr * wb_re - li * wb_im
        v_im = lr * wb_im + li * wb_re
        s = SSM_T - 1 - k
        p_ref[s * LANES:(s + 1) * LANES, :SSM_SW] = v_re.astype(BF16)
        p_ref[s * LANES:(s + 1) * LANES, SSM_SW:] = v_im.astype(BF16)
        bd = (dot_nt_hi_lo(v_re, cb_re_parts) - dot_nt_hi_lo(v_im, cb_im_parts)).astype(BF16)
        for s in range(SSM_T - k):
            t = s + k
            m_ref[s * LANES:(s + 1) * LANES, t * LANES:(t + 1) * LANES] = bd
        lr1, li1 = plr_ref[k + 1:k + 2, :], pli_ref[k + 1:k + 2, :]
        q_ref[:SSM_SW, k * LANES:(k + 1) * LANES] = (cb_re * lr1 - cb_im * li1).T.astype(BF16)
        q_ref[SSM_SW:, k * LANES:(k + 1) * LANES] = (-(cb_re * li1 + cb_im * lr1)).T.astype(BF16)


def _ssm_weights(a_re, a_im, log_dt, b_re, b_im, c_re, c_im, d):
    t_ = SSM_T
    dt = jnp.exp(log_dt)[:, None]
    mag = jnp.exp(a_re * dt)
    lam_re, lam_im = mag * jnp.cos(a_im * dt), mag * jnp.sin(a_im * dt)
    den = a_re * a_re + a_im * a_im
    f_re = ((lam_re - 1.0) * a_re + lam_im * a_im) / den
    f_im = (lam_im * a_re - (lam_re - 1.0) * a_im) / den
    w_re = f_re[:, :, None] * b_re - f_im[:, :, None] * b_im
    w_im = f_re[:, :, None] * b_im + f_im[:, :, None] * b_re
    k = jnp.arange(t_ + 1, dtype=F32)[None, :, None]
    pmag = jnp.exp(a_re[:, None, :] * dt[:, :, None] * k)
    pang = a_im[:, None, :] * dt[:, :, None] * k
    pw_re, pw_im = pmag * jnp.cos(pang), pmag * jnp.sin(pang)

    def lanes_gn(x):
        x = x.reshape(SSM_NJ, SSM_GPB, t_ + 1, SSM_STATE)
        return jnp.swapaxes(x, 1, 2).reshape(SSM_NJ, t_ + 1, SSM_SW)

    pl_re, pl_im = lanes_gn(pw_re), lanes_gn(pw_im)
    wcat = [jnp.swapaxes(x, 1, 2).reshape(SSM_NJ, LANES, SSM_STATE) for x in (w_re, w_im)]
    ccat = [x.reshape(SSM_NJ, LANES, SSM_STATE) for x in (c_re, c_im)]
    rep = jnp.tile(jnp.eye(SSM_STATE, dtype=BF16), (1, SSM_GPB))
    tw = t_ * LANES
    sq = 2 * SSM_SW
    blk = lambda j: (j, 0, 0)
    const = lambda j: (0, 0)
    per_j = lambda a: pl.BlockSpec((None,) + a.shape[1:], blk)
    ins = [wcat[0], wcat[1], ccat[0], ccat[1], pl_re, pl_im]
    m, p, q = pl.pallas_call(
        _ssm_fold_kernel, grid=(SSM_NJ,),
        in_specs=[per_j(a) for a in ins] + [pl.BlockSpec(rep.shape, const)],
        out_specs=[pl.BlockSpec((None, tw, tw), blk), pl.BlockSpec((None, tw, sq), blk),
                   pl.BlockSpec((None, sq, tw), blk)],
        out_shape=[jax.ShapeDtypeStruct((SSM_NJ, tw, tw), BF16), jax.ShapeDtypeStruct((SSM_NJ, tw, sq), BF16),
                   jax.ShapeDtypeStruct((SSM_NJ, sq, tw), BF16)],
        compiler_params=_cp(("arbitrary",)), name="ssm_fold",
    )(*ins, rep)
    lam_t = jnp.concatenate([pl_re[:, t_][:, None, :], pl_im[:, t_][:, None, :]], axis=2)
    dvec = jnp.tile(d.reshape(SSM_NJ, 1, LANES), (1, 1, t_))
    return {"M": m, "P": p, "Q": q, "lam": lam_t, "D": dvec}


def _mixer_kernel(o_ref, s_ref, ga_ref, gb_ref, x_ref, wo_ref, wv_ref, wg_ref, wout_ref, h_ref, s_sc):
    o_a = _dot(o_ref[...], wo_ref[...])
    nchunk = s_sc.shape[1] // SSM_T
    for j in range(SSM_NJ):
        sj = s_ref[j].astype(F32)
        for t in range(SSM_T):
            s_sc[j, pl.ds(t, nchunk, stride=SSM_T), :] = sj[:, t * LANES:(t + 1) * LANES]
    s = jnp.concatenate([s_sc[j] for j in range(SSM_NJ)], axis=1).astype(BF16)
    o_b = _dot(s, wv_ref[...]) * jax.nn.sigmoid(_dot(s, wg_ref[...]))
    merged = jax.nn.sigmoid(ga_ref[...]) * o_a + jax.nn.sigmoid(gb_ref[...]) * o_b
    h_ref[...] = x_ref[...] + _dot(merged.astype(BF16), wout_ref[...])


def _mixer(o, s, ga, gb, x, w, tm):
    n = x.shape[0]
    row = lambda i: (i, 0)
    const = lambda i: (0, 0)
    wspec = pl.BlockSpec((D_MODEL, D_MODEL), const, pipeline_mode=pl.Buffered(1))
    act = pl.BlockSpec((tm, D_MODEL), row)
    return pl.pallas_call(
        _mixer_kernel, grid=(n // tm,),
        in_specs=[act, pl.BlockSpec((SSM_NJ, tm // SSM_T, SSM_T * LANES), lambda i: (0, i, 0)), act, act, act,
                  wspec, wspec, wspec, wspec],
        out_specs=act, out_shape=jax.ShapeDtypeStruct((n, D_MODEL), F32),
        scratch_shapes=[pltpu.VMEM((SSM_NJ, tm, LANES), F32)],
        compiler_params=_cp(("arbitrary",)), name="mixer_out",
    )(o, s, ga, gb, x, w["wo"], w["wv"], w["wg"], w["wout"])


def _mlp_kernel(h_ref, g_ref, wup_ref, wdn_ref, y_ref):
    h = h_ref[...]
    hn = _rms(h, g_ref[...]).astype(BF16)
    a = jnp.maximum(_dot(hn, wup_ref[...]), 0.0)
    y_ref[...] = h + _dot((a * a).astype(BF16), wdn_ref[...])


def _mlp(h, w, tm):
    n = h.shape[0]
    row = lambda i: (i, 0)
    const = lambda i: (0, 0)
    act = pl.BlockSpec((tm, D_MODEL), row)
    return pl.pallas_call(
        _mlp_kernel, grid=(n // tm,),
        in_specs=[act, pl.BlockSpec((1, D_MODEL), const),
                  pl.BlockSpec((D_MODEL, D_FF), const, pipeline_mode=pl.Buffered(1)),
                  pl.BlockSpec((D_FF, D_MODEL), const, pipeline_mode=pl.Buffered(1))],
        out_specs=act, out_shape=jax.ShapeDtypeStruct((n, D_MODEL), F32),
        compiler_params=_cp(("arbitrary",)), name="mlp",
    )(h, w["gmlp"], w["wup"], w["wdn"])


def _cos_sin(pos):
    half = QK_ROPE // 2
    inv = ROPE_THETA ** (-jnp.arange(half, dtype=F32) / half)
    ang = pos.astype(F32)[:, None] * inv[None, :]
    return jnp.cos(ang), jnp.sin(ang)


def _rope_tables(pos=None, length=None):
    if pos is not None:
        cos, sin = _cos_sin(pos)
    else:
        half = QK_ROPE // 2
        ch, sh = _cos_sin(jnp.arange(length // CHUNK, dtype=jnp.int32) * CHUNK)
        cl, sl = _cos_sin(jnp.arange(CHUNK, dtype=jnp.int32))
        cos = (ch[:, None, :] * cl[None] - sh[:, None, :] * sl[None]).reshape(length, half)
        sin = (sh[:, None, :] * cl[None] + ch[:, None, :] * sl[None]).reshape(length, half)
    z = jnp.zeros_like(cos)
    pad = jnp.zeros((cos.shape[0], LANES - QK_ROPE), F32)
    ct = jnp.concatenate([cos, cos, pad], axis=1)
    s1 = jnp.concatenate([-sin, z, pad], axis=1)
    s2 = jnp.concatenate([z, sin, pad], axis=1)
    return ct, s1, s2


def _pad_lanes(a, width):
    return jnp.pad(a, [(0, 0)] * (a.ndim - 1) + [(0, width - a.shape[-1])])


def _layer_weights(norm_mix, w_in, q_lora_norm, w_uq, q_nope_norm, q_rope_norm, kv_lora_norm, k_rope_norm,
                   w_ukv, k_nope_norm, w_o_attn, w_glu_v, w_glu_g, w_out, norm_mlp, w_mlp_up, w_mlp_down):
    o_kv = Q_LORA
    o_kr = o_kv + KV_LORA
    o_ssm = o_kr + QK_ROPE
    bf = lambda a: a.astype(BF16)
    r2 = lambda a: a.reshape(1, -1).astype(F32)
    return {
        "nmix": r2(norm_mix),
        "win": bf(jnp.concatenate([w_in[:, :o_kr], w_in[:, o_ssm:],
                                   _pad_lanes(w_in[:, o_kr:o_ssm], LANES)], axis=1)),
        "gq": r2(q_lora_norm),
        "wuqn": bf(w_uq[:, :, :QK_NOPE].reshape(Q_LORA, N_HEADS * QK_NOPE)),
        "wuqr": bf(_pad_lanes(w_uq[:, :, QK_NOPE:], LANES).reshape(Q_LORA, N_HEADS * LANES)),
        "gqn": r2(q_nope_norm) * Q_SCALE, "gqr": _pad_lanes(r2(q_rope_norm), LANES) * Q_SCALE,
        "gkv": r2(kv_lora_norm), "gkr": _pad_lanes(r2(k_rope_norm), LANES),
        "wuk": bf(w_ukv[:, :, :QK_NOPE].reshape(KV_LORA, N_HEADS * QK_NOPE)),
        "wukg": bf((w_ukv[:, :, :QK_NOPE] * k_nope_norm.astype(F32)).reshape(KV_LORA, N_HEADS * QK_NOPE)),
        "wuv": bf(w_ukv[:, :, QK_NOPE:].reshape(KV_LORA, N_HEADS * V_DIM)),
        "gkn": r2(k_nope_norm),
        "wo": bf(w_o_attn.reshape(N_HEADS * V_DIM, D_MODEL)),
        "wv": bf(w_glu_v), "wg": bf(w_glu_g), "wout": bf(w_out),
        "gmlp": r2(norm_mlp), "wup": bf(w_mlp_up), "wdn": bf(w_mlp_down),
    }


def kernel(x_prompt, x_sample, cache_latent, cache_krope, cache_meta_latent, cache_meta_krope, state_ssm_re, state_ssm_im, meta_tokens, norm_mix, w_in, q_lora_norm, w_uq, q_nope_norm, q_rope_norm, kv_lora_norm, k_rope_norm, w_ukv, k_nope_norm, w_o_attn, ssm_a_re, ssm_a_im, ssm_log_dt, ssm_b_re, ssm_b_im, ssm_c_re, ssm_c_im, ssm_d, w_glu_v, w_glu_g, w_out, norm_mlp, w_mlp_up, w_mlp_down):
    bsz_p, seq_p = x_prompt.shape[0], x_prompt.shape[1]
    bsz_s, seq_s = x_sample.shape[0], x_sample.shape[1]
    past = cache_latent.shape[2]
    depth = w_in.shape[0]
    assert depth == 1, "single-layer step"
    l = 0
    w = _layer_weights(norm_mix[l], w_in[l], q_lora_norm[l], w_uq[l], q_nope_norm[l], q_rope_norm[l],
                       kv_lora_norm[l], k_rope_norm[l], w_ukv[l], k_nope_norm[l], w_o_attn[l], w_glu_v[l],
                       w_glu_g[l], w_out[l], norm_mlp[l], w_mlp_up[l], w_mlp_down[l])
    sw = _ssm_weights(ssm_a_re[l].astype(F32), ssm_a_im[l].astype(F32), ssm_log_dt[l].astype(F32),
                      ssm_b_re[l].astype(F32), ssm_b_im[l].astype(F32), ssm_c_re[l].astype(F32),
                      ssm_c_im[l].astype(F32), ssm_d[l].astype(F32))
    sq = 2 * SSM_SW

    tabs_m = _rope_tables(jnp.arange(N_META, dtype=jnp.int32) - N_META)
    (_, mk, mv, mckv, mkr, mu, _, _) = _front(meta_tokens.astype(F32), tabs_m, 1, w, N_META)
    h_meta = _ssm_state(mu, sw, N_META // SSM_T)

    n_p = bsz_p * seq_p
    xp = x_prompt.reshape(n_p, D_MODEL)
    tabs_p = _rope_tables(length=seq_p)
    q, k, v, ckv_p, kr_p, u_p, ga, gb = _front(xp, tabs_p, seq_p // TM_PROMPT, w, TM_PROMPT, kr_seq=seq_p)
    o_p = _attn_prompt(q, k, v, mk, mv, bsz_p, seq_p)
    h0_p = jnp.broadcast_to(h_meta, (SSM_NJ, bsz_p, sq))
    s_p, hfin_p = _ssm_prompt(u_p, sw, h0_p, bsz_p, seq_p // SSM_T, SSM_RT)
    h1_p = _mixer(o_p, s_p, ga, gb, xp, w, TM_PROMPT)
    y_p = _mlp(h1_p, w, TM_PROMPT)

    n_s = bsz_s * seq_s
    xs = x_sample.reshape(n_s, D_MODEL)
    pos_s = past + jnp.arange(seq_s, dtype=jnp.int32)
    tabs_s = tuple(jnp.tile(t, (bsz_s, 1)) for t in _rope_tables(pos_s))
    q_s, k_s, _, ckv_s, kr_s, u_s, ga_s, gb_s = _front(xs, tabs_s, n_s // TM_SAMPLE, w, TM_SAMPLE)
    o_s = _attn_sample(q_s, k_s, ckv_s, cache_latent[l].astype(F32), cache_krope[l].astype(F32),
                       cache_meta_latent[l].astype(F32), cache_meta_krope[l].astype(F32), w, bsz_s, seq_s, past)

    def to_blocks(st):
        return st.astype(F32).reshape(bsz_s, SSM_NJ, SSM_SW).transpose(1, 0, 2)

    h0_s = jnp.concatenate([to_blocks(state_ssm_re[l]), to_blocks(state_ssm_im[l])], axis=2)
    s_s, hfin_s = _ssm_sample(u_s, sw, h0_s, bsz_s, seq_s // SSM_T)
    h1_s = _mixer(o_s, s_s, ga_s, gb_s, xs, w, TM_SAMPLE)
    y_s = _mlp(h1_s, w, TM_SAMPLE)

    def from_blocks(hf, nb):
        re = hf[:, :, :SSM_SW].transpose(1, 0, 2).reshape(1, nb, N_GROUPS, SSM_STATE)
        im = hf[:, :, SSM_SW:].transpose(1, 0, 2).reshape(1, nb, N_GROUPS, SSM_STATE)
        return re, im

    sre_p, sim_p = from_blocks(hfin_p, bsz_p)
    sre_s, sim_s = from_blocks(hfin_s, bsz_s)
    return (y_p.reshape(bsz_p, seq_p, D_MODEL), y_s.reshape(bsz_s, seq_s, D_MODEL),
            ckv_p.reshape(1, bsz_p, seq_p, KV_LORA), jnp.swapaxes(kr_p, 1, 2)[None],
            jnp.broadcast_to(mckv[None, None], (1, bsz_p, N_META, KV_LORA)),
            jnp.broadcast_to(mkr[None, None], (1, bsz_p, N_META, QK_ROPE)),
            sre_p, sim_p,
            ckv_s.reshape(1, bsz_s, seq_s, KV_LORA), kr_s.reshape(1, bsz_s, seq_s, QK_ROPE),
            sre_s, sim_s)
```

```python
import functools
import math

import jax
import jax.numpy as jnp
from jax import lax
from jax.experimental import pallas as pl
from jax.experimental.pallas import tpu as pltpu

F32 = jnp.float32
BF16 = jnp.bfloat16

D_MODEL = 1024
CHUNK = 64
N_META = 16
N_HEADS = 8
QK_NOPE = 128
QK_ROPE = 64
V_DIM = 128
QK_DIM = QK_NOPE + QK_ROPE
Q_LORA = 384
KV_LORA = 256
SSM_GROUP = 16
N_GROUPS = D_MODEL // SSM_GROUP
SSM_STATE = 64
D_FF = 4 * D_MODEL
ROPE_THETA = 10000.0
EPS = 1e-6
ATTN_SCALE = QK_DIM ** -0.5
Q_SCALE = ATTN_SCALE * math.log2(math.e)
NEG_INF = -1e30

LANES = 128
HEAD_PAD = 2 * LANES
SSM_T = 8
SSM_NJ = D_MODEL // LANES
SSM_GPB = LANES // SSM_GROUP
SSM_SW = SSM_GPB * SSM_STATE
VMEM_LIMIT = 56 * 1024 * 1024
TM_PROMPT = 512
TM_SAMPLE = 512
SSM_RT = 256


def _cp(sem):
    return pltpu.CompilerParams(dimension_semantics=sem, vmem_limit_bytes=VMEM_LIMIT)


def _dot(a, b):
    return jnp.dot(a, b, preferred_element_type=F32)


def _dot_nt(a, b):
    return lax.dot_general(a, b, (((1,), (1,)), ((), ())), preferred_element_type=F32)


def _rms(x, g, n=None):
    n = x.shape[-1] if n is None else n
    ms = jnp.sum(x * x, axis=-1, keepdims=True) * (1.0 / n)
    return x * lax.rsqrt(ms + EPS) * g


def _rope128(b, ct, s1, s2):
    return b * ct + pltpu.roll(b, LANES - QK_ROPE // 2, 1) * s1 + pltpu.roll(b, QK_ROPE // 2, 1) * s2


IN_Q = 0
IN_KV = IN_Q + Q_LORA
IN_U = IN_KV + KV_LORA
IN_GA = IN_U + D_MODEL
IN_GB = IN_GA + D_MODEL
IN_KR = IN_GB + D_MODEL
IN_END = IN_KR + LANES


def _front_kernel(x_ref, ct_ref, s1_ref, s2_ref, nmix_ref, win_ref, gq_ref, wuqn_ref, wuqr_ref, gqn_ref,
                  gqr_ref, gkv_ref, gkr_ref, wuk_ref, wuv_ref, gkn_ref,
                  q_ref, k_ref, v_ref, ckv_ref, kr_ref, u_ref, ga_ref, gb_ref, u_sc, *, kr_transposed):
    x = x_ref[...]
    xn = _rms(x, nmix_ref[...]).astype(BF16)
    ct, s1, s2 = ct_ref[...], s1_ref[...], s2_ref[...]


    q_lat = _dot(xn, win_ref[:, IN_Q:IN_KV])
    kv_lat = _dot(xn, win_ref[:, IN_KV:IN_U])
    kr_raw = _dot(xn, win_ref[:, IN_KR:IN_END])

    c_q = _rms(q_lat, gq_ref[...]).astype(BF16)
    qn = _dot(c_q, wuqn_ref[...])
    qr = _dot(c_q, wuqr_ref[...])
    c_kv = _rms(kv_lat, gkv_ref[...])
    ckv_ref[...] = c_kv
    c16 = c_kv.astype(BF16)
    kn = _dot(c16, wuk_ref[...])
    v_ref[...] = _dot(c16, wuv_ref[...]).astype(BF16)

    for h in range(N_HEADS):
        sl = slice(h * LANES, (h + 1) * LANES)
        q_ref[:, h * HEAD_PAD:h * HEAD_PAD + LANES] = _rms(qn[:, sl], gqn_ref[...]).astype(BF16)
        r = _rope128(_rms(qr[:, sl], gqr_ref[...], QK_ROPE), ct, s1, s2)
        q_ref[:, h * HEAD_PAD + LANES:(h + 1) * HEAD_PAD] = r.astype(BF16)
    kr = _rms(kr_raw, gkr_ref[...], QK_ROPE)
    kr = _rope128(kr, ct, s1, s2)
    kr_ref[...] = kr.T[:QK_ROPE, :] if kr_transposed else kr[:, :QK_ROPE]
    kr16 = kr.astype(BF16)
    for h in range(N_HEADS):
        sl = slice(h * LANES, (h + 1) * LANES)
        k_ref[:, h * HEAD_PAD:h * HEAD_PAD + LANES] = _rms(kn[:, sl], gkn_ref[...]).astype(BF16)
        k_ref[:, h * HEAD_PAD + LANES:(h + 1) * HEAD_PAD] = kr16

    ga_ref[...] = _dot(xn, win_ref[:, IN_GA:IN_GB])
    u = _dot(xn, win_ref[:, IN_U:IN_GA])
    nchunk = u_sc.shape[1] // SSM_T
    for j in range(SSM_NJ):
        u_sc[j] = u[:, j * LANES:(j + 1) * LANES]
        for t in range(SSM_T):
            u_ref[j, :, t * LANES:(t + 1) * LANES] = u_sc[j, pl.ds(t, nchunk, stride=SSM_T), :]
    gb_ref[...] = _dot(xn, win_ref[:, IN_GB:IN_KR])


def _front(x, tabs, tab_blocks, w, tm, kr_seq=None):
    n = x.shape[0]
    nt = n // tm
    ct, s1, s2 = tabs

    def row(i):
        return (i, 0)

    def tab(i):
        return (i % tab_blocks, 0)

    def const(i):
        return (0, 0)

    def full(a):
        return pl.BlockSpec(a.shape, const, pipeline_mode=pl.Buffered(1))

    weights = [w["nmix"], w["win"], w["gq"], w["wuqn"], w["wuqr"], w["gqn"], w["gqr"], w["gkv"], w["gkr"],
               w["wuk"], w["wuv"], w["gkn"]]
    wide = N_HEADS * LANES
    qk_wide = N_HEADS * HEAD_PAD
    out_shape = [
        jax.ShapeDtypeStruct((n, qk_wide), BF16),
        jax.ShapeDtypeStruct((n, qk_wide), BF16),
        jax.ShapeDtypeStruct((n, wide), BF16),
        jax.ShapeDtypeStruct((n, KV_LORA), F32),
        (jax.ShapeDtypeStruct((n, QK_ROPE), F32) if kr_seq is None
         else jax.ShapeDtypeStruct((n // kr_seq, QK_ROPE, kr_seq), F32)),
        jax.ShapeDtypeStruct((SSM_NJ, n // SSM_T, SSM_T * LANES), F32),
        jax.ShapeDtypeStruct((n, D_MODEL), F32),
        jax.ShapeDtypeStruct((n, D_MODEL), F32),
    ]
    out_specs = [
        pl.BlockSpec((tm, qk_wide), row), pl.BlockSpec((tm, qk_wide), row),
        pl.BlockSpec((tm, wide), row), pl.BlockSpec((tm, KV_LORA), row),
        (pl.BlockSpec((tm, QK_ROPE), row) if kr_seq is None else
         pl.BlockSpec((None, QK_ROPE, tm), lambda i: (i // (kr_seq // tm), 0, i % (kr_seq // tm)))),
        pl.BlockSpec((SSM_NJ, tm // SSM_T, SSM_T * LANES), lambda i: (0, i, 0)),
        pl.BlockSpec((tm, D_MODEL), row), pl.BlockSpec((tm, D_MODEL), row),
    ]
    in_specs = ([pl.BlockSpec((tm, D_MODEL), row)] + [pl.BlockSpec((tm, LANES), tab)] * 3
                + [full(a) for a in weights])
    return pl.pallas_call(
        functools.partial(_front_kernel, kr_transposed=kr_seq is not None), grid=(nt,), in_specs=in_specs,
        out_specs=out_specs, out_shape=out_shape,
        scratch_shapes=[pltpu.VMEM((SSM_NJ, tm, LANES), F32)],
        compiler_params=_cp(("arbitrary",)), name="front",
    )(x, ct, s1, s2, *weights)


ATT_T = 512


def _with_ones(v):
    return jnp.concatenate([v, jnp.ones_like(v)], axis=1)


def _softmax_step(s, v, m_ref, acc_ref):
    m_prev = m_ref[...]
    m_new = jnp.maximum(m_prev, jnp.max(s, axis=-1, keepdims=True))
    alpha = jnp.exp2(m_prev - m_new)
    p = jnp.exp2(s - jnp.tile(m_new, (1, s.shape[1] // LANES)))
    acc_ref[...] = jnp.tile(alpha, (1, 2)) * acc_ref[...] + _dot(p.astype(BF16), _with_ones(v))
    m_ref[...] = m_new


ATT_HG = 4


def _attn_prompt_kernel(q_ref, k_ref, v_ref, mk_ref, mv_ref, o_ref, a_sc, b_sc, m_ref, acc_ref):
    step = pl.program_id(2)
    npairs = step
    heads = range(ATT_HG)

    def key_rows(kt):
        return pl.ds(pl.multiple_of(kt * ATT_T, ATT_T), ATT_T)

    def q_rows(slot):
        return slice(slot * ATT_T, (slot + 1) * ATT_T)

    def qk_lanes(g):
        return slice(g * HEAD_PAD, (g + 1) * HEAD_PAD)

    def v_lanes(g):
        return slice(g * V_DIM, (g + 1) * V_DIM)

    def scores(slot, g, kt):
        return _dot_nt(q_ref[q_rows(slot), qk_lanes(g)], k_ref[key_rows(kt), qk_lanes(g)])

    def consume(slot, g, s, kt, diagonal):
        if diagonal:
            qc = lax.broadcasted_iota(jnp.int32, (ATT_T, ATT_T), 0) // CHUNK
            kc = lax.broadcasted_iota(jnp.int32, (ATT_T, ATT_T), 1) // CHUNK
            s = jnp.where(kc <= qc, s, NEG_INF)
        _softmax_step(s, v_ref[key_rows(kt), v_lanes(g)], m_ref.at[slot, g], acc_ref.at[slot, g])

    def head(slot):
        for g in heads:
            a_sc[slot, g] = scores(slot, g, 0)
        for g in heads:
            s = _dot_nt(q_ref[q_rows(slot), qk_lanes(g)], mk_ref[:, qk_lanes(g)])
            m0 = jnp.max(s, axis=-1, keepdims=True)
            p = jnp.exp2(s - m0)
            m_ref[slot, g] = jnp.broadcast_to(m0, (ATT_T, LANES))
            acc_ref[slot, g] = _dot(p.astype(BF16), _with_ones(mv_ref[:, v_lanes(g)]))

    def full_pairs(slot):
        def pair(pi, carry):
            kt = 2 * pi
            for g in heads:
                b_sc[g] = scores(slot, g, kt + 1)
                consume(slot, g, a_sc[slot, g], kt, False)
            for g in heads:
                a_sc[slot, g] = scores(slot, g, kt + 2)
                consume(slot, g, b_sc[g], kt + 1, False)
            return carry

        lax.fori_loop(0, npairs, pair, 0)

    def tail(slot):
        i = 2 * step + slot
        if slot == 1:
            for g in heads:
                b_sc[g] = scores(slot, g, i)
                consume(slot, g, a_sc[slot, g], i - 1, False)
            for g in heads:
                consume(slot, g, b_sc[g], i, True)
        else:
            for g in heads:
                consume(slot, g, a_sc[slot, g], i, True)
        for g in heads:
            o_ref[q_rows(slot), v_lanes(g)] = (acc_ref[slot, g, :, :V_DIM]
                                               / acc_ref[slot, g, :, V_DIM:]).astype(BF16)

    head(0)
    full_pairs(0)
    tail(0)
    head(1)
    full_pairs(1)
    tail(1)


def _attn_prompt(q, k, v, mk, mv, bsz, seq):
    nstep = seq // (2 * ATT_T)
    n = bsz * seq
    qmap = lambda b, h, i: (b * nstep + i, h)
    return pl.pallas_call(
        _attn_prompt_kernel, grid=(bsz, N_HEADS // ATT_HG, nstep),
        in_specs=[pl.BlockSpec((2 * ATT_T, ATT_HG * HEAD_PAD), qmap),
                  pl.BlockSpec((seq, ATT_HG * HEAD_PAD), lambda b, h, i: (b, h)),
                  pl.BlockSpec((seq, ATT_HG * V_DIM), lambda b, h, i: (b, h)),
                  pl.BlockSpec((N_META, ATT_HG * HEAD_PAD), lambda b, h, i: (0, h)),
                  pl.BlockSpec((N_META, ATT_HG * V_DIM), lambda b, h, i: (0, h))],
        out_specs=pl.BlockSpec((2 * ATT_T, ATT_HG * V_DIM), qmap),
        out_shape=jax.ShapeDtypeStruct((n, N_HEADS * V_DIM), BF16),
        scratch_shapes=[pltpu.VMEM((2, ATT_HG, ATT_T, ATT_T), F32), pltpu.VMEM((ATT_HG, ATT_T, ATT_T), F32),
                        pltpu.VMEM((2, ATT_HG, ATT_T, LANES), F32),
                        pltpu.VMEM((2, ATT_HG, ATT_T, 2 * V_DIM), F32)],
        compiler_params=_cp(("arbitrary", "arbitrary", "arbitrary")), name="attn_prompt",
    )(q, k, v, mk, mv)


SAMPLE_KEY_CHUNK = 512


def _attn_sample_kernel(q_ref, cm_ref, cn_ref, cc_ref, krm_ref, krn_ref, krc_ref, wuk_ref, wukg_ref, wuv_ref,
                        hsum_ref, o_ref, c_sc, kr_sc, st_sc, pt_sc, *, seq, past):
    n_small = N_META + seq
    n_keys = n_small + past

    c_sc[:, KV_LORA:] = jnp.ones((n_keys, LANES), BF16)
    c_sc[0:N_META, :KV_LORA] = cm_ref[...].astype(BF16)
    c_sc[N_META:n_small, :KV_LORA] = cn_ref[...].astype(BF16)
    c_sc[n_small:, :KV_LORA] = cc_ref[...].astype(BF16)
    kr_sc[:, QK_ROPE:] = jnp.zeros((n_small, LANES - QK_ROPE), BF16)
    kr_sc[0:N_META, :QK_ROPE] = krm_ref[...].astype(BF16)
    kr_sc[N_META:n_small, :] = krn_ref[...]

    qa, qr = [], []
    for h in range(N_HEADS):
        qn_h = q_ref[:, h * HEAD_PAD:h * HEAD_PAD + LANES]
        qa.append(_dot_nt(qn_h, wukg_ref[:, h * LANES:(h + 1) * LANES]))
        qr.append(q_ref[:, h * HEAD_PAD + LANES:(h + 1) * HEAD_PAD])
    qa = jnp.concatenate(qa, axis=0).astype(BF16)
    qr = jnp.concatenate(qr, axis=0)

    bounds = [(0, n_small)] + [(n_small + i, n_small + i + SAMPLE_KEY_CHUNK)
                               for i in range(0, past, SAMPLE_KEY_CHUNK)]
    m = None
    for lo, hi in bounds:
        c16 = c_sc[lo:hi, :KV_LORA]
        kn = _dot(c16, wuk_ref[...])
        ssq = _dot((kn * kn).astype(BF16), hsum_ref[...])
        r = lax.rsqrt(ssq * (1.0 / QK_NOPE) + EPS)
        if lo == 0:
            s_rope = _dot_nt(kr_sc[...], qr)
        else:
            kr_t = krc_ref[:, lo - n_small:hi - n_small].astype(BF16)
            s_rope = lax.dot_general(kr_t, qr[:, :QK_ROPE], (((0,), (1,)), ((), ())),
                                     preferred_element_type=F32)
        st = _dot_nt(c16, qa) * r + s_rope
        st_sc[lo:hi, :] = st
        cm = jnp.max(st, axis=0, keepdims=True)
        m = cm if m is None else jnp.maximum(m, cm)

    for lo, hi in bounds:
        pt_sc[lo:hi, :] = jnp.exp2(st_sc[lo:hi, :] - m).astype(BF16)
    pc = lax.dot_general(pt_sc[...], c_sc[...], (((0,), (0,)), ((), ())), preferred_element_type=F32)
    pcn = (pc[:, :KV_LORA] / jnp.tile(pc[:, KV_LORA:], (1, KV_LORA // LANES))).astype(BF16)
    for h in range(N_HEADS):
        o_ref[:, h * V_DIM:(h + 1) * V_DIM] = _dot(pcn[h * seq:(h + 1) * seq, :],
                                                   wuv_ref[:, h * V_DIM:(h + 1) * V_DIM]).astype(BF16)


def _attn_sample(q, k_new, c_new, cache_c, cache_kr, meta_c, meta_kr, w, bsz, seq, past):
    n_keys = N_META + seq + past
    hq = N_HEADS * seq
    hsum = jnp.repeat(jnp.repeat(jnp.eye(N_HEADS, dtype=BF16), QK_NOPE, axis=0), seq, axis=1)
    row = lambda b: (b, 0)
    b3 = lambda b: (b, 0, 0)
    const = lambda b: (0, 0)
    kern = functools.partial(_attn_sample_kernel, seq=seq, past=past)
    return pl.pallas_call(
        kern, grid=(bsz,),
        in_specs=[pl.BlockSpec((seq, N_HEADS * HEAD_PAD), row),
                  pl.BlockSpec((None, N_META, KV_LORA), b3), pl.BlockSpec((seq, KV_LORA), row),
                  pl.BlockSpec((None, past, KV_LORA), b3),
                  pl.BlockSpec((None, N_META, QK_ROPE), b3), pl.BlockSpec((seq, LANES), lambda b: (b, 1)),
                  pl.BlockSpec((None, QK_ROPE, past), b3),
                  pl.BlockSpec(w["wuk"].shape, const), pl.BlockSpec(w["wukg"].shape, const),
                  pl.BlockSpec(w["wuv"].shape, const), pl.BlockSpec(hsum.shape, const)],
        out_specs=pl.BlockSpec((seq, N_HEADS * V_DIM), row),
        out_shape=jax.ShapeDtypeStruct((bsz * seq, N_HEADS * V_DIM), BF16),
        scratch_shapes=[pltpu.VMEM((n_keys, KV_LORA + LANES), BF16), pltpu.VMEM((N_META + seq, LANES), BF16),
                        pltpu.VMEM((n_keys, hq), F32), pltpu.VMEM((n_keys, hq), BF16)],
        compiler_params=_cp(("arbitrary",)), name="attn_sample",
    )(q, meta_c, c_new, cache_c, meta_kr, k_new, jnp.swapaxes(cache_kr, 1, 2), w["wuk"], w["wukg"], w["wuv"],
      hsum)


MXU_DIM = 256


def _dot_causal(u16, m_ref):
    nblk = m_ref.shape[0] // MXU_DIM
    return jnp.concatenate(
        [_dot(u16[:, :(cb + 1) * MXU_DIM], m_ref[:(cb + 1) * MXU_DIM, cb * MXU_DIM:(cb + 1) * MXU_DIM])
         for cb in range(nblk)], axis=1)


SSM_PACK = 4
SSM_SC = SSM_SW // LANES
SUBLANES = 2 * SSM_PACK


def _ssm_prompt_kernel(u_ref, m_ref, p_ref, q_ref, lam_ref, d_ref, h0_ref, s_ref, hfin_ref, s_sc, hp_sc, h_sc, y_sc,
                       *, nb, rt):
    t = pl.program_id(1)

    @pl.when(t == 0)
    def _():
        h_sc[...] = h0_ref[...]

    for b in range(nb):
        sb = _dot(u_ref[b].astype(BF16), p_ref[...])
        for c in range(SSM_SC):
            s_sc[c, pl.ds(b, rt, stride=SUBLANES), :] = sb[:, c * LANES:(c + 1) * LANES]
            s_sc[c, pl.ds(SSM_PACK + b, rt, stride=SUBLANES), :] = sb[:, SSM_SW + c * LANES:SSM_SW + (c + 1) * LANES]
    for b in range(nb):
        ub = u_ref[b]
        y_sc[b] = _dot_causal(ub.astype(BF16), m_ref) + ub * d_ref[...]

    lam = lam_ref[...]
    im_rows = lax.broadcasted_iota(jnp.int32, (SUBLANES, LANES), 0) >= SSM_PACK
    coef = []
    for c in range(SSM_SC):
        lr = jnp.broadcast_to(lam[:, c * LANES:(c + 1) * LANES], (SUBLANES, LANES))
        li = jnp.broadcast_to(lam[:, SSM_SW + c * LANES:SSM_SW + (c + 1) * LANES], (SUBLANES, LANES))
        coef.append((lr, jnp.where(im_rows, li, -li)))

    hs = [h_sc[c] for c in range(SSM_SC)]
    for r in range(rt):
        rows = slice(r * SUBLANES, (r + 1) * SUBLANES)
        for c in range(SSM_SC):
            hp_sc[c, rows, :] = hs[c]
            a, bb = coef[c]
            hs[c] = a * hs[c] + bb * pltpu.roll(hs[c], SSM_PACK, 0) + s_sc[c, rows, :]
    for c in range(SSM_SC):
        h_sc[c] = hs[c]

    for b in range(nb):
        hb = jnp.concatenate([hp_sc[c, pl.ds(b, rt, stride=SUBLANES), :] for c in range(SSM_SC)]
                             + [hp_sc[c, pl.ds(SSM_PACK + b, rt, stride=SUBLANES), :] for c in range(SSM_SC)],
                             axis=1)
        y = y_sc[b] + _dot(hb.astype(BF16), q_ref[...])
        s_ref[b] = jax.nn.gelu(y).astype(BF16)

    @pl.when(t == pl.num_programs(1) - 1)
    def _():
        hfin_ref[...] = h_sc[...]


def _pack_state(h):
    nj, nb, _ = h.shape
    h = jnp.pad(h, ((0, 0), (0, SSM_PACK - nb), (0, 0)))
    re = h[:, :, :SSM_SW].reshape(nj, SSM_PACK, SSM_SC, LANES)
    im = h[:, :, SSM_SW:].reshape(nj, SSM_PACK, SSM_SC, LANES)
    return jnp.concatenate([re, im], axis=1).transpose(0, 2, 1, 3)


def _unpack_state(hp, nb):
    nj = hp.shape[0]
    x = hp.transpose(0, 2, 1, 3).reshape(nj, SUBLANES, SSM_SW)
    return jnp.concatenate([x[:, :nb], x[:, SSM_PACK:SSM_PACK + nb]], axis=2)


def _ssm_prompt(u, sw, h0, nb, lc, rt):
    assert nb == SSM_PACK and lc % rt == 0
    tw = SSM_T * LANES
    u4 = u.reshape(SSM_NJ, nb, lc, tw)
    sq = 2 * SSM_SW
    kern = functools.partial(_ssm_prompt_kernel, nb=nb, rt=rt)
    st_spec = pl.BlockSpec((None, SSM_SC, SUBLANES, LANES), lambda j, t: (j, 0, 0, 0))
    s, hfin = pl.pallas_call(
        kern, grid=(SSM_NJ, lc // rt),
        in_specs=[pl.BlockSpec((None, nb, rt, tw), lambda j, t: (j, 0, t, 0)),
                  pl.BlockSpec((None, tw, tw), lambda j, t: (j, 0, 0)),
                  pl.BlockSpec((None, tw, sq), lambda j, t: (j, 0, 0)),
                  pl.BlockSpec((None, sq, tw), lambda j, t: (j, 0, 0)),
                  pl.BlockSpec((None, 1, sq), lambda j, t: (j, 0, 0)),
                  pl.BlockSpec((None, 1, tw), lambda j, t: (j, 0, 0)),
                  st_spec],
        out_specs=[pl.BlockSpec((None, nb, rt, tw), lambda j, t: (j, 0, t, 0)), st_spec],
        out_shape=[jax.ShapeDtypeStruct((SSM_NJ, nb, lc, tw), BF16),
                   jax.ShapeDtypeStruct((SSM_NJ, SSM_SC, SUBLANES, LANES), F32)],
        scratch_shapes=[pltpu.VMEM((SSM_SC, rt * SUBLANES, LANES), F32),
                        pltpu.VMEM((SSM_SC, rt * SUBLANES, LANES), F32),
                        pltpu.VMEM((SSM_SC, SUBLANES, LANES), F32),
                        pltpu.VMEM((nb, rt, tw), F32)],
        compiler_params=_cp(("arbitrary", "arbitrary")), name="ssm_prompt",
    )(u4, sw["M"], sw["P"], sw["Q"], sw["lam"], sw["D"], _pack_state(h0))
    return s.reshape(SSM_NJ, nb * lc, tw), _unpack_state(hfin, nb)


def _ssm_state_kernel(u_ref, p_ref, lam_ref, h_ref):
    s = _dot(u_ref[...].astype(BF16), p_ref[...])
    lam = lam_ref[...]
    lr, li = lam[:, :SSM_SW], lam[:, SSM_SW:]
    h = s[0:1, :]
    for r in range(1, s.shape[0]):
        hre, him = h[:, :SSM_SW], h[:, SSM_SW:]
        h = jnp.concatenate([lr * hre - li * him, lr * him + li * hre], axis=1) + s[r:r + 1, :]
    h_ref[...] = h


def _ssm_state(u, sw, lc):
    tw = SSM_T * LANES
    sq = 2 * SSM_SW
    blk = lambda j: (j, 0, 0)
    return pl.pallas_call(
        _ssm_state_kernel, grid=(SSM_NJ,),
        in_specs=[pl.BlockSpec((None, lc, tw), blk), pl.BlockSpec((None, tw, sq), blk),
                  pl.BlockSpec((None, 1, sq), blk)],
        out_specs=pl.BlockSpec((None, 1, sq), blk),
        out_shape=jax.ShapeDtypeStruct((SSM_NJ, 1, sq), F32),
        compiler_params=_cp(("arbitrary",)), name="ssm_state",
    )(u, sw["P"], sw["lam"])


def _ssm_sample_kernel(u_ref, m_ref, p_ref, q_ref, lam_ref, d_ref, h0_ref, s_ref, hfin_ref, s_sc,
                       *, nb, lc):
    u = u_ref[...]
    u16 = u.astype(BF16)
    s_sc[...] = _dot(u16, p_ref[...])
    lam = lam_ref[...]
    lr, li = lam[:, :SSM_SW], lam[:, SSM_SW:]
    h = h0_ref[...]
    for r in range(lc):
        rows = slice(r * nb, (r + 1) * nb)
        s = s_sc[rows, :]
        s_sc[rows, :] = h
        hre, him = h[:, :SSM_SW], h[:, SSM_SW:]
        h = jnp.concatenate([lr * hre - li * him + s[:, :SSM_SW], lr * him + li * hre + s[:, SSM_SW:]], axis=1)
    hfin_ref[...] = h
    y = _dot_causal(u16, m_ref) + _dot(s_sc[...].astype(BF16), q_ref[...]) + u * d_ref[...]
    s_ref[...] = jax.nn.gelu(y).astype(BF16)


def _ssm_sample(u, sw, h0, nb, lc):
    tw = SSM_T * LANES
    sq = 2 * SSM_SW
    u2 = u.reshape(SSM_NJ, nb, lc, tw).transpose(0, 2, 1, 3).reshape(SSM_NJ, lc * nb, tw)
    kern = functools.partial(_ssm_sample_kernel, nb=nb, lc=lc)
    blk = lambda j: (j, 0, 0)
    s, hfin = pl.pallas_call(
        kern, grid=(SSM_NJ,),
        in_specs=[pl.BlockSpec((None, lc * nb, tw), blk), pl.BlockSpec((None, tw, tw), blk),
                  pl.BlockSpec((None, tw, sq), blk), pl.BlockSpec((None, sq, tw), blk),
                  pl.BlockSpec((None, 1, sq), blk), pl.BlockSpec((None, 1, tw), blk),
                  pl.BlockSpec((None, nb, sq), blk)],
        out_specs=[pl.BlockSpec((None, lc * nb, tw), blk), pl.BlockSpec((None, nb, sq), blk)],
        out_shape=[jax.ShapeDtypeStruct((SSM_NJ, lc * nb, tw), BF16),
                   jax.ShapeDtypeStruct((SSM_NJ, nb, sq), F32)],
        scratch_shapes=[pltpu.VMEM((lc * nb, sq), F32)],
        compiler_params=_cp(("arbitrary",)), name="ssm_sample",
    )(u2, sw["M"], sw["P"], sw["Q"], sw["lam"], sw["D"], h0)
    s = s.reshape(SSM_NJ, lc, nb, tw).transpose(0, 2, 1, 3).reshape(SSM_NJ, nb * lc, tw)
    return s, hfin


def _ssm_fold_kernel(wre_ref, wim_ref, cre_ref, cim_ref, plr_ref, pli_ref, rep_ref, m_ref, p_ref, q_ref):
    def split(a, terms):
        out = []
        for _ in range(terms):
            t = a.astype(BF16)
            out.append(t)
            a = a - t.astype(F32)
        return out

    def repeat_lanes(a, rep):
        return sum(_dot(t, rep) for t in split(a, 3))

    def dot_nt_hi_lo(a, b_parts):
        a_hi, a_lo = split(a, 2)
        b_hi, b_lo = b_parts
        return _dot_nt(a_hi, b_hi) + (_dot_nt(a_hi, b_lo) + _dot_nt(a_lo, b_hi))

    same_group = (lax.broadcasted_iota(jnp.int32, (LANES, SSM_SW), 0) // SSM_GROUP
                  == lax.broadcasted_iota(jnp.int32, (LANES, SSM_SW), 1) // SSM_STATE)
    rep = rep_ref[...]
    wb_re = jnp.where(same_group, repeat_lanes(wre_ref[...], rep), 0.0)
    wb_im = jnp.where(same_group, repeat_lanes(wim_ref[...], rep), 0.0)
    cb_re = jnp.where(same_group, repeat_lanes(cre_ref[...], rep), 0.0)
    cb_im = jnp.where(same_group, repeat_lanes(cim_ref[...], rep), 0.0)
    cb_re_parts, cb_im_parts = split(cb_re, 2), split(cb_im, 2)

    m_ref[...] = jnp.zeros_like(m_ref)
    for k in range(SSM_T):
        lr, li = plr_ref[k:k + 1, :], pli_ref[k:k + 1, :]
        v_re = lr * wb_re - li * wb_im
        v_im = lr * wb_im + li * wb_re
        s = SSM_T - 1 - k
        p_ref[s * LANES:(s + 1) * LANES, :SSM_SW] = v_re.astype(BF16)
        p_ref[s * LANES:(s + 1) * LANES, SSM_SW:] = v_im.astype(BF16)
        bd = (dot_nt_hi_lo(v_re, cb_re_parts) - dot_nt_hi_lo(v_im, cb_im_parts)).astype(BF16)
        for s in range(SSM_T - k):
            t = s + k
            m_ref[s * LANES:(s + 1) * LANES, t * LANES:(t + 1) * LANES] = bd
        lr1, li1 = plr_ref[k + 1:k + 2, :], pli_ref[k + 1:k + 2, :]
        q_ref[:SSM_SW, k * LANES:(k + 1) * LANES] = (cb_re * lr1 - cb_im * li1).T.astype(BF16)
        q_ref[SSM_SW:, k * LANES:(k + 1) * LANES] = (-(cb_re * li1 + cb_im * lr1)).T.astype(BF16)


def _ssm_weights(a_re, a_im, log_dt, b_re, b_im, c_re, c_im, d):
    t_ = SSM_T
    dt = jnp.exp(log_dt)[:, None]
    mag = jnp.exp(a_re * dt)
    lam_re, lam_im = mag * jnp.cos(a_im * dt), mag * jnp.sin(a_im * dt)
    den = a_re * a_re + a_im * a_im
    f_re = ((lam_re - 1.0) * a_re + lam_im * a_im) / den
    f_im = (lam_im * a_re - (lam_re - 1.0) * a_im) / den
    w_re = f_re[:, :, None] * b_re - f_im[:, :, None] * b_im
    w_im = f_re[:, :, None] * b_im + f_im[:, :, None] * b_re
    k = jnp.arange(t_ + 1, dtype=F32)[None, :, None]
    pmag = jnp.exp(a_re[:, None, :] * dt[:, :, None] * k)
    pang = a_im[:, None, :] * dt[:, :, None] * k
    pw_re, pw_im = pmag * jnp.cos(pang), pmag * jnp.sin(pang)

    def lanes_gn(x):
        x = x.reshape(SSM_NJ, SSM_GPB, t_ + 1, SSM_STATE)
        return jnp.swapaxes(x, 1, 2).reshape(SSM_NJ, t_ + 1, SSM_SW)

    pl_re, pl_im = lanes_gn(pw_re), lanes_gn(pw_im)
    wcat = [jnp.swapaxes(x, 1, 2).reshape(SSM_NJ, LANES, SSM_STATE) for x in (w_re, w_im)]
    ccat = [x.reshape(SSM_NJ, LANES, SSM_STATE) for x in (c_re, c_im)]
    rep = jnp.tile(jnp.eye(SSM_STATE, dtype=BF16), (1, SSM_GPB))
    tw = t_ * LANES
    sq = 2 * SSM_SW
    blk = lambda j: (j, 0, 0)
    const = lambda j: (0, 0)
    per_j = lambda a: pl.BlockSpec((None,) + a.shape[1:], blk)
    ins = [wcat[0], wcat[1], ccat[0], ccat[1], pl_re, pl_im]
    m, p, q = pl.pallas_call(
        _ssm_fold_kernel, grid=(SSM_NJ,),
        in_specs=[per_j(a) for a in ins] + [pl.BlockSpec(rep.shape, const)],
        out_specs=[pl.BlockSpec((None, tw, tw), blk), pl.BlockSpec((None, tw, sq), blk),
                   pl.BlockSpec((None, sq, tw), blk)],
        out_shape=[jax.ShapeDtypeStruct((SSM_NJ, tw, tw), BF16), jax.ShapeDtypeStruct((SSM_NJ, tw, sq), BF16),
                   jax.ShapeDtypeStruct((SSM_NJ, sq, tw), BF16)],
        compiler_params=_cp(("arbitrary",)), name="ssm_fold",
    )(*ins, rep)
    lam_t = jnp.concatenate([pl_re[:, t_][:, None, :], pl_im[:, t_][:, None, :]], axis=2)
    dvec = jnp.tile(d.reshape(SSM_NJ, 1, LANES), (1, 1, t_))
    return {"M": m, "P": p, "Q": q, "lam": lam_t, "D": dvec}


def _mixer_kernel(o_ref, s_ref, ga_ref, gb_ref, x_ref, wo_ref, wv_ref, wg_ref, wout_ref, h_ref, s_sc):
    o_a = _dot(o_ref[...], wo_ref[...])
    nchunk = s_sc.shape[1] // SSM_T
    for j in range(SSM_NJ):
        sj = s_ref[j].astype(F32)
        for t in range(SSM_T):
            s_sc[j, pl.ds(t, nchunk, stride=SSM_T), :] = sj[:, t * LANES:(t + 1) * LANES]
    s = jnp.concatenate([s_sc[j] for j in range(SSM_NJ)], axis=1).astype(BF16)
    o_b = _dot(s, wv_ref[...]) * jax.nn.sigmoid(_dot(s, wg_ref[...]))
    merged = jax.nn.sigmoid(ga_ref[...]) * o_a + jax.nn.sigmoid(gb_ref[...]) * o_b
    h_ref[...] = x_ref[...] + _dot(merged.astype(BF16), wout_ref[...])


def _mixer(o, s, ga, gb, x, w, tm):
    n = x.shape[0]
    row = lambda i: (i, 0)
    const = lambda i: (0, 0)
    wspec = pl.BlockSpec((D_MODEL, D_MODEL), const, pipeline_mode=pl.Buffered(1))
    act = pl.BlockSpec((tm, D_MODEL), row)
    return pl.pallas_call(
        _mixer_kernel, grid=(n // tm,),
        in_specs=[act, pl.BlockSpec((SSM_NJ, tm // SSM_T, SSM_T * LANES), lambda i: (0, i, 0)), act, act, act,
                  wspec, wspec, wspec, wspec],
        out_specs=act, out_shape=jax.ShapeDtypeStruct((n, D_MODEL), F32),
        scratch_shapes=[pltpu.VMEM((SSM_NJ, tm, LANES), F32)],
        compiler_params=_cp(("arbitrary",)), name="mixer_out",
    )(o, s, ga, gb, x, w["wo"], w["wv"], w["wg"], w["wout"])


def _mlp_kernel(h_ref, g_ref, wup_ref, wdn_ref, y_ref):
    h = h_ref[...]
    hn = _rms(h, g_ref[...]).astype(BF16)
    a = jnp.maximum(_dot(hn, wup_ref[...]), 0.0)
    y_ref[...] = h + _dot((a * a).astype(BF16), wdn_ref[...])


def _mlp(h, w, tm):
    n = h.shape[0]
    row = lambda i: (i, 0)
    const = lambda i: (0, 0)
    act = pl.BlockSpec((tm, D_MODEL), row)
    return pl.pallas_call(
        _mlp_kernel, grid=(n // tm,),
        in_specs=[act, pl.BlockSpec((1, D_MODEL), const),
                  pl.BlockSpec((D_MODEL, D_FF), const, pipeline_mode=pl.Buffered(1)),
                  pl.BlockSpec((D_FF, D_MODEL), const, pipeline_mode=pl.Buffered(1))],
        out_specs=act, out_shape=jax.ShapeDtypeStruct((n, D_MODEL), F32),
        compiler_params=_cp(("arbitrary",)), name="mlp",
    )(h, w["gmlp"], w["wup"], w["wdn"])


def _cos_sin(pos):
    half = QK_ROPE // 2
    inv = ROPE_THETA ** (-jnp.arange(half, dtype=F32) / half)
    ang = pos.astype(F32)[:, None] * inv[None, :]
    return jnp.cos(ang), jnp.sin(ang)


def _rope_tables(pos=None, length=None):
    if pos is not None:
        cos, sin = _cos_sin(pos)
    else:
        half = QK_ROPE // 2
        ch, sh = _cos_sin(jnp.arange(length // CHUNK, dtype=jnp.int32) * CHUNK)
        cl, sl = _cos_sin(jnp.arange(CHUNK, dtype=jnp.int32))
        cos = (ch[:, None, :] * cl[None] - sh[:, None, :] * sl[None]).reshape(length, half)
        sin = (sh[:, None, :] * cl[None] + ch[:, None, :] * sl[None]).reshape(length, half)
    z = jnp.zeros_like(cos)
    pad = jnp.zeros((cos.shape[0], LANES - QK_ROPE), F32)
    ct = jnp.concatenate([cos, cos, pad], axis=1)
    s1 = jnp.concatenate([-sin, z, pad], axis=1)
    s2 = jnp.concatenate([z, sin, pad], axis=1)
    return ct, s1, s2


def _pad_lanes(a, width):
    return jnp.pad(a, [(0, 0)] * (a.ndim - 1) + [(0, width - a.shape[-1])])


def _layer_weights(norm_mix, w_in, q_lora_norm, w_uq, q_nope_norm, q_rope_norm, kv_lora_norm, k_rope_norm,
                   w_ukv, k_nope_norm, w_o_attn, w_glu_v, w_glu_g, w_out, norm_mlp, w_mlp_up, w_mlp_down):
    o_kv = Q_LORA
    o_kr = o_kv + KV_LORA
    o_ssm = o_kr + QK_ROPE
    bf = lambda a: a.astype(BF16)
    r2 = lambda a: a.reshape(1, -1).astype(F32)
    return {
        "nmix": r2(norm_mix),
        "win": bf(jnp.concatenate([w_in[:, :o_kr], w_in[:, o_ssm:],
                                   _pad_lanes(w_in[:, o_kr:o_ssm], LANES)], axis=1)),
        "gq": r2(q_lora_norm),
        "wuqn": bf(w_uq[:, :, :QK_NOPE].reshape(Q_LORA, N_HEADS * QK_NOPE)),
        "wuqr": bf(_pad_lanes(w_uq[:, :, QK_NOPE:], LANES).reshape(Q_LORA, N_HEADS * LANES)),
        "gqn": r2(q_nope_norm) * Q_SCALE, "gqr": _pad_lanes(r2(q_rope_norm), LANES) * Q_SCALE,
        "gkv": r2(kv_lora_norm), "gkr": _pad_lanes(r2(k_rope_norm), LANES),
        "wuk": bf(w_ukv[:, :, :QK_NOPE].reshape(KV_LORA, N_HEADS * QK_NOPE)),
        "wukg": bf((w_ukv[:, :, :QK_NOPE] * k_nope_norm.astype(F32)).reshape(KV_LORA, N_HEADS * QK_NOPE)),
        "wuv": bf(w_ukv[:, :, QK_NOPE:].reshape(KV_LORA, N_HEADS * V_DIM)),
        "gkn": r2(k_nope_norm),
        "wo": bf(w_o_attn.reshape(N_HEADS * V_DIM, D_MODEL)),
        "wv": bf(w_glu_v), "wg": bf(w_glu_g), "wout": bf(w_out),
        "gmlp": r2(norm_mlp), "wup": bf(w_mlp_up), "wdn": bf(w_mlp_down),
    }


def kernel(x_prompt, x_sample, cache_latent, cache_krope, cache_meta_latent, cache_meta_krope, state_ssm_re, state_ssm_im, meta_tokens, norm_mix, w_in, q_lora_norm, w_uq, q_nope_norm, q_rope_norm, kv_lora_norm, k_rope_norm, w_ukv, k_nope_norm, w_o_attn, ssm_a_re, ssm_a_im, ssm_log_dt, ssm_b_re, ssm_b_im, ssm_c_re, ssm_c_im, ssm_d, w_glu_v, w_glu_g, w_out, norm_mlp, w_mlp_up, w_mlp_down):
    bsz_p, seq_p = x_prompt.shape[0], x_prompt.shape[1]
    bsz_s, seq_s = x_sample.shape[0], x_sample.shape[1]
    past = cache_latent.shape[2]
    depth = w_in.shape[0]
    assert depth == 1, "single-layer step"
    l = 0
    w = _layer_weights(norm_mix[l], w_in[l], q_lora_norm[l], w_uq[l], q_nope_norm[l], q_rope_norm[l],
                       kv_lora_norm[l], k_rope_norm[l], w_ukv[l], k_nope_norm[l], w_o_attn[l], w_glu_v[l],
                       w_glu_g[l], w_out[l], norm_mlp[l], w_mlp_up[l], w_mlp_down[l])
    sw = _ssm_weights(ssm_a_re[l].astype(F32), ssm_a_im[l].astype(F32), ssm_log_dt[l].astype(F32),
                      ssm_b_re[l].astype(F32), ssm_b_im[l].astype(F32), ssm_c_re[l].astype(F32),
                      ssm_c_im[l].astype(F32), ssm_d[l].astype(F32))
    sq = 2 * SSM_SW

    tabs_m = _rope_tables(jnp.arange(N_META, dtype=jnp.int32) - N_META)
    (_, mk, mv, mckv, mkr, mu, _, _) = _front(meta_tokens.astype(F32), tabs_m, 1, w, N_META)
    h_meta = _ssm_state(mu, sw, N_META // SSM_T)

    n_p = bsz_p * seq_p
    xp = x_prompt.reshape(n_p, D_MODEL)
    tabs_p = _rope_tables(length=seq_p)
    q, k, v, ckv_p, kr_p, u_p, ga, gb = _front(xp, tabs_p, seq_p // TM_PROMPT, w, TM_PROMPT, kr_seq=seq_p)
    o_p = _attn_prompt(q, k, v, mk, mv, bsz_p, seq_p)
    h0_p = jnp.broadcast_to(h_meta, (SSM_NJ, bsz_p, sq))
    s_p, hfin_p = _ssm_prompt(u_p, sw, h0_p, bsz_p, seq_p // SSM_T, SSM_RT)
    h1_p = _mixer(o_p, s_p, ga, gb, xp, w, TM_PROMPT)
    y_p = _mlp(h1_p, w, TM_PROMPT)

    n_s = bsz_s * seq_s
    xs = x_sample.reshape(n_s, D_MODEL)
    pos_s = past + jnp.arange(seq_s, dtype=jnp.int32)
    tabs_s = tuple(jnp.tile(t, (bsz_s, 1)) for t in _rope_tables(pos_s))
    q_s, k_s, _, ckv_s, kr_s, u_s, ga_s, gb_s = _front(xs, tabs_s, n_s // TM_SAMPLE, w, TM_SAMPLE)
    o_s = _attn_sample(q_s, k_s, ckv_s, cache_latent[l].astype(F32), cache_krope[l].astype(F32),
                       cache_meta_latent[l].astype(F32), cache_meta_krope[l].astype(F32), w, bsz_s, seq_s, past)

    def to_blocks(st):
        return st.astype(F32).reshape(bsz_s, SSM_NJ, SSM_SW).transpose(1, 0, 2)

    h0_s = jnp.concatenate([to_blocks(state_ssm_re[l]), to_blocks(state_ssm_im[l])], axis=2)
    s_s, hfin_s = _ssm_sample(u_s, sw, h0_s, bsz_s, seq_s // SSM_T)
    h1_s = _mixer(o_s, s_s, ga_s, gb_s, xs, w, TM_SAMPLE)
    y_s = _mlp(h1_s, w, TM_SAMPLE)

    def from_blocks(hf, nb):
        re = hf[:, :, :SSM_SW].transpose(1, 0, 2).reshape(1, nb, N_GROUPS, SSM_STATE)
        im = hf[:, :, SSM_SW:].transpose(1, 0, 2).reshape(1, nb, N_GROUPS, SSM_STATE)
        return re, im

    sre_p, sim_p = from_blocks(hfin_p, bsz_p)
    sre_s, sim_s = from_blocks(hfin_s, bsz_s)
    return (y_p.reshape(bsz_p, seq_p, D_MODEL), y_s.reshape(bsz_s, seq_s, D_MODEL),
            ckv_p.reshape(1, bsz_p, seq_p, KV_LORA), jnp.swapaxes(kr_p, 1, 2)[None],
            jnp.broadcast_to(mckv[None, None], (1, bsz_p, N_META, KV_LORA)),
            jnp.broadcast_to(mkr[None, None], (1, bsz_p, N_META, QK_ROPE)),
            sre_p, sim_p,
            ckv_s.reshape(1, bsz_s, seq_s, KV_LORA), kr_s.reshape(1, bsz_s, seq_s, QK_ROPE),
            sre_s, sim_s)
```

```python
import functools
import math

import jax
import jax.numpy as jnp
from jax import lax
from jax.experimental import pallas as pl
from jax.experimental.pallas import tpu as pltpu

F32 = jnp.float32
BF16 = jnp.bfloat16

D_MODEL = 1024
CHUNK = 64
N_META = 16
N_HEADS = 8
QK_NOPE = 128
QK_ROPE = 64
V_DIM = 128
QK_DIM = QK_NOPE + QK_ROPE
Q_LORA = 384
KV_LORA = 256
SSM_GROUP = 16
N_GROUPS = D_MODEL // SSM_GROUP
SSM_STATE = 64
D_FF = 4 * D_MODEL
ROPE_THETA = 10000.0
EPS = 1e-6
ATTN_SCALE = QK_DIM ** -0.5
Q_SCALE = ATTN_SCALE * math.log2(math.e)
NEG_INF = -1e30

LANES = 128
HEAD_PAD = 2 * LANES
SSM_T = 8
SSM_NJ = D_MODEL // LANES
SSM_GPB = LANES // SSM_GROUP
SSM_SW = SSM_GPB * SSM_STATE
VMEM_LIMIT = 56 * 1024 * 1024
TM_PROMPT = 512
TM_SAMPLE = 512
SSM_RT = 256


def _cp(sem):
    return pltpu.CompilerParams(dimension_semantics=sem, vmem_limit_bytes=VMEM_LIMIT)


def _dot(a, b):
    return jnp.dot(a, b, preferred_element_type=F32)


def _dot_nt(a, b):
    return lax.dot_general(a, b, (((1,), (1,)), ((), ())), preferred_element_type=F32)


def _rms(x, g, n=None):
    n = x.shape[-1] if n is None else n
    ms = jnp.sum(x * x, axis=-1, keepdims=True) * (1.0 / n)
    return x * lax.rsqrt(ms + EPS) * g


def _rope128(b, ct, s1, s2):
    return b * ct + pltpu.roll(b, LANES - QK_ROPE // 2, 1) * s1 + pltpu.roll(b, QK_ROPE // 2, 1) * s2


IN_Q = 0
IN_KV = IN_Q + Q_LORA
IN_U = IN_KV + KV_LORA
IN_GA = IN_U + D_MODEL
IN_GB = IN_GA + D_MODEL
IN_KR = IN_GB + D_MODEL
IN_END = IN_KR + LANES


def _front_kernel(x_ref, ct_ref, s1_ref, s2_ref, nmix_ref, win_ref, gq_ref, wuqn_ref, wuqr_ref, gqn_ref,
                  gqr_ref, gkv_ref, gkr_ref, wuk_ref, wuv_ref, gkn_ref,
                  q_ref, k_ref, v_ref, ckv_ref, kr_ref, u_ref, ga_ref, gb_ref, u_sc, *, kr_transposed):
    x = x_ref[...]
    xn = _rms(x, nmix_ref[...]).astype(BF16)
    ct, s1, s2 = ct_ref[...], s1_ref[...], s2_ref[...]


    q_lat = _dot(xn, win_ref[:, IN_Q:IN_KV])
    kv_lat = _dot(xn, win_ref[:, IN_KV:IN_U])
    kr_raw = _dot(xn, win_ref[:, IN_KR:IN_END])

    c_q = _rms(q_lat, gq_ref[...]).astype(BF16)
    qn = _dot(c_q, wuqn_ref[...])
    qr = _dot(c_q, wuqr_ref[...])
    c_kv = _rms(kv_lat, gkv_ref[...])
    ckv_ref[...] = c_kv
    c16 = c_kv.astype(BF16)
    kn = _dot(c16, wuk_ref[...])
    v_ref[...] = _dot(c16, wuv_ref[...]).astype(BF16)

    for h in range(N_HEADS):
        sl = slice(h * LANES, (h + 1) * LANES)
        q_ref[:, h * HEAD_PAD:h * HEAD_PAD + LANES] = _rms(qn[:, sl], gqn_ref[...]).astype(BF16)
        r = _rope128(_rms(qr[:, sl], gqr_ref[...], QK_ROPE), ct, s1, s2)
        q_ref[:, h * HEAD_PAD + LANES:(h + 1) * HEAD_PAD] = r.astype(BF16)
    kr = _rms(kr_raw, gkr_ref[...], QK_ROPE)
    kr = _rope128(kr, ct, s1, s2)
    kr_ref[...] = kr.T[:QK_ROPE, :] if kr_transposed else kr[:, :QK_ROPE]
    kr16 = kr.astype(BF16)
    for h in range(N_HEADS):
        sl = slice(h * LANES, (h + 1) * LANES)
        k_ref[:, h * HEAD_PAD:h * HEAD_PAD + LANES] = _rms(kn[:, sl], gkn_ref[...]).astype(BF16)
        k_ref[:, h * HEAD_PAD + LANES:(h + 1) * HEAD_PAD] = kr16

    ga_ref[...] = _dot(xn, win_ref[:, IN_GA:IN_GB])
    u = _dot(xn, win_ref[:, IN_U:IN_GA])
    nchunk = u_sc.shape[1] // SSM_T
    for j in range(SSM_NJ):
        u_sc[j] = u[:, j * LANES:(j + 1) * LANES]
        for t in range(SSM_T):
            u_ref[j, :, t * LANES:(t + 1) * LANES] = u_sc[j, pl.ds(t, nchunk, stride=SSM_T), :]
    gb_ref[...] = _dot(xn, win_ref[:, IN_GB:IN_KR])


def _front(x, tabs, tab_blocks, w, tm, kr_seq=None):
    n = x.shape[0]
    nt = n // tm
    ct, s1, s2 = tabs

    def row(i):
        return (i, 0)

    def tab(i):
        return (i % tab_blocks, 0)

    def const(i):
        return (0, 0)

    def full(a):
        return pl.BlockSpec(a.shape, const, pipeline_mode=pl.Buffered(1))

    weights = [w["nmix"], w["win"], w["gq"], w["wuqn"], w["wuqr"], w["gqn"], w["gqr"], w["gkv"], w["gkr"],
               w["wuk"], w["wuv"], w["gkn"]]
    wide = N_HEADS * LANES
    qk_wide = N_HEADS * HEAD_PAD
    out_shape = [
        jax.ShapeDtypeStruct((n, qk_wide), BF16),
        jax.ShapeDtypeStruct((n, qk_wide), BF16),
        jax.ShapeDtypeStruct((n, wide), BF16),
        jax.ShapeDtypeStruct((n, KV_LORA), F32),
        (jax.ShapeDtypeStruct((n, QK_ROPE), F32) if kr_seq is None
         else jax.ShapeDtypeStruct((n // kr_seq, QK_ROPE, kr_seq), F32)),
        jax.ShapeDtypeStruct((SSM_NJ, n // SSM_T, SSM_T * LANES), F32),
        jax.ShapeDtypeStruct((n, D_MODEL), F32),
        jax.ShapeDtypeStruct((n, D_MODEL), F32),
    ]
    out_specs = [
        pl.BlockSpec((tm, qk_wide), row), pl.BlockSpec((tm, qk_wide), row),
        pl.BlockSpec((tm, wide), row), pl.BlockSpec((tm, KV_LORA), row),
        (pl.BlockSpec((tm, QK_ROPE), row) if kr_seq is None else
         pl.BlockSpec((None, QK_ROPE, tm), lambda i: (i // (kr_seq // tm), 0, i % (kr_seq // tm)))),
        pl.BlockSpec((SSM_NJ, tm // SSM_T, SSM_T * LANES), lambda i: (0, i, 0)),
        pl.BlockSpec((tm, D_MODEL), row), pl.BlockSpec((tm, D_MODEL), row),
    ]
    in_specs = ([pl.BlockSpec((tm, D_MODEL), row)] + [pl.BlockSpec((tm, LANES), tab)] * 3
                + [full(a) for a in weights])
    return pl.pallas_call(
        functools.partial(_front_kernel, kr_transposed=kr_seq is not None), grid=(nt,), in_specs=in_specs,
        out_specs=out_specs, out_shape=out_shape,
        scratch_shapes=[pltpu.VMEM((SSM_NJ, tm, LANES), F32)],
        compiler_params=_cp(("arbitrary",)), name="front",
    )(x, ct, s1, s2, *weights)


ATT_T = 512


def _with_ones(v):
    return jnp.concatenate([v, jnp.ones_like(v)], axis=1)


def _softmax_step(s, v, m_ref, acc_ref):
    m_prev = m_ref[...]
    m_new = jnp.maximum(m_prev, jnp.max(s, axis=-1, keepdims=True))
    alpha = jnp.exp2(m_prev - m_new)
    p = jnp.exp2(s - jnp.tile(m_new, (1, s.shape[1] // LANES)))
    acc_ref[...] = jnp.tile(alpha, (1, 2)) * acc_ref[...] + _dot(p.astype(BF16), _with_ones(v))
    m_ref[...] = m_new


ATT_HG = 4


def _attn_prompt_kernel(q_ref, k_ref, v_ref, mk_ref, mv_ref, o_ref, a_sc, b_sc, m_ref, acc_ref):
    step = pl.program_id(2)
    npairs = step
    heads = range(ATT_HG)

    def key_rows(kt):
        return pl.ds(pl.multiple_of(kt * ATT_T, ATT_T), ATT_T)

    def q_rows(slot):
        return slice(slot * ATT_T, (slot + 1) * ATT_T)

    def qk_lanes(g):
        return slice(g * HEAD_PAD, (g + 1) * HEAD_PAD)

    def v_lanes(g):
        return slice(g * V_DIM, (g + 1) * V_DIM)

    def scores(slot, g, kt):
        return _dot_nt(q_ref[q_rows(slot), qk_lanes(g)], k_ref[key_rows(kt), qk_lanes(g)])

    def consume(slot, g, s, kt, diagonal):
        if diagonal:
            qc = lax.broadcasted_iota(jnp.int32, (ATT_T, ATT_T), 0) // CHUNK
            kc = lax.broadcasted_iota(jnp.int32, (ATT_T, ATT_T), 1) // CHUNK
            s = jnp.where(kc <= qc, s, NEG_INF)
        _softmax_step(s, v_ref[key_rows(kt), v_lanes(g)], m_ref.at[slot, g], acc_ref.at[slot, g])

    def head(slot):
        for g in heads:
            a_sc[slot, g] = scores(slot, g, 0)
        for g in heads:
            s = _dot_nt(q_ref[q_rows(slot), qk_lanes(g)], mk_ref[:, qk_lanes(g)])
            m0 = jnp.max(s, axis=-1, keepdims=True)
            p = jnp.exp2(s - m0)
            m_ref[slot, g] = jnp.broadcast_to(m0, (ATT_T, LANES))
            acc_ref[slot, g] = _dot(p.astype(BF16), _with_ones(mv_ref[:, v_lanes(g)]))

    def full_pairs(slot):
        def pair(pi, carry):
            kt = 2 * pi
            for g in heads:
                b_sc[g] = scores(slot, g, kt + 1)
                consume(slot, g, a_sc[slot, g], kt, False)
            for g in heads:
                a_sc[slot, g] = scores(slot, g, kt + 2)
                consume(slot, g, b_sc[g], kt + 1, False)
            return carry

        lax.fori_loop(0, npairs, pair, 0)

    def tail(slot):
        i = 2 * step + slot
        if slot == 1:
            for g in heads:
                b_sc[g] = scores(slot, g, i)
                consume(slot, g, a_sc[slot, g], i - 1, False)
            for g in heads:
                consume(slot, g, b_sc[g], i, True)
        else:
            for g in heads:
                consume(slot, g, a_sc[slot, g], i, True)

    def finish(slot):
        for g in heads:
            o_ref[q_rows(slot), v_lanes(g)] = (acc_ref[slot, g, :, :V_DIM]
                                               / acc_ref[slot, g, :, V_DIM:]).astype(BF16)

    head(0)
    head(1)
    full_pairs(0)
    tail(0)
    finish(0)
    full_pairs(1)
    tail(1)
    finish(1)


def _attn_prompt(q, k, v, mk, mv, bsz, seq):
    nstep = seq // (2 * ATT_T)
    n = bsz * seq
    qmap = lambda b, h, i: (b * nstep + i, h)
    return pl.pallas_call(
        _attn_prompt_kernel, grid=(bsz, N_HEADS // ATT_HG, nstep),
        in_specs=[pl.BlockSpec((2 * ATT_T, ATT_HG * HEAD_PAD), qmap),
                  pl.BlockSpec((seq, ATT_HG * HEAD_PAD), lambda b, h, i: (b, h)),
                  pl.BlockSpec((seq, ATT_HG * V_DIM), lambda b, h, i: (b, h)),
                  pl.BlockSpec((N_META, ATT_HG * HEAD_PAD), lambda b, h, i: (0, h)),
                  pl.BlockSpec((N_META, ATT_HG * V_DIM), lambda b, h, i: (0, h))],
        out_specs=pl.BlockSpec((2 * ATT_T, ATT_HG * V_DIM), qmap),
        out_shape=jax.ShapeDtypeStruct((n, N_HEADS * V_DIM), BF16),
        scratch_shapes=[pltpu.VMEM((2, ATT_HG, ATT_T, ATT_T), F32), pltpu.VMEM((ATT_HG, ATT_T, ATT_T), F32),
                        pltpu.VMEM((2, ATT_HG, ATT_T, LANES), F32),
                        pltpu.VMEM((2, ATT_HG, ATT_T, 2 * V_DIM), F32)],
        compiler_params=_cp(("arbitrary", "arbitrary", "arbitrary")), name="attn_prompt",
    )(q, k, v, mk, mv)


SAMPLE_KEY_CHUNK = 256


def _attn_sample_kernel(q_ref, cm_ref, cn_ref, cc_ref, krm_ref, krn_ref, krc_ref, wuk_ref, wukg_ref, wuv_ref,
                        hsum_ref, o_ref, c_sc, kr_sc, st_sc, pt_sc, *, seq, past):
    n_small = N_META + seq
    n_keys = n_small + past

    c_sc[:, KV_LORA:] = jnp.ones((n_keys, LANES), BF16)
    c_sc[0:N_META, :KV_LORA] = cm_ref[...].astype(BF16)
    c_sc[N_META:n_small, :KV_LORA] = cn_ref[...].astype(BF16)
    c_sc[n_small:, :KV_LORA] = cc_ref[...].astype(BF16)
    kr_sc[:, QK_ROPE:] = jnp.zeros((n_small, LANES - QK_ROPE), BF16)
    kr_sc[0:N_META, :QK_ROPE] = krm_ref[...].astype(BF16)
    kr_sc[N_META:n_small, :] = krn_ref[...]

    qa, qr = [], []
    for h in range(N_HEADS):
        qn_h = q_ref[:, h * HEAD_PAD:h * HEAD_PAD + LANES]
        qa.append(_dot_nt(qn_h, wukg_ref[:, h * LANES:(h + 1) * LANES]))
        qr.append(q_ref[:, h * HEAD_PAD + LANES:(h + 1) * HEAD_PAD])
    qa = jnp.concatenate(qa, axis=0).astype(BF16)
    qr = jnp.concatenate(qr, axis=0)

    bounds = [(0, n_small)] + [(n_small + i, n_small + i + SAMPLE_KEY_CHUNK)
                               for i in range(0, past, SAMPLE_KEY_CHUNK)]
    m = None
    for lo, hi in bounds:
        c16 = c_sc[lo:hi, :KV_LORA]
        kn = _dot(c16, wuk_ref[...])
        ssq = _dot((kn * kn).astype(BF16), hsum_ref[...])
        r = lax.rsqrt(ssq * (1.0 / QK_NOPE) + EPS)
        if lo == 0:
            s_rope = _dot_nt(kr_sc[...], qr)
        else:
            kr_t = krc_ref[:, lo - n_small:hi - n_small].astype(BF16)
            s_rope = lax.dot_general(kr_t, qr[:, :QK_ROPE], (((0,), (1,)), ((), ())),
                                     preferred_element_type=F32)
        st = _dot_nt(c16, qa) * r + s_rope
        st_sc[lo:hi, :] = st
        cm = jnp.max(st, axis=0, keepdims=True)
        m = cm if m is None else jnp.maximum(m, cm)

    for lo, hi in bounds:
        pt_sc[lo:hi, :] = jnp.exp2(st_sc[lo:hi, :] - m).astype(BF16)
    pc = lax.dot_general(pt_sc[...], c_sc[...], (((0,), (0,)), ((), ())), preferred_element_type=F32)
    pcn = (pc[:, :KV_LORA] / jnp.tile(pc[:, KV_LORA:], (1, KV_LORA // LANES))).astype(BF16)
    for h in range(N_HEADS):
        o_ref[:, h * V_DIM:(h + 1) * V_DIM] = _dot(pcn[h * seq:(h + 1) * seq, :],
                                                   wuv_ref[:, h * V_DIM:(h + 1) * V_DIM]).astype(BF16)


def _attn_sample(q, k_new, c_new, cache_c, cache_kr, meta_c, meta_kr, w, bsz, seq, past):
    n_keys = N_META + seq + past
    hq = N_HEADS * seq
    hsum = jnp.repeat(jnp.repeat(jnp.eye(N_HEADS, dtype=BF16), QK_NOPE, axis=0), seq, axis=1)
    row = lambda b: (b, 0)
    b3 = lambda b: (b, 0, 0)
    const = lambda b: (0, 0)
    kern = functools.partial(_attn_sample_kernel, seq=seq, past=past)
    return pl.pallas_call(
        kern, grid=(bsz,),
        in_specs=[pl.BlockSpec((seq, N_HEADS * HEAD_PAD), row),
                  pl.BlockSpec((None, N_META, KV_LORA), b3), pl.BlockSpec((seq, KV_LORA), row),
                  pl.BlockSpec((None, past, KV_LORA), b3),
                  pl.BlockSpec((None, N_META, QK_ROPE), b3), pl.BlockSpec((seq, LANES), lambda b: (b, 1)),
                  pl.BlockSpec((None, QK_ROPE, past), b3),
                  pl.BlockSpec(w["wuk"].shape, const), pl.BlockSpec(w["wukg"].shape, const),
                  pl.BlockSpec(w["wuv"].shape, const), pl.BlockSpec(hsum.shape, const)],
        out_specs=pl.BlockSpec((seq, N_HEADS * V_DIM), row),
        out_shape=jax.ShapeDtypeStruct((bsz * seq, N_HEADS * V_DIM), BF16),
        scratch_shapes=[pltpu.VMEM((n_keys, KV_LORA + LANES), BF16), pltpu.VMEM((N_META + seq, LANES), BF16),
                        pltpu.VMEM((n_keys, hq), F32), pltpu.VMEM((n_keys, hq), BF16)],
        compiler_params=_cp(("arbitrary",)), name="attn_sample",
    )(q, meta_c, c_new, cache_c, meta_kr, k_new, jnp.swapaxes(cache_kr, 1, 2), w["wuk"], w["wukg"], w["wuv"],
      hsum)


MXU_DIM = 256


def _dot_causal(u16, m_ref):
    nblk = m_ref.shape[0] // MXU_DIM
    return jnp.concatenate(
        [_dot(u16[:, :(cb + 1) * MXU_DIM], m_ref[:(cb + 1) * MXU_DIM, cb * MXU_DIM:(cb + 1) * MXU_DIM])
         for cb in range(nblk)], axis=1)


SSM_PACK = 4
SSM_SC = SSM_SW // LANES
SUBLANES = 2 * SSM_PACK


def _ssm_prompt_kernel(u_ref, m_ref, p_ref, q_ref, lam_ref, d_ref, h0_ref, s_ref, hfin_ref, s_sc, hp_sc, h_sc, y_sc,
                       *, nb, rt):
    t = pl.program_id(1)

    @pl.when(t == 0)
    def _():
        h_sc[...] = h0_ref[...]

    for b in range(nb):
        sb = _dot(u_ref[b].astype(BF16), p_ref[...])
        for c in range(SSM_SC):
            s_sc[c, pl.ds(b, rt, stride=SUBLANES), :] = sb[:, c * LANES:(c + 1) * LANES]
            s_sc[c, pl.ds(SSM_PACK + b, rt, stride=SUBLANES), :] = sb[:, SSM_SW + c * LANES:SSM_SW + (c + 1) * LANES]
    for b in range(nb):
        ub = u_ref[b]
        y_sc[b] = _dot_causal(ub.astype(BF16), m_ref) + ub * d_ref[...]

    lam = lam_ref[...]
    im_rows = lax.broadcasted_iota(jnp.int32, (SUBLANES, LANES), 0) >= SSM_PACK
    coef = []
    for c in range(SSM_SC):
        lr = jnp.broadcast_to(lam[:, c * LANES:(c + 1) * LANES], (SUBLANES, LANES))
        li = jnp.broadcast_to(lam[:, SSM_SW + c * LANES:SSM_SW + (c + 1) * LANES], (SUBLANES, LANES))
        coef.append((lr, jnp.where(im_rows, li, -li)))

    hs = [h_sc[c] for c in range(SSM_SC)]
    for r in range(rt):
        rows = slice(r * SUBLANES, (r + 1) * SUBLANES)
        for c in range(SSM_SC):
            hp_sc[c, rows, :] = hs[c]
            a, bb = coef[c]
            hs[c] = a * hs[c] + bb * pltpu.roll(hs[c], SSM_PACK, 0) + s_sc[c, rows, :]
    for c in range(SSM_SC):
        h_sc[c] = hs[c]

    for b in range(nb):
        hb = jnp.concatenate([hp_sc[c, pl.ds(b, rt, stride=SUBLANES), :] for c in range(SSM_SC)]
                             + [hp_sc[c, pl.ds(SSM_PACK + b, rt, stride=SUBLANES), :] for c in range(SSM_SC)],
                             axis=1)
        y = y_sc[b] + _dot(hb.astype(BF16), q_ref[...])
        s_ref[b] = jax.nn.gelu(y).astype(BF16)

    @pl.when(t == pl.num_programs(1) - 1)
    def _():
        hfin_ref[...] = h_sc[...]


def _pack_state(h):
    nj, nb, _ = h.shape
    h = jnp.pad(h, ((0, 0), (0, SSM_PACK - nb), (0, 0)))
    re = h[:, :, :SSM_SW].reshape(nj, SSM_PACK, SSM_SC, LANES)
    im = h[:, :, SSM_SW:].reshape(nj, SSM_PACK, SSM_SC, LANES)
    return jnp.concatenate([re, im], axis=1).transpose(0, 2, 1, 3)


def _unpack_state(hp, nb):
    nj = hp.shape[0]
    x = hp.transpose(0, 2, 1, 3).reshape(nj, SUBLANES, SSM_SW)
    return jnp.concatenate([x[:, :nb], x[:, SSM_PACK:SSM_PACK + nb]], axis=2)


def _ssm_prompt(u, sw, h0, nb, lc, rt):
    assert nb == SSM_PACK and lc % rt == 0
    tw = SSM_T * LANES
    u4 = u.reshape(SSM_NJ, nb, lc, tw)
    sq = 2 * SSM_SW
    kern = functools.partial(_ssm_prompt_kernel, nb=nb, rt=rt)
    st_spec = pl.BlockSpec((None, SSM_SC, SUBLANES, LANES), lambda j, t: (j, 0, 0, 0))
    s, hfin = pl.pallas_call(
        kern, grid=(SSM_NJ, lc // rt),
        in_specs=[pl.BlockSpec((None, nb, rt, tw), lambda j, t: (j, 0, t, 0)),
                  pl.BlockSpec((None, tw, tw), lambda j, t: (j, 0, 0)),
                  pl.BlockSpec((None, tw, sq), lambda j, t: (j, 0, 0)),
                  pl.BlockSpec((None, sq, tw), lambda j, t: (j, 0, 0)),
                  pl.BlockSpec((None, 1, sq), lambda j, t: (j, 0, 0)),
                  pl.BlockSpec((None, 1, tw), lambda j, t: (j, 0, 0)),
                  st_spec],
        out_specs=[pl.BlockSpec((None, nb, rt, tw), lambda j, t: (j, 0, t, 0)), st_spec],
        out_shape=[jax.ShapeDtypeStruct((SSM_NJ, nb, lc, tw), BF16),
                   jax.ShapeDtypeStruct((SSM_NJ, SSM_SC, SUBLANES, LANES), F32)],
        scratch_shapes=[pltpu.VMEM((SSM_SC, rt * SUBLANES, LANES), F32),
                        pltpu.VMEM((SSM_SC, rt * SUBLANES, LANES), F32),
                        pltpu.VMEM((SSM_SC, SUBLANES, LANES), F32),
                        pltpu.VMEM((nb, rt, tw), F32)],
        compiler_params=_cp(("arbitrary", "arbitrary")), name="ssm_prompt",
    )(u4, sw["M"], sw["P"], sw["Q"], sw["lam"], sw["D"], _pack_state(h0))
    return s.reshape(SSM_NJ, nb * lc, tw), _unpack_state(hfin, nb)


def _ssm_state_kernel(u_ref, p_ref, lam_ref, h_ref):
    s = _dot(u_ref[...].astype(BF16), p_ref[...])
    lam = lam_ref[...]
    lr, li = lam[:, :SSM_SW], lam[:, SSM_SW:]
    h = s[0:1, :]
    for r in range(1, s.shape[0]):
        hre, him = h[:, :SSM_SW], h[:, SSM_SW:]
        h = jnp.concatenate([lr * hre - li * him, lr * him + li * hre], axis=1) + s[r:r + 1, :]
    h_ref[...] = h


def _ssm_state(u, sw, lc):
    tw = SSM_T * LANES
    sq = 2 * SSM_SW
    blk = lambda j: (j, 0, 0)
    return pl.pallas_call(
        _ssm_state_kernel, grid=(SSM_NJ,),
        in_specs=[pl.BlockSpec((None, lc, tw), blk), pl.BlockSpec((None, tw, sq), blk),
                  pl.BlockSpec((None, 1, sq), blk)],
        out_specs=pl.BlockSpec((None, 1, sq), blk),
        out_shape=jax.ShapeDtypeStruct((SSM_NJ, 1, sq), F32),
        compiler_params=_cp(("arbitrary",)), name="ssm_state",
    )(u, sw["P"], sw["lam"])


def _ssm_sample_kernel(u_ref, m_ref, p_ref, q_ref, lam_ref, d_ref, h0_ref, s_ref, hfin_ref, s_sc,
                       *, nb, lc):
    u = u_ref[...]
    u16 = u.astype(BF16)
    s_sc[...] = _dot(u16, p_ref[...])
    lam = lam_ref[...]
    lr, li = lam[:, :SSM_SW], lam[:, SSM_SW:]
    h = h0_ref[...]
    for r in range(lc):
        rows = slice(r * nb, (r + 1) * nb)
        s = s_sc[rows, :]
        s_sc[rows, :] = h
        hre, him = h[:, :SSM_SW], h[:, SSM_SW:]
        h = jnp.concatenate([lr * hre - li * him + s[:, :SSM_SW], lr * him + li * hre + s[:, SSM_SW:]], axis=1)
    hfin_ref[...] = h
    y = _dot_causal(u16, m_ref) + _dot(s_sc[...].astype(BF16), q_ref[...]) + u * d_ref[...]
    s_ref[...] = jax.nn.gelu(y).astype(BF16)


def _ssm_sample(u, sw, h0, nb, lc):
    tw = SSM_T * LANES
    sq = 2 * SSM_SW
    u2 = u.reshape(SSM_NJ, nb, lc, tw).transpose(0, 2, 1, 3).reshape(SSM_NJ, lc * nb, tw)
    kern = functools.partial(_ssm_sample_kernel, nb=nb, lc=lc)
    blk = lambda j: (j, 0, 0)
    s, hfin = pl.pallas_call(
        kern, grid=(SSM_NJ,),
        in_specs=[pl.BlockSpec((None, lc * nb, tw), blk), pl.BlockSpec((None, tw, tw), blk),
                  pl.BlockSpec((None, tw, sq), blk), pl.BlockSpec((None, sq, tw), blk),
                  pl.BlockSpec((None, 1, sq), blk), pl.BlockSpec((None, 1, tw), blk),
                  pl.BlockSpec((None, nb, sq), blk)],
        out_specs=[pl.BlockSpec((None, lc * nb, tw), blk), pl.BlockSpec((None, nb, sq), blk)],
        out_shape=[jax.ShapeDtypeStruct((SSM_NJ, lc * nb, tw), BF16),
                   jax.ShapeDtypeStruct((SSM_NJ, nb, sq), F32)],
        scratch_shapes=[pltpu.VMEM((lc * nb, sq), F32)],
        compiler_params=_cp(("arbitrary",)), name="ssm_sample",
    )(u2, sw["M"], sw["P"], sw["Q"], sw["lam"], sw["D"], h0)
    s = s.reshape(SSM_NJ, lc, nb, tw).transpose(0, 2, 1, 3).reshape(SSM_NJ, nb * lc, tw)
    return s, hfin


def _ssm_fold_kernel(wre_ref, wim_ref, cre_ref, cim_ref, plr_ref, pli_ref, rep_ref, m_ref, p_ref, q_ref):
    def split(a, terms):
        out = []
        for _ in range(terms):
            t = a.astype(BF16)
            out.append(t)
            a = a - t.astype(F32)
        return out

    def repeat_lanes(a, rep):
        return sum(_dot(t, rep) for t in split(a, 3))

    def dot_nt_hi_lo(a, b_parts):
        a_hi, a_lo = split(a, 2)
        b_hi, b_lo = b_parts
        return _dot_nt(a_hi, b_hi) + (_dot_nt(a_hi, b_lo) + _dot_nt(a_lo, b_hi))

    same_group = (lax.broadcasted_iota(jnp.int32, (LANES, SSM_SW), 0) // SSM_GROUP
                  == lax.broadcasted_iota(jnp.int32, (LANES, SSM_SW), 1) // SSM_STATE)
    rep = rep_ref[...]
    wb_re = jnp.where(same_group, repeat_lanes(wre_ref[...], rep), 0.0)
    wb_im = jnp.where(same_group, repeat_lanes(wim_ref[...], rep), 0.0)
    cb_re = jnp.where(same_group, repeat_lanes(cre_ref[...], rep), 0.0)
    cb_im = jnp.where(same_group, repeat_lanes(cim_ref[...], rep), 0.0)
    cb_re_parts, cb_im_parts = split(cb_re, 2), split(cb_im, 2)

    m_ref[...] = jnp.zeros_like(m_ref)
    for k in range(SSM_T):
        lr, li = plr_ref[k:k + 1, :], pli_ref[k:k + 1, :]
        v_re = lr * wb_re - li * wb_im
        v_im = lr * wb_im + li * wb_re
        s = SSM_T - 1 - k
        p_ref[s * LANES:(s + 1) * LANES, :SSM_SW] = v_re.astype(BF16)
        p_ref[s * LANES:(s + 1) * LANES, SSM_SW:] = v_im.astype(BF16)
        bd = (dot_nt_hi_lo(v_re, cb_re_parts) - dot_nt_hi_lo(v_im, cb_im_parts)).astype(BF16)
        for s in range(SSM_T - k):
            t = s + k
            m_ref[s * LANES:(s + 1) * LANES, t * LANES:(t + 1) * LANES] = bd
        lr1, li1 = plr_ref[k + 1:k + 2, :], pli_ref[k + 1:k + 2, :]
        q_ref[:SSM_SW, k * LANES:(k + 1) * LANES] = (cb_re * lr1 - cb_im * li1).T.astype(BF16)
        q_ref[SSM_SW:, k * LANES:(k + 1) * LANES] = (-(cb_re * li1 + cb_im * lr1)).T.astype(BF16)


def _ssm_weights(a_re, a_im, log_dt, b_re, b_im, c_re, c_im, d):
    t_ = SSM_T
    dt = jnp.exp(log_dt)[:, None]
    mag = jnp.exp(a_re * dt)
    lam_re, lam_im = mag * jnp.cos(a_im * dt), mag * jnp.sin(a_im * dt)
    den = a_re * a_re + a_im * a_im
    f_re = ((lam_re - 1.0) * a_re + lam_im * a_im) / den
    f_im = (lam_im * a_re - (lam_re - 1.0) * a_im) / den
    w_re = f_re[:, :, None] * b_re - f_im[:, :, None] * b_im
    w_im = f_re[:, :, None] * b_im + f_im[:, :, None] * b_re
    k = jnp.arange(t_ + 1, dtype=F32)[None, :, None]
    pmag = jnp.exp(a_re[:, None, :] * dt[:, :, None] * k)
    pang = a_im[:, None, :] * dt[:, :, None] * k
    pw_re, pw_im = pmag * jnp.cos(pang), pmag * jnp.sin(pang)

    def lanes_gn(x):
        x = x.reshape(SSM_NJ, SSM_GPB, t_ + 1, SSM_STATE)
        return jnp.swapaxes(x, 1, 2).reshape(SSM_NJ, t_ + 1, SSM_SW)

    pl_re, pl_im = lanes_gn(pw_re), lanes_gn(pw_im)
    wcat = [jnp.swapaxes(x, 1, 2).reshape(SSM_NJ, LANES, SSM_STATE) for x in (w_re, w_im)]
    ccat = [x.reshape(SSM_NJ, LANES, SSM_STATE) for x in (c_re, c_im)]
    rep = jnp.tile(jnp.eye(SSM_STATE, dtype=BF16), (1, SSM_GPB))
    tw = t_ * LANES
    sq = 2 * SSM_SW
    blk = lambda j: (j, 0, 0)
    const = lambda j: (0, 0)
    per_j = lambda a: pl.BlockSpec((None,) + a.shape[1:], blk)
    ins = [wcat[0], wcat[1], ccat[0], ccat[1], pl_re, pl_im]
    m, p, q = pl.pallas_call(
        _ssm_fold_kernel, grid=(SSM_NJ,),
        in_specs=[per_j(a) for a in ins] + [pl.BlockSpec(rep.shape, const)],
        out_specs=[pl.BlockSpec((None, tw, tw), blk), pl.BlockSpec((None, tw, sq), blk),
                   pl.BlockSpec((None, sq, tw), blk)],
        out_shape=[jax.ShapeDtypeStruct((SSM_NJ, tw, tw), BF16), jax.ShapeDtypeStruct((SSM_NJ, tw, sq), BF16),
                   jax.ShapeDtypeStruct((SSM_NJ, sq, tw), BF16)],
        compiler_params=_cp(("arbitrary",)), name="ssm_fold",
    )(*ins, rep)
    lam_t = jnp.concatenate([pl_re[:, t_][:, None, :], pl_im[:, t_][:, None, :]], axis=2)
    dvec = jnp.tile(d.reshape(SSM_NJ, 1, LANES), (1, 1, t_))
    return {"M": m, "P": p, "Q": q, "lam": lam_t, "D": dvec}


def _mixer_kernel(o_ref, s_ref, ga_ref, gb_ref, x_ref, wo_ref, wv_ref, wg_ref, wout_ref, h_ref, s_sc):
    o_a = _dot(o_ref[...], wo_ref[...])
    nchunk = s_sc.shape[1] // SSM_T
    for j in range(SSM_NJ):
        sj = s_ref[j].astype(F32)
        for t in range(SSM_T):
            s_sc[j, pl.ds(t, nchunk, stride=SSM_T), :] = sj[:, t * LANES:(t + 1) * LANES]
    s = jnp.concatenate([s_sc[j] for j in range(SSM_NJ)], axis=1).astype(BF16)
    o_b = _dot(s, wv_ref[...]) * jax.nn.sigmoid(_dot(s, wg_ref[...]))
    merged = jax.nn.sigmoid(ga_ref[...]) * o_a + jax.nn.sigmoid(gb_ref[...]) * o_b
    h_ref[...] = x_ref[...] + _dot(merged.astype(BF16), wout_ref[...])


def _mixer(o, s, ga, gb, x, w, tm):
    n = x.shape[0]
    row = lambda i: (i, 0)
    const = lambda i: (0, 0)
    wspec = pl.BlockSpec((D_MODEL, D_MODEL), const, pipeline_mode=pl.Buffered(1))
    act = pl.BlockSpec((tm, D_MODEL), row)
    return pl.pallas_call(
        _mixer_kernel, grid=(n // tm,),
        in_specs=[act, pl.BlockSpec((SSM_NJ, tm // SSM_T, SSM_T * LANES), lambda i: (0, i, 0)), act, act, act,
                  wspec, wspec, wspec, wspec],
        out_specs=act, out_shape=jax.ShapeDtypeStruct((n, D_MODEL), F32),
        scratch_shapes=[pltpu.VMEM((SSM_NJ, tm, LANES), F32)],
        compiler_params=_cp(("arbitrary",)), name="mixer_out",
    )(o, s, ga, gb, x, w["wo"], w["wv"], w["wg"], w["wout"])


def _mlp_kernel(h_ref, g_ref, wup_ref, wdn_ref, y_ref):
    h = h_ref[...]
    hn = _rms(h, g_ref[...]).astype(BF16)
    a = jnp.maximum(_dot(hn, wup_ref[...]), 0.0)
    y_ref[...] = h + _dot((a * a).astype(BF16), wdn_ref[...])


def _mlp(h, w, tm):
    n = h.shape[0]
    row = lambda i: (i, 0)
    const = lambda i: (0, 0)
    act = pl.BlockSpec((tm, D_MODEL), row)
    return pl.pallas_call(
        _mlp_kernel, grid=(n // tm,),
        in_specs=[act, pl.BlockSpec((1, D_MODEL), const),
                  pl.BlockSpec((D_MODEL, D_FF), const, pipeline_mode=pl.Buffered(1)),
                  pl.BlockSpec((D_FF, D_MODEL), const, pipeline_mode=pl.Buffered(1))],
        out_specs=act, out_shape=jax.ShapeDtypeStruct((n, D_MODEL), F32),
        compiler_params=_cp(("arbitrary",)), name="mlp",
    )(h, w["gmlp"], w["wup"], w["wdn"])


def _cos_sin(pos):
    half = QK_ROPE // 2
    inv = ROPE_THETA ** (-jnp.arange(half, dtype=F32) / half)
    ang = pos.astype(F32)[:, None] * inv[None, :]
    return jnp.cos(ang), jnp.sin(ang)


def _rope_tables(pos=None, length=None):
    if pos is not None:
        cos, sin = _cos_sin(pos)
    else:
        half = QK_ROPE // 2
        ch, sh = _cos_sin(jnp.arange(length // CHUNK, dtype=jnp.int32) * CHUNK)
        cl, sl = _cos_sin(jnp.arange(CHUNK, dtype=jnp.int32))
        cos = (ch[:, None, :] * cl[None] - sh[:, None, :] * sl[None]).reshape(length, half)
        sin = (sh[:, None, :] * cl[None] + ch[:, None, :] * sl[None]).reshape(length, half)
    z = jnp.zeros_like(cos)
    pad = jnp.zeros((cos.shape[0], LANES - QK_ROPE), F32)
    ct = jnp.concatenate([cos, cos, pad], axis=1)
    s1 = jnp.concatenate([-sin, z, pad], axis=1)
    s2 = jnp.concatenate([z, sin, pad], axis=1)
    return ct, s1, s2


def _pad_lanes(a, width):
    return jnp.pad(a, [(0, 0)] * (a.ndim - 1) + [(0, width - a.shape[-1])])


def _layer_weights(norm_mix, w_in, q_lora_norm, w_uq, q_nope_norm, q_rope_norm, kv_lora_norm, k_rope_norm,
                   w_ukv, k_nope_norm, w_o_attn, w_glu_v, w_glu_g, w_out, norm_mlp, w_mlp_up, w_mlp_down):
    o_kv = Q_LORA
    o_kr = o_kv + KV_LORA
    o_ssm = o_kr + QK_ROPE
    bf = lambda a: a.astype(BF16)
    r2 = lambda a: a.reshape(1, -1).astype(F32)
    return {
        "nmix": r2(norm_mix),
        "win": bf(jnp.concatenate([w_in[:, :o_kr], w_in[:, o_ssm:],
                                   _pad_lanes(w_in[:, o_kr:o_ssm], LANES)], axis=1)),
        "gq": r2(q_lora_norm),
        "wuqn": bf(w_uq[:, :, :QK_NOPE].reshape(Q_LORA, N_HEADS * QK_NOPE)),
        "wuqr": bf(_pad_lanes(w_uq[:, :, QK_NOPE:], LANES).reshape(Q_LORA, N_HEADS * LANES)),
        "gqn": r2(q_nope_norm) * Q_SCALE, "gqr": _pad_lanes(r2(q_rope_norm), LANES) * Q_SCALE,
        "gkv": r2(kv_lora_norm), "gkr": _pad_lanes(r2(k_rope_norm), LANES),
        "wuk": bf(w_ukv[:, :, :QK_NOPE].reshape(KV_LORA, N_HEADS * QK_NOPE)),
        "wukg": bf((w_ukv[:, :, :QK_NOPE] * k_nope_norm.astype(F32)).reshape(KV_LORA, N_HEADS * QK_NOPE)),
        "wuv": bf(w_ukv[:, :, QK_NOPE:].reshape(KV_LORA, N_HEADS * V_DIM)),
        "gkn": r2(k_nope_norm),
        "wo": bf(w_o_attn.reshape(N_HEADS * V_DIM, D_MODEL)),
        "wv": bf(w_glu_v), "wg": bf(w_glu_g), "wout": bf(w_out),
        "gmlp": r2(norm_mlp), "wup": bf(w_mlp_up), "wdn": bf(w_mlp_down),
    }


def kernel(x_prompt, x_sample, cache_latent, cache_krope, cache_meta_latent, cache_meta_krope, state_ssm_re, state_ssm_im, meta_tokens, norm_mix, w_in, q_lora_norm, w_uq, q_nope_norm, q_rope_norm, kv_lora_norm, k_rope_norm, w_ukv, k_nope_norm, w_o_attn, ssm_a_re, ssm_a_im, ssm_log_dt, ssm_b_re, ssm_b_im, ssm_c_re, ssm_c_im, ssm_d, w_glu_v, w_glu_g, w_out, norm_mlp, w_mlp_up, w_mlp_down):
    bsz_p, seq_p = x_prompt.shape[0], x_prompt.shape[1]
    bsz_s, seq_s = x_sample.shape[0], x_sample.shape[1]
    past = cache_latent.shape[2]
    depth = w_in.shape[0]
    assert depth == 1, "single-layer step"
    l = 0
    w = _layer_weights(norm_mix[l], w_in[l], q_lora_norm[l], w_uq[l], q_nope_norm[l], q_rope_norm[l],
                       kv_lora_norm[l], k_rope_norm[l], w_ukv[l], k_nope_norm[l], w_o_attn[l], w_glu_v[l],
                       w_glu_g[l], w_out[l], norm_mlp[l], w_mlp_up[l], w_mlp_down[l])
    sw = _ssm_weights(ssm_a_re[l].astype(F32), ssm_a_im[l].astype(F32), ssm_log_dt[l].astype(F32),
                      ssm_b_re[l].astype(F32), ssm_b_im[l].astype(F32), ssm_c_re[l].astype(F32),
                      ssm_c_im[l].astype(F32), ssm_d[l].astype(F32))
    sq = 2 * SSM_SW

    tabs_m = _rope_tables(jnp.arange(N_META, dtype=jnp.int32) - N_META)
    (_, mk, mv, mckv, mkr, mu, _, _) = _front(meta_tokens.astype(F32), tabs_m, 1, w, N_META)
    h_meta = _ssm_state(mu, sw, N_META // SSM_T)

    n_p = bsz_p * seq_p
    xp = x_prompt.reshape(n_p, D_MODEL)
    tabs_p = _rope_tables(length=seq_p)
    q, k, v, ckv_p, kr_p, u_p, ga, gb = _front(xp, tabs_p, seq_p // TM_PROMPT, w, TM_PROMPT, kr_seq=seq_p)
    o_p = _attn_prompt(q, k, v, mk, mv, bsz_p, seq_p)
    h0_p = jnp.broadcast_to(h_meta, (SSM_NJ, bsz_p, sq))
    s_p, hfin_p = _ssm_prompt(u_p, sw, h0_p, bsz_p, seq_p // SSM_T, SSM_RT)
    h1_p = _mixer(o_p, s_p, ga, gb, xp, w, TM_PROMPT)
    y_p = _mlp(h1_p, w, TM_PROMPT)

    n_s = bsz_s * seq_s
    xs = x_sample.reshape(n_s, D_MODEL)
    pos_s = past + jnp.arange(seq_s, dtype=jnp.int32)
    tabs_s = tuple(jnp.tile(t, (bsz_s, 1)) for t in _rope_tables(pos_s))
    q_s, k_s, _, ckv_s, kr_s, u_s, ga_s, gb_s = _front(xs, tabs_s, n_s // TM_SAMPLE, w, TM_SAMPLE)
    o_s = _attn_sample(q_s, k_s, ckv_s, cache_latent[l].astype(F32), cache_krope[l].astype(F32),
                       cache_meta_latent[l].astype(F32), cache_meta_krope[l].astype(F32), w, bsz_s, seq_s, past)

    def to_blocks(st):
        return st.astype(F32).reshape(bsz_s, SSM_NJ, SSM_SW).transpose(1, 0, 2)

    h0_s = jnp.concatenate([to_blocks(state_ssm_re[l]), to_blocks(state_ssm_im[l])], axis=2)
    s_s, hfin_s = _ssm_sample(u_s, sw, h0_s, bsz_s, seq_s // SSM_T)
    h1_s = _mixer(o_s, s_s, ga_s, gb_s, xs, w, TM_SAMPLE)
    y_s = _mlp(h1_s, w, TM_SAMPLE)

    def from_blocks(hf, nb):
        re = hf[:, :, :SSM_SW].transpose(1, 0, 2).reshape(1, nb, N_GROUPS, SSM_STATE)
        im = hf[:, :, SSM_SW:].transpose(1, 0, 2).reshape(1, nb, N_GROUPS, SSM_STATE)
        return re, im

    sre_p, sim_p = from_blocks(hfin_p, bsz_p)
    sre_s, sim_s = from_blocks(hfin_s, bsz_s)
    return (y_p.reshape(bsz_p, seq_p, D_MODEL), y_s.reshape(bsz_s, seq_s, D_MODEL),
            ckv_p.reshape(1, bsz_p, seq_p, KV_LORA), jnp.swapaxes(kr_p, 1, 2)[None],
            jnp.broadcast_to(mckv[None, None], (1, bsz_p, N_META, KV_LORA)),
            jnp.broadcast_to(mkr[None, None], (1, bsz_p, N_META, QK_ROPE)),
            sre_p, sim_p,
            ckv_s.reshape(1, bsz_s, seq_s, KV_LORA), kr_s.reshape(1, bsz_s, seq_s, QK_ROPE),
            sre_s, sim_s)
```

```python
import functools
import math

import jax
import jax.numpy as jnp
from jax import lax
from jax.experimental import pallas as pl
from jax.experimental.pallas import tpu as pltpu

F32 = jnp.float32
BF16 = jnp.bfloat16

D_MODEL = 1024
CHUNK = 64
N_META = 16
N_HEADS = 8
QK_NOPE = 128
QK_ROPE = 64
V_DIM = 128
QK_DIM = QK_NOPE + QK_ROPE
Q_LORA = 384
KV_LORA = 256
SSM_GROUP = 16
N_GROUPS = D_MODEL // SSM_GROUP
SSM_STATE = 64
D_FF = 4 * D_MODEL
ROPE_THETA = 10000.0
EPS = 1e-6
ATTN_SCALE = QK_DIM ** -0.5
Q_SCALE = ATTN_SCALE * math.log2(math.e)
NEG_INF = -1e30

LANES = 128
HEAD_PAD = 2 * LANES
SSM_T = 8
SSM_NJ = D_MODEL // LANES
SSM_GPB = LANES // SSM_GROUP
SSM_SW = SSM_GPB * SSM_STATE
VMEM_LIMIT = 56 * 1024 * 1024
TM_PROMPT = 512
TM_SAMPLE = 512
SSM_RT = 256


def _cp(sem):
    return pltpu.CompilerParams(dimension_semantics=sem, vmem_limit_bytes=VMEM_LIMIT)


def _dot(a, b):
    return jnp.dot(a, b, preferred_element_type=F32)


def _dot_nt(a, b):
    return lax.dot_general(a, b, (((1,), (1,)), ((), ())), preferred_element_type=F32)


def _rms(x, g, n=None):
    n = x.shape[-1] if n is None else n
    ms = jnp.sum(x * x, axis=-1, keepdims=True) * (1.0 / n)
    return x * lax.rsqrt(ms + EPS) * g


def _rope128(b, ct, s1, s2):
    return b * ct + pltpu.roll(b, LANES - QK_ROPE // 2, 1) * s1 + pltpu.roll(b, QK_ROPE // 2, 1) * s2


IN_Q = 0
IN_KV = IN_Q + Q_LORA
IN_KR = IN_KV + KV_LORA
IN_U = IN_KR + QK_ROPE
IN_GA = IN_U + D_MODEL
IN_GB = IN_GA + D_MODEL
IN_END = IN_GB + D_MODEL


def _front_kernel(x_ref, ct_ref, s1_ref, s2_ref, nmix_ref, win_ref, gq_ref, wuqn_ref, wuqr_ref, gqn_ref,
                  gqr_ref, gkv_ref, gkr_ref, wuk_ref, wuv_ref, gkn_ref,
                  q_ref, k_ref, v_ref, ckv_ref, kr_ref, u_ref, ga_ref, gb_ref, u_sc, *, kr_transposed):
    x = x_ref[...]
    xn = _rms(x, nmix_ref[...]).astype(BF16)
    ct, s1, s2 = ct_ref[...], s1_ref[...], s2_ref[...]


    q_lat = _dot_nt(xn, win_ref[IN_Q:IN_KV, :])
    kv_lat = _dot_nt(xn, win_ref[IN_KV:IN_KR, :])
    kr_raw = _dot_nt(xn, win_ref[IN_KR:IN_KR + LANES, :])
    kr_raw = jnp.where(lax.broadcasted_iota(jnp.int32, kr_raw.shape, 1) < QK_ROPE, kr_raw, 0.0)

    c_q = _rms(q_lat, gq_ref[...]).astype(BF16)
    qn = _dot(c_q, wuqn_ref[...])
    qr = _dot(c_q, wuqr_ref[...])
    c_kv = _rms(kv_lat, gkv_ref[...])
    ckv_ref[...] = c_kv
    c16 = c_kv.astype(BF16)
    kn = _dot(c16, wuk_ref[...])
    v_ref[...] = _dot(c16, wuv_ref[...]).astype(BF16)

    for h in range(N_HEADS):
        sl = slice(h * LANES, (h + 1) * LANES)
        q_ref[:, h * HEAD_PAD:h * HEAD_PAD + LANES] = _rms(qn[:, sl], gqn_ref[...]).astype(BF16)
        r = _rope128(_rms(qr[:, sl], gqr_ref[...], QK_ROPE), ct, s1, s2)
        q_ref[:, h * HEAD_PAD + LANES:(h + 1) * HEAD_PAD] = r.astype(BF16)
    kr = _rms(kr_raw, gkr_ref[...], QK_ROPE)
    kr = _rope128(kr, ct, s1, s2)
    kr_ref[...] = kr.T[:QK_ROPE, :] if kr_transposed else kr[:, :QK_ROPE]
    kr16 = kr.astype(BF16)
    for h in range(N_HEADS):
        sl = slice(h * LANES, (h + 1) * LANES)
        k_ref[:, h * HEAD_PAD:h * HEAD_PAD + LANES] = _rms(kn[:, sl], gkn_ref[...]).astype(BF16)
        k_ref[:, h * HEAD_PAD + LANES:(h + 1) * HEAD_PAD] = kr16

    ga_ref[...] = _dot_nt(xn, win_ref[IN_GA:IN_GB, :])
    u = _dot_nt(xn, win_ref[IN_U:IN_GA, :])
    nchunk = u_sc.shape[1] // SSM_T
    for j in range(SSM_NJ):
        u_sc[j] = u[:, j * LANES:(j + 1) * LANES]
        for t in range(SSM_T):
            u_ref[j, :, t * LANES:(t + 1) * LANES] = u_sc[j, pl.ds(t, nchunk, stride=SSM_T), :]
    gb_ref[...] = _dot_nt(xn, win_ref[IN_GB:IN_END, :])


def _front(x, tabs, tab_blocks, w, tm, kr_seq=None):
    n = x.shape[0]
    nt = n // tm
    ct, s1, s2 = tabs

    def row(i):
        return (i, 0)

    def tab(i):
        return (i % tab_blocks, 0)

    def const(i):
        return (0, 0)

    def full(a):
        return pl.BlockSpec(a.shape, const, pipeline_mode=pl.Buffered(1))

    weights = [w["nmix"], w["win"], w["gq"], w["wuqn"], w["wuqr"], w["gqn"], w["gqr"], w["gkv"], w["gkr"],
               w["wuk"], w["wuv"], w["gkn"]]
    wide = N_HEADS * LANES
    qk_wide = N_HEADS * HEAD_PAD
    out_shape = [
        jax.ShapeDtypeStruct((n, qk_wide), BF16),
        jax.ShapeDtypeStruct((n, qk_wide), BF16),
        jax.ShapeDtypeStruct((n, wide), BF16),
        jax.ShapeDtypeStruct((n, KV_LORA), F32),
        (jax.ShapeDtypeStruct((n, QK_ROPE), F32) if kr_seq is None
         else jax.ShapeDtypeStruct((n // kr_seq, QK_ROPE, kr_seq), F32)),
        jax.ShapeDtypeStruct((SSM_NJ, n // SSM_T, SSM_T * LANES), F32),
        jax.ShapeDtypeStruct((n, D_MODEL), F32),
        jax.ShapeDtypeStruct((n, D_MODEL), F32),
    ]
    out_specs = [
        pl.BlockSpec((tm, qk_wide), row), pl.BlockSpec((tm, qk_wide), row),
        pl.BlockSpec((tm, wide), row), pl.BlockSpec((tm, KV_LORA), row),
        (pl.BlockSpec((tm, QK_ROPE), row) if kr_seq is None else
         pl.BlockSpec((None, QK_ROPE, tm), lambda i: (i // (kr_seq // tm), 0, i % (kr_seq // tm)))),
        pl.BlockSpec((SSM_NJ, tm // SSM_T, SSM_T * LANES), lambda i: (0, i, 0)),
        pl.BlockSpec((tm, D_MODEL), row), pl.BlockSpec((tm, D_MODEL), row),
    ]
    in_specs = ([pl.BlockSpec((tm, D_MODEL), row)] + [pl.BlockSpec((tm, LANES), tab)] * 3
                + [full(a) for a in weights])
    return pl.pallas_call(
        functools.partial(_front_kernel, kr_transposed=kr_seq is not None), grid=(nt,), in_specs=in_specs,
        out_specs=out_specs, out_shape=out_shape,
        scratch_shapes=[pltpu.VMEM((SSM_NJ, tm, LANES), F32)],
        compiler_params=_cp(("arbitrary",)), name="front",
    )(x, ct, s1, s2, *weights)


ATT_T = 512


def _with_ones(v):
    return jnp.concatenate([v, jnp.ones_like(v)], axis=1)


def _softmax_step(s, v, m_ref, acc_ref):
    m_prev = m_ref[...]
    m_new = jnp.maximum(m_prev, jnp.max(s, axis=-1, keepdims=True))
    alpha = jnp.exp2(m_prev - m_new)
    p = jnp.exp2(s - jnp.tile(m_new, (1, s.shape[1] // LANES)))
    acc_ref[...] = jnp.tile(alpha, (1, 2)) * acc_ref[...] + _dot(p.astype(BF16), _with_ones(v))
    m_ref[...] = m_new


ATT_HG = 4


def _attn_prompt_kernel(q_ref, k_ref, v_ref, mk_ref, mv_ref, o_ref, a_sc, b_sc, m_ref, acc_ref):
    step = pl.program_id(2)
    npairs = step
    heads = range(ATT_HG)

    def key_rows(kt):
        return pl.ds(pl.multiple_of(kt * ATT_T, ATT_T), ATT_T)

    def q_rows(slot):
        return slice(slot * ATT_T, (slot + 1) * ATT_T)

    def qk_lanes(g):
        return slice(g * HEAD_PAD, (g + 1) * HEAD_PAD)

    def v_lanes(g):
        return slice(g * V_DIM, (g + 1) * V_DIM)

    def scores(slot, g, kt):
        return _dot_nt(q_ref[q_rows(slot), qk_lanes(g)], k_ref[key_rows(kt), qk_lanes(g)])

    def consume(slot, g, s, kt, diagonal):
        if diagonal:
            qc = lax.broadcasted_iota(jnp.int32, (ATT_T, ATT_T), 0) // CHUNK
            kc = lax.broadcasted_iota(jnp.int32, (ATT_T, ATT_T), 1) // CHUNK
            s = jnp.where(kc <= qc, s, NEG_INF)
        _softmax_step(s, v_ref[key_rows(kt), v_lanes(g)], m_ref.at[slot, g], acc_ref.at[slot, g])

    def head(slot):
        for g in heads:
            a_sc[slot, g] = scores(slot, g, 0)
        for g in heads:
            s = _dot_nt(q_ref[q_rows(slot), qk_lanes(g)], mk_ref[:, qk_lanes(g)])
            m0 = jnp.max(s, axis=-1, keepdims=True)
            p = jnp.exp2(s - m0)
            m_ref[slot, g] = jnp.broadcast_to(m0, (ATT_T, LANES))
            acc_ref[slot, g] = _dot(p.astype(BF16), _with_ones(mv_ref[:, v_lanes(g)]))

    def full_pairs(slot):
        def pair(pi, carry):
            kt = 2 * pi
            for g in heads:
                b_sc[g] = scores(slot, g, kt + 1)
                consume(slot, g, a_sc[slot, g], kt, False)
            for g in heads:
                a_sc[slot, g] = scores(slot, g, kt + 2)
                consume(slot, g, b_sc[g], kt + 1, False)
            return carry

        lax.fori_loop(0, npairs, pair, 0)

    def tail(slot):
        i = 2 * step + slot
        if slot == 1:
            for g in heads:
                b_sc[g] = scores(slot, g, i)
                consume(slot, g, a_sc[slot, g], i - 1, False)
            for g in heads:
                consume(slot, g, b_sc[g], i, True)
        else:
            for g in heads:
                consume(slot, g, a_sc[slot, g], i, True)

    def finish(slot):
        for g in heads:
            o_ref[q_rows(slot), v_lanes(g)] = (acc_ref[slot, g, :, :V_DIM]
                                               / acc_ref[slot, g, :, V_DIM:]).astype(BF16)

    head(0)
    head(1)
    full_pairs(0)
    tail(0)
    finish(0)
    full_pairs(1)
    tail(1)
    finish(1)


def _attn_prompt(q, k, v, mk, mv, bsz, seq):
    nstep = seq // (2 * ATT_T)
    n = bsz * seq
    qmap = lambda b, h, i: (b * nstep + i, h)
    return pl.pallas_call(
        _attn_prompt_kernel, grid=(bsz, N_HEADS // ATT_HG, nstep),
        in_specs=[pl.BlockSpec((2 * ATT_T, ATT_HG * HEAD_PAD), qmap),
                  pl.BlockSpec((seq, ATT_HG * HEAD_PAD), lambda b, h, i: (b, h)),
                  pl.BlockSpec((seq, ATT_HG * V_DIM), lambda b, h, i: (b, h)),
                  pl.BlockSpec((N_META, ATT_HG * HEAD_PAD), lambda b, h, i: (0, h)),
                  pl.BlockSpec((N_META, ATT_HG * V_DIM), lambda b, h, i: (0, h))],
        out_specs=pl.BlockSpec((2 * ATT_T, ATT_HG * V_DIM), qmap),
        out_shape=jax.ShapeDtypeStruct((n, N_HEADS * V_DIM), BF16),
        scratch_shapes=[pltpu.VMEM((2, ATT_HG, ATT_T, ATT_T), F32), pltpu.VMEM((ATT_HG, ATT_T, ATT_T), F32),
                        pltpu.VMEM((2, ATT_HG, ATT_T, LANES), F32),
                        pltpu.VMEM((2, ATT_HG, ATT_T, 2 * V_DIM), F32)],
        compiler_params=_cp(("arbitrary", "arbitrary", "arbitrary")), name="attn_prompt",
    )(q, k, v, mk, mv)


SAMPLE_KEY_CHUNK = 256


def _attn_sample_kernel(q_ref, cm_ref, cn_ref, cc_ref, krm_ref, krn_ref, krc_ref, wuk_ref, wukg_ref, wuv_ref,
                        hsum_ref, o_ref, c_sc, kr_sc, st_sc, pt_sc, *, seq, past):
    n_small = N_META + seq
    n_keys = n_small + past

    c_sc[:, KV_LORA:] = jnp.ones((n_keys, LANES), BF16)
    c_sc[0:N_META, :KV_LORA] = cm_ref[...].astype(BF16)
    c_sc[N_META:n_small, :KV_LORA] = cn_ref[...].astype(BF16)
    c_sc[n_small:, :KV_LORA] = cc_ref[...].astype(BF16)
    kr_sc[:, QK_ROPE:] = jnp.zeros((n_small, LANES - QK_ROPE), BF16)
    kr_sc[0:N_META, :QK_ROPE] = krm_ref[...].astype(BF16)
    kr_sc[N_META:n_small, :] = krn_ref[...]

    qa, qr = [], []
    for h in range(N_HEADS):
        qn_h = q_ref[:, h * HEAD_PAD:h * HEAD_PAD + LANES]
        qa.append(_dot_nt(qn_h, wukg_ref[:, h * LANES:(h + 1) * LANES]))
        qr.append(q_ref[:, h * HEAD_PAD + LANES:(h + 1) * HEAD_PAD])
    qa = jnp.concatenate(qa, axis=0).astype(BF16)
    qr = jnp.concatenate(qr, axis=0)

    bounds = [(0, n_small)] + [(n_small + i, n_small + i + SAMPLE_KEY_CHUNK)
                               for i in range(0, past, SAMPLE_KEY_CHUNK)]
    m = None
    for lo, hi in bounds:
        c16 = c_sc[lo:hi, :KV_LORA]
        kn = _dot(c16, wuk_ref[...])
        ssq = _dot((kn * kn).astype(BF16), hsum_ref[...])
        r = lax.rsqrt(ssq * (1.0 / QK_NOPE) + EPS)
        if lo == 0:
            s_rope = _dot_nt(kr_sc[...], qr)
        else:
            kr_t = krc_ref[:, lo - n_small:hi - n_small].astype(BF16)
            s_rope = lax.dot_general(kr_t, qr[:, :QK_ROPE], (((0,), (1,)), ((), ())),
                                     preferred_element_type=F32)
        st = _dot_nt(c16, qa) * r + s_rope
        st_sc[lo:hi, :] = st
        cm = jnp.max(st, axis=0, keepdims=True)
        m = cm if m is None else jnp.maximum(m, cm)

    for lo, hi in bounds:
        pt_sc[lo:hi, :] = jnp.exp2(st_sc[lo:hi, :] - m).astype(BF16)
    pc = lax.dot_general(pt_sc[...], c_sc[...], (((0,), (0,)), ((), ())), preferred_element_type=F32)
    pcn = (pc[:, :KV_LORA] / jnp.tile(pc[:, KV_LORA:], (1, KV_LORA // LANES))).astype(BF16)
    for h in range(N_HEADS):
        o_ref[:, h * V_DIM:(h + 1) * V_DIM] = _dot(pcn[h * seq:(h + 1) * seq, :],
                                                   wuv_ref[:, h * V_DIM:(h + 1) * V_DIM]).astype(BF16)


def _attn_sample(q, k_new, c_new, cache_c, cache_kr, meta_c, meta_kr, w, bsz, seq, past):
    n_keys = N_META + seq + past
    hq = N_HEADS * seq
    hsum = jnp.repeat(jnp.repeat(jnp.eye(N_HEADS, dtype=BF16), QK_NOPE, axis=0), seq, axis=1)
    row = lambda b: (b, 0)
    b3 = lambda b: (b, 0, 0)
    const = lambda b: (0, 0)
    kern = functools.partial(_attn_sample_kernel, seq=seq, past=past)
    return pl.pallas_call(
        kern, grid=(bsz,),
        in_specs=[pl.BlockSpec((seq, N_HEADS * HEAD_PAD), row),
                  pl.BlockSpec((None, N_META, KV_LORA), b3), pl.BlockSpec((seq, KV_LORA), row),
                  pl.BlockSpec((None, past, KV_LORA), b3),
                  pl.BlockSpec((None, N_META, QK_ROPE), b3), pl.BlockSpec((seq, LANES), lambda b: (b, 1)),
                  pl.BlockSpec((None, QK_ROPE, past), b3),
                  pl.BlockSpec(w["wuk"].shape, const), pl.BlockSpec(w["wukg"].shape, const),
                  pl.BlockSpec(w["wuv"].shape, const), pl.BlockSpec(hsum.shape, const)],
        out_specs=pl.BlockSpec((seq, N_HEADS * V_DIM), row),
        out_shape=jax.ShapeDtypeStruct((bsz * seq, N_HEADS * V_DIM), BF16),
        scratch_shapes=[pltpu.VMEM((n_keys, KV_LORA + LANES), BF16), pltpu.VMEM((N_META + seq, LANES), BF16),
                        pltpu.VMEM((n_keys, hq), F32), pltpu.VMEM((n_keys, hq), BF16)],
        compiler_params=_cp(("arbitrary",)), name="attn_sample",
    )(q, meta_c, c_new, cache_c, meta_kr, k_new, jnp.swapaxes(cache_kr, 1, 2), w["wuk"], w["wukg"], w["wuv"],
      hsum)


MXU_DIM = 256


def _dot_causal(u16, m_ref):
    nblk = m_ref.shape[0] // MXU_DIM
    return jnp.concatenate(
        [_dot(u16[:, :(cb + 1) * MXU_DIM], m_ref[:(cb + 1) * MXU_DIM, cb * MXU_DIM:(cb + 1) * MXU_DIM])
         for cb in range(nblk)], axis=1)


SSM_PACK = 4
SSM_SC = SSM_SW // LANES
SUBLANES = 2 * SSM_PACK


def _ssm_prompt_kernel(u_ref, m_ref, p_ref, q_ref, lam_ref, d_ref, h0_ref, s_ref, hfin_ref, s_sc, hp_sc, h_sc, y_sc,
                       *, nb, rt):
    t = pl.program_id(1)

    @pl.when(t == 0)
    def _():
        h_sc[...] = h0_ref[...]

    for b in range(nb):
        sb = _dot(u_ref[b].astype(BF16), p_ref[...])
        for c in range(SSM_SC):
            s_sc[c, pl.ds(b, rt, stride=SUBLANES), :] = sb[:, c * LANES:(c + 1) * LANES]
            s_sc[c, pl.ds(SSM_PACK + b, rt, stride=SUBLANES), :] = sb[:, SSM_SW + c * LANES:SSM_SW + (c + 1) * LANES]
    for b in range(nb):
        ub = u_ref[b]
        y_sc[b] = _dot_causal(ub.astype(BF16), m_ref) + ub * d_ref[...]

    lam = lam_ref[...]
    im_rows = lax.broadcasted_iota(jnp.int32, (SUBLANES, LANES), 0) >= SSM_PACK
    coef = []
    for c in range(SSM_SC):
        lr = jnp.broadcast_to(lam[:, c * LANES:(c + 1) * LANES], (SUBLANES, LANES))
        li = jnp.broadcast_to(lam[:, SSM_SW + c * LANES:SSM_SW + (c + 1) * LANES], (SUBLANES, LANES))
        coef.append((lr, jnp.where(im_rows, li, -li)))

    hs = [h_sc[c] for c in range(SSM_SC)]
    for r in range(rt):
        rows = slice(r * SUBLANES, (r + 1) * SUBLANES)
        for c in range(SSM_SC):
            hp_sc[c, rows, :] = hs[c]
            a, bb = coef[c]
            hs[c] = a * hs[c] + bb * pltpu.roll(hs[c], SSM_PACK, 0) + s_sc[c, rows, :]
    for c in range(SSM_SC):
        h_sc[c] = hs[c]

    for b in range(nb):
        hb = jnp.concatenate([hp_sc[c, pl.ds(b, rt, stride=SUBLANES), :] for c in range(SSM_SC)]
                             + [hp_sc[c, pl.ds(SSM_PACK + b, rt, stride=SUBLANES), :] for c in range(SSM_SC)],
                             axis=1)
        y = y_sc[b] + _dot(hb.astype(BF16), q_ref[...])
        s_ref[b] = jax.nn.gelu(y).astype(BF16)

    @pl.when(t == pl.num_programs(1) - 1)
    def _():
        hfin_ref[...] = h_sc[...]


def _pack_state(h):
    nj, nb, _ = h.shape
    h = jnp.pad(h, ((0, 0), (0, SSM_PACK - nb), (0, 0)))
    re = h[:, :, :SSM_SW].reshape(nj, SSM_PACK, SSM_SC, LANES)
    im = h[:, :, SSM_SW:].reshape(nj, SSM_PACK, SSM_SC, LANES)
    return jnp.concatenate([re, im], axis=1).transpose(0, 2, 1, 3)


def _unpack_state(hp, nb):
    nj = hp.shape[0]
    x = hp.transpose(0, 2, 1, 3).reshape(nj, SUBLANES, SSM_SW)
    return jnp.concatenate([x[:, :nb], x[:, SSM_PACK:SSM_PACK + nb]], axis=2)


def _ssm_prompt(u, sw, h0, nb, lc, rt):
    assert nb == SSM_PACK and lc % rt == 0
    tw = SSM_T * LANES
    u4 = u.reshape(SSM_NJ, nb, lc, tw)
    sq = 2 * SSM_SW
    kern = functools.partial(_ssm_prompt_kernel, nb=nb, rt=rt)
    st_spec = pl.BlockSpec((None, SSM_SC, SUBLANES, LANES), lambda j, t: (j, 0, 0, 0))
    s, hfin = pl.pallas_call(
        kern, grid=(SSM_NJ, lc // rt),
        in_specs=[pl.BlockSpec((None, nb, rt, tw), lambda j, t: (j, 0, t, 0)),
                  pl.BlockSpec((None, tw, tw), lambda j, t: (j, 0, 0)),
                  pl.BlockSpec((None, tw, sq), lambda j, t: (j, 0, 0)),
                  pl.BlockSpec((None, sq, tw), lambda j, t: (j, 0, 0)),
                  pl.BlockSpec((None, 1, sq), lambda j, t: (j, 0, 0)),
                  pl.BlockSpec((None, 1, tw), lambda j, t: (j, 0, 0)),
                  st_spec],
        out_specs=[pl.BlockSpec((None, nb, rt, tw), lambda j, t: (j, 0, t, 0)), st_spec],
        out_shape=[jax.ShapeDtypeStruct((SSM_NJ, nb, lc, tw), BF16),
                   jax.ShapeDtypeStruct((SSM_NJ, SSM_SC, SUBLANES, LANES), F32)],
        scratch_shapes=[pltpu.VMEM((SSM_SC, rt * SUBLANES, LANES), F32),
                        pltpu.VMEM((SSM_SC, rt * SUBLANES, LANES), F32),
                        pltpu.VMEM((SSM_SC, SUBLANES, LANES), F32),
                        pltpu.VMEM((nb, rt, tw), F32)],
        compiler_params=_cp(("arbitrary", "arbitrary")), name="ssm_prompt",
    )(u4, sw["M"], sw["P"], sw["Q"], sw["lam"], sw["D"], _pack_state(h0))
    return s.reshape(SSM_NJ, nb * lc, tw), _unpack_state(hfin, nb)


def _ssm_state_kernel(u_ref, p_ref, lam_ref, h_ref):
    s = _dot(u_ref[...].astype(BF16), p_ref[...])
    lam = lam_ref[...]
    lr, li = lam[:, :SSM_SW], lam[:, SSM_SW:]
    h = s[0:1, :]
    for r in range(1, s.shape[0]):
        hre, him = h[:, :SSM_SW], h[:, SSM_SW:]
        h = jnp.concatenate([lr * hre - li * him, lr * him + li * hre], axis=1) + s[r:r + 1, :]
    h_ref[...] = h


def _ssm_state(u, sw, lc):
    tw = SSM_T * LANES
    sq = 2 * SSM_SW
    blk = lambda j: (j, 0, 0)
    return pl.pallas_call(
        _ssm_state_kernel, grid=(SSM_NJ,),
        in_specs=[pl.BlockSpec((None, lc, tw), blk), pl.BlockSpec((None, tw, sq), blk),
                  pl.BlockSpec((None, 1, sq), blk)],
        out_specs=pl.BlockSpec((None, 1, sq), blk),
        out_shape=jax.ShapeDtypeStruct((SSM_NJ, 1, sq), F32),
        compiler_params=_cp(("arbitrary",)), name="ssm_state",
    )(u, sw["P"], sw["lam"])


def _ssm_sample_kernel(u_ref, m_ref, p_ref, q_ref, lam_ref, d_ref, h0_ref, s_ref, hfin_ref, s_sc,
                       *, nb, lc):
    u = u_ref[...]
    u16 = u.astype(BF16)
    s_sc[...] = _dot(u16, p_ref[...])
    lam = lam_ref[...]
    lr, li = lam[:, :SSM_SW], lam[:, SSM_SW:]
    h = h0_ref[...]
    for r in range(lc):
        rows = slice(r * nb, (r + 1) * nb)
        s = s_sc[rows, :]
        s_sc[rows, :] = h
        hre, him = h[:, :SSM_SW], h[:, SSM_SW:]
        h = jnp.concatenate([lr * hre - li * him + s[:, :SSM_SW], lr * him + li * hre + s[:, SSM_SW:]], axis=1)
    hfin_ref[...] = h
    y = _dot_causal(u16, m_ref) + _dot(s_sc[...].astype(BF16), q_ref[...]) + u * d_ref[...]
    s_ref[...] = jax.nn.gelu(y).astype(BF16)


def _ssm_sample(u, sw, h0, nb, lc):
    tw = SSM_T * LANES
    sq = 2 * SSM_SW
    u2 = u.reshape(SSM_NJ, nb, lc, tw).transpose(0, 2, 1, 3).reshape(SSM_NJ, lc * nb, tw)
    kern = functools.partial(_ssm_sample_kernel, nb=nb, lc=lc)
    blk = lambda j: (j, 0, 0)
    s, hfin = pl.pallas_call(
        kern, grid=(SSM_NJ,),
        in_specs=[pl.BlockSpec((None, lc * nb, tw), blk), pl.BlockSpec((None, tw, tw), blk),
                  pl.BlockSpec((None, tw, sq), blk), pl.BlockSpec((None, sq, tw), blk),
                  pl.BlockSpec((None, 1, sq), blk), pl.BlockSpec((None, 1, tw), blk),
                  pl.BlockSpec((None, nb, sq), blk)],
        out_specs=[pl.BlockSpec((None, lc * nb, tw), blk), pl.BlockSpec((None, nb, sq), blk)],
        out_shape=[jax.ShapeDtypeStruct((SSM_NJ, lc * nb, tw), BF16),
                   jax.ShapeDtypeStruct((SSM_NJ, nb, sq), F32)],
        scratch_shapes=[pltpu.VMEM((lc * nb, sq), F32)],
        compiler_params=_cp(("arbitrary",)), name="ssm_sample",
    )(u2, sw["M"], sw["P"], sw["Q"], sw["lam"], sw["D"], h0)
    s = s.reshape(SSM_NJ, lc, nb, tw).transpose(0, 2, 1, 3).reshape(SSM_NJ, nb * lc, tw)
    return s, hfin


def _ssm_fold_kernel(wre_ref, wim_ref, cre_ref, cim_ref, plr_ref, pli_ref, rep_ref, m_ref, p_ref, q_ref):
    def split(a, terms):
        out = []
        for _ in range(terms):
            t = a.astype(BF16)
            out.append(t)
            a = a - t.astype(F32)
        return out

    def repeat_lanes(a, rep):
        return sum(_dot(t, rep) for t in split(a, 3))

    def dot_nt_hi_lo(a, b_parts):
        a_hi, a_lo = split(a, 2)
        b_hi, b_lo = b_parts
        return _dot_nt(a_hi, b_hi) + (_dot_nt(a_hi, b_lo) + _dot_nt(a_lo, b_hi))

    same_group = (lax.broadcasted_iota(jnp.int32, (LANES, SSM_SW), 0) // SSM_GROUP
                  == lax.broadcasted_iota(jnp.int32, (LANES, SSM_SW), 1) // SSM_STATE)
    rep = rep_ref[...]
    wb_re = jnp.where(same_group, repeat_lanes(wre_ref[...], rep), 0.0)
    wb_im = jnp.where(same_group, repeat_lanes(wim_ref[...], rep), 0.0)
    cb_re = jnp.where(same_group, repeat_lanes(cre_ref[...], rep), 0.0)
    cb_im = jnp.where(same_group, repeat_lanes(cim_ref[...], rep), 0.0)
    cb_re_parts, cb_im_parts = split(cb_re, 2), split(cb_im, 2)

    m_ref[...] = jnp.zeros_like(m_ref)
    for k in range(SSM_T):
        lr, li = plr_ref[k:k + 1, :], pli_ref[k:k + 1, :]
        v_re = lr * wb_re - li * wb_im
        v_im = lr * wb_im + li * wb_re
        s = SSM_T - 1 - k
        p_ref[s * LANES:(s + 1) * LANES, :SSM_SW] = v_re.astype(BF16)
        p_ref[s * LANES:(s + 1) * LANES, SSM_SW:] = v_im.astype(BF16)
        bd = (dot_nt_hi_lo(v_re, cb_re_parts) - dot_nt_hi_lo(v_im, cb_im_parts)).astype(BF16)
        for s in range(SSM_T - k):
            t = s + k
            m_ref[s * LANES:(s + 1) * LANES, t * LANES:(t + 1) * LANES] = bd
        lr1, li1 = plr_ref[k + 1:k + 2, :], pli_ref[k + 1:k + 2, :]
        q_ref[:SSM_SW, k * LANES:(k + 1) * LANES] = (cb_re * lr1 - cb_im * li1).T.astype(BF16)
        q_ref[SSM_SW:, k * LANES:(k + 1) * LANES] = (-(cb_re * li1 + cb_im * lr1)).T.astype(BF16)


def _ssm_weights(a_re, a_im, log_dt, b_re, b_im, c_re, c_im, d):
    t_ = SSM_T
    dt = jnp.exp(log_dt)[:, None]
    mag = jnp.exp(a_re * dt)
    lam_re, lam_im = mag * jnp.cos(a_im * dt), mag * jnp.sin(a_im * dt)
    den = a_re * a_re + a_im * a_im
    f_re = ((lam_re - 1.0) * a_re + lam_im * a_im) / den
    f_im = (lam_im * a_re - (lam_re - 1.0) * a_im) / den
    w_re = f_re[:, :, None] * b_re - f_im[:, :, None] * b_im
    w_im = f_re[:, :, None] * b_im + f_im[:, :, None] * b_re
    k = jnp.arange(t_ + 1, dtype=F32)[None, :, None]
    pmag = jnp.exp(a_re[:, None, :] * dt[:, :, None] * k)
    pang = a_im[:, None, :] * dt[:, :, None] * k
    pw_re, pw_im = pmag * jnp.cos(pang), pmag * jnp.sin(pang)

    def lanes_gn(x):
        x = x.reshape(SSM_NJ, SSM_GPB, t_ + 1, SSM_STATE)
        return jnp.swapaxes(x, 1, 2).reshape(SSM_NJ, t_ + 1, SSM_SW)

    pl_re, pl_im = lanes_gn(pw_re), lanes_gn(pw_im)
    wcat = [jnp.swapaxes(x, 1, 2).reshape(SSM_NJ, LANES, SSM_STATE) for x in (w_re, w_im)]
    ccat = [x.reshape(SSM_NJ, LANES, SSM_STATE) for x in (c_re, c_im)]
    rep = jnp.tile(jnp.eye(SSM_STATE, dtype=BF16), (1, SSM_GPB))
    tw = t_ * LANES
    sq = 2 * SSM_SW
    blk = lambda j: (j, 0, 0)
    const = lambda j: (0, 0)
    per_j = lambda a: pl.BlockSpec((None,) + a.shape[1:], blk)
    ins = [wcat[0], wcat[1], ccat[0], ccat[1], pl_re, pl_im]
    m, p, q = pl.pallas_call(
        _ssm_fold_kernel, grid=(SSM_NJ,),
        in_specs=[per_j(a) for a in ins] + [pl.BlockSpec(rep.shape, const)],
        out_specs=[pl.BlockSpec((None, tw, tw), blk), pl.BlockSpec((None, tw, sq), blk),
                   pl.BlockSpec((None, sq, tw), blk)],
        out_shape=[jax.ShapeDtypeStruct((SSM_NJ, tw, tw), BF16), jax.ShapeDtypeStruct((SSM_NJ, tw, sq), BF16),
                   jax.ShapeDtypeStruct((SSM_NJ, sq, tw), BF16)],
        compiler_params=_cp(("arbitrary",)), name="ssm_fold",
    )(*ins, rep)
    lam_t = jnp.concatenate([pl_re[:, t_][:, None, :], pl_im[:, t_][:, None, :]], axis=2)
    dvec = jnp.tile(d.reshape(SSM_NJ, 1, LANES), (1, 1, t_))
    return {"M": m, "P": p, "Q": q, "lam": lam_t, "D": dvec}


def _mixer_kernel(o_ref, s_ref, ga_ref, gb_ref, x_ref, wo_ref, wv_ref, wg_ref, wout_ref, h_ref, s_sc):
    o_a = _dot(o_ref[...], wo_ref[...])
    nchunk = s_sc.shape[1] // SSM_T
    for j in range(SSM_NJ):
        sj = s_ref[j].astype(F32)
        for t in range(SSM_T):
            s_sc[j, pl.ds(t, nchunk, stride=SSM_T), :] = sj[:, t * LANES:(t + 1) * LANES]
    s = jnp.concatenate([s_sc[j] for j in range(SSM_NJ)], axis=1).astype(BF16)
    o_b = _dot(s, wv_ref[...]) * jax.nn.sigmoid(_dot(s, wg_ref[...]))
    merged = jax.nn.sigmoid(ga_ref[...]) * o_a + jax.nn.sigmoid(gb_ref[...]) * o_b
    h_ref[...] = x_ref[...] + _dot(merged.astype(BF16), wout_ref[...])


def _mixer(o, s, ga, gb, x, w, tm):
    n = x.shape[0]
    row = lambda i: (i, 0)
    const = lambda i: (0, 0)
    wspec = pl.BlockSpec((D_MODEL, D_MODEL), const, pipeline_mode=pl.Buffered(1))
    act = pl.BlockSpec((tm, D_MODEL), row)
    return pl.pallas_call(
        _mixer_kernel, grid=(n // tm,),
        in_specs=[act, pl.BlockSpec((SSM_NJ, tm // SSM_T, SSM_T * LANES), lambda i: (0, i, 0)), act, act, act,
                  wspec, wspec, wspec, wspec],
        out_specs=act, out_shape=jax.ShapeDtypeStruct((n, D_MODEL), F32),
        scratch_shapes=[pltpu.VMEM((SSM_NJ, tm, LANES), F32)],
        compiler_params=_cp(("arbitrary",)), name="mixer_out",
    )(o, s, ga, gb, x, w["wo"], w["wv"], w["wg"], w["wout"])


def _mlp_kernel(h_ref, g_ref, wup_ref, wdn_ref, y_ref):
    h = h_ref[...]
    hn = _rms(h, g_ref[...]).astype(BF16)
    a = jnp.maximum(_dot(hn, wup_ref[...]), 0.0)
    y_ref[...] = h + _dot((a * a).astype(BF16), wdn_ref[...])


def _mlp(h, w, tm):
    n = h.shape[0]
    row = lambda i: (i, 0)
    const = lambda i: (0, 0)
    act = pl.BlockSpec((tm, D_MODEL), row)
    return pl.pallas_call(
        _mlp_kernel, grid=(n // tm,),
        in_specs=[act, pl.BlockSpec((1, D_MODEL), const),
                  pl.BlockSpec((D_MODEL, D_FF), const, pipeline_mode=pl.Buffered(1)),
                  pl.BlockSpec((D_FF, D_MODEL), const, pipeline_mode=pl.Buffered(1))],
        out_specs=act, out_shape=jax.ShapeDtypeStruct((n, D_MODEL), F32),
        compiler_params=_cp(("arbitrary",)), name="mlp",
    )(h, w["gmlp"], w["wup"], w["wdn"])


def _cos_sin(pos):
    half = QK_ROPE // 2
    inv = ROPE_THETA ** (-jnp.arange(half, dtype=F32) / half)
    ang = pos.astype(F32)[:, None] * inv[None, :]
    return jnp.cos(ang), jnp.sin(ang)


def _rope_tables(pos=None, length=None):
    if pos is not None:
        cos, sin = _cos_sin(pos)
    else:
        half = QK_ROPE // 2
        ch, sh = _cos_sin(jnp.arange(length // CHUNK, dtype=jnp.int32) * CHUNK)
        cl, sl = _cos_sin(jnp.arange(CHUNK, dtype=jnp.int32))
        cos = (ch[:, None, :] * cl[None] - sh[:, None, :] * sl[None]).reshape(length, half)
        sin = (sh[:, None, :] * cl[None] + ch[:, None, :] * sl[None]).reshape(length, half)
    z = jnp.zeros_like(cos)
    pad = jnp.zeros((cos.shape[0], LANES - QK_ROPE), F32)
    ct = jnp.concatenate([cos, cos, pad], axis=1)
    s1 = jnp.concatenate([-sin, z, pad], axis=1)
    s2 = jnp.concatenate([z, sin, pad], axis=1)
    return ct, s1, s2


def _pad_lanes(a, width):
    return jnp.pad(a, [(0, 0)] * (a.ndim - 1) + [(0, width - a.shape[-1])])


def _layer_weights(norm_mix, w_in, q_lora_norm, w_uq, q_nope_norm, q_rope_norm, kv_lora_norm, k_rope_norm,
                   w_ukv, k_nope_norm, w_o_attn, w_glu_v, w_glu_g, w_out, norm_mlp, w_mlp_up, w_mlp_down):
    bf = lambda a: a.astype(BF16)
    r2 = lambda a: a.reshape(1, -1).astype(F32)
    return {
        "nmix": r2(norm_mix),
        "win": bf(jnp.swapaxes(w_in, 0, 1)),
        "gq": r2(q_lora_norm),
        "wuqn": bf(w_uq[:, :, :QK_NOPE].reshape(Q_LORA, N_HEADS * QK_NOPE)),
        "wuqr": bf(_pad_lanes(w_uq[:, :, QK_NOPE:], LANES).reshape(Q_LORA, N_HEADS * LANES)),
        "gqn": r2(q_nope_norm) * Q_SCALE, "gqr": _pad_lanes(r2(q_rope_norm), LANES) * Q_SCALE,
        "gkv": r2(kv_lora_norm), "gkr": _pad_lanes(r2(k_rope_norm), LANES),
        "wuk": bf(w_ukv[:, :, :QK_NOPE].reshape(KV_LORA, N_HEADS * QK_NOPE)),
        "wukg": bf((w_ukv[:, :, :QK_NOPE] * k_nope_norm.astype(F32)).reshape(KV_LORA, N_HEADS * QK_NOPE)),
        "wuv": bf(w_ukv[:, :, QK_NOPE:].reshape(KV_LORA, N_HEADS * V_DIM)),
        "gkn": r2(k_nope_norm),
        "wo": bf(w_o_attn.reshape(N_HEADS * V_DIM, D_MODEL)),
        "wv": bf(w_glu_v), "wg": bf(w_glu_g), "wout": bf(w_out),
        "gmlp": r2(norm_mlp), "wup": bf(w_mlp_up), "wdn": bf(w_mlp_down),
    }


def kernel(x_prompt, x_sample, cache_latent, cache_krope, cache_meta_latent, cache_meta_krope, state_ssm_re, state_ssm_im, meta_tokens, norm_mix, w_in, q_lora_norm, w_uq, q_nope_norm, q_rope_norm, kv_lora_norm, k_rope_norm, w_ukv, k_nope_norm, w_o_attn, ssm_a_re, ssm_a_im, ssm_log_dt, ssm_b_re, ssm_b_im, ssm_c_re, ssm_c_im, ssm_d, w_glu_v, w_glu_g, w_out, norm_mlp, w_mlp_up, w_mlp_down):
    bsz_p, seq_p = x_prompt.shape[0], x_prompt.shape[1]
    bsz_s, seq_s = x_sample.shape[0], x_sample.shape[1]
    past = cache_latent.shape[2]
    depth = w_in.shape[0]
    assert depth == 1, "single-layer step"
    l = 0
    w = _layer_weights(norm_mix[l], w_in[l], q_lora_norm[l], w_uq[l], q_nope_norm[l], q_rope_norm[l],
                       kv_lora_norm[l], k_rope_norm[l], w_ukv[l], k_nope_norm[l], w_o_attn[l], w_glu_v[l],
                       w_glu_g[l], w_out[l], norm_mlp[l], w_mlp_up[l], w_mlp_down[l])
    sw = _ssm_weights(ssm_a_re[l].astype(F32), ssm_a_im[l].astype(F32), ssm_log_dt[l].astype(F32),
                      ssm_b_re[l].astype(F32), ssm_b_im[l].astype(F32), ssm_c_re[l].astype(F32),
                      ssm_c_im[l].astype(F32), ssm_d[l].astype(F32))
    sq = 2 * SSM_SW

    tabs_m = _rope_tables(jnp.arange(N_META, dtype=jnp.int32) - N_META)
    (_, mk, mv, mckv, mkr, mu, _, _) = _front(meta_tokens.astype(F32), tabs_m, 1, w, N_META)
    h_meta = _ssm_state(mu, sw, N_META // SSM_T)

    n_p = bsz_p * seq_p
    xp = x_prompt.reshape(n_p, D_MODEL)
    tabs_p = _rope_tables(length=seq_p)
    q, k, v, ckv_p, kr_p, u_p, ga, gb = _front(xp, tabs_p, seq_p // TM_PROMPT, w, TM_PROMPT, kr_seq=seq_p)
    o_p = _attn_prompt(q, k, v, mk, mv, bsz_p, seq_p)
    h0_p = jnp.broadcast_to(h_meta, (SSM_NJ, bsz_p, sq))
    s_p, hfin_p = _ssm_prompt(u_p, sw, h0_p, bsz_p, seq_p // SSM_T, SSM_RT)
    h1_p = _mixer(o_p, s_p, ga, gb, xp, w, TM_PROMPT)
    y_p = _mlp(h1_p, w, TM_PROMPT)

    n_s = bsz_s * seq_s
    xs = x_sample.reshape(n_s, D_MODEL)
    pos_s = past + jnp.arange(seq_s, dtype=jnp.int32)
    tabs_s = tuple(jnp.tile(t, (bsz_s, 1)) for t in _rope_tables(pos_s))
    q_s, k_s, _, ckv_s, kr_s, u_s, ga_s, gb_s = _front(xs, tabs_s, n_s // TM_SAMPLE, w, TM_SAMPLE)
    o_s = _attn_sample(q_s, k_s, ckv_s, cache_latent[l].astype(F32), cache_krope[l].astype(F32),
                       cache_meta_latent[l].astype(F32), cache_meta_krope[l].astype(F32), w, bsz_s, seq_s, past)

    def to_blocks(st):
        return st.astype(F32).reshape(bsz_s, SSM_NJ, SSM_SW).transpose(1, 0, 2)

    h0_s = jnp.concatenate([to_blocks(state_ssm_re[l]), to_blocks(state_ssm_im[l])], axis=2)
    s_s, hfin_s = _ssm_sample(u_s, sw, h0_s, bsz_s, seq_s // SSM_T)
    h1_s = _mixer(o_s, s_s, ga_s, gb_s, xs, w, TM_SAMPLE)
    y_s = _mlp(h1_s, w, TM_SAMPLE)

    def from_blocks(hf, nb):
        re = hf[:, :, :SSM_SW].transpose(1, 0, 2).reshape(1, nb, N_GROUPS, SSM_STATE)
        im = hf[:, :, SSM_SW:].transpose(1, 0, 2).reshape(1, nb, N_GROUPS, SSM_STATE)
        return re, im

    sre_p, sim_p = from_blocks(hfin_p, bsz_p)
    sre_s, sim_s = from_blocks(hfin_s, bsz_s)
    return (y_p.reshape(bsz_p, seq_p, D_MODEL), y_s.reshape(bsz_s, seq_s, D_MODEL),
            ckv_p.reshape(1, bsz_p, seq_p, KV_LORA), jnp.swapaxes(kr_p, 1, 2)[None],
            jnp.broadcast_to(mckv[None, None], (1, bsz_p, N_META, KV_LORA)),
            jnp.broadcast_to(mkr[None, None], (1, bsz_p, N_META, QK_ROPE)),
            sre_p, sim_p,
            ckv_s.reshape(1, bsz_s, seq_s, KV_LORA), kr_s.reshape(1, bsz_s, seq_s, QK_ROPE),
            sre_s, sim_s)
```

```python
import functools
import math

import jax
import jax.numpy as jnp
from jax import lax
from jax.experimental import pallas as pl
from jax.experimental.pallas import tpu as pltpu

F32 = jnp.float32
BF16 = jnp.bfloat16

D_MODEL = 1024
CHUNK = 64
N_META = 16
N_HEADS = 8
QK_NOPE = 128
QK_ROPE = 64
V_DIM = 128
QK_DIM = QK_NOPE + QK_ROPE
Q_LORA = 384
KV_LORA = 256
SSM_GROUP = 16
N_GROUPS = D_MODEL // SSM_GROUP
SSM_STATE = 64
D_FF = 4 * D_MODEL
ROPE_THETA = 10000.0
EPS = 1e-6
ATTN_SCALE = QK_DIM ** -0.5
Q_SCALE = ATTN_SCALE * math.log2(math.e)
NEG_INF = -1e30

LANES = 128
HEAD_PAD = 2 * LANES
SSM_T = 8
SSM_NJ = D_MODEL // LANES
SSM_GPB = LANES // SSM_GROUP
SSM_SW = SSM_GPB * SSM_STATE
VMEM_LIMIT = 56 * 1024 * 1024
TM_PROMPT = 512
TM_SAMPLE = 512
SSM_RT = 256


def _cp(sem):
    return pltpu.CompilerParams(dimension_semantics=sem, vmem_limit_bytes=VMEM_LIMIT)


def _dot(a, b):
    return jnp.dot(a, b, preferred_element_type=F32)


def _dot_nt(a, b):
    return lax.dot_general(a, b, (((1,), (1,)), ((), ())), preferred_element_type=F32)


def _rms(x, g, n=None):
    n = x.shape[-1] if n is None else n
    ms = jnp.sum(x * x, axis=-1, keepdims=True) * (1.0 / n)
    return x * lax.rsqrt(ms + EPS) * g


def _rope128(b, ct, s1, s2):
    return b * ct + pltpu.roll(b, LANES - QK_ROPE // 2, 1) * s1 + pltpu.roll(b, QK_ROPE // 2, 1) * s2


IN_Q = 0
IN_KV = IN_Q + Q_LORA
IN_KR = IN_KV + KV_LORA
IN_U = IN_KR + QK_ROPE
IN_GA = IN_U + D_MODEL
IN_GB = IN_GA + D_MODEL
IN_END = IN_GB + D_MODEL


def _front_kernel(x_ref, ct_ref, s1_ref, s2_ref, nmix_ref, win_ref, gq_ref, wuqn_ref, wuqr_ref, gqn_ref,
                  gqr_ref, gkv_ref, gkr_ref, wuk_ref, wuv_ref, gkn_ref,
                  q_ref, k_ref, v_ref, ckv_ref, kr_ref, u_ref, ga_ref, gb_ref, u_sc, *, kr_transposed):
    x = x_ref[...]
    xn = _rms(x, nmix_ref[...]).astype(BF16)
    ct, s1, s2 = ct_ref[...], s1_ref[...], s2_ref[...]


    q_lat = _dot_nt(xn, win_ref[IN_Q:IN_KV, :])
    kv_lat = _dot_nt(xn, win_ref[IN_KV:IN_KR, :])
    kr_raw = _dot_nt(xn, win_ref[IN_KR:IN_KR + LANES, :])
    kr_raw = jnp.where(lax.broadcasted_iota(jnp.int32, kr_raw.shape, 1) < QK_ROPE, kr_raw, 0.0)

    c_q = _rms(q_lat, gq_ref[...]).astype(BF16)
    qn = _dot(c_q, wuqn_ref[...])
    qr = _dot(c_q, wuqr_ref[...])
    c_kv = _rms(kv_lat, gkv_ref[...])
    ckv_ref[...] = c_kv
    c16 = c_kv.astype(BF16)
    kn = _dot(c16, wuk_ref[...])
    v_ref[...] = _dot(c16, wuv_ref[...]).astype(BF16)

    for h in range(N_HEADS):
        sl = slice(h * LANES, (h + 1) * LANES)
        q_ref[:, h * HEAD_PAD:h * HEAD_PAD + LANES] = _rms(qn[:, sl], gqn_ref[...]).astype(BF16)
        r = _rope128(_rms(qr[:, sl], gqr_ref[...], QK_ROPE), ct, s1, s2)
        q_ref[:, h * HEAD_PAD + LANES:(h + 1) * HEAD_PAD] = r.astype(BF16)
    kr = _rms(kr_raw, gkr_ref[...], QK_ROPE)
    kr = _rope128(kr, ct, s1, s2)
    kr_ref[...] = kr.T[:QK_ROPE, :] if kr_transposed else kr[:, :QK_ROPE]
    kr16 = kr.astype(BF16)
    for h in range(N_HEADS):
        sl = slice(h * LANES, (h + 1) * LANES)
        k_ref[:, h * HEAD_PAD:h * HEAD_PAD + LANES] = _rms(kn[:, sl], gkn_ref[...]).astype(BF16)
        k_ref[:, h * HEAD_PAD + LANES:(h + 1) * HEAD_PAD] = kr16

    ga_ref[...] = _dot_nt(xn, win_ref[IN_GA:IN_GB, :])
    u = _dot_nt(xn, win_ref[IN_U:IN_GA, :])
    nchunk = u_sc.shape[1] // SSM_T
    for j in range(SSM_NJ):
        u_sc[j] = u[:, j * LANES:(j + 1) * LANES]
        for t in range(SSM_T):
            u_ref[j, :, t * LANES:(t + 1) * LANES] = u_sc[j, pl.ds(t, nchunk, stride=SSM_T), :]
    gb_ref[...] = _dot_nt(xn, win_ref[IN_GB:IN_END, :])


def _front(x, tabs, tab_blocks, w, tm, kr_seq=None):
    n = x.shape[0]
    nt = n // tm
    ct, s1, s2 = tabs

    def row(i):
        return (i, 0)

    def tab(i):
        return (i % tab_blocks, 0)

    def const(i):
        return (0, 0)

    def full(a):
        return pl.BlockSpec(a.shape, const, pipeline_mode=pl.Buffered(1))

    weights = [w["nmix"], w["win"], w["gq"], w["wuqn"], w["wuqr"], w["gqn"], w["gqr"], w["gkv"], w["gkr"],
               w["wuk"], w["wuv"], w["gkn"]]
    wide = N_HEADS * LANES
    qk_wide = N_HEADS * HEAD_PAD
    out_shape = [
        jax.ShapeDtypeStruct((n, qk_wide), BF16),
        jax.ShapeDtypeStruct((n, qk_wide), BF16),
        jax.ShapeDtypeStruct((n, wide), BF16),
        jax.ShapeDtypeStruct((n, KV_LORA), F32),
        (jax.ShapeDtypeStruct((n, QK_ROPE), F32) if kr_seq is None
         else jax.ShapeDtypeStruct((n // kr_seq, QK_ROPE, kr_seq), F32)),
        jax.ShapeDtypeStruct((SSM_NJ, n // SSM_T, SSM_T * LANES), F32),
        jax.ShapeDtypeStruct((n, D_MODEL), F32),
        jax.ShapeDtypeStruct((n, D_MODEL), F32),
    ]
    out_specs = [
        pl.BlockSpec((tm, qk_wide), row), pl.BlockSpec((tm, qk_wide), row),
        pl.BlockSpec((tm, wide), row), pl.BlockSpec((tm, KV_LORA), row),
        (pl.BlockSpec((tm, QK_ROPE), row) if kr_seq is None else
         pl.BlockSpec((None, QK_ROPE, tm), lambda i: (i // (kr_seq // tm), 0, i % (kr_seq // tm)))),
        pl.BlockSpec((SSM_NJ, tm // SSM_T, SSM_T * LANES), lambda i: (0, i, 0)),
        pl.BlockSpec((tm, D_MODEL), row), pl.BlockSpec((tm, D_MODEL), row),
    ]
    in_specs = ([pl.BlockSpec((tm, D_MODEL), row)] + [pl.BlockSpec((tm, LANES), tab)] * 3
                + [full(a) for a in weights])
    return pl.pallas_call(
        functools.partial(_front_kernel, kr_transposed=kr_seq is not None), grid=(nt,), in_specs=in_specs,
        out_specs=out_specs, out_shape=out_shape,
        scratch_shapes=[pltpu.VMEM((SSM_NJ, tm, LANES), F32)],
        compiler_params=_cp(("arbitrary",)), name="front",
    )(x, ct, s1, s2, *weights)


ATT_T = 512


def _with_ones(v):
    return jnp.concatenate([v, jnp.ones_like(v)], axis=1)


def _softmax_step(s, v, m_ref, acc_ref):
    m_prev = m_ref[...]
    m_new = jnp.maximum(m_prev, jnp.max(s, axis=-1, keepdims=True))
    alpha = jnp.exp2(m_prev - m_new)
    p = jnp.exp2(s - jnp.tile(m_new, (1, s.shape[1] // LANES)))
    acc_ref[...] = jnp.tile(alpha, (1, 2)) * acc_ref[...] + _dot(p.astype(BF16), _with_ones(v))
    m_ref[...] = m_new


ATT_HG = 4


def _attn_prompt_kernel(q_ref, k_ref, v_ref, mk_ref, mv_ref, o_ref, a_sc, b_sc, m_ref, acc_ref):
    step = pl.program_id(2)
    npairs = step
    heads = range(ATT_HG)

    def key_rows(kt):
        return pl.ds(pl.multiple_of(kt * ATT_T, ATT_T), ATT_T)

    def q_rows(slot):
        return slice(slot * ATT_T, (slot + 1) * ATT_T)

    def qk_lanes(g):
        return slice(g * HEAD_PAD, (g + 1) * HEAD_PAD)

    def v_lanes(g):
        return slice(g * V_DIM, (g + 1) * V_DIM)

    def scores(slot, g, kt):
        return _dot_nt(q_ref[q_rows(slot), qk_lanes(g)], k_ref[key_rows(kt), qk_lanes(g)])

    def consume(slot, g, s, kt, diagonal):
        if diagonal:
            qc = lax.broadcasted_iota(jnp.int32, (ATT_T, ATT_T), 0) // CHUNK
            kc = lax.broadcasted_iota(jnp.int32, (ATT_T, ATT_T), 1) // CHUNK
            s = jnp.where(kc <= qc, s, NEG_INF)
        _softmax_step(s, v_ref[key_rows(kt), v_lanes(g)], m_ref.at[slot, g], acc_ref.at[slot, g])

    def head(slot):
        for g in heads:
            a_sc[slot, g] = scores(slot, g, 0)
        for g in heads:
            s = _dot_nt(q_ref[q_rows(slot), qk_lanes(g)], mk_ref[:, qk_lanes(g)])
            m0 = jnp.max(s, axis=-1, keepdims=True)
            p = jnp.exp2(s - m0)
            m_ref[slot, g] = jnp.broadcast_to(m0, (ATT_T, LANES))
            acc_ref[slot, g] = _dot(p.astype(BF16), _with_ones(mv_ref[:, v_lanes(g)]))

    def full_pairs(slot):
        def pair(pi, carry):
            kt = 2 * pi
            for g in heads:
                b_sc[g] = scores(slot, g, kt + 1)
                consume(slot, g, a_sc[slot, g], kt, False)
            for g in heads:
                a_sc[slot, g] = scores(slot, g, kt + 2)
                consume(slot, g, b_sc[g], kt + 1, False)
            return carry

        lax.fori_loop(0, npairs, pair, 0)

    def tail(slot):
        i = 2 * step + slot
        if slot == 1:
            for g in heads:
                b_sc[g] = scores(slot, g, i)
                consume(slot, g, a_sc[slot, g], i - 1, False)
            for g in heads:
                consume(slot, g, b_sc[g], i, True)
        else:
            for g in heads:
                consume(slot, g, a_sc[slot, g], i, True)

    def finish(slot):
        for g in heads:
            o_ref[q_rows(slot), v_lanes(g)] = (acc_ref[slot, g, :, :V_DIM]
                                               / acc_ref[slot, g, :, V_DIM:]).astype(BF16)

    head(0)
    head(1)
    full_pairs(0)
    tail(0)
    finish(0)
    full_pairs(1)
    tail(1)
    finish(1)


def _attn_prompt(q, k, v, mk, mv, bsz, seq):
    nstep = seq // (2 * ATT_T)
    n = bsz * seq
    qmap = lambda b, h, i: (b * nstep + i, h)
    return pl.pallas_call(
        _attn_prompt_kernel, grid=(bsz, N_HEADS // ATT_HG, nstep),
        in_specs=[pl.BlockSpec((2 * ATT_T, ATT_HG * HEAD_PAD), qmap),
                  pl.BlockSpec((seq, ATT_HG * HEAD_PAD), lambda b, h, i: (b, h)),
                  pl.BlockSpec((seq, ATT_HG * V_DIM), lambda b, h, i: (b, h)),
                  pl.BlockSpec((N_META, ATT_HG * HEAD_PAD), lambda b, h, i: (0, h)),
                  pl.BlockSpec((N_META, ATT_HG * V_DIM), lambda b, h, i: (0, h))],
        out_specs=pl.BlockSpec((2 * ATT_T, ATT_HG * V_DIM), qmap),
        out_shape=jax.ShapeDtypeStruct((n, N_HEADS * V_DIM), BF16),
        scratch_shapes=[pltpu.VMEM((2, ATT_HG, ATT_T, ATT_T), F32), pltpu.VMEM((ATT_HG, ATT_T, ATT_T), F32),
                        pltpu.VMEM((2, ATT_HG, ATT_T, LANES), F32),
                        pltpu.VMEM((2, ATT_HG, ATT_T, 2 * V_DIM), F32)],
        compiler_params=_cp(("arbitrary", "arbitrary", "arbitrary")), name="attn_prompt",
    )(q, k, v, mk, mv)


SAMPLE_KEY_CHUNK = 256


def _attn_sample_kernel(q_ref, cm_ref, cn_ref, cc_ref, krm_ref, krn_ref, krc_ref, wuk_ref, wukg_ref, wuv_ref,
                        hsum_ref, o_ref, c_sc, kr_sc, st_sc, pt_sc, *, seq, past):
    n_small = N_META + seq
    n_keys = n_small + past

    c_sc[:, KV_LORA:] = jnp.ones((n_keys, LANES), BF16)
    c_sc[0:N_META, :KV_LORA] = cm_ref[...].astype(BF16)
    c_sc[N_META:n_small, :KV_LORA] = cn_ref[...].astype(BF16)
    c_sc[n_small:, :KV_LORA] = cc_ref[...].astype(BF16)
    kr_sc[:, QK_ROPE:] = jnp.zeros((n_small, LANES - QK_ROPE), BF16)
    kr_sc[0:N_META, :QK_ROPE] = krm_ref[...].astype(BF16)
    kr_sc[N_META:n_small, :] = krn_ref[...]

    qa, qr = [], []
    for h in range(N_HEADS):
        qn_h = q_ref[:, h * HEAD_PAD:h * HEAD_PAD + LANES]
        qa.append(_dot_nt(qn_h, wukg_ref[:, h * LANES:(h + 1) * LANES]))
        qr.append(q_ref[:, h * HEAD_PAD + LANES:(h + 1) * HEAD_PAD])
    qa = jnp.concatenate(qa, axis=0).astype(BF16)
    qr = jnp.concatenate(qr, axis=0)

    bounds = [(0, n_small)] + [(n_small + i, n_small + i + SAMPLE_KEY_CHUNK)
                               for i in range(0, past, SAMPLE_KEY_CHUNK)]
    m = None
    for lo, hi in bounds:
        c16 = c_sc[lo:hi, :KV_LORA]
        kn = _dot(c16, wuk_ref[...])
        ssq = _dot((kn * kn).astype(BF16), hsum_ref[...])
        r = lax.rsqrt(ssq * (1.0 / QK_NOPE) + EPS)
        if lo == 0:
            s_rope = _dot_nt(kr_sc[...], qr)
        else:
            kr_t = krc_ref[:, lo - n_small:hi - n_small].astype(BF16)
            s_rope = lax.dot_general(kr_t, qr[:, :QK_ROPE], (((0,), (1,)), ((), ())),
                                     preferred_element_type=F32)
        st = _dot_nt(c16, qa) * r + s_rope
        st_sc[lo:hi, :] = st
        cm = jnp.max(st, axis=0, keepdims=True)
        m = cm if m is None else jnp.maximum(m, cm)

    for lo, hi in bounds:
        pt_sc[lo:hi, :] = jnp.exp2(st_sc[lo:hi, :] - m).astype(BF16)
    pc = lax.dot_general(pt_sc[...], c_sc[...], (((0,), (0,)), ((), ())), preferred_element_type=F32)
    pcn = (pc[:, :KV_LORA] / jnp.tile(pc[:, KV_LORA:], (1, KV_LORA // LANES))).astype(BF16)
    for h in range(N_HEADS):
        o_ref[:, h * V_DIM:(h + 1) * V_DIM] = _dot(pcn[h * seq:(h + 1) * seq, :],
                                                   wuv_ref[:, h * V_DIM:(h + 1) * V_DIM]).astype(BF16)


def _attn_sample(q, k_new, c_new, cache_c, cache_kr, meta_c, meta_kr, w, bsz, seq, past):
    n_keys = N_META + seq + past
    hq = N_HEADS * seq
    hsum = jnp.repeat(jnp.repeat(jnp.eye(N_HEADS, dtype=BF16), QK_NOPE, axis=0), seq, axis=1)
    row = lambda b: (b, 0)
    b3 = lambda b: (b, 0, 0)
    const = lambda b: (0, 0)
    kern = functools.partial(_attn_sample_kernel, seq=seq, past=past)
    return pl.pallas_call(
        kern, grid=(bsz,),
        in_specs=[pl.BlockSpec((seq, N_HEADS * HEAD_PAD), row),
                  pl.BlockSpec((None, N_META, KV_LORA), b3), pl.BlockSpec((seq, KV_LORA), row),
                  pl.BlockSpec((None, past, KV_LORA), b3),
                  pl.BlockSpec((None, N_META, QK_ROPE), b3), pl.BlockSpec((seq, LANES), lambda b: (b, 1)),
                  pl.BlockSpec((None, QK_ROPE, past), b3),
                  pl.BlockSpec(w["wuk"].shape, const), pl.BlockSpec(w["wukg"].shape, const),
                  pl.BlockSpec(w["wuv"].shape, const), pl.BlockSpec(hsum.shape, const)],
        out_specs=pl.BlockSpec((seq, N_HEADS * V_DIM), row),
        out_shape=jax.ShapeDtypeStruct((bsz * seq, N_HEADS * V_DIM), BF16),
        scratch_shapes=[pltpu.VMEM((n_keys, KV_LORA + LANES), BF16), pltpu.VMEM((N_META + seq, LANES), BF16),
                        pltpu.VMEM((n_keys, hq), F32), pltpu.VMEM((n_keys, hq), BF16)],
        compiler_params=_cp(("arbitrary",)), name="attn_sample",
    )(q, meta_c, c_new, cache_c, meta_kr, k_new, jnp.swapaxes(cache_kr, 1, 2), w["wuk"], w["wukg"], w["wuv"],
      hsum)


MXU_DIM = 256


def _dot_causal(u16, m_ref):
    nblk = m_ref.shape[0] // MXU_DIM
    return jnp.concatenate(
        [_dot(u16[:, :(cb + 1) * MXU_DIM], m_ref[:(cb + 1) * MXU_DIM, cb * MXU_DIM:(cb + 1) * MXU_DIM])
         for cb in range(nblk)], axis=1)


SSM_PACK = 4
SSM_SC = SSM_SW // LANES
SUBLANES = 2 * SSM_PACK


def _ssm_prompt_kernel(u_ref, m_ref, p_ref, q_ref, lam_ref, d_ref, h0_ref, s_ref, hfin_ref, s_sc, hp_sc, h_sc, y_sc,
                       *, nb, rt):
    t = pl.program_id(1)

    @pl.when(t == 0)
    def _():
        h_sc[...] = h0_ref[...]

    for b in range(nb):
        sb = _dot(u_ref[b].astype(BF16), p_ref[...])
        for c in range(SSM_SC):
            s_sc[c, pl.ds(b, rt, stride=SUBLANES), :] = sb[:, c * LANES:(c + 1) * LANES]
            s_sc[c, pl.ds(SSM_PACK + b, rt, stride=SUBLANES), :] = sb[:, SSM_SW + c * LANES:SSM_SW + (c + 1) * LANES]
    for b in range(nb):
        ub = u_ref[b]
        y_sc[b] = _dot_causal(ub.astype(BF16), m_ref) + ub * d_ref[...]

    lam = lam_ref[...]
    im_rows = lax.broadcasted_iota(jnp.int32, (SUBLANES, LANES), 0) >= SSM_PACK
    coef = []
    for c in range(SSM_SC):
        lr = jnp.broadcast_to(lam[:, c * LANES:(c + 1) * LANES], (SUBLANES, LANES))
        li = jnp.broadcast_to(lam[:, SSM_SW + c * LANES:SSM_SW + (c + 1) * LANES], (SUBLANES, LANES))
        coef.append((lr, jnp.where(im_rows, li, -li)))

    hs = [h_sc[c] for c in range(SSM_SC)]
    for r in range(rt):
        rows = slice(r * SUBLANES, (r + 1) * SUBLANES)
        for c in range(SSM_SC):
            hp_sc[c, rows, :] = hs[c]
            a, bb = coef[c]
            hs[c] = a * hs[c] + bb * pltpu.roll(hs[c], SSM_PACK, 0) + s_sc[c, rows, :]
    for c in range(SSM_SC):
        h_sc[c] = hs[c]

    for b in range(nb):
        hb = jnp.concatenate([hp_sc[c, pl.ds(b, rt, stride=SUBLANES), :] for c in range(SSM_SC)]
                             + [hp_sc[c, pl.ds(SSM_PACK + b, rt, stride=SUBLANES), :] for c in range(SSM_SC)],
                             axis=1)
        y = y_sc[b] + _dot(hb.astype(BF16), q_ref[...])
        s_ref[b] = jax.nn.gelu(y).astype(BF16)

    @pl.when(t == pl.num_programs(1) - 1)
    def _():
        hfin_ref[...] = h_sc[...]


def _pack_state(h):
    nj, nb, _ = h.shape
    h = jnp.pad(h, ((0, 0), (0, SSM_PACK - nb), (0, 0)))
    re = h[:, :, :SSM_SW].reshape(nj, SSM_PACK, SSM_SC, LANES)
    im = h[:, :, SSM_SW:].reshape(nj, SSM_PACK, SSM_SC, LANES)
    return jnp.concatenate([re, im], axis=1).transpose(0, 2, 1, 3)


def _unpack_state(hp, nb):
    nj = hp.shape[0]
    x = hp.transpose(0, 2, 1, 3).reshape(nj, SUBLANES, SSM_SW)
    return jnp.concatenate([x[:, :nb], x[:, SSM_PACK:SSM_PACK + nb]], axis=2)


def _ssm_prompt(u, sw, h0, nb, lc, rt):
    assert nb == SSM_PACK and lc % rt == 0
    tw = SSM_T * LANES
    u4 = u.reshape(SSM_NJ, nb, lc, tw)
    sq = 2 * SSM_SW
    kern = functools.partial(_ssm_prompt_kernel, nb=nb, rt=rt)
    st_spec = pl.BlockSpec((None, SSM_SC, SUBLANES, LANES), lambda j, t: (j, 0, 0, 0))
    s, hfin = pl.pallas_call(
        kern, grid=(SSM_NJ, lc // rt),
        in_specs=[pl.BlockSpec((None, nb, rt, tw), lambda j, t: (j, 0, t, 0)),
                  pl.BlockSpec((None, tw, tw), lambda j, t: (j, 0, 0)),
                  pl.BlockSpec((None, tw, sq), lambda j, t: (j, 0, 0)),
                  pl.BlockSpec((None, sq, tw), lambda j, t: (j, 0, 0)),
                  pl.BlockSpec((None, 1, sq), lambda j, t: (j, 0, 0)),
                  pl.BlockSpec((None, 1, tw), lambda j, t: (j, 0, 0)),
                  st_spec],
        out_specs=[pl.BlockSpec((None, nb, rt, tw), lambda j, t: (j, 0, t, 0)), st_spec],
        out_shape=[jax.ShapeDtypeStruct((SSM_NJ, nb, lc, tw), BF16),
                   jax.ShapeDtypeStruct((SSM_NJ, SSM_SC, SUBLANES, LANES), F32)],
        scratch_shapes=[pltpu.VMEM((SSM_SC, rt * SUBLANES, LANES), F32),
                        pltpu.VMEM((SSM_SC, rt * SUBLANES, LANES), F32),
                        pltpu.VMEM((SSM_SC, SUBLANES, LANES), F32),
                        pltpu.VMEM((nb, rt, tw), F32)],
        compiler_params=_cp(("arbitrary", "arbitrary")), name="ssm_prompt",
    )(u4, sw["M"], sw["P"], sw["Q"], sw["lam"], sw["D"], _pack_state(h0))
    return s.reshape(SSM_NJ, nb * lc, tw), _unpack_state(hfin, nb)


def _ssm_state_kernel(u_ref, p_ref, lam_ref, h_ref):
    s = _dot(u_ref[...].astype(BF16), p_ref[...])
    lam = lam_ref[...]
    lr, li = lam[:, :SSM_SW], lam[:, SSM_SW:]
    h = s[0:1, :]
    for r in range(1, s.shape[0]):
        hre, him = h[:, :SSM_SW], h[:, SSM_SW:]
        h = jnp.concatenate([lr * hre - li * him, lr * him + li * hre], axis=1) + s[r:r + 1, :]
    h_ref[...] = h


def _ssm_state(u, sw, lc):
    tw = SSM_T * LANES
    sq = 2 * SSM_SW
    blk = lambda j: (j, 0, 0)
    return pl.pallas_call(
        _ssm_state_kernel, grid=(SSM_NJ,),
        in_specs=[pl.BlockSpec((None, lc, tw), blk), pl.BlockSpec((None, tw, sq), blk),
                  pl.BlockSpec((None, 1, sq), blk)],
        out_specs=pl.BlockSpec((None, 1, sq), blk),
        out_shape=jax.ShapeDtypeStruct((SSM_NJ, 1, sq), F32),
        compiler_params=_cp(("arbitrary",)), name="ssm_state",
    )(u, sw["P"], sw["lam"])


def _ssm_sample_kernel(u_ref, m_ref, p_ref, q_ref, lam_ref, d_ref, h0_ref, s_ref, hfin_ref, s_sc,
                       *, nb, lc):
    u = u_ref[...]
    u16 = u.astype(BF16)
    s_sc[...] = _dot(u16, p_ref[...])
    lam = lam_ref[...]
    lr, li = lam[:, :SSM_SW], lam[:, SSM_SW:]
    h = h0_ref[...]
    for r in range(lc):
        rows = slice(r * nb, (r + 1) * nb)
        s = s_sc[rows, :]
        s_sc[rows, :] = h
        hre, him = h[:, :SSM_SW], h[:, SSM_SW:]
        h = jnp.concatenate([lr * hre - li * him + s[:, :SSM_SW], lr * him + li * hre + s[:, SSM_SW:]], axis=1)
    hfin_ref[...] = h
    y = _dot_causal(u16, m_ref) + _dot(s_sc[...].astype(BF16), q_ref[...]) + u * d_ref[...]
    s_ref[...] = jax.nn.gelu(y).astype(BF16)


def _ssm_sample(u, sw, h0, nb, lc):
    tw = SSM_T * LANES
    sq = 2 * SSM_SW
    u2 = u.reshape(SSM_NJ, nb, lc, tw).transpose(0, 2, 1, 3).reshape(SSM_NJ, lc * nb, tw)
    kern = functools.partial(_ssm_sample_kernel, nb=nb, lc=lc)
    blk = lambda j: (j, 0, 0)
    s, hfin = pl.pallas_call(
        kern, grid=(SSM_NJ,),
        in_specs=[pl.BlockSpec((None, lc * nb, tw), blk), pl.BlockSpec((None, tw, tw), blk),
                  pl.BlockSpec((None, tw, sq), blk), pl.BlockSpec((None, sq, tw), blk),
                  pl.BlockSpec((None, 1, sq), blk), pl.BlockSpec((None, 1, tw), blk),
                  pl.BlockSpec((None, nb, sq), blk)],
        out_specs=[pl.BlockSpec((None, lc * nb, tw), blk), pl.BlockSpec((None, nb, sq), blk)],
        out_shape=[jax.ShapeDtypeStruct((SSM_NJ, lc * nb, tw), BF16),
                   jax.ShapeDtypeStruct((SSM_NJ, nb, sq), F32)],
        scratch_shapes=[pltpu.VMEM((lc * nb, sq), F32)],
        compiler_params=_cp(("arbitrary",)), name="ssm_sample",
    )(u2, sw["M"], sw["P"], sw["Q"], sw["lam"], sw["D"], h0)
    s = s.reshape(SSM_NJ, lc, nb, tw).transpose(0, 2, 1, 3).reshape(SSM_NJ, nb * lc, tw)
    return s, hfin


def _ssm_fold_kernel(wre_ref, wim_ref, cre_ref, cim_ref, plr_ref, pli_ref, rep_ref, m_ref, p_ref, q_ref):
    def split(a, terms):
        out = []
        for _ in range(terms):
            t = a.astype(BF16)
            out.append(t)
            a = a - t.astype(F32)
        return out

    def repeat_lanes(a, rep):
        return sum(_dot(t, rep) for t in split(a, 3))

    def dot_nt_hi_lo(a, b_parts):
        a_hi, a_lo = split(a, 2)
        b_hi, b_lo = b_parts
        return _dot_nt(a_hi, b_hi) + (_dot_nt(a_hi, b_lo) + _dot_nt(a_lo, b_hi))

    same_group = (lax.broadcasted_iota(jnp.int32, (LANES, SSM_SW), 0) // SSM_GROUP
                  == lax.broadcasted_iota(jnp.int32, (LANES, SSM_SW), 1) // SSM_STATE)
    rep = rep_ref[...]
    wb_re = jnp.where(same_group, repeat_lanes(wre_ref[...], rep), 0.0)
    wb_im = jnp.where(same_group, repeat_lanes(wim_ref[...], rep), 0.0)
    cb_re = jnp.where(same_group, repeat_lanes(cre_ref[...], rep), 0.0)
    cb_im = jnp.where(same_group, repeat_lanes(cim_ref[...], rep), 0.0)
    cb_re_parts, cb_im_parts = split(cb_re, 2), split(cb_im, 2)

    m_ref[...] = jnp.zeros_like(m_ref)
    for k in range(SSM_T):
        lr, li = plr_ref[k:k + 1, :], pli_ref[k:k + 1, :]
        v_re = lr * wb_re - li * wb_im
        v_im = lr * wb_im + li * wb_re
        s = SSM_T - 1 - k
        p_ref[s * LANES:(s + 1) * LANES, :SSM_SW] = v_re.astype(BF16)
        p_ref[s * LANES:(s + 1) * LANES, SSM_SW:] = v_im.astype(BF16)
        bd = (dot_nt_hi_lo(v_re, cb_re_parts) - dot_nt_hi_lo(v_im, cb_im_parts)).astype(BF16)
        for s in range(SSM_T - k):
            t = s + k
            m_ref[s * LANES:(s + 1) * LANES, t * LANES:(t + 1) * LANES] = bd
        lr1, li1 = plr_ref[k + 1:k + 2, :], pli_ref[k + 1:k + 2, :]
        q_ref[:SSM_SW, k * LANES:(k + 1) * LANES] = (cb_re * lr1 - cb_im * li1).T.astype(BF16)
        q_ref[SSM_SW:, k * LANES:(k + 1) * LANES] = (-(cb_re * li1 + cb_im * lr1)).T.astype(BF16)


def _ssm_weights(a_re, a_im, log_dt, b_re, b_im, c_re, c_im, d):
    t_ = SSM_T
    dt = jnp.exp(log_dt)[:, None]
    mag = jnp.exp(a_re * dt)
    lam_re, lam_im = mag * jnp.cos(a_im * dt), mag * jnp.sin(a_im * dt)
    den = a_re * a_re + a_im * a_im
    f_re = ((lam_re - 1.0) * a_re + lam_im * a_im) / den
    f_im = (lam_im * a_re - (lam_re - 1.0) * a_im) / den
    w_re = f_re[:, :, None] * b_re - f_im[:, :, None] * b_im
    w_im = f_re[:, :, None] * b_im + f_im[:, :, None] * b_re
    k = jnp.arange(t_ + 1, dtype=F32)[None, :, None]
    pmag = jnp.exp(a_re[:, None, :] * dt[:, :, None] * k)
    pang = a_im[:, None, :] * dt[:, :, None] * k
    pw_re, pw_im = pmag * jnp.cos(pang), pmag * jnp.sin(pang)

    def lanes_gn(x):
        x = x.reshape(SSM_NJ, SSM_GPB, t_ + 1, SSM_STATE)
        return jnp.swapaxes(x, 1, 2).reshape(SSM_NJ, t_ + 1, SSM_SW)

    pl_re, pl_im = lanes_gn(pw_re), lanes_gn(pw_im)
    wcat = [jnp.swapaxes(x, 1, 2).reshape(SSM_NJ, LANES, SSM_STATE) for x in (w_re, w_im)]
    ccat = [x.reshape(SSM_NJ, LANES, SSM_STATE) for x in (c_re, c_im)]
    rep = jnp.tile(jnp.eye(SSM_STATE, dtype=BF16), (1, SSM_GPB))
    tw = t_ * LANES
    sq = 2 * SSM_SW
    blk = lambda j: (j, 0, 0)
    const = lambda j: (0, 0)
    per_j = lambda a: pl.BlockSpec((None,) + a.shape[1:], blk)
    ins = [wcat[0], wcat[1], ccat[0], ccat[1], pl_re, pl_im]
    m, p, q = pl.pallas_call(
        _ssm_fold_kernel, grid=(SSM_NJ,),
        in_specs=[per_j(a) for a in ins] + [pl.BlockSpec(rep.shape, const)],
        out_specs=[pl.BlockSpec((None, tw, tw), blk), pl.BlockSpec((None, tw, sq), blk),
                   pl.BlockSpec((None, sq, tw), blk)],
        out_shape=[jax.ShapeDtypeStruct((SSM_NJ, tw, tw), BF16), jax.ShapeDtypeStruct((SSM_NJ, tw, sq), BF16),
                   jax.ShapeDtypeStruct((SSM_NJ, sq, tw), BF16)],
        compiler_params=_cp(("arbitrary",)), name="ssm_fold",
    )(*ins, rep)
    lam_t = jnp.concatenate([pl_re[:, t_][:, None, :], pl_im[:, t_][:, None, :]], axis=2)
    dvec = jnp.tile(d.reshape(SSM_NJ, 1, LANES), (1, 1, t_))
    return {"M": m, "P": p, "Q": q, "lam": lam_t, "D": dvec}


def _mixer_kernel(o_ref, s_ref, ga_ref, gb_ref, x_ref, wo_ref, wv_ref, wg_ref, wout_ref, h_ref, s_sc):
    o_a = _dot(o_ref[...], wo_ref[...])
    nchunk = s_sc.shape[1] // SSM_T
    for j in range(SSM_NJ):
        sj = s_ref[j].astype(F32)
        for t in range(SSM_T):
            s_sc[j, pl.ds(t, nchunk, stride=SSM_T), :] = sj[:, t * LANES:(t + 1) * LANES]
    s = jnp.concatenate([s_sc[j] for j in range(SSM_NJ)], axis=1).astype(BF16)
    o_b = _dot(s, wv_ref[...]) * jax.nn.sigmoid(_dot(s, wg_ref[...]))
    merged = jax.nn.sigmoid(ga_ref[...]) * o_a + jax.nn.sigmoid(gb_ref[...]) * o_b
    h_ref[...] = x_ref[...] + _dot(merged.astype(BF16), wout_ref[...])


def _mixer(o, s, ga, gb, x, w, tm):
    n = x.shape[0]
    row = lambda i: (i, 0)
    const = lambda i: (0, 0)
    wspec = pl.BlockSpec((D_MODEL, D_MODEL), const, pipeline_mode=pl.Buffered(1))
    act = pl.BlockSpec((tm, D_MODEL), row)
    return pl.pallas_call(
        _mixer_kernel, grid=(n // tm,),
        in_specs=[act, pl.BlockSpec((SSM_NJ, tm // SSM_T, SSM_T * LANES), lambda i: (0, i, 0)), act, act, act,
                  wspec, wspec, wspec, wspec],
        out_specs=act, out_shape=jax.ShapeDtypeStruct((n, D_MODEL), F32),
        scratch_shapes=[pltpu.VMEM((SSM_NJ, tm, LANES), F32)],
        compiler_params=_cp(("arbitrary",)), name="mixer_out",
    )(o, s, ga, gb, x, w["wo"], w["wv"], w["wg"], w["wout"])


def _mlp_kernel(h_ref, g_ref, wup_ref, wdn_ref, y_ref):
    h = h_ref[...]
    hn = _rms(h, g_ref[...]).astype(BF16)
    a = jnp.maximum(_dot(hn, wup_ref[...]), 0.0)
    y_ref[...] = h + _dot((a * a).astype(BF16), wdn_ref[...])


def _mlp(h, w, tm):
    n = h.shape[0]
    row = lambda i: (i, 0)
    const = lambda i: (0, 0)
    act = pl.BlockSpec((tm, D_MODEL), row)
    return pl.pallas_call(
        _mlp_kernel, grid=(n // tm,),
        in_specs=[act, pl.BlockSpec((1, D_MODEL), const),
                  pl.BlockSpec((D_MODEL, D_FF), const, pipeline_mode=pl.Buffered(1)),
                  pl.BlockSpec((D_FF, D_MODEL), const, pipeline_mode=pl.Buffered(1))],
        out_specs=act, out_shape=jax.ShapeDtypeStruct((n, D_MODEL), F32),
        compiler_params=_cp(("arbitrary",)), name="mlp",
    )(h, w["gmlp"], w["wup"], w["wdn"])


def _cos_sin(pos):
    half = QK_ROPE // 2
    inv = ROPE_THETA ** (-jnp.arange(half, dtype=F32) / half)
    inv = jnp.concatenate([inv, inv, jnp.zeros((LANES - QK_ROPE,), F32)])
    ang = pos.astype(F32)[:, None] * inv[None, :]
    return jnp.cos(ang), jnp.sin(ang)


def _rope_tables(pos=None, length=None):
    if pos is not None:
        cos, sin = _cos_sin(pos)
    else:
        ch, sh = _cos_sin(jnp.arange(length // CHUNK, dtype=jnp.int32) * CHUNK)
        cl, sl = _cos_sin(jnp.arange(CHUNK, dtype=jnp.int32))
        cos = (ch[:, None, :] * cl[None] - sh[:, None, :] * sl[None]).reshape(length, LANES)
        sin = (sh[:, None, :] * cl[None] + ch[:, None, :] * sl[None]).reshape(length, LANES)
    lane = jnp.arange(LANES)
    half = QK_ROPE // 2
    ct = jnp.where(lane < QK_ROPE, cos, 0.0)
    s1 = jnp.where(lane < half, -sin, 0.0)
    s2 = jnp.where((lane >= half) & (lane < QK_ROPE), sin, 0.0)
    return ct, s1, s2


def _pad_lanes(a, width):
    return jnp.pad(a, [(0, 0)] * (a.ndim - 1) + [(0, width - a.shape[-1])])


def _layer_weights(norm_mix, w_in, q_lora_norm, w_uq, q_nope_norm, q_rope_norm, kv_lora_norm, k_rope_norm,
                   w_ukv, k_nope_norm, w_o_attn, w_glu_v, w_glu_g, w_out, norm_mlp, w_mlp_up, w_mlp_down):
    bf = lambda a: a.astype(BF16)
    r2 = lambda a: a.reshape(1, -1).astype(F32)
    return {
        "nmix": r2(norm_mix),
        "win": bf(jnp.swapaxes(w_in, 0, 1)),
        "gq": r2(q_lora_norm),
        "wuqn": bf(w_uq[:, :, :QK_NOPE].reshape(Q_LORA, N_HEADS * QK_NOPE)),
        "wuqr": bf(_pad_lanes(w_uq[:, :, QK_NOPE:], LANES).reshape(Q_LORA, N_HEADS * LANES)),
        "gqn": r2(q_nope_norm) * Q_SCALE, "gqr": _pad_lanes(r2(q_rope_norm), LANES) * Q_SCALE,
        "gkv": r2(kv_lora_norm), "gkr": _pad_lanes(r2(k_rope_norm), LANES),
        "wuk": bf(w_ukv[:, :, :QK_NOPE].reshape(KV_LORA, N_HEADS * QK_NOPE)),
        "wukg": bf((w_ukv[:, :, :QK_NOPE] * k_nope_norm.astype(F32)).reshape(KV_LORA, N_HEADS * QK_NOPE)),
        "wuv": bf(w_ukv[:, :, QK_NOPE:].reshape(KV_LORA, N_HEADS * V_DIM)),
        "gkn": r2(k_nope_norm),
        "wo": bf(w_o_attn.reshape(N_HEADS * V_DIM, D_MODEL)),
        "wv": bf(w_glu_v), "wg": bf(w_glu_g), "wout": bf(w_out),
        "gmlp": r2(norm_mlp), "wup": bf(w_mlp_up), "wdn": bf(w_mlp_down),
    }


def kernel(x_prompt, x_sample, cache_latent, cache_krope, cache_meta_latent, cache_meta_krope, state_ssm_re, state_ssm_im, meta_tokens, norm_mix, w_in, q_lora_norm, w_uq, q_nope_norm, q_rope_norm, kv_lora_norm, k_rope_norm, w_ukv, k_nope_norm, w_o_attn, ssm_a_re, ssm_a_im, ssm_log_dt, ssm_b_re, ssm_b_im, ssm_c_re, ssm_c_im, ssm_d, w_glu_v, w_glu_g, w_out, norm_mlp, w_mlp_up, w_mlp_down):
    bsz_p, seq_p = x_prompt.shape[0], x_prompt.shape[1]
    bsz_s, seq_s = x_sample.shape[0], x_sample.shape[1]
    past = cache_latent.shape[2]
    depth = w_in.shape[0]
    assert depth == 1, "single-layer step"
    l = 0
    w = _layer_weights(norm_mix[l], w_in[l], q_lora_norm[l], w_uq[l], q_nope_norm[l], q_rope_norm[l],
                       kv_lora_norm[l], k_rope_norm[l], w_ukv[l], k_nope_norm[l], w_o_attn[l], w_glu_v[l],
                       w_glu_g[l], w_out[l], norm_mlp[l], w_mlp_up[l], w_mlp_down[l])
    sw = _ssm_weights(ssm_a_re[l].astype(F32), ssm_a_im[l].astype(F32), ssm_log_dt[l].astype(F32),
                      ssm_b_re[l].astype(F32), ssm_b_im[l].astype(F32), ssm_c_re[l].astype(F32),
                      ssm_c_im[l].astype(F32), ssm_d[l].astype(F32))
    sq = 2 * SSM_SW

    tabs_m = _rope_tables(jnp.arange(N_META, dtype=jnp.int32) - N_META)
    (_, mk, mv, mckv, mkr, mu, _, _) = _front(meta_tokens.astype(F32), tabs_m, 1, w, N_META)
    h_meta = _ssm_state(mu, sw, N_META // SSM_T)

    n_p = bsz_p * seq_p
    xp = x_prompt.reshape(n_p, D_MODEL)
    tabs_p = _rope_tables(length=seq_p)
    q, k, v, ckv_p, kr_p, u_p, ga, gb = _front(xp, tabs_p, seq_p // TM_PROMPT, w, TM_PROMPT, kr_seq=seq_p)
    o_p = _attn_prompt(q, k, v, mk, mv, bsz_p, seq_p)
    h0_p = jnp.broadcast_to(h_meta, (SSM_NJ, bsz_p, sq))
    s_p, hfin_p = _ssm_prompt(u_p, sw, h0_p, bsz_p, seq_p // SSM_T, SSM_RT)
    h1_p = _mixer(o_p, s_p, ga, gb, xp, w, TM_PROMPT)
    y_p = _mlp(h1_p, w, TM_PROMPT)

    n_s = bsz_s * seq_s
    xs = x_sample.reshape(n_s, D_MODEL)
    pos_s = past + jnp.arange(seq_s, dtype=jnp.int32)
    tabs_s = tuple(jnp.tile(t, (bsz_s, 1)) for t in _rope_tables(pos_s))
    q_s, k_s, _, ckv_s, kr_s, u_s, ga_s, gb_s = _front(xs, tabs_s, n_s // TM_SAMPLE, w, TM_SAMPLE)
    o_s = _attn_sample(q_s, k_s, ckv_s, cache_latent[l].astype(F32), cache_krope[l].astype(F32),
                       cache_meta_latent[l].astype(F32), cache_meta_krope[l].astype(F32), w, bsz_s, seq_s, past)

    def to_blocks(st):
        return st.astype(F32).reshape(bsz_s, SSM_NJ, SSM_SW).transpose(1, 0, 2)

    h0_s = jnp.concatenate([to_blocks(state_ssm_re[l]), to_blocks(state_ssm_im[l])], axis=2)
    s_s, hfin_s = _ssm_sample(u_s, sw, h0_s, bsz_s, seq_s // SSM_T)
    h1_s = _mixer(o_s, s_s, ga_s, gb_s, xs, w, TM_SAMPLE)
    y_s = _mlp(h1_s, w, TM_SAMPLE)

    def from_blocks(hf, nb):
        re = hf[:, :, :SSM_SW].transpose(1, 0, 2).reshape(1, nb, N_GROUPS, SSM_STATE)
        im = hf[:, :, SSM_SW:].transpose(1, 0, 2).reshape(1, nb, N_GROUPS, SSM_STATE)
        return re, im

    sre_p, sim_p = from_blocks(hfin_p, bsz_p)
    sre_s, sim_s = from_blocks(hfin_s, bsz_s)
    return (y_p.reshape(bsz_p, seq_p, D_MODEL), y_s.reshape(bsz_s, seq_s, D_MODEL),
            ckv_p.reshape(1, bsz_p, seq_p, KV_LORA), jnp.swapaxes(kr_p, 1, 2)[None],
            jnp.broadcast_to(mckv[None, None], (1, bsz_p, N_META, KV_LORA)),
            jnp.broadcast_to(mkr[None, None], (1, bsz_p, N_META, QK_ROPE)),
            sre_p, sim_p,
            ckv_s.reshape(1, bsz_s, seq_s, KV_LORA), kr_s.reshape(1, bsz_s, seq_s, QK_ROPE),
            sre_s, sim_s)
```

```python
import functools
import math

import jax
import jax.numpy as jnp
from jax import lax
from jax.experimental import pallas as pl
from jax.experimental.pallas import tpu as pltpu

F32 = jnp.float32
BF16 = jnp.bfloat16

D_MODEL = 1024
CHUNK = 64
N_META = 16
N_HEADS = 8
QK_NOPE = 128
QK_ROPE = 64
V_DIM = 128
QK_DIM = QK_NOPE + QK_ROPE
Q_LORA = 384
KV_LORA = 256
SSM_GROUP = 16
N_GROUPS = D_MODEL // SSM_GROUP
SSM_STATE = 64
D_FF = 4 * D_MODEL
ROPE_THETA = 10000.0
EPS = 1e-6
ATTN_SCALE = QK_DIM ** -0.5
Q_SCALE = ATTN_SCALE * math.log2(math.e)
NEG_INF = -1e30

LANES = 128
HEAD_PAD = 2 * LANES
SSM_T = 8
SSM_NJ = D_MODEL // LANES
SSM_GPB = LANES // SSM_GROUP
SSM_SW = SSM_GPB * SSM_STATE
VMEM_LIMIT = 56 * 1024 * 1024
TM_PROMPT = 512
TM_SAMPLE = 512
SSM_RT = 256


def _cp(sem):
    return pltpu.CompilerParams(dimension_semantics=sem, vmem_limit_bytes=VMEM_LIMIT)


def _dot(a, b):
    return jnp.dot(a, b, preferred_element_type=F32)


def _dot_nt(a, b):
    return lax.dot_general(a, b, (((1,), (1,)), ((), ())), preferred_element_type=F32)


def _rms(x, g, n=None):
    n = x.shape[-1] if n is None else n
    ms = jnp.sum(x * x, axis=-1, keepdims=True) * (1.0 / n)
    return x * lax.rsqrt(ms + EPS) * g


def _rope128(b, ct, s1, s2):
    return b * ct + pltpu.roll(b, LANES - QK_ROPE // 2, 1) * s1 + pltpu.roll(b, QK_ROPE // 2, 1) * s2


IN_Q = 0
IN_KV = IN_Q + Q_LORA
IN_KR = IN_KV + KV_LORA
IN_U = IN_KR + QK_ROPE
IN_GA = IN_U + D_MODEL
IN_GB = IN_GA + D_MODEL
IN_END = IN_GB + D_MODEL


def _front_kernel(x_ref, ct_ref, s1_ref, s2_ref, nmix_ref, win_ref, gq_ref, wuqn_ref, wuqr_ref, gqn_ref,
                  gqr_ref, gkv_ref, gkr_ref, wuk_ref, wuv_ref, gkn_ref,
                  q_ref, k_ref, v_ref, ckv_ref, kr_ref, u_ref, ga_ref, gb_ref, u_sc, *, kr_transposed):
    x = x_ref[...]
    xn = _rms(x, nmix_ref[...]).astype(BF16)
    ct, s1, s2 = ct_ref[...], s1_ref[...], s2_ref[...]


    q_lat = _dot_nt(xn, win_ref[IN_Q:IN_KV, :])
    kv_lat = _dot_nt(xn, win_ref[IN_KV:IN_KR, :])
    kr_raw = _dot_nt(xn, win_ref[IN_KR:IN_KR + LANES, :])
    kr_raw = jnp.where(lax.broadcasted_iota(jnp.int32, kr_raw.shape, 1) < QK_ROPE, kr_raw, 0.0)

    c_q = _rms(q_lat, gq_ref[...]).astype(BF16)
    qn = _dot(c_q, wuqn_ref[...])
    qr = _dot(c_q, wuqr_ref[...])
    c_kv = _rms(kv_lat, gkv_ref[...])
    ckv_ref[...] = c_kv
    c16 = c_kv.astype(BF16)
    kn = _dot(c16, wuk_ref[...])
    v_ref[...] = _dot(c16, wuv_ref[...]).astype(BF16)

    for h in range(N_HEADS):
        sl = slice(h * LANES, (h + 1) * LANES)
        q_ref[:, h * HEAD_PAD:h * HEAD_PAD + LANES] = _rms(qn[:, sl], gqn_ref[...]).astype(BF16)
        r = _rope128(_rms(qr[:, sl], gqr_ref[...], QK_ROPE), ct, s1, s2)
        q_ref[:, h * HEAD_PAD + LANES:(h + 1) * HEAD_PAD] = r.astype(BF16)
    kr = _rms(kr_raw, gkr_ref[...], QK_ROPE)
    kr = _rope128(kr, ct, s1, s2)
    kr_ref[...] = kr.T[:QK_ROPE, :] if kr_transposed else kr[:, :QK_ROPE]
    kr16 = kr.astype(BF16)
    for h in range(N_HEADS):
        sl = slice(h * LANES, (h + 1) * LANES)
        k_ref[:, h * HEAD_PAD:h * HEAD_PAD + LANES] = _rms(kn[:, sl], gkn_ref[...]).astype(BF16)
        k_ref[:, h * HEAD_PAD + LANES:(h + 1) * HEAD_PAD] = kr16

    ga_ref[...] = _dot_nt(xn, win_ref[IN_GA:IN_GB, :])
    u = _dot_nt(xn, win_ref[IN_U:IN_GA, :])
    nchunk = u_sc.shape[1] // SSM_T
    for j in range(SSM_NJ):
        u_sc[j] = u[:, j * LANES:(j + 1) * LANES]
        for t in range(SSM_T):
            u_ref[j, :, t * LANES:(t + 1) * LANES] = u_sc[j, pl.ds(t, nchunk, stride=SSM_T), :]
    gb_ref[...] = _dot_nt(xn, win_ref[IN_GB:IN_END, :])


def _front(x, tabs, tab_blocks, w, tm, kr_seq=None):
    n = x.shape[0]
    nt = n // tm
    ct, s1, s2 = tabs

    def row(i):
        return (i, 0)

    def tab(i):
        return (i % tab_blocks, 0)

    def const(i):
        return (0, 0)

    def full(a):
        return pl.BlockSpec(a.shape, const, pipeline_mode=pl.Buffered(1))

    weights = [w["nmix"], w["win"], w["gq"], w["wuqn"], w["wuqr"], w["gqn"], w["gqr"], w["gkv"], w["gkr"],
               w["wuk"], w["wuv"], w["gkn"]]
    wide = N_HEADS * LANES
    qk_wide = N_HEADS * HEAD_PAD
    out_shape = [
        jax.ShapeDtypeStruct((n, qk_wide), BF16),
        jax.ShapeDtypeStruct((n, qk_wide), BF16),
        jax.ShapeDtypeStruct((n, wide), BF16),
        jax.ShapeDtypeStruct((n, KV_LORA), F32),
        (jax.ShapeDtypeStruct((n, QK_ROPE), F32) if kr_seq is None
         else jax.ShapeDtypeStruct((n // kr_seq, QK_ROPE, kr_seq), F32)),
        jax.ShapeDtypeStruct((SSM_NJ, n // SSM_T, SSM_T * LANES), F32),
        jax.ShapeDtypeStruct((n, D_MODEL), F32),
        jax.ShapeDtypeStruct((n, D_MODEL), F32),
    ]
    out_specs = [
        pl.BlockSpec((tm, qk_wide), row), pl.BlockSpec((tm, qk_wide), row),
        pl.BlockSpec((tm, wide), row), pl.BlockSpec((tm, KV_LORA), row),
        (pl.BlockSpec((tm, QK_ROPE), row) if kr_seq is None else
         pl.BlockSpec((None, QK_ROPE, tm), lambda i: (i // (kr_seq // tm), 0, i % (kr_seq // tm)))),
        pl.BlockSpec((SSM_NJ, tm // SSM_T, SSM_T * LANES), lambda i: (0, i, 0)),
        pl.BlockSpec((tm, D_MODEL), row), pl.BlockSpec((tm, D_MODEL), row),
    ]
    in_specs = ([pl.BlockSpec((tm, D_MODEL), row)] + [pl.BlockSpec((tm, LANES), tab)] * 3
                + [full(a) for a in weights])
    return pl.pallas_call(
        functools.partial(_front_kernel, kr_transposed=kr_seq is not None), grid=(nt,), in_specs=in_specs,
        out_specs=out_specs, out_shape=out_shape,
        scratch_shapes=[pltpu.VMEM((SSM_NJ, tm, LANES), F32)],
        compiler_params=_cp(("arbitrary",)), name="front",
    )(x, ct, s1, s2, *weights)


ATT_T = 512


def _with_ones(v):
    return jnp.concatenate([v, jnp.ones_like(v)], axis=1)


def _softmax_step(s, v, m_ref, acc_ref):
    m_prev = m_ref[...]
    m_new = jnp.maximum(m_prev, jnp.max(s, axis=-1, keepdims=True))
    alpha = jnp.exp2(m_prev - m_new)
    p = jnp.exp2(s - jnp.tile(m_new, (1, s.shape[1] // LANES)))
    acc_ref[...] = jnp.tile(alpha, (1, 2)) * acc_ref[...] + _dot(p.astype(BF16), _with_ones(v))
    m_ref[...] = m_new


ATT_HG = 4


def _attn_prompt_kernel(q_ref, k_ref, v_ref, mk_ref, mv_ref, o_ref, a_sc, b_sc, m_ref, acc_ref):
    step = pl.program_id(2)
    npairs = step
    heads = range(ATT_HG)

    def key_rows(kt):
        return pl.ds(pl.multiple_of(kt * ATT_T, ATT_T), ATT_T)

    def q_rows(slot):
        return slice(slot * ATT_T, (slot + 1) * ATT_T)

    def qk_lanes(g):
        return slice(g * HEAD_PAD, (g + 1) * HEAD_PAD)

    def v_lanes(g):
        return slice(g * V_DIM, (g + 1) * V_DIM)

    def scores(slot, g, kt):
        return _dot_nt(q_ref[q_rows(slot), qk_lanes(g)], k_ref[key_rows(kt), qk_lanes(g)])

    def consume(slot, g, s, kt, diagonal):
        if diagonal:
            qc = lax.broadcasted_iota(jnp.int32, (ATT_T, ATT_T), 0) // CHUNK
            kc = lax.broadcasted_iota(jnp.int32, (ATT_T, ATT_T), 1) // CHUNK
            s = jnp.where(kc <= qc, s, NEG_INF)
        _softmax_step(s, v_ref[key_rows(kt), v_lanes(g)], m_ref.at[slot, g], acc_ref.at[slot, g])

    def head(slot):
        for g in heads:
            a_sc[slot, g] = scores(slot, g, 0)
        for g in heads:
            s = _dot_nt(q_ref[q_rows(slot), qk_lanes(g)], mk_ref[:, qk_lanes(g)])
            m0 = jnp.max(s, axis=-1, keepdims=True)
            p = jnp.exp2(s - m0)
            m_ref[slot, g] = jnp.broadcast_to(m0, (ATT_T, LANES))
            acc_ref[slot, g] = _dot(p.astype(BF16), _with_ones(mv_ref[:, v_lanes(g)]))

    def full_pairs(slot):
        def pair(pi, carry):
            kt = 2 * pi
            for g in heads:
                b_sc[g] = scores(slot, g, kt + 1)
                consume(slot, g, a_sc[slot, g], kt, False)
            for g in heads:
                a_sc[slot, g] = scores(slot, g, kt + 2)
                consume(slot, g, b_sc[g], kt + 1, False)
            return carry

        lax.fori_loop(0, npairs, pair, 0)

    def tail(slot):
        i = 2 * step + slot
        if slot == 1:
            for g in heads:
                b_sc[g] = scores(slot, g, i)
                consume(slot, g, a_sc[slot, g], i - 1, False)
            for g in heads:
                consume(slot, g, b_sc[g], i, True)
        else:
            for g in heads:
                consume(slot, g, a_sc[slot, g], i, True)

    def finish(slot):
        for g in heads:
            o_ref[q_rows(slot), v_lanes(g)] = (acc_ref[slot, g, :, :V_DIM]
                                               / acc_ref[slot, g, :, V_DIM:]).astype(BF16)

    head(0)
    head(1)
    full_pairs(0)
    tail(0)
    finish(0)
    full_pairs(1)
    tail(1)
    finish(1)


def _attn_prompt(q, k, v, mk, mv, bsz, seq):
    nstep = seq // (2 * ATT_T)
    n = bsz * seq
    qmap = lambda b, h, i: (b * nstep + i, h)
    return pl.pallas_call(
        _attn_prompt_kernel, grid=(bsz, N_HEADS // ATT_HG, nstep),
        in_specs=[pl.BlockSpec((2 * ATT_T, ATT_HG * HEAD_PAD), qmap),
                  pl.BlockSpec((seq, ATT_HG * HEAD_PAD), lambda b, h, i: (b, h)),
                  pl.BlockSpec((seq, ATT_HG * V_DIM), lambda b, h, i: (b, h)),
                  pl.BlockSpec((N_META, ATT_HG * HEAD_PAD), lambda b, h, i: (0, h)),
                  pl.BlockSpec((N_META, ATT_HG * V_DIM), lambda b, h, i: (0, h))],
        out_specs=pl.BlockSpec((2 * ATT_T, ATT_HG * V_DIM), qmap),
        out_shape=jax.ShapeDtypeStruct((n, N_HEADS * V_DIM), BF16),
        scratch_shapes=[pltpu.VMEM((2, ATT_HG, ATT_T, ATT_T), F32), pltpu.VMEM((ATT_HG, ATT_T, ATT_T), F32),
                        pltpu.VMEM((2, ATT_HG, ATT_T, LANES), F32),
                        pltpu.VMEM((2, ATT_HG, ATT_T, 2 * V_DIM), F32)],
        compiler_params=_cp(("arbitrary", "arbitrary", "arbitrary")), name="attn_prompt",
    )(q, k, v, mk, mv)


SAMPLE_KEY_CHUNK = 256


def _attn_sample_kernel(q_ref, cm_ref, cn_ref, cc_ref, krm_ref, krn_ref, krc_ref, wuk_ref, wukg_ref, wuv_ref,
                        hsum_ref, o_ref, c_sc, kr_sc, st_sc, pt_sc, *, seq, past):
    n_small = N_META + seq
    n_keys = n_small + past

    c_sc[:, KV_LORA:] = jnp.ones((n_keys, LANES), BF16)
    c_sc[0:N_META, :KV_LORA] = cm_ref[...].astype(BF16)
    c_sc[N_META:n_small, :KV_LORA] = cn_ref[...].astype(BF16)
    c_sc[n_small:, :KV_LORA] = cc_ref[...].astype(BF16)
    kr_sc[:, QK_ROPE:] = jnp.zeros((n_small, LANES - QK_ROPE), BF16)
    kr_sc[0:N_META, :QK_ROPE] = krm_ref[...].astype(BF16)
    kr_sc[N_META:n_small, :] = krn_ref[...]

    qa, qr = [], []
    for h in range(N_HEADS):
        qn_h = q_ref[:, h * HEAD_PAD:h * HEAD_PAD + LANES]
        qa.append(_dot_nt(qn_h, wukg_ref[:, h * LANES:(h + 1) * LANES]))
        qr.append(q_ref[:, h * HEAD_PAD + LANES:(h + 1) * HEAD_PAD])
    qa = jnp.concatenate(qa, axis=0).astype(BF16)
    qr = jnp.concatenate(qr, axis=0)

    bounds = [(0, n_small)] + [(n_small + i, n_small + i + SAMPLE_KEY_CHUNK)
                               for i in range(0, past, SAMPLE_KEY_CHUNK)]
    m = None
    for lo, hi in bounds:
        c16 = c_sc[lo:hi, :KV_LORA]
        kn = _dot(c16, wuk_ref[...])
        ssq = _dot((kn * kn).astype(BF16), hsum_ref[...])
        r = lax.rsqrt(ssq * (1.0 / QK_NOPE) + EPS)
        if lo == 0:
            s_rope = _dot_nt(kr_sc[...], qr)
        else:
            kr_t = krc_ref[:, lo - n_small:hi - n_small].astype(BF16)
            s_rope = lax.dot_general(kr_t, qr[:, :QK_ROPE], (((0,), (1,)), ((), ())),
                                     preferred_element_type=F32)
        st = _dot_nt(c16, qa) * r + s_rope
        st_sc[lo:hi, :] = st
        cm = jnp.max(st, axis=0, keepdims=True)
        m = cm if m is None else jnp.maximum(m, cm)

    for lo, hi in bounds:
        pt_sc[lo:hi, :] = jnp.exp2(st_sc[lo:hi, :] - m).astype(BF16)
    pc = lax.dot_general(pt_sc[...], c_sc[...], (((0,), (0,)), ((), ())), preferred_element_type=F32)
    pcn = (pc[:, :KV_LORA] / jnp.tile(pc[:, KV_LORA:], (1, KV_LORA // LANES))).astype(BF16)
    for h in range(N_HEADS):
        o_ref[:, h * V_DIM:(h + 1) * V_DIM] = _dot(pcn[h * seq:(h + 1) * seq, :],
                                                   wuv_ref[:, h * V_DIM:(h + 1) * V_DIM]).astype(BF16)


def _attn_sample(q, k_new, c_new, cache_c, cache_kr, meta_c, meta_kr, w, bsz, seq, past):
    n_keys = N_META + seq + past
    hq = N_HEADS * seq
    hsum = jnp.repeat(jnp.repeat(jnp.eye(N_HEADS, dtype=BF16), QK_NOPE, axis=0), seq, axis=1)
    row = lambda b: (b, 0)
    b3 = lambda b: (b, 0, 0)
    const = lambda b: (0, 0)
    kern = functools.partial(_attn_sample_kernel, seq=seq, past=past)
    return pl.pallas_call(
        kern, grid=(bsz,),
        in_specs=[pl.BlockSpec((seq, N_HEADS * HEAD_PAD), row),
                  pl.BlockSpec((None, N_META, KV_LORA), b3), pl.BlockSpec((seq, KV_LORA), row),
                  pl.BlockSpec((None, past, KV_LORA), b3),
                  pl.BlockSpec((None, N_META, QK_ROPE), b3), pl.BlockSpec((seq, LANES), lambda b: (b, 1)),
                  pl.BlockSpec((None, QK_ROPE, past), b3),
                  pl.BlockSpec(w["wuk"].shape, const), pl.BlockSpec(w["wukg"].shape, const),
                  pl.BlockSpec(w["wuv"].shape, const), pl.BlockSpec(hsum.shape, const)],
        out_specs=pl.BlockSpec((seq, N_HEADS * V_DIM), row),
        out_shape=jax.ShapeDtypeStruct((bsz * seq, N_HEADS * V_DIM), BF16),
        scratch_shapes=[pltpu.VMEM((n_keys, KV_LORA + LANES), BF16), pltpu.VMEM((N_META + seq, LANES), BF16),
                        pltpu.VMEM((n_keys, hq), F32), pltpu.VMEM((n_keys, hq), BF16)],
        compiler_params=_cp(("arbitrary",)), name="attn_sample",
    )(q, meta_c, c_new, cache_c, meta_kr, k_new, jnp.swapaxes(cache_kr, 1, 2), w["wuk"], w["wukg"], w["wuv"],
      hsum)


MXU_DIM = 256


def _dot_causal(u16, m_ref):
    nblk = m_ref.shape[0] // MXU_DIM
    return jnp.concatenate(
        [_dot(u16[:, :(cb + 1) * MXU_DIM], m_ref[:(cb + 1) * MXU_DIM, cb * MXU_DIM:(cb + 1) * MXU_DIM])
         for cb in range(nblk)], axis=1)


SSM_PACK = 4
SSM_SC = SSM_SW // LANES
SUBLANES = 2 * SSM_PACK
SSM_PITCH = 12


def _ssm_prompt_kernel(u_ref, m_ref, p_ref, q_ref, lam_ref, d_ref, h0_ref, s_ref, hfin_ref, s_sc, hp_sc, h_sc, y_sc,
                       *, nb, rt):
    t = pl.program_id(1)

    @pl.when(t == 0)
    def _():
        h_sc[...] = h0_ref[...]

    for b in range(nb):
        sb = _dot(u_ref[b].astype(BF16), p_ref[...])
        for c in range(SSM_SC):
            s_sc[c, pl.ds(b, rt, stride=SSM_PITCH), :] = sb[:, c * LANES:(c + 1) * LANES]
            s_sc[c, pl.ds(SSM_PACK + b, rt, stride=SSM_PITCH), :] = sb[:, SSM_SW + c * LANES:SSM_SW + (c + 1) * LANES]
    for b in range(nb):
        ub = u_ref[b]
        y_sc[b] = _dot_causal(ub.astype(BF16), m_ref) + ub * d_ref[...]

    lam = lam_ref[...]
    im_rows = lax.broadcasted_iota(jnp.int32, (SUBLANES, LANES), 0) >= SSM_PACK
    coef = []
    for c in range(SSM_SC):
        lr = jnp.broadcast_to(lam[:, c * LANES:(c + 1) * LANES], (SUBLANES, LANES))
        li = jnp.broadcast_to(lam[:, SSM_SW + c * LANES:SSM_SW + (c + 1) * LANES], (SUBLANES, LANES))
        coef.append((lr, jnp.where(im_rows, li, -li)))

    hs = [h_sc[c] for c in range(SSM_SC)]
    for r in range(rt):
        rows = slice(r * SSM_PITCH, r * SSM_PITCH + SUBLANES)
        for c in range(SSM_SC):
            hp_sc[c, rows, :] = hs[c]
            a, bb = coef[c]
            hs[c] = a * hs[c] + bb * pltpu.roll(hs[c], SSM_PACK, 0) + s_sc[c, rows, :]
    for c in range(SSM_SC):
        h_sc[c] = hs[c]

    for b in range(nb):
        hb = jnp.concatenate([hp_sc[c, pl.ds(b, rt, stride=SSM_PITCH), :] for c in range(SSM_SC)]
                             + [hp_sc[c, pl.ds(SSM_PACK + b, rt, stride=SSM_PITCH), :] for c in range(SSM_SC)],
                             axis=1)
        y = y_sc[b] + _dot(hb.astype(BF16), q_ref[...])
        s_ref[b] = jax.nn.gelu(y).astype(BF16)

    @pl.when(t == pl.num_programs(1) - 1)
    def _():
        hfin_ref[...] = h_sc[...]


def _pack_state(h):
    nj, nb, _ = h.shape
    h = jnp.pad(h, ((0, 0), (0, SSM_PACK - nb), (0, 0)))
    re = h[:, :, :SSM_SW].reshape(nj, SSM_PACK, SSM_SC, LANES)
    im = h[:, :, SSM_SW:].reshape(nj, SSM_PACK, SSM_SC, LANES)
    return jnp.concatenate([re, im], axis=1).transpose(0, 2, 1, 3)


def _unpack_state(hp, nb):
    nj = hp.shape[0]
    x = hp.transpose(0, 2, 1, 3).reshape(nj, SUBLANES, SSM_SW)
    return jnp.concatenate([x[:, :nb], x[:, SSM_PACK:SSM_PACK + nb]], axis=2)


def _ssm_prompt(u, sw, h0, nb, lc, rt):
    assert nb == SSM_PACK and lc % rt == 0
    tw = SSM_T * LANES
    u4 = u.reshape(SSM_NJ, nb, lc, tw)
    sq = 2 * SSM_SW
    kern = functools.partial(_ssm_prompt_kernel, nb=nb, rt=rt)
    st_spec = pl.BlockSpec((None, SSM_SC, SUBLANES, LANES), lambda j, t: (j, 0, 0, 0))
    s, hfin = pl.pallas_call(
        kern, grid=(SSM_NJ, lc // rt),
        in_specs=[pl.BlockSpec((None, nb, rt, tw), lambda j, t: (j, 0, t, 0)),
                  pl.BlockSpec((None, tw, tw), lambda j, t: (j, 0, 0)),
                  pl.BlockSpec((None, tw, sq), lambda j, t: (j, 0, 0)),
                  pl.BlockSpec((None, sq, tw), lambda j, t: (j, 0, 0)),
                  pl.BlockSpec((None, 1, sq), lambda j, t: (j, 0, 0)),
                  pl.BlockSpec((None, 1, tw), lambda j, t: (j, 0, 0)),
                  st_spec],
        out_specs=[pl.BlockSpec((None, nb, rt, tw), lambda j, t: (j, 0, t, 0)), st_spec],
        out_shape=[jax.ShapeDtypeStruct((SSM_NJ, nb, lc, tw), BF16),
                   jax.ShapeDtypeStruct((SSM_NJ, SSM_SC, SUBLANES, LANES), F32)],
        scratch_shapes=[pltpu.VMEM((SSM_SC, rt * SSM_PITCH, LANES), F32),
                        pltpu.VMEM((SSM_SC, rt * SSM_PITCH, LANES), F32),
                        pltpu.VMEM((SSM_SC, SUBLANES, LANES), F32),
                        pltpu.VMEM((nb, rt, tw), F32)],
        compiler_params=_cp(("arbitrary", "arbitrary")), name="ssm_prompt",
    )(u4, sw["M"], sw["P"], sw["Q"], sw["lam"], sw["D"], _pack_state(h0))
    return s.reshape(SSM_NJ, nb * lc, tw), _unpack_state(hfin, nb)


def _ssm_state_kernel(u_ref, p_ref, lam_ref, h_ref):
    s = _dot(u_ref[...].astype(BF16), p_ref[...])
    lam = lam_ref[...]
    lr, li = lam[:, :SSM_SW], lam[:, SSM_SW:]
    h = s[0:1, :]
    for r in range(1, s.shape[0]):
        hre, him = h[:, :SSM_SW], h[:, SSM_SW:]
        h = jnp.concatenate([lr * hre - li * him, lr * him + li * hre], axis=1) + s[r:r + 1, :]
    h_ref[...] = h


def _ssm_state(u, sw, lc):
    tw = SSM_T * LANES
    sq = 2 * SSM_SW
    blk = lambda j: (j, 0, 0)
    return pl.pallas_call(
        _ssm_state_kernel, grid=(SSM_NJ,),
        in_specs=[pl.BlockSpec((None, lc, tw), blk), pl.BlockSpec((None, tw, sq), blk),
                  pl.BlockSpec((None, 1, sq), blk)],
        out_specs=pl.BlockSpec((None, 1, sq), blk),
        out_shape=jax.ShapeDtypeStruct((SSM_NJ, 1, sq), F32),
        compiler_params=_cp(("arbitrary",)), name="ssm_state",
    )(u, sw["P"], sw["lam"])


def _ssm_sample_kernel(u_ref, m_ref, p_ref, q_ref, lam_ref, d_ref, h0_ref, s_ref, hfin_ref, s_sc,
                       *, nb, lc):
    u = u_ref[...]
    u16 = u.astype(BF16)
    s_sc[...] = _dot(u16, p_ref[...])
    lam = lam_ref[...]
    lr, li = lam[:, :SSM_SW], lam[:, SSM_SW:]
    h = h0_ref[...]
    for r in range(lc):
        rows = slice(r * nb, (r + 1) * nb)
        s = s_sc[rows, :]
        s_sc[rows, :] = h
        hre, him = h[:, :SSM_SW], h[:, SSM_SW:]
        h = jnp.concatenate([lr * hre - li * him + s[:, :SSM_SW], lr * him + li * hre + s[:, SSM_SW:]], axis=1)
    hfin_ref[...] = h
    y = _dot_causal(u16, m_ref) + _dot(s_sc[...].astype(BF16), q_ref[...]) + u * d_ref[...]
    s_ref[...] = jax.nn.gelu(y).astype(BF16)


def _ssm_sample(u, sw, h0, nb, lc):
    tw = SSM_T * LANES
    sq = 2 * SSM_SW
    u2 = u.reshape(SSM_NJ, nb, lc, tw).transpose(0, 2, 1, 3).reshape(SSM_NJ, lc * nb, tw)
    kern = functools.partial(_ssm_sample_kernel, nb=nb, lc=lc)
    blk = lambda j: (j, 0, 0)
    s, hfin = pl.pallas_call(
        kern, grid=(SSM_NJ,),
        in_specs=[pl.BlockSpec((None, lc * nb, tw), blk), pl.BlockSpec((None, tw, tw), blk),
                  pl.BlockSpec((None, tw, sq), blk), pl.BlockSpec((None, sq, tw), blk),
                  pl.BlockSpec((None, 1, sq), blk), pl.BlockSpec((None, 1, tw), blk),
                  pl.BlockSpec((None, nb, sq), blk)],
        out_specs=[pl.BlockSpec((None, lc * nb, tw), blk), pl.BlockSpec((None, nb, sq), blk)],
        out_shape=[jax.ShapeDtypeStruct((SSM_NJ, lc * nb, tw), BF16),
                   jax.ShapeDtypeStruct((SSM_NJ, nb, sq), F32)],
        scratch_shapes=[pltpu.VMEM((lc * nb, sq), F32)],
        compiler_params=_cp(("arbitrary",)), name="ssm_sample",
    )(u2, sw["M"], sw["P"], sw["Q"], sw["lam"], sw["D"], h0)
    s = s.reshape(SSM_NJ, lc, nb, tw).transpose(0, 2, 1, 3).reshape(SSM_NJ, nb * lc, tw)
    return s, hfin


def _ssm_fold_kernel(wre_ref, wim_ref, cre_ref, cim_ref, plr_ref, pli_ref, rep_ref, m_ref, p_ref, q_ref):
    def split(a, terms):
        out = []
        for _ in range(terms):
            t = a.astype(BF16)
            out.append(t)
            a = a - t.astype(F32)
        return out

    def repeat_lanes(a, rep):
        return sum(_dot(t, rep) for t in split(a, 3))

    def dot_nt_hi_lo(a, b_parts):
        a_hi, a_lo = split(a, 2)
        b_hi, b_lo = b_parts
        return _dot_nt(a_hi, b_hi) + (_dot_nt(a_hi, b_lo) + _dot_nt(a_lo, b_hi))

    same_group = (lax.broadcasted_iota(jnp.int32, (LANES, SSM_SW), 0) // SSM_GROUP
                  == lax.broadcasted_iota(jnp.int32, (LANES, SSM_SW), 1) // SSM_STATE)
    rep = rep_ref[...]
    wb_re = jnp.where(same_group, repeat_lanes(wre_ref[...], rep), 0.0)
    wb_im = jnp.where(same_group, repeat_lanes(wim_ref[...], rep), 0.0)
    cb_re = jnp.where(same_group, repeat_lanes(cre_ref[...], rep), 0.0)
    cb_im = jnp.where(same_group, repeat_lanes(cim_ref[...], rep), 0.0)
    cb_re_parts, cb_im_parts = split(cb_re, 2), split(cb_im, 2)

    m_ref[...] = jnp.zeros_like(m_ref)
    for k in range(SSM_T):
        lr, li = plr_ref[k:k + 1, :], pli_ref[k:k + 1, :]
        v_re = lr * wb_re - li * wb_im
        v_im = lr * wb_im + li * wb_re
        s = SSM_T - 1 - k
        p_ref[s * LANES:(s + 1) * LANES, :SSM_SW] = v_re.astype(BF16)
        p_ref[s * LANES:(s + 1) * LANES, SSM_SW:] = v_im.astype(BF16)
        bd = (dot_nt_hi_lo(v_re, cb_re_parts) - dot_nt_hi_lo(v_im, cb_im_parts)).astype(BF16)
        for s in range(SSM_T - k):
            t = s + k
            m_ref[s * LANES:(s + 1) * LANES, t * LANES:(t + 1) * LANES] = bd
        lr1, li1 = plr_ref[k + 1:k + 2, :], pli_ref[k + 1:k + 2, :]
        q_ref[:SSM_SW, k * LANES:(k + 1) * LANES] = (cb_re * lr1 - cb_im * li1).T.astype(BF16)
        q_ref[SSM_SW:, k * LANES:(k + 1) * LANES] = (-(cb_re * li1 + cb_im * lr1)).T.astype(BF16)


def _ssm_weights(a_re, a_im, log_dt, b_re, b_im, c_re, c_im, d):
    t_ = SSM_T
    dt = jnp.exp(log_dt)[:, None]
    mag = jnp.exp(a_re * dt)
    lam_re, lam_im = mag * jnp.cos(a_im * dt), mag * jnp.sin(a_im * dt)
    den = a_re * a_re + a_im * a_im
    f_re = ((lam_re - 1.0) * a_re + lam_im * a_im) / den
    f_im = (lam_im * a_re - (lam_re - 1.0) * a_im) / den
    w_re = f_re[:, :, None] * b_re - f_im[:, :, None] * b_im
    w_im = f_re[:, :, None] * b_im + f_im[:, :, None] * b_re
    k = jnp.arange(t_ + 1, dtype=F32)[None, :, None]
    pmag = jnp.exp(a_re[:, None, :] * dt[:, :, None] * k)
    pang = a_im[:, None, :] * dt[:, :, None] * k
    pw_re, pw_im = pmag * jnp.cos(pang), pmag * jnp.sin(pang)

    def lanes_gn(x):
        x = x.reshape(SSM_NJ, SSM_GPB, t_ + 1, SSM_STATE)
        return jnp.swapaxes(x, 1, 2).reshape(SSM_NJ, t_ + 1, SSM_SW)

    pl_re, pl_im = lanes_gn(pw_re), lanes_gn(pw_im)
    wcat = [jnp.swapaxes(x, 1, 2).reshape(SSM_NJ, LANES, SSM_STATE) for x in (w_re, w_im)]
    ccat = [x.reshape(SSM_NJ, LANES, SSM_STATE) for x in (c_re, c_im)]
    rep = jnp.tile(jnp.eye(SSM_STATE, dtype=BF16), (1, SSM_GPB))
    tw = t_ * LANES
    sq = 2 * SSM_SW
    blk = lambda j: (j, 0, 0)
    const = lambda j: (0, 0)
    per_j = lambda a: pl.BlockSpec((None,) + a.shape[1:], blk)
    ins = [wcat[0], wcat[1], ccat[0], ccat[1], pl_re, pl_im]
    m, p, q = pl.pallas_call(
        _ssm_fold_kernel, grid=(SSM_NJ,),
        in_specs=[per_j(a) for a in ins] + [pl.BlockSpec(rep.shape, const)],
        out_specs=[pl.BlockSpec((None, tw, tw), blk), pl.BlockSpec((None, tw, sq), blk),
                   pl.BlockSpec((None, sq, tw), blk)],
        out_shape=[jax.ShapeDtypeStruct((SSM_NJ, tw, tw), BF16), jax.ShapeDtypeStruct((SSM_NJ, tw, sq), BF16),
                   jax.ShapeDtypeStruct((SSM_NJ, sq, tw), BF16)],
        compiler_params=_cp(("arbitrary",)), name="ssm_fold",
    )(*ins, rep)
    lam_t = jnp.concatenate([pl_re[:, t_][:, None, :], pl_im[:, t_][:, None, :]], axis=2)
    dvec = jnp.tile(d.reshape(SSM_NJ, 1, LANES), (1, 1, t_))
    return {"M": m, "P": p, "Q": q, "lam": lam_t, "D": dvec}


def _mixer_kernel(o_ref, s_ref, ga_ref, gb_ref, x_ref, wo_ref, wv_ref, wg_ref, wout_ref, h_ref, s_sc):
    o_a = _dot(o_ref[...], wo_ref[...])
    nchunk = s_sc.shape[1] // SSM_T
    for j in range(SSM_NJ):
        sj = s_ref[j].astype(F32)
        for t in range(SSM_T):
            s_sc[j, pl.ds(t, nchunk, stride=SSM_T), :] = sj[:, t * LANES:(t + 1) * LANES]
    s = jnp.concatenate([s_sc[j] for j in range(SSM_NJ)], axis=1).astype(BF16)
    o_b = _dot(s, wv_ref[...]) * jax.nn.sigmoid(_dot(s, wg_ref[...]))
    merged = jax.nn.sigmoid(ga_ref[...]) * o_a + jax.nn.sigmoid(gb_ref[...]) * o_b
    h_ref[...] = x_ref[...] + _dot(merged.astype(BF16), wout_ref[...])


def _mixer(o, s, ga, gb, x, w, tm):
    n = x.shape[0]
    row = lambda i: (i, 0)
    const = lambda i: (0, 0)
    wspec = pl.BlockSpec((D_MODEL, D_MODEL), const, pipeline_mode=pl.Buffered(1))
    act = pl.BlockSpec((tm, D_MODEL), row)
    return pl.pallas_call(
        _mixer_kernel, grid=(n // tm,),
        in_specs=[act, pl.BlockSpec((SSM_NJ, tm // SSM_T, SSM_T * LANES), lambda i: (0, i, 0)), act, act, act,
                  wspec, wspec, wspec, wspec],
        out_specs=act, out_shape=jax.ShapeDtypeStruct((n, D_MODEL), F32),
        scratch_shapes=[pltpu.VMEM((SSM_NJ, tm, LANES), F32)],
        compiler_params=_cp(("arbitrary",)), name="mixer_out",
    )(o, s, ga, gb, x, w["wo"], w["wv"], w["wg"], w["wout"])


def _mlp_kernel(h_ref, g_ref, wup_ref, wdn_ref, y_ref):
    h = h_ref[...]
    hn = _rms(h, g_ref[...]).astype(BF16)
    a = jnp.maximum(_dot(hn, wup_ref[...]), 0.0)
    y_ref[...] = h + _dot((a * a).astype(BF16), wdn_ref[...])


def _mlp(h, w, tm):
    n = h.shape[0]
    row = lambda i: (i, 0)
    const = lambda i: (0, 0)
    act = pl.BlockSpec((tm, D_MODEL), row)
    return pl.pallas_call(
        _mlp_kernel, grid=(n // tm,),
        in_specs=[act, pl.BlockSpec((1, D_MODEL), const),
                  pl.BlockSpec((D_MODEL, D_FF), const, pipeline_mode=pl.Buffered(1)),
                  pl.BlockSpec((D_FF, D_MODEL), const, pipeline_mode=pl.Buffered(1))],
        out_specs=act, out_shape=jax.ShapeDtypeStruct((n, D_MODEL), F32),
        compiler_params=_cp(("arbitrary",)), name="mlp",
    )(h, w["gmlp"], w["wup"], w["wdn"])


def _cos_sin(pos):
    half = QK_ROPE // 2
    inv = ROPE_THETA ** (-jnp.arange(half, dtype=F32) / half)
    inv = jnp.concatenate([inv, inv, jnp.zeros((LANES - QK_ROPE,), F32)])
    ang = pos.astype(F32)[:, None] * inv[None, :]
    return jnp.cos(ang), jnp.sin(ang)


def _rope_tables(pos=None, length=None):
    if pos is not None:
        cos, sin = _cos_sin(pos)
    else:
        ch, sh = _cos_sin(jnp.arange(length // CHUNK, dtype=jnp.int32) * CHUNK)
        cl, sl = _cos_sin(jnp.arange(CHUNK, dtype=jnp.int32))
        cos = (ch[:, None, :] * cl[None] - sh[:, None, :] * sl[None]).reshape(length, LANES)
        sin = (sh[:, None, :] * cl[None] + ch[:, None, :] * sl[None]).reshape(length, LANES)
    lane = jnp.arange(LANES)
    half = QK_ROPE // 2
    ct = jnp.where(lane < QK_ROPE, cos, 0.0)
    s1 = jnp.where(lane < half, -sin, 0.0)
    s2 = jnp.where((lane >= half) & (lane < QK_ROPE), sin, 0.0)
    return ct, s1, s2


def _pad_lanes(a, width):
    return jnp.pad(a, [(0, 0)] * (a.ndim - 1) + [(0, width - a.shape[-1])])


def _layer_weights(norm_mix, w_in, q_lora_norm, w_uq, q_nope_norm, q_rope_norm, kv_lora_norm, k_rope_norm,
                   w_ukv, k_nope_norm, w_o_attn, w_glu_v, w_glu_g, w_out, norm_mlp, w_mlp_up, w_mlp_down):
    bf = lambda a: a.astype(BF16)
    r2 = lambda a: a.reshape(1, -1).astype(F32)
    return {
        "nmix": r2(norm_mix),
        "win": bf(jnp.swapaxes(w_in, 0, 1)),
        "gq": r2(q_lora_norm),
        "wuqn": bf(w_uq[:, :, :QK_NOPE].reshape(Q_LORA, N_HEADS * QK_NOPE)),
        "wuqr": bf(_pad_lanes(w_uq[:, :, QK_NOPE:], LANES).reshape(Q_LORA, N_HEADS * LANES)),
        "gqn": r2(q_nope_norm) * Q_SCALE, "gqr": _pad_lanes(r2(q_rope_norm), LANES) * Q_SCALE,
        "gkv": r2(kv_lora_norm), "gkr": _pad_lanes(r2(k_rope_norm), LANES),
        "wuk": bf(w_ukv[:, :, :QK_NOPE].reshape(KV_LORA, N_HEADS * QK_NOPE)),
        "wukg": bf((w_ukv[:, :, :QK_NOPE] * k_nope_norm.astype(F32)).reshape(KV_LORA, N_HEADS * QK_NOPE)),
        "wuv": bf(w_ukv[:, :, QK_NOPE:].reshape(KV_LORA, N_HEADS * V_DIM)),
        "gkn": r2(k_nope_norm),
        "wo": bf(w_o_attn.reshape(N_HEADS * V_DIM, D_MODEL)),
        "wv": bf(w_glu_v), "wg": bf(w_glu_g), "wout": bf(w_out),
        "gmlp": r2(norm_mlp), "wup": bf(w_mlp_up), "wdn": bf(w_mlp_down),
    }


def kernel(x_prompt, x_sample, cache_latent, cache_krope, cache_meta_latent, cache_meta_krope, state_ssm_re, state_ssm_im, meta_tokens, norm_mix, w_in, q_lora_norm, w_uq, q_nope_norm, q_rope_norm, kv_lora_norm, k_rope_norm, w_ukv, k_nope_norm, w_o_attn, ssm_a_re, ssm_a_im, ssm_log_dt, ssm_b_re, ssm_b_im, ssm_c_re, ssm_c_im, ssm_d, w_glu_v, w_glu_g, w_out, norm_mlp, w_mlp_up, w_mlp_down):
    bsz_p, seq_p = x_prompt.shape[0], x_prompt.shape[1]
    bsz_s, seq_s = x_sample.shape[0], x_sample.shape[1]
    past = cache_latent.shape[2]
    depth = w_in.shape[0]
    assert depth == 1, "single-layer step"
    l = 0
    w = _layer_weights(norm_mix[l], w_in[l], q_lora_norm[l], w_uq[l], q_nope_norm[l], q_rope_norm[l],
                       kv_lora_norm[l], k_rope_norm[l], w_ukv[l], k_nope_norm[l], w_o_attn[l], w_glu_v[l],
                       w_glu_g[l], w_out[l], norm_mlp[l], w_mlp_up[l], w_mlp_down[l])
    sw = _ssm_weights(ssm_a_re[l].astype(F32), ssm_a_im[l].astype(F32), ssm_log_dt[l].astype(F32),
                      ssm_b_re[l].astype(F32), ssm_b_im[l].astype(F32), ssm_c_re[l].astype(F32),
                      ssm_c_im[l].astype(F32), ssm_d[l].astype(F32))
    sq = 2 * SSM_SW

    tabs_m = _rope_tables(jnp.arange(N_META, dtype=jnp.int32) - N_META)
    (_, mk, mv, mckv, mkr, mu, _, _) = _front(meta_tokens.astype(F32), tabs_m, 1, w, N_META)
    h_meta = _ssm_state(mu, sw, N_META // SSM_T)

    n_p = bsz_p * seq_p
    xp = x_prompt.reshape(n_p, D_MODEL)
    tabs_p = _rope_tables(length=seq_p)
    q, k, v, ckv_p, kr_p, u_p, ga, gb = _front(xp, tabs_p, seq_p // TM_PROMPT, w, TM_PROMPT, kr_seq=seq_p)
    o_p = _attn_prompt(q, k, v, mk, mv, bsz_p, seq_p)
    h0_p = jnp.broadcast_to(h_meta, (SSM_NJ, bsz_p, sq))
    s_p, hfin_p = _ssm_prompt(u_p, sw, h0_p, bsz_p, seq_p // SSM_T, SSM_RT)
    h1_p = _mixer(o_p, s_p, ga, gb, xp, w, TM_PROMPT)
    y_p = _mlp(h1_p, w, TM_PROMPT)

    n_s = bsz_s * seq_s
    xs = x_sample.reshape(n_s, D_MODEL)
    pos_s = past + jnp.arange(seq_s, dtype=jnp.int32)
    tabs_s = tuple(jnp.tile(t, (bsz_s, 1)) for t in _rope_tables(pos_s))
    q_s, k_s, _, ckv_s, kr_s, u_s, ga_s, gb_s = _front(xs, tabs_s, n_s // TM_SAMPLE, w, TM_SAMPLE)
    o_s = _attn_sample(q_s, k_s, ckv_s, cache_latent[l].astype(F32), cache_krope[l].astype(F32),
                       cache_meta_latent[l].astype(F32), cache_meta_krope[l].astype(F32), w, bsz_s, seq_s, past)

    def to_blocks(st):
        return st.astype(F32).reshape(bsz_s, SSM_NJ, SSM_SW).transpose(1, 0, 2)

    h0_s = jnp.concatenate([to_blocks(state_ssm_re[l]), to_blocks(state_ssm_im[l])], axis=2)
    s_s, hfin_s = _ssm_sample(u_s, sw, h0_s, bsz_s, seq_s // SSM_T)
    h1_s = _mixer(o_s, s_s, ga_s, gb_s, xs, w, TM_SAMPLE)
    y_s = _mlp(h1_s, w, TM_SAMPLE)

    def from_blocks(hf, nb):
        re = hf[:, :, :SSM_SW].transpose(1, 0, 2).reshape(1, nb, N_GROUPS, SSM_STATE)
        im = hf[:, :, SSM_SW:].transpose(1, 0, 2).reshape(1, nb, N_GROUPS, SSM_STATE)
        return re, im

    sre_p, sim_p = from_blocks(hfin_p, bsz_p)
    sre_s, sim_s = from_blocks(hfin_s, bsz_s)
    return (y_p.reshape(bsz_p, seq_p, D_MODEL), y_s.reshape(bsz_s, seq_s, D_MODEL),
            ckv_p.reshape(1, bsz_p, seq_p, KV_LORA), jnp.swapaxes(kr_p, 1, 2)[None],
            jnp.broadcast_to(mckv[None, None], (1, bsz_p, N_META, KV_LORA)),
            jnp.broadcast_to(mkr[None, None], (1, bsz_p, N_META, QK_ROPE)),
            sre_p, sim_p,
            ckv_s.reshape(1, bsz_s, seq_s, KV_LORA), kr_s.reshape(1, bsz_s, seq_s, QK_ROPE),
            sre_s, sim_s)
```

```python
import functools
import math

import jax
import jax.numpy as jnp
from jax import lax
from jax.experimental import pallas as pl
from jax.experimental.pallas import tpu as pltpu

F32 = jnp.float32
BF16 = jnp.bfloat16

D_MODEL = 1024
CHUNK = 64
N_META = 16
N_HEADS = 8
QK_NOPE = 128
QK_ROPE = 64
V_DIM = 128
QK_DIM = QK_NOPE + QK_ROPE
Q_LORA = 384
KV_LORA = 256
SSM_GROUP = 16
N_GROUPS = D_MODEL // SSM_GROUP
SSM_STATE = 64
D_FF = 4 * D_MODEL
ROPE_THETA = 10000.0
EPS = 1e-6
ATTN_SCALE = QK_DIM ** -0.5
Q_SCALE = ATTN_SCALE * math.log2(math.e)
NEG_INF = -1e30

LANES = 128
HEAD_PAD = 2 * LANES
SSM_T = 8
SSM_NJ = D_MODEL // LANES
SSM_GPB = LANES // SSM_GROUP
SSM_SW = SSM_GPB * SSM_STATE
VMEM_LIMIT = 56 * 1024 * 1024
TM_PROMPT = 512
TM_SAMPLE = 512
SSM_RT = 256


def _cp(sem):
    return pltpu.CompilerParams(dimension_semantics=sem, vmem_limit_bytes=VMEM_LIMIT)


def _dot(a, b):
    return jnp.dot(a, b, preferred_element_type=F32)


def _dot_nt(a, b):
    return lax.dot_general(a, b, (((1,), (1,)), ((), ())), preferred_element_type=F32)


def _rms(x, g, n=None):
    n = x.shape[-1] if n is None else n
    ms = jnp.sum(x * x, axis=-1, keepdims=True) * (1.0 / n)
    return x * lax.rsqrt(ms + EPS) * g


def _rope128(b, ct, s1, s2):
    return b * ct + pltpu.roll(b, LANES - QK_ROPE // 2, 1) * s1 + pltpu.roll(b, QK_ROPE // 2, 1) * s2


IN_Q = 0
IN_KV = IN_Q + Q_LORA
IN_KR = IN_KV + KV_LORA
IN_U = IN_KR + QK_ROPE
IN_GA = IN_U + D_MODEL
IN_GB = IN_GA + D_MODEL
IN_END = IN_GB + D_MODEL


def _front_kernel(x_ref, ct_ref, s1_ref, s2_ref, nmix_ref, win_ref, gq_ref, wuqn_ref, wuqr_ref, gqn_ref,
                  gqr_ref, gkv_ref, gkr_ref, wuk_ref, wuv_ref, gkn_ref,
                  q_ref, k_ref, v_ref, ckv_ref, kr_ref, u_ref, ga_ref, gb_ref, u_sc, *, kr_transposed):
    x = x_ref[...]
    xn = _rms(x, nmix_ref[...]).astype(BF16)
    ct, s1, s2 = ct_ref[...], s1_ref[...], s2_ref[...]


    q_lat = _dot_nt(xn, win_ref[IN_Q:IN_KV, :])
    kv_lat = _dot_nt(xn, win_ref[IN_KV:IN_KR, :])
    kr_raw = _dot_nt(xn, win_ref[IN_KR:IN_KR + LANES, :])
    kr_raw = jnp.where(lax.broadcasted_iota(jnp.int32, kr_raw.shape, 1) < QK_ROPE, kr_raw, 0.0)

    c_q = _rms(q_lat, gq_ref[...]).astype(BF16)
    qn = _dot(c_q, wuqn_ref[...])
    qr = _dot(c_q, wuqr_ref[...])
    c_kv = _rms(kv_lat, gkv_ref[...])
    ckv_ref[...] = c_kv
    c16 = c_kv.astype(BF16)
    kn = _dot(c16, wuk_ref[...])
    v_ref[...] = _dot(c16, wuv_ref[...]).astype(BF16)

    for h in range(N_HEADS):
        sl = slice(h * LANES, (h + 1) * LANES)
        q_ref[:, h * HEAD_PAD:h * HEAD_PAD + LANES] = _rms(qn[:, sl], gqn_ref[...]).astype(BF16)
        r = _rope128(_rms(qr[:, sl], gqr_ref[...], QK_ROPE), ct, s1, s2)
        q_ref[:, h * HEAD_PAD + LANES:(h + 1) * HEAD_PAD] = r.astype(BF16)
    kr = _rms(kr_raw, gkr_ref[...], QK_ROPE)
    kr = _rope128(kr, ct, s1, s2)
    kr_ref[...] = kr.T[:QK_ROPE, :] if kr_transposed else kr[:, :QK_ROPE]
    kr16 = kr.astype(BF16)
    for h in range(N_HEADS):
        sl = slice(h * LANES, (h + 1) * LANES)
        k_ref[:, h * HEAD_PAD:h * HEAD_PAD + LANES] = _rms(kn[:, sl], gkn_ref[...]).astype(BF16)
        k_ref[:, h * HEAD_PAD + LANES:(h + 1) * HEAD_PAD] = kr16

    ga_ref[...] = _dot_nt(xn, win_ref[IN_GA:IN_GB, :])
    u = _dot_nt(xn, win_ref[IN_U:IN_GA, :])
    nchunk = u.shape[0] // SSM_T
    for j in range(SSM_NJ):
        for r in range(nchunk):
            u_sc[j, r * SSM_PITCH:r * SSM_PITCH + SSM_T, :] = u[r * SSM_T:(r + 1) * SSM_T, j * LANES:(j + 1) * LANES]
        for t in range(SSM_T):
            u_ref[j, :, t * LANES:(t + 1) * LANES] = u_sc[j, pl.ds(t, nchunk, stride=SSM_PITCH), :]
    gb_ref[...] = _dot_nt(xn, win_ref[IN_GB:IN_END, :])


def _front(x, tabs, tab_blocks, w, tm, kr_seq=None):
    n = x.shape[0]
    nt = n // tm
    ct, s1, s2 = tabs

    def row(i):
        return (i, 0)

    def tab(i):
        return (i % tab_blocks, 0)

    def const(i):
        return (0, 0)

    def full(a):
        return pl.BlockSpec(a.shape, const, pipeline_mode=pl.Buffered(1))

    weights = [w["nmix"], w["win"], w["gq"], w["wuqn"], w["wuqr"], w["gqn"], w["gqr"], w["gkv"], w["gkr"],
               w["wuk"], w["wuv"], w["gkn"]]
    wide = N_HEADS * LANES
    qk_wide = N_HEADS * HEAD_PAD
    out_shape = [
        jax.ShapeDtypeStruct((n, qk_wide), BF16),
        jax.ShapeDtypeStruct((n, qk_wide), BF16),
        jax.ShapeDtypeStruct((n, wide), BF16),
        jax.ShapeDtypeStruct((n, KV_LORA), F32),
        (jax.ShapeDtypeStruct((n, QK_ROPE), F32) if kr_seq is None
         else jax.ShapeDtypeStruct((n // kr_seq, QK_ROPE, kr_seq), F32)),
        jax.ShapeDtypeStruct((SSM_NJ, n // SSM_T, SSM_T * LANES), F32),
        jax.ShapeDtypeStruct((n, D_MODEL), F32),
        jax.ShapeDtypeStruct((n, D_MODEL), F32),
    ]
    out_specs = [
        pl.BlockSpec((tm, qk_wide), row), pl.BlockSpec((tm, qk_wide), row),
        pl.BlockSpec((tm, wide), row), pl.BlockSpec((tm, KV_LORA), row),
        (pl.BlockSpec((tm, QK_ROPE), row) if kr_seq is None else
         pl.BlockSpec((None, QK_ROPE, tm), lambda i: (i // (kr_seq // tm), 0, i % (kr_seq // tm)))),
        pl.BlockSpec((SSM_NJ, tm // SSM_T, SSM_T * LANES), lambda i: (0, i, 0)),
        pl.BlockSpec((tm, D_MODEL), row), pl.BlockSpec((tm, D_MODEL), row),
    ]
    in_specs = ([pl.BlockSpec((tm, D_MODEL), row)] + [pl.BlockSpec((tm, LANES), tab)] * 3
                + [full(a) for a in weights])
    return pl.pallas_call(
        functools.partial(_front_kernel, kr_transposed=kr_seq is not None), grid=(nt,), in_specs=in_specs,
        out_specs=out_specs, out_shape=out_shape,
        scratch_shapes=[pltpu.VMEM((SSM_NJ, tm // SSM_T * SSM_PITCH, LANES), F32)],
        compiler_params=_cp(("arbitrary",)), name="front",
    )(x, ct, s1, s2, *weights)


ATT_T = 512


def _with_ones(v):
    return jnp.concatenate([v, jnp.ones_like(v)], axis=1)


def _softmax_step(s, v, m_ref, acc_ref):
    m_prev = m_ref[...]
    m_new = jnp.maximum(m_prev, jnp.max(s, axis=-1, keepdims=True))
    alpha = jnp.exp2(m_prev - m_new)
    p = jnp.exp2(s - jnp.tile(m_new, (1, s.shape[1] // LANES)))
    acc_ref[...] = jnp.tile(alpha, (1, 2)) * acc_ref[...] + _dot(p.astype(BF16), _with_ones(v))
    m_ref[...] = m_new


ATT_HG = 4


def _attn_prompt_kernel(q_ref, k_ref, v_ref, mk_ref, mv_ref, o_ref, a_sc, b_sc, m_ref, acc_ref):
    step = pl.program_id(2)
    npairs = step
    heads = range(ATT_HG)

    def key_rows(kt):
        return pl.ds(pl.multiple_of(kt * ATT_T, ATT_T), ATT_T)

    def q_rows(slot):
        return slice(slot * ATT_T, (slot + 1) * ATT_T)

    def qk_lanes(g):
        return slice(g * HEAD_PAD, (g + 1) * HEAD_PAD)

    def v_lanes(g):
        return slice(g * V_DIM, (g + 1) * V_DIM)

    def scores(slot, g, kt):
        return _dot_nt(q_ref[q_rows(slot), qk_lanes(g)], k_ref[key_rows(kt), qk_lanes(g)])

    def consume(slot, g, s, kt, diagonal):
        if diagonal:
            qc = lax.broadcasted_iota(jnp.int32, (ATT_T, ATT_T), 0) // CHUNK
            kc = lax.broadcasted_iota(jnp.int32, (ATT_T, ATT_T), 1) // CHUNK
            s = jnp.where(kc <= qc, s, NEG_INF)
        _softmax_step(s, v_ref[key_rows(kt), v_lanes(g)], m_ref.at[slot, g], acc_ref.at[slot, g])

    def head(slot):
        for g in heads:
            a_sc[slot, g] = scores(slot, g, 0)
        for g in heads:
            s = _dot_nt(q_ref[q_rows(slot), qk_lanes(g)], mk_ref[:, qk_lanes(g)])
            m0 = jnp.max(s, axis=-1, keepdims=True)
            p = jnp.exp2(s - m0)
            m_ref[slot, g] = jnp.broadcast_to(m0, (ATT_T, LANES))
            acc_ref[slot, g] = _dot(p.astype(BF16), _with_ones(mv_ref[:, v_lanes(g)]))

    def full_pairs(slot):
        def pair(pi, carry):
            kt = 2 * pi
            for g in heads:
                b_sc[g] = scores(slot, g, kt + 1)
                consume(slot, g, a_sc[slot, g], kt, False)
            for g in heads:
                a_sc[slot, g] = scores(slot, g, kt + 2)
                consume(slot, g, b_sc[g], kt + 1, False)
            return carry

        lax.fori_loop(0, npairs, pair, 0)

    def tail(slot):
        i = 2 * step + slot
        if slot == 1:
            for g in heads:
                b_sc[g] = scores(slot, g, i)
                consume(slot, g, a_sc[slot, g], i - 1, False)
            for g in heads:
                consume(slot, g, b_sc[g], i, True)
        else:
            for g in heads:
                consume(slot, g, a_sc[slot, g], i, True)

    def finish(slot):
        for g in heads:
            o_ref[q_rows(slot), v_lanes(g)] = (acc_ref[slot, g, :, :V_DIM]
                                               / acc_ref[slot, g, :, V_DIM:]).astype(BF16)

    head(0)
    head(1)
    full_pairs(0)
    tail(0)
    finish(0)
    full_pairs(1)
    tail(1)
    finish(1)


def _attn_prompt(q, k, v, mk, mv, bsz, seq):
    nstep = seq // (2 * ATT_T)
    n = bsz * seq
    qmap = lambda b, h, i: (b * nstep + i, h)
    return pl.pallas_call(
        _attn_prompt_kernel, grid=(bsz, N_HEADS // ATT_HG, nstep),
        in_specs=[pl.BlockSpec((2 * ATT_T, ATT_HG * HEAD_PAD), qmap),
                  pl.BlockSpec((seq, ATT_HG * HEAD_PAD), lambda b, h, i: (b, h)),
                  pl.BlockSpec((seq, ATT_HG * V_DIM), lambda b, h, i: (b, h)),
                  pl.BlockSpec((N_META, ATT_HG * HEAD_PAD), lambda b, h, i: (0, h)),
                  pl.BlockSpec((N_META, ATT_HG * V_DIM), lambda b, h, i: (0, h))],
        out_specs=pl.BlockSpec((2 * ATT_T, ATT_HG * V_DIM), qmap),
        out_shape=jax.ShapeDtypeStruct((n, N_HEADS * V_DIM), BF16),
        scratch_shapes=[pltpu.VMEM((2, ATT_HG, ATT_T, ATT_T), F32), pltpu.VMEM((ATT_HG, ATT_T, ATT_T), F32),
                        pltpu.VMEM((2, ATT_HG, ATT_T, LANES), F32),
                        pltpu.VMEM((2, ATT_HG, ATT_T, 2 * V_DIM), F32)],
        compiler_params=_cp(("arbitrary", "arbitrary", "arbitrary")), name="attn_prompt",
    )(q, k, v, mk, mv)


SAMPLE_KEY_CHUNK = 256


def _attn_sample_kernel(q_ref, cm_ref, cn_ref, cc_ref, krm_ref, krn_ref, krc_ref, wuk_ref, wukg_ref, wuv_ref,
                        hsum_ref, o_ref, c_sc, kr_sc, st_sc, pt_sc, *, seq, past):
    n_small = N_META + seq
    n_keys = n_small + past

    c_sc[:, KV_LORA:] = jnp.ones((n_keys, LANES), BF16)
    c_sc[0:N_META, :KV_LORA] = cm_ref[...].astype(BF16)
    c_sc[N_META:n_small, :KV_LORA] = cn_ref[...].astype(BF16)
    c_sc[n_small:, :KV_LORA] = cc_ref[...].astype(BF16)
    kr_sc[:, QK_ROPE:] = jnp.zeros((n_small, LANES - QK_ROPE), BF16)
    kr_sc[0:N_META, :QK_ROPE] = krm_ref[...].astype(BF16)
    kr_sc[N_META:n_small, :] = krn_ref[...]

    qa, qr = [], []
    for h in range(N_HEADS):
        qn_h = q_ref[:, h * HEAD_PAD:h * HEAD_PAD + LANES]
        qa.append(_dot_nt(qn_h, wukg_ref[:, h * LANES:(h + 1) * LANES]))
        qr.append(q_ref[:, h * HEAD_PAD + LANES:(h + 1) * HEAD_PAD])
    qa = jnp.concatenate(qa, axis=0).astype(BF16)
    qr = jnp.concatenate(qr, axis=0)

    bounds = [(0, n_small)] + [(n_small + i, n_small + i + SAMPLE_KEY_CHUNK)
                               for i in range(0, past, SAMPLE_KEY_CHUNK)]
    m = None
    for lo, hi in bounds:
        c16 = c_sc[lo:hi, :KV_LORA]
        kn = _dot(c16, wuk_ref[...])
        ssq = _dot((kn * kn).astype(BF16), hsum_ref[...])
        r = lax.rsqrt(ssq * (1.0 / QK_NOPE) + EPS)
        if lo == 0:
            s_rope = _dot_nt(kr_sc[...], qr)
        else:
            kr_t = krc_ref[:, lo - n_small:hi - n_small].astype(BF16)
            s_rope = lax.dot_general(kr_t, qr[:, :QK_ROPE], (((0,), (1,)), ((), ())),
                                     preferred_element_type=F32)
        st = _dot_nt(c16, qa) * r + s_rope
        st_sc[lo:hi, :] = st
        cm = jnp.max(st, axis=0, keepdims=True)
        m = cm if m is None else jnp.maximum(m, cm)

    for lo, hi in bounds:
        pt_sc[lo:hi, :] = jnp.exp2(st_sc[lo:hi, :] - m).astype(BF16)
    pc = lax.dot_general(pt_sc[...], c_sc[...], (((0,), (0,)), ((), ())), preferred_element_type=F32)
    pcn = (pc[:, :KV_LORA] / jnp.tile(pc[:, KV_LORA:], (1, KV_LORA // LANES))).astype(BF16)
    for h in range(N_HEADS):
        o_ref[:, h * V_DIM:(h + 1) * V_DIM] = _dot(pcn[h * seq:(h + 1) * seq, :],
                                                   wuv_ref[:, h * V_DIM:(h + 1) * V_DIM]).astype(BF16)


def _attn_sample(q, k_new, c_new, cache_c, cache_kr, meta_c, meta_kr, w, bsz, seq, past):
    n_keys = N_META + seq + past
    hq = N_HEADS * seq
    hsum = jnp.repeat(jnp.repeat(jnp.eye(N_HEADS, dtype=BF16), QK_NOPE, axis=0), seq, axis=1)
    row = lambda b: (b, 0)
    b3 = lambda b: (b, 0, 0)
    const = lambda b: (0, 0)
    kern = functools.partial(_attn_sample_kernel, seq=seq, past=past)
    return pl.pallas_call(
        kern, grid=(bsz,),
        in_specs=[pl.BlockSpec((seq, N_HEADS * HEAD_PAD), row),
                  pl.BlockSpec((None, N_META, KV_LORA), b3), pl.BlockSpec((seq, KV_LORA), row),
                  pl.BlockSpec((None, past, KV_LORA), b3),
                  pl.BlockSpec((None, N_META, QK_ROPE), b3), pl.BlockSpec((seq, LANES), lambda b: (b, 1)),
                  pl.BlockSpec((None, QK_ROPE, past), b3),
                  pl.BlockSpec(w["wuk"].shape, const), pl.BlockSpec(w["wukg"].shape, const),
                  pl.BlockSpec(w["wuv"].shape, const), pl.BlockSpec(hsum.shape, const)],
        out_specs=pl.BlockSpec((seq, N_HEADS * V_DIM), row),
        out_shape=jax.ShapeDtypeStruct((bsz * seq, N_HEADS * V_DIM), BF16),
        scratch_shapes=[pltpu.VMEM((n_keys, KV_LORA + LANES), BF16), pltpu.VMEM((N_META + seq, LANES), BF16),
                        pltpu.VMEM((n_keys, hq), F32), pltpu.VMEM((n_keys, hq), BF16)],
        compiler_params=_cp(("arbitrary",)), name="attn_sample",
    )(q, meta_c, c_new, cache_c, meta_kr, k_new, jnp.swapaxes(cache_kr, 1, 2), w["wuk"], w["wukg"], w["wuv"],
      hsum)


MXU_DIM = 256


def _dot_causal(u16, m_ref):
    nblk = m_ref.shape[0] // MXU_DIM
    return jnp.concatenate(
        [_dot(u16[:, :(cb + 1) * MXU_DIM], m_ref[:(cb + 1) * MXU_DIM, cb * MXU_DIM:(cb + 1) * MXU_DIM])
         for cb in range(nblk)], axis=1)


SSM_PACK = 4
SSM_SC = SSM_SW // LANES
SUBLANES = 2 * SSM_PACK
SSM_PITCH = 12


def _ssm_prompt_kernel(u_ref, m_ref, p_ref, q_ref, lam_ref, d_ref, h0_ref, s_ref, hfin_ref, s_sc, hp_sc, h_sc, y_sc,
                       *, nb, rt):
    t = pl.program_id(1)

    @pl.when(t == 0)
    def _():
        h_sc[...] = h0_ref[...]

    for b in range(nb):
        sb = _dot(u_ref[b].astype(BF16), p_ref[...])
        for c in range(SSM_SC):
            s_sc[c, pl.ds(b, rt, stride=SSM_PITCH), :] = sb[:, c * LANES:(c + 1) * LANES]
            s_sc[c, pl.ds(SSM_PACK + b, rt, stride=SSM_PITCH), :] = sb[:, SSM_SW + c * LANES:SSM_SW + (c + 1) * LANES]
    for b in range(nb):
        ub = u_ref[b]
        y_sc[b] = _dot_causal(ub.astype(BF16), m_ref) + ub * d_ref[...]

    lam = lam_ref[...]
    im_rows = lax.broadcasted_iota(jnp.int32, (SUBLANES, LANES), 0) >= SSM_PACK
    coef = []
    for c in range(SSM_SC):
        lr = jnp.broadcast_to(lam[:, c * LANES:(c + 1) * LANES], (SUBLANES, LANES))
        li = jnp.broadcast_to(lam[:, SSM_SW + c * LANES:SSM_SW + (c + 1) * LANES], (SUBLANES, LANES))
        coef.append((lr, jnp.where(im_rows, li, -li)))

    hs = [h_sc[c] for c in range(SSM_SC)]
    for r in range(rt):
        rows = slice(r * SSM_PITCH, r * SSM_PITCH + SUBLANES)
        for c in range(SSM_SC):
            hp_sc[c, rows, :] = hs[c]
            a, bb = coef[c]
            hs[c] = a * hs[c] + bb * pltpu.roll(hs[c], SSM_PACK, 0) + s_sc[c, rows, :]
    for c in range(SSM_SC):
        h_sc[c] = hs[c]

    for b in range(nb):
        hb = jnp.concatenate([hp_sc[c, pl.ds(b, rt, stride=SSM_PITCH), :] for c in range(SSM_SC)]
                             + [hp_sc[c, pl.ds(SSM_PACK + b, rt, stride=SSM_PITCH), :] for c in range(SSM_SC)],
                             axis=1)
        y = y_sc[b] + _dot(hb.astype(BF16), q_ref[...])
        s_ref[b] = jax.nn.gelu(y).astype(BF16)

    @pl.when(t == pl.num_programs(1) - 1)
    def _():
        hfin_ref[...] = h_sc[...]


def _pack_state(h):
    nj, nb, _ = h.shape
    h = jnp.pad(h, ((0, 0), (0, SSM_PACK - nb), (0, 0)))
    re = h[:, :, :SSM_SW].reshape(nj, SSM_PACK, SSM_SC, LANES)
    im = h[:, :, SSM_SW:].reshape(nj, SSM_PACK, SSM_SC, LANES)
    return jnp.concatenate([re, im], axis=1).transpose(0, 2, 1, 3)


def _unpack_state(hp, nb):
    nj = hp.shape[0]
    x = hp.transpose(0, 2, 1, 3).reshape(nj, SUBLANES, SSM_SW)
    return jnp.concatenate([x[:, :nb], x[:, SSM_PACK:SSM_PACK + nb]], axis=2)


def _ssm_prompt(u, sw, h0, nb, lc, rt):
    assert nb == SSM_PACK and lc % rt == 0
    tw = SSM_T * LANES
    u4 = u.reshape(SSM_NJ, nb, lc, tw)
    sq = 2 * SSM_SW
    kern = functools.partial(_ssm_prompt_kernel, nb=nb, rt=rt)
    st_spec = pl.BlockSpec((None, SSM_SC, SUBLANES, LANES), lambda j, t: (j, 0, 0, 0))
    s, hfin = pl.pallas_call(
        kern, grid=(SSM_NJ, lc // rt),
        in_specs=[pl.BlockSpec((None, nb, rt, tw), lambda j, t: (j, 0, t, 0)),
                  pl.BlockSpec((None, tw, tw), lambda j, t: (j, 0, 0)),
                  pl.BlockSpec((None, tw, sq), lambda j, t: (j, 0, 0)),
                  pl.BlockSpec((None, sq, tw), lambda j, t: (j, 0, 0)),
                  pl.BlockSpec((None, 1, sq), lambda j, t: (j, 0, 0)),
                  pl.BlockSpec((None, 1, tw), lambda j, t: (j, 0, 0)),
                  st_spec],
        out_specs=[pl.BlockSpec((None, nb, rt, tw), lambda j, t: (j, 0, t, 0)), st_spec],
        out_shape=[jax.ShapeDtypeStruct((SSM_NJ, nb, lc, tw), BF16),
                   jax.ShapeDtypeStruct((SSM_NJ, SSM_SC, SUBLANES, LANES), F32)],
        scratch_shapes=[pltpu.VMEM((SSM_SC, rt * SSM_PITCH, LANES), F32),
                        pltpu.VMEM((SSM_SC, rt * SSM_PITCH, LANES), F32),
                        pltpu.VMEM((SSM_SC, SUBLANES, LANES), F32),
                        pltpu.VMEM((nb, rt, tw), F32)],
        compiler_params=_cp(("arbitrary", "arbitrary")), name="ssm_prompt",
    )(u4, sw["M"], sw["P"], sw["Q"], sw["lam"], sw["D"], _pack_state(h0))
    return s.reshape(SSM_NJ, nb * lc, tw), _unpack_state(hfin, nb)


def _ssm_state_kernel(u_ref, p_ref, lam_ref, h_ref):
    s = _dot(u_ref[...].astype(BF16), p_ref[...])
    lam = lam_ref[...]
    lr, li = lam[:, :SSM_SW], lam[:, SSM_SW:]
    h = s[0:1, :]
    for r in range(1, s.shape[0]):
        hre, him = h[:, :SSM_SW], h[:, SSM_SW:]
        h = jnp.concatenate([lr * hre - li * him, lr * him + li * hre], axis=1) + s[r:r + 1, :]
    h_ref[...] = h


def _ssm_state(u, sw, lc):
    tw = SSM_T * LANES
    sq = 2 * SSM_SW
    blk = lambda j: (j, 0, 0)
    return pl.pallas_call(
        _ssm_state_kernel, grid=(SSM_NJ,),
        in_specs=[pl.BlockSpec((None, lc, tw), blk), pl.BlockSpec((None, tw, sq), blk),
                  pl.BlockSpec((None, 1, sq), blk)],
        out_specs=pl.BlockSpec((None, 1, sq), blk),
        out_shape=jax.ShapeDtypeStruct((SSM_NJ, 1, sq), F32),
        compiler_params=_cp(("arbitrary",)), name="ssm_state",
    )(u, sw["P"], sw["lam"])


def _ssm_sample_kernel(u_ref, m_ref, p_ref, q_ref, lam_ref, d_ref, h0_ref, s_ref, hfin_ref, s_sc,
                       *, nb, lc):
    u = u_ref[...]
    u16 = u.astype(BF16)
    s_sc[...] = _dot(u16, p_ref[...])
    lam = lam_ref[...]
    lr, li = lam[:, :SSM_SW], lam[:, SSM_SW:]
    h = h0_ref[...]
    for r in range(lc):
        rows = slice(r * nb, (r + 1) * nb)
        s = s_sc[rows, :]
        s_sc[rows, :] = h
        hre, him = h[:, :SSM_SW], h[:, SSM_SW:]
        h = jnp.concatenate([lr * hre - li * him + s[:, :SSM_SW], lr * him + li * hre + s[:, SSM_SW:]], axis=1)
    hfin_ref[...] = h
    y = _dot_causal(u16, m_ref) + _dot(s_sc[...].astype(BF16), q_ref[...]) + u * d_ref[...]
    s_ref[...] = jax.nn.gelu(y).astype(BF16)


def _ssm_sample(u, sw, h0, nb, lc):
    tw = SSM_T * LANES
    sq = 2 * SSM_SW
    u2 = u.reshape(SSM_NJ, nb, lc, tw).transpose(0, 2, 1, 3).reshape(SSM_NJ, lc * nb, tw)
    kern = functools.partial(_ssm_sample_kernel, nb=nb, lc=lc)
    blk = lambda j: (j, 0, 0)
    s, hfin = pl.pallas_call(
        kern, grid=(SSM_NJ,),
        in_specs=[pl.BlockSpec((None, lc * nb, tw), blk), pl.BlockSpec((None, tw, tw), blk),
                  pl.BlockSpec((None, tw, sq), blk), pl.BlockSpec((None, sq, tw), blk),
                  pl.BlockSpec((None, 1, sq), blk), pl.BlockSpec((None, 1, tw), blk),
                  pl.BlockSpec((None, nb, sq), blk)],
        out_specs=[pl.BlockSpec((None, lc * nb, tw), blk), pl.BlockSpec((None, nb, sq), blk)],
        out_shape=[jax.ShapeDtypeStruct((SSM_NJ, lc * nb, tw), BF16),
                   jax.ShapeDtypeStruct((SSM_NJ, nb, sq), F32)],
        scratch_shapes=[pltpu.VMEM((lc * nb, sq), F32)],
        compiler_params=_cp(("arbitrary",)), name="ssm_sample",
    )(u2, sw["M"], sw["P"], sw["Q"], sw["lam"], sw["D"], h0)
    s = s.reshape(SSM_NJ, lc, nb, tw).transpose(0, 2, 1, 3).reshape(SSM_NJ, nb * lc, tw)
    return s, hfin


def _ssm_fold_kernel(wre_ref, wim_ref, cre_ref, cim_ref, plr_ref, pli_ref, rep_ref, m_ref, p_ref, q_ref):
    def split(a, terms):
        out = []
        for _ in range(terms):
            t = a.astype(BF16)
            out.append(t)
            a = a - t.astype(F32)
        return out

    def repeat_lanes(a, rep):
        return sum(_dot(t, rep) for t in split(a, 3))

    def dot_nt_hi_lo(a, b_parts):
        a_hi, a_lo = split(a, 2)
        b_hi, b_lo = b_parts
        return _dot_nt(a_hi, b_hi) + (_dot_nt(a_hi, b_lo) + _dot_nt(a_lo, b_hi))

    same_group = (lax.broadcasted_iota(jnp.int32, (LANES, SSM_SW), 0) // SSM_GROUP
                  == lax.broadcasted_iota(jnp.int32, (LANES, SSM_SW), 1) // SSM_STATE)
    rep = rep_ref[...]
    wb_re = jnp.where(same_group, repeat_lanes(wre_ref[...], rep), 0.0)
    wb_im = jnp.where(same_group, repeat_lanes(wim_ref[...], rep), 0.0)
    cb_re = jnp.where(same_group, repeat_lanes(cre_ref[...], rep), 0.0)
    cb_im = jnp.where(same_group, repeat_lanes(cim_ref[...], rep), 0.0)
    cb_re_parts, cb_im_parts = split(cb_re, 2), split(cb_im, 2)

    m_ref[...] = jnp.zeros_like(m_ref)
    for k in range(SSM_T):
        lr, li = plr_ref[k:k + 1, :], pli_ref[k:k + 1, :]
        v_re = lr * wb_re - li * wb_im
        v_im = lr * wb_im + li * wb_re
        s = SSM_T - 1 - k
        p_ref[s * LANES:(s + 1) * LANES, :SSM_SW] = v_re.astype(BF16)
        p_ref[s * LANES:(s + 1) * LANES, SSM_SW:] = v_im.astype(BF16)
        bd = (dot_nt_hi_lo(v_re, cb_re_parts) - dot_nt_hi_lo(v_im, cb_im_parts)).astype(BF16)
        for s in range(SSM_T - k):
            t = s + k
            m_ref[s * LANES:(s + 1) * LANES, t * LANES:(t + 1) * LANES] = bd
        lr1, li1 = plr_ref[k + 1:k + 2, :], pli_ref[k + 1:k + 2, :]
        q_ref[:SSM_SW, k * LANES:(k + 1) * LANES] = (cb_re * lr1 - cb_im * li1).T.astype(BF16)
        q_ref[SSM_SW:, k * LANES:(k + 1) * LANES] = (-(cb_re * li1 + cb_im * lr1)).T.astype(BF16)


def _ssm_weights(a_re, a_im, log_dt, b_re, b_im, c_re, c_im, d):
    t_ = SSM_T
    dt = jnp.exp(log_dt)[:, None]
    mag = jnp.exp(a_re * dt)
    lam_re, lam_im = mag * jnp.cos(a_im * dt), mag * jnp.sin(a_im * dt)
    den = a_re * a_re + a_im * a_im
    f_re = ((lam_re - 1.0) * a_re + lam_im * a_im) / den
    f_im = (lam_im * a_re - (lam_re - 1.0) * a_im) / den
    w_re = f_re[:, :, None] * b_re - f_im[:, :, None] * b_im
    w_im = f_re[:, :, None] * b_im + f_im[:, :, None] * b_re
    k = jnp.arange(t_ + 1, dtype=F32)[None, :, None]
    pmag = jnp.exp(a_re[:, None, :] * dt[:, :, None] * k)
    pang = a_im[:, None, :] * dt[:, :, None] * k
    pw_re, pw_im = pmag * jnp.cos(pang), pmag * jnp.sin(pang)

    def lanes_gn(x):
        x = x.reshape(SSM_NJ, SSM_GPB, t_ + 1, SSM_STATE)
        return jnp.swapaxes(x, 1, 2).reshape(SSM_NJ, t_ + 1, SSM_SW)

    pl_re, pl_im = lanes_gn(pw_re), lanes_gn(pw_im)
    wcat = [jnp.swapaxes(x, 1, 2).reshape(SSM_NJ, LANES, SSM_STATE) for x in (w_re, w_im)]
    ccat = [x.reshape(SSM_NJ, LANES, SSM_STATE) for x in (c_re, c_im)]
    rep = jnp.tile(jnp.eye(SSM_STATE, dtype=BF16), (1, SSM_GPB))
    tw = t_ * LANES
    sq = 2 * SSM_SW
    blk = lambda j: (j, 0, 0)
    const = lambda j: (0, 0)
    per_j = lambda a: pl.BlockSpec((None,) + a.shape[1:], blk)
    ins = [wcat[0], wcat[1], ccat[0], ccat[1], pl_re, pl_im]
    m, p, q = pl.pallas_call(
        _ssm_fold_kernel, grid=(SSM_NJ,),
        in_specs=[per_j(a) for a in ins] + [pl.BlockSpec(rep.shape, const)],
        out_specs=[pl.BlockSpec((None, tw, tw), blk), pl.BlockSpec((None, tw, sq), blk),
                   pl.BlockSpec((None, sq, tw), blk)],
        out_shape=[jax.ShapeDtypeStruct((SSM_NJ, tw, tw), BF16), jax.ShapeDtypeStruct((SSM_NJ, tw, sq), BF16),
                   jax.ShapeDtypeStruct((SSM_NJ, sq, tw), BF16)],
        compiler_params=_cp(("arbitrary",)), name="ssm_fold",
    )(*ins, rep)
    lam_t = jnp.concatenate([pl_re[:, t_][:, None, :], pl_im[:, t_][:, None, :]], axis=2)
    dvec = jnp.tile(d.reshape(SSM_NJ, 1, LANES), (1, 1, t_))
    return {"M": m, "P": p, "Q": q, "lam": lam_t, "D": dvec}


def _mixer_kernel(o_ref, s_ref, ga_ref, gb_ref, x_ref, wo_ref, wv_ref, wg_ref, wout_ref, h_ref, s_sc):
    o_a = _dot(o_ref[...], wo_ref[...])
    nchunk = s_sc.shape[1] // SSM_T
    for j in range(SSM_NJ):
        sj = s_ref[j].astype(F32)
        for t in range(SSM_T):
            s_sc[j, pl.ds(t, nchunk, stride=SSM_T), :] = sj[:, t * LANES:(t + 1) * LANES]
    s = jnp.concatenate([s_sc[j] for j in range(SSM_NJ)], axis=1).astype(BF16)
    o_b = _dot(s, wv_ref[...]) * jax.nn.sigmoid(_dot(s, wg_ref[...]))
    merged = jax.nn.sigmoid(ga_ref[...]) * o_a + jax.nn.sigmoid(gb_ref[...]) * o_b
    h_ref[...] = x_ref[...] + _dot(merged.astype(BF16), wout_ref[...])


def _mixer(o, s, ga, gb, x, w, tm):
    n = x.shape[0]
    row = lambda i: (i, 0)
    const = lambda i: (0, 0)
    wspec = pl.BlockSpec((D_MODEL, D_MODEL), const, pipeline_mode=pl.Buffered(1))
    act = pl.BlockSpec((tm, D_MODEL), row)
    return pl.pallas_call(
        _mixer_kernel, grid=(n // tm,),
        in_specs=[act, pl.BlockSpec((SSM_NJ, tm // SSM_T, SSM_T * LANES), lambda i: (0, i, 0)), act, act, act,
                  wspec, wspec, wspec, wspec],
        out_specs=act, out_shape=jax.ShapeDtypeStruct((n, D_MODEL), F32),
        scratch_shapes=[pltpu.VMEM((SSM_NJ, tm, LANES), F32)],
        compiler_params=_cp(("arbitrary",)), name="mixer_out",
    )(o, s, ga, gb, x, w["wo"], w["wv"], w["wg"], w["wout"])


def _mlp_kernel(h_ref, g_ref, wup_ref, wdn_ref, y_ref):
    h = h_ref[...]
    hn = _rms(h, g_ref[...]).astype(BF16)
    a = jnp.maximum(_dot(hn, wup_ref[...]), 0.0)
    y_ref[...] = h + _dot((a * a).astype(BF16), wdn_ref[...])


def _mlp(h, w, tm):
    n = h.shape[0]
    row = lambda i: (i, 0)
    const = lambda i: (0, 0)
    act = pl.BlockSpec((tm, D_MODEL), row)
    return pl.pallas_call(
        _mlp_kernel, grid=(n // tm,),
        in_specs=[act, pl.BlockSpec((1, D_MODEL), const),
                  pl.BlockSpec((D_MODEL, D_FF), const, pipeline_mode=pl.Buffered(1)),
                  pl.BlockSpec((D_FF, D_MODEL), const, pipeline_mode=pl.Buffered(1))],
        out_specs=act, out_shape=jax.ShapeDtypeStruct((n, D_MODEL), F32),
        compiler_params=_cp(("arbitrary",)), name="mlp",
    )(h, w["gmlp"], w["wup"], w["wdn"])


def _cos_sin(pos):
    half = QK_ROPE // 2
    inv = ROPE_THETA ** (-jnp.arange(half, dtype=F32) / half)
    inv = jnp.concatenate([inv, inv, jnp.zeros((LANES - QK_ROPE,), F32)])
    ang = pos.astype(F32)[:, None] * inv[None, :]
    return jnp.cos(ang), jnp.sin(ang)


def _rope_tables(pos=None, length=None):
    if pos is not None:
        cos, sin = _cos_sin(pos)
    else:
        ch, sh = _cos_sin(jnp.arange(length // CHUNK, dtype=jnp.int32) * CHUNK)
        cl, sl = _cos_sin(jnp.arange(CHUNK, dtype=jnp.int32))
        cos = (ch[:, None, :] * cl[None] - sh[:, None, :] * sl[None]).reshape(length, LANES)
        sin = (sh[:, None, :] * cl[None] + ch[:, None, :] * sl[None]).reshape(length, LANES)
    lane = jnp.arange(LANES)
    half = QK_ROPE // 2
    ct = jnp.where(lane < QK_ROPE, cos, 0.0)
    s1 = jnp.where(lane < half, -sin, 0.0)
    s2 = jnp.where((lane >= half) & (lane < QK_ROPE), sin, 0.0)
    return ct, s1, s2


def _pad_lanes(a, width):
    return jnp.pad(a, [(0, 0)] * (a.ndim - 1) + [(0, width - a.shape[-1])])


def _layer_weights(norm_mix, w_in, q_lora_norm, w_uq, q_nope_norm, q_rope_norm, kv_lora_norm, k_rope_norm,
                   w_ukv, k_nope_norm, w_o_attn, w_glu_v, w_glu_g, w_out, norm_mlp, w_mlp_up, w_mlp_down):
    bf = lambda a: a.astype(BF16)
    r2 = lambda a: a.reshape(1, -1).astype(F32)
    return {
        "nmix": r2(norm_mix),
        "win": bf(jnp.swapaxes(w_in, 0, 1)),
        "gq": r2(q_lora_norm),
        "wuqn": bf(w_uq[:, :, :QK_NOPE].reshape(Q_LORA, N_HEADS * QK_NOPE)),
        "wuqr": bf(_pad_lanes(w_uq[:, :, QK_NOPE:], LANES).reshape(Q_LORA, N_HEADS * LANES)),
        "gqn": r2(q_nope_norm) * Q_SCALE, "gqr": _pad_lanes(r2(q_rope_norm), LANES) * Q_SCALE,
        "gkv": r2(kv_lora_norm), "gkr": _pad_lanes(r2(k_rope_norm), LANES),
        "wuk": bf(w_ukv[:, :, :QK_NOPE].reshape(KV_LORA, N_HEADS * QK_NOPE)),
        "wukg": bf((w_ukv[:, :, :QK_NOPE] * k_nope_norm.astype(F32)).reshape(KV_LORA, N_HEADS * QK_NOPE)),
        "wuv": bf(w_ukv[:, :, QK_NOPE:].reshape(KV_LORA, N_HEADS * V_DIM)),
        "gkn": r2(k_nope_norm),
        "wo": bf(w_o_attn.reshape(N_HEADS * V_DIM, D_MODEL)),
        "wv": bf(w_glu_v), "wg": bf(w_glu_g), "wout": bf(w_out),
        "gmlp": r2(norm_mlp), "wup": bf(w_mlp_up), "wdn": bf(w_mlp_down),
    }


def kernel(x_prompt, x_sample, cache_latent, cache_krope, cache_meta_latent, cache_meta_krope, state_ssm_re, state_ssm_im, meta_tokens, norm_mix, w_in, q_lora_norm, w_uq, q_nope_norm, q_rope_norm, kv_lora_norm, k_rope_norm, w_ukv, k_nope_norm, w_o_attn, ssm_a_re, ssm_a_im, ssm_log_dt, ssm_b_re, ssm_b_im, ssm_c_re, ssm_c_im, ssm_d, w_glu_v, w_glu_g, w_out, norm_mlp, w_mlp_up, w_mlp_down):
    bsz_p, seq_p = x_prompt.shape[0], x_prompt.shape[1]
    bsz_s, seq_s = x_sample.shape[0], x_sample.shape[1]
    past = cache_latent.shape[2]
    depth = w_in.shape[0]
    assert depth == 1, "single-layer step"
    l = 0
    w = _layer_weights(norm_mix[l], w_in[l], q_lora_norm[l], w_uq[l], q_nope_norm[l], q_rope_norm[l],
                       kv_lora_norm[l], k_rope_norm[l], w_ukv[l], k_nope_norm[l], w_o_attn[l], w_glu_v[l],
                       w_glu_g[l], w_out[l], norm_mlp[l], w_mlp_up[l], w_mlp_down[l])
    sw = _ssm_weights(ssm_a_re[l].astype(F32), ssm_a_im[l].astype(F32), ssm_log_dt[l].astype(F32),
                      ssm_b_re[l].astype(F32), ssm_b_im[l].astype(F32), ssm_c_re[l].astype(F32),
                      ssm_c_im[l].astype(F32), ssm_d[l].astype(F32))
    sq = 2 * SSM_SW

    tabs_m = _rope_tables(jnp.arange(N_META, dtype=jnp.int32) - N_META)
    (_, mk, mv, mckv, mkr, mu, _, _) = _front(meta_tokens.astype(F32), tabs_m, 1, w, N_META)
    h_meta = _ssm_state(mu, sw, N_META // SSM_T)

    n_p = bsz_p * seq_p
    xp = x_prompt.reshape(n_p, D_MODEL)
    tabs_p = _rope_tables(length=seq_p)
    q, k, v, ckv_p, kr_p, u_p, ga, gb = _front(xp, tabs_p, seq_p // TM_PROMPT, w, TM_PROMPT, kr_seq=seq_p)
    o_p = _attn_prompt(q, k, v, mk, mv, bsz_p, seq_p)
    h0_p = jnp.broadcast_to(h_meta, (SSM_NJ, bsz_p, sq))
    s_p, hfin_p = _ssm_prompt(u_p, sw, h0_p, bsz_p, seq_p // SSM_T, SSM_RT)
    h1_p = _mixer(o_p, s_p, ga, gb, xp, w, TM_PROMPT)
    y_p = _mlp(h1_p, w, TM_PROMPT)

    n_s = bsz_s * seq_s
    xs = x_sample.reshape(n_s, D_MODEL)
    pos_s = past + jnp.arange(seq_s, dtype=jnp.int32)
    tabs_s = tuple(jnp.tile(t, (bsz_s, 1)) for t in _rope_tables(pos_s))
    q_s, k_s, _, ckv_s, kr_s, u_s, ga_s, gb_s = _front(xs, tabs_s, n_s // TM_SAMPLE, w, TM_SAMPLE)
    o_s = _attn_sample(q_s, k_s, ckv_s, cache_latent[l].astype(F32), cache_krope[l].astype(F32),
                       cache_meta_latent[l].astype(F32), cache_meta_krope[l].astype(F32), w, bsz_s, seq_s, past)

    def to_blocks(st):
        return st.astype(F32).reshape(bsz_s, SSM_NJ, SSM_SW).transpose(1, 0, 2)

    h0_s = jnp.concatenate([to_blocks(state_ssm_re[l]), to_blocks(state_ssm_im[l])], axis=2)
    s_s, hfin_s = _ssm_sample(u_s, sw, h0_s, bsz_s, seq_s // SSM_T)
    h1_s = _mixer(o_s, s_s, ga_s, gb_s, xs, w, TM_SAMPLE)
    y_s = _mlp(h1_s, w, TM_SAMPLE)

    def from_blocks(hf, nb):
        re = hf[:, :, :SSM_SW].transpose(1, 0, 2).reshape(1, nb, N_GROUPS, SSM_STATE)
        im = hf[:, :, SSM_SW:].transpose(1, 0, 2).reshape(1, nb, N_GROUPS, SSM_STATE)
        return re, im

    sre_p, sim_p = from_blocks(hfin_p, bsz_p)
    sre_s, sim_s = from_blocks(hfin_s, bsz_s)
    return (y_p.reshape(bsz_p, seq_p, D_MODEL), y_s.reshape(bsz_s, seq_s, D_MODEL),
            ckv_p.reshape(1, bsz_p, seq_p, KV_LORA), jnp.swapaxes(kr_p, 1, 2)[None],
            jnp.broadcast_to(mckv[None, None], (1, bsz_p, N_META, KV_LORA)),
            jnp.broadcast_to(mkr[None, None], (1, bsz_p, N_META, QK_ROPE)),
            sre_p, sim_p,
            ckv_s.reshape(1, bsz_s, seq_s, KV_LORA), kr_s.reshape(1, bsz_s, seq_s, QK_ROPE),
            sre_s, sim_s)
```
